```python
import math
import jax
import jax.numpy as jnp
from jax import lax
import numpy as np

D_MODEL = 1024
BATCH = 32
SEQ = 256
DEPTH = 4
DEC_BATCH = 8
DEC_SEQ = 2048
PAST_LEN = 512

F32 = jnp.float32
EPS = 1e-6
GRID_W = 64
MIX_GROUP = D_MODEL // 4
MIX_WIDTH = 4 * MIX_GROUP
CONV_CH = MIX_GROUP
CONV_WIDTH = 31
MLSTM_HEADS = 4
MLSTM_HD = MIX_GROUP // MLSTM_HEADS
MLSTM_CHUNK = 128
MLA_HEADS = 4
MLA_NOPE = 64
MLA_ROPE = 32
MLA_V = MIX_GROUP // MLA_HEADS
MLA_Q_RANK = D_MODEL // 4
MLA_KV_RANK = D_MODEL // 8
ROPE_BASE = 10000.0
ATTN_BLOCK = 128
S5_GC = 16
S5_GROUPS = MIX_GROUP // S5_GC
S5_STATE = 64
MOE_GROUPS = 4
MOE_PER_GROUP = 8
MOE_EXPERTS = MOE_GROUPS * MOE_PER_GROUP
MOE_TOPK = 2
MOE_FF = D_MODEL // 4
A_COLS = 2 * CONV_CH
B_COLS = 4 * MIX_GROUP + 4 * MLSTM_HEADS
C_COLS = MLA_Q_RANK + MLA_KV_RANK + MLA_ROPE
D_COLS = MIX_GROUP
IN_COLS = A_COLS + B_COLS + C_COLS + D_COLS

kernel_name = "hybrid_diffusion_prefix_trunk"


def rmsnorm(x, g):
    xf = x.astype(F32)
    y = xf * lax.rsqrt(jnp.mean(xf * xf, axis=-1, keepdims=True) + EPS)
    return (y * g.astype(F32)).astype(x.dtype)


def conv_module(u, w, b, ln_g, ln_b):
    a, gt = jnp.split(u, 2, axis=-1)
    h = a * jax.nn.sigmoid(gt)
    pad = CONV_WIDTH // 2
    y = lax.conv_general_dilated(h, w[:, None, :].astype(h.dtype), window_strides=(1,),
                                 padding=[(pad, pad)], dimension_numbers=("NWC", "WIO", "NWC"),
                                 feature_group_count=CONV_CH) + b
    yf = y.astype(F32)
    mu = jnp.mean(yf, axis=-1, keepdims=True)
    var = jnp.mean(jnp.square(yf - mu), axis=-1, keepdims=True)
    yn = (yf - mu) * lax.rsqrt(var + EPS) * ln_g.astype(F32) + ln_b.astype(F32)
    return jax.nn.silu(yn).astype(u.dtype)


def mlstm_scan(q, k, v, ig, lf, C0, n0, m0):
    bsz, nh, t, hd = q.shape
    nc = t // MLSTM_CHUNK

    def chunks(a):
        a = a.reshape((bsz, nh, nc, MLSTM_CHUNK) + a.shape[3:])
        return jnp.moveaxis(a, 2, 0)

    causal = jnp.tril(jnp.ones((MLSTM_CHUNK, MLSTM_CHUNK), dtype=bool))

    def step(carry, xs):
        C, n, m = carry
        qc, kc, vc, ic, fc = xs
        bcum = jnp.cumsum(fc, axis=-1)
        log_d = bcum[..., :, None] - bcum[..., None, :] + ic[..., None, :]
        log_d = jnp.where(causal, log_d, -jnp.inf)
        m_inter = bcum + m[..., None]
        m_t = jnp.maximum(m_inter, jnp.max(log_d, axis=-1))
        d_mat = jnp.exp(log_d - m_t[..., None])
        inter = jnp.exp(m_inter - m_t)
        s = jnp.einsum("bhtd,bhsd->bhts", qc, kc) * d_mat
        num = inter[..., None] * jnp.einsum("bhtd,bhde->bhte", qc, C) + jnp.einsum("bhts,bhse->bhte", s, vc)
        den = inter * jnp.einsum("bhtd,bhd->bht", qc, n) + jnp.sum(s, axis=-1)
        h = num / jnp.maximum(jnp.abs(den), jnp.exp(-m_t))[..., None]
        b_last = bcum[..., -1]
        log_w = b_last[..., None] - bcum + ic
        m_new = jnp.maximum(b_last + m, jnp.max(log_w, axis=-1))
        w = jnp.exp(log_w - m_new[..., None])
        decay = jnp.exp(b_last + m - m_new)
        C_new = decay[..., None, None] * C + jnp.einsum("bhs,bhsd,bhse->bhde", w, kc, vc)
        n_new = decay[..., None] * n + jnp.einsum("bhs,bhsd->bhd", w, kc)
        return (C_new, n_new, m_new), h

    (C, n, m), hs = lax.scan(step, (C0, n0, m0), (chunks(q), chunks(k), chunks(v), chunks(ig), chunks(lf)))
    h = jnp.moveaxis(hs, 0, 2).reshape(bsz, nh, t, hd)
    return h, C, n, m


def mlstm_mixer(z, gate_b, norm_g, state):
    bsz, t, _ = z.shape
    q, k, v, o, g = jnp.split(z, [MIX_GROUP, 2 * MIX_GROUP, 3 * MIX_GROUP, 4 * MIX_GROUP], axis=-1)

    def heads(a):
        return a.astype(F32).reshape(bsz, t, MLSTM_HEADS, MLSTM_HD).transpose(0, 2, 1, 3)

    q = heads(q) * (MLSTM_HD ** -0.5)
    k = heads(k)
    v = heads(v)
    g = g.astype(F32).reshape(bsz, t, 2, 2, MLSTM_HEADS) + gate_b.astype(F32)
    g = g.transpose(0, 2, 3, 4, 1)
    ig = g[:, 0]
    lf = jax.nn.log_sigmoid(g[:, 1])
    h_dirs, Cs, ns, ms = [], [], [], []
    for dr in range(2):
        if state is None:
            C0 = jnp.zeros((bsz, MLSTM_HEADS, MLSTM_HD, MLSTM_HD), F32)
            n0 = jnp.zeros((bsz, MLSTM_HEADS, MLSTM_HD), F32)
            m0 = jnp.zeros((bsz, MLSTM_HEADS), F32)
        else:
            C0 = state[0][:, dr].astype(F32)
            n0 = state[1][:, dr].astype(F32)
            m0 = state[2][:, dr].astype(F32)
        if dr == 0:
            h, C, n, m = mlstm_scan(q, k, v, ig[:, 0], lf[:, 0], C0, n0, m0)
        else:
            h, C, n, m = mlstm_scan(jnp.flip(q, 2), jnp.flip(k, 2), jnp.flip(v, 2),
                                    jnp.flip(ig[:, 1], -1), jnp.flip(lf[:, 1], -1), C0, n0, m0)
            h = jnp.flip(h, 2)
        h_dirs.append(h)
        Cs.append(C)
        ns.append(n)
        ms.append(m)
    h = h_dirs[0] + h_dirs[1]
    h = h * lax.rsqrt(jnp.mean(h * h, axis=-1, keepdims=True) + EPS)
    h = h.transpose(0, 2, 1, 3).reshape(bsz, t, MIX_GROUP) * norm_g.astype(F32)
    out = h.astype(z.dtype) * jax.nn.sigmoid(o)
    return out, (jnp.stack(Cs, 1), jnp.stack(ns, 1), jnp.stack(ms, 1))


def rope_1d(x, ang):
    x1, x2 = jnp.split(x, 2, axis=-1)
    cos = jnp.cos(ang)[:, None, :].astype(x.dtype)
    sin = jnp.sin(ang)[:, None, :].astype(x.dtype)
    return jnp.concatenate([x1 * cos - x2 * sin, x1 * sin + x2 * cos], axis=-1)


def rope2d(x):
    t = x.shape[1]
    rows = t // GRID_W
    row = jnp.repeat(jnp.arange(rows, dtype=F32), GRID_W)
    col = jnp.tile(jnp.arange(GRID_W, dtype=F32), rows)
    nf = MLA_ROPE // 4
    inv = ROPE_BASE ** (-jnp.arange(nf, dtype=F32) / nf)
    xr, xc = jnp.split(x, 2, axis=-1)
    return jnp.concatenate([rope_1d(xr, row[:, None] * inv), rope_1d(xc, col[:, None] * inv)], axis=-1)


def block_attention(q, k, v):
    b, tq, h, dq = q.shape
    nb = tq // ATTN_BLOCK
    qb = jnp.moveaxis(q.reshape(b, nb, ATTN_BLOCK, h, dq), 1, 0)
    scale = dq ** -0.5

    def one_block(qblk):
        s = jnp.einsum("bqhd,bkhd->bhqk", qblk, k).astype(F32) * scale
        p = jax.nn.softmax(s, axis=-1).astype(v.dtype)
        return jnp.einsum("bhqk,bkhd->bqhd", p, v)

    out = lax.map(one_block, qb)
    return jnp.moveaxis(out, 0, 1).reshape(b, tq, h, v.shape[-1])


def mla_mixer(z, q_norm_g, w_uq, kv_norm_g, w_ukv, ctx):
    bsz, t, _ = z.shape
    c_q, c_kv, k_rope = jnp.split(z, [MLA_Q_RANK, MLA_Q_RANK + MLA_KV_RANK], axis=-1)
    q = (rmsnorm(c_q, q_norm_g) @ w_uq).reshape(bsz, t, MLA_HEADS, MLA_NOPE + MLA_ROPE)
    q_nope, q_rope = jnp.split(q, [MLA_NOPE], axis=-1)
    ckv = rmsnorm(c_kv, kv_norm_g)

    def up_kv(lat, kr):
        n_tok = lat.shape[1]
        kv = (lat @ w_ukv).reshape(bsz, n_tok, MLA_HEADS, MLA_NOPE + MLA_V)
        k_nope, val = jnp.split(kv, [MLA_NOPE], axis=-1)
        kr_h = jnp.broadcast_to(kr[:, :, None, :], (bsz, n_tok, MLA_HEADS, MLA_ROPE))
        return jnp.concatenate([k_nope, kr_h], axis=-1), val

    if ctx is None:
        keys, vals = up_kv(ckv, k_rope)
    else:
        q_rope = rope2d(q_rope)
        k_rope_lat = rope2d(k_rope[:, :, None, :])[:, :, 0]
        k_lat, v_lat = up_kv(ckv, k_rope_lat)
        k_ctx, v_ctx = up_kv(ctx[0].astype(z.dtype), ctx[1].astype(z.dtype))
        keys = jnp.concatenate([k_ctx, k_lat], axis=1)
        vals = jnp.concatenate([v_ctx, v_lat], axis=1)
    qf = jnp.concatenate([q_nope, q_rope], axis=-1)
    out = block_attention(qf, keys, vals).reshape(bsz, t, MIX_GROUP)
    return out, (ckv, k_rope)


def _cmul_combine(c1, c2):
    a1r, a1i, b1r, b1i = c1
    a2r, a2i, b2r, b2i = c2
    return (a2r * a1r - a2i * a1i, a2r * a1i + a2i * a1r,
            a2r * b1r - a2i * b1i + b2r, a2r * b1i + a2i * b1r + b2i)


def s5_scan(u, a_re, a_im, log_dt, b_re, b_im, c_re, c_im, s0):
    a_re = a_re.astype(F32)
    a_im = a_im.astype(F32)
    dt = jnp.exp(log_dt.astype(F32))[:, None]
    mag = jnp.exp(a_re * dt)
    ab_re = mag * jnp.cos(a_im * dt)
    ab_im = mag * jnp.sin(a_im * dt)
    den = a_re * a_re + a_im * a_im
    f_re = ((ab_re - 1.0) * a_re + ab_im * a_im) / den
    f_im = (ab_im * a_re - (ab_re - 1.0) * a_im) / den
    b_re = b_re.astype(F32)
    b_im = b_im.astype(F32)
    bb_re = f_re[..., None] * b_re - f_im[..., None] * b_im
    bb_im = f_re[..., None] * b_im + f_im[..., None] * b_re
    bu_re = jnp.einsum("btgc,gnc->btgn", u, bb_re)
    bu_im = jnp.einsum("btgc,gnc->btgn", u, bb_im)
    if s0 is not None:
        s_re0 = s0[..., 0].astype(F32)
        s_im0 = s0[..., 1].astype(F32)
        bu_re = bu_re.at[:, 0].add(ab_re * s_re0 - ab_im * s_im0)
        bu_im = bu_im.at[:, 0].add(ab_re * s_im0 + ab_im * s_re0)
    A_re = jnp.broadcast_to(ab_re, bu_re.shape)
    A_im = jnp.broadcast_to(ab_im, bu_re.shape)
    _, _, s_re, s_im = lax.associative_scan(_cmul_combine, (A_re, A_im, bu_re, bu_im), axis=1)
    y = (jnp.einsum("btgn,gcn->btgc", s_re, c_re.astype(F32))
         - jnp.einsum("btgn,gcn->btgc", s_im, c_im.astype(F32)))
    final = jnp.stack([s_re[:, -1], s_im[:, -1]], axis=-1)
    return y, final


def s5_mixer(z, a_re, a_im, log_dt, b_re, b_im, c_re, c_im, d, w_glu, b_glu, state):
    bsz, t, _ = z.shape
    u = z.astype(F32).reshape(bsz, t, S5_GROUPS, S5_GC)
    ys, finals = [], []
    for dr in range(2):
        s0 = None if state is None else state[:, dr]
        ud = u if dr == 0 else jnp.flip(u, 1)
        yd, fin = s5_scan(ud, a_re[dr], a_im[dr], log_dt[dr], b_re[dr], b_im[dr], c_re[dr], c_im[dr], s0)
        ys.append(yd if dr == 0 else jnp.flip(yd, 1))
        finals.append(fin)
    y = (ys[0] + ys[1]).reshape(bsz, t, MIX_GROUP) + d.astype(F32) * z.astype(F32)
    gl = jax.nn.gelu(y).astype(z.dtype) @ w_glu + b_glu
    a, gt = jnp.split(gl, 2, axis=-1)
    return a * jax.nn.sigmoid(gt), jnp.stack(finals, 1)


def hier_moe(h, w_g, b_g, w_e, b_e, w1, w3, w2):
    b, t, d = h.shape
    x = h.reshape(b * t, d)
    g_logit = (x @ w_g + b_g).astype(F32)
    g_prob = jax.nn.softmax(g_logit, axis=-1)
    g_idx = jnp.argmax(g_logit, axis=-1)
    g_onehot = jax.nn.one_hot(g_idx, MOE_GROUPS, dtype=F32)
    g_sel = jnp.sum(g_prob * g_onehot, axis=-1, keepdims=True)
    e_logit = (x @ w_e + b_e).astype(F32).reshape(-1, MOE_GROUPS, MOE_PER_GROUP)
    e_in_group = jnp.einsum("ngk,ng->nk", e_logit, g_onehot)
    top_v, top_i = lax.top_k(e_in_group, MOE_TOPK)
    w_top = jax.nn.softmax(top_v, axis=-1) * g_sel
    e_idx = g_idx[:, None] * MOE_PER_GROUP + top_i
    gate = jnp.einsum("nk,nke->ne", w_top, jax.nn.one_hot(e_idx, MOE_EXPERTS, dtype=F32)).astype(x.dtype)
    hid = jax.nn.silu(jnp.einsum("nd,edf->nef", x, w1)) * jnp.einsum("nd,edf->nef", x, w3)
    out = jnp.einsum("nef,ne,efd->nd", hid, gate, w2)
    return out.reshape(b, t, d)


def trunk_layer(x, mod, p, ctx):
    sh1, sc1, g1, sh2, sc2, g2 = [mod[:, i][:, None, :] for i in range(6)]
    h = rmsnorm(x, p["norm1_g"]) * (1 + sc1) + sh1
    z = h @ p["w_in"]
    za, zb, zc, zd = jnp.split(z, [A_COLS, A_COLS + B_COLS, A_COLS + B_COLS + C_COLS], axis=-1)
    ya = conv_module(za, p["conv_w"], p["conv_b"], p["conv_ln_g"], p["conv_ln_b"])
    yb, mst = mlstm_mixer(zb, p["mlstm_gate_b"], p["mlstm_norm_g"], None if ctx is None else ctx["mlstm"])
    yc, mla_kv = mla_mixer(zc, p["mla_q_norm_g"], p["mla_w_uq"], p["mla_kv_norm_g"], p["mla_w_ukv"],
                           None if ctx is None else ctx["mla"])
    yd, s5st = s5_mixer(zd, p["s5_a_re"], p["s5_a_im"], p["s5_log_dt"], p["s5_b_re"], p["s5_b_im"],
                        p["s5_c_re"], p["s5_c_im"], p["s5_d"], p["s5_w_glu"], p["s5_b_glu"],
                        None if ctx is None else ctx["s5"])
    x = x + g1 * (jnp.concatenate([ya, yb, yc, yd], axis=-1) @ p["w_out"])
    h2 = rmsnorm(x, p["norm2_g"]) * (1 + sc2) + sh2
    x = x + g2 * hier_moe(h2, p["moe_w_group"], p["moe_b_group"], p["moe_w_expert"], p["moe_b_expert"],
                          p["moe_w1"], p["moe_w3"], p["moe_w2"])
    return x, (mla_kv, mst, s5st)


def setup_inputs(seed: int = 0) -> dict:
    key = jax.random.key(seed)
    kit = iter(jax.random.split(key, 64))
    L, D, H, G, N = DEPTH, D_MODEL, MLSTM_HEADS, S5_GROUPS, S5_STATE

    def nrm(shape, scale=1.0):
        return jax.random.normal(next(kit), shape, F32) * scale

    def gain(shape):
        return 1.0 + nrm(shape, 0.01)

    n_idx = jnp.arange(N, dtype=F32)
    f_bias = jnp.linspace(3.0, 6.0, H, dtype=F32)
    mlstm_gate_b = jnp.concatenate([nrm((L, 1, 2, H), 0.1), f_bias + nrm((L, 1, 2, H), 0.1)], axis=1)
    s5_log_dt = jax.random.uniform(next(kit), (L, 2, G), F32, math.log(1e-3), math.log(1e-1))
    return dict(
        x_prompt=nrm((BATCH, SEQ, D)),
        x_sample=nrm((DEC_BATCH, DEC_SEQ, D)),
        cache_mla_ckv=nrm((DEC_BATCH, L, PAST_LEN, MLA_KV_RANK)),
        cache_mla_krope=nrm((DEC_BATCH, L, PAST_LEN, MLA_ROPE)),
        state_mlstm_C=nrm((DEC_BATCH, L, 2, H, MLSTM_HD, MLSTM_HD), 0.1),
        state_mlstm_n=nrm((DEC_BATCH, L, 2, H, MLSTM_HD), 0.1),
        state_mlstm_m=nrm((DEC_BATCH, L, 2, H)),
        state_s5=nrm((DEC_BATCH, L, 2, G, N, 2), 0.5),
        c=nrm((DEC_BATCH, D)),
        c_ctx=nrm((D,)),
        norm1_g=gain((L, D)),
        norm2_g=gain((L, D)),
        final_g=gain((D,)),
        w_mod=nrm((L, D, 6 * D), 0.5 * D ** -0.5),
        b_mod=nrm((L, 6 * D), 0.02),
        w_in=nrm((L, D, IN_COLS), D ** -0.5),
        w_out=nrm((L, MIX_WIDTH, D), MIX_WIDTH ** -0.5),
        conv_w=nrm((L, CONV_WIDTH, CONV_CH), CONV_WIDTH ** -0.5),
        conv_b=nrm((L, CONV_CH), 0.01),
        conv_ln_g=gain((L, CONV_CH)),
        conv_ln_b=nrm((L, CONV_CH), 0.01),
        mlstm_gate_b=mlstm_gate_b,
        mlstm_norm_g=gain((L, MIX_GROUP)),
        mla_q_norm_g=gain((L, MLA_Q_RANK)),
        mla_w_uq=nrm((L, MLA_Q_RANK, MLA_HEADS * (MLA_NOPE + MLA_ROPE)), MLA_Q_RANK ** -0.5),
        mla_kv_norm_g=gain((L, MLA_KV_RANK)),
        mla_w_ukv=nrm((L, MLA_KV_RANK, MLA_HEADS * (MLA_NOPE + MLA_V)), MLA_KV_RANK ** -0.5),
        s5_a_re=-0.5 + nrm((L, 2, G, N), 0.01),
        s5_a_im=jnp.broadcast_to(math.pi * n_idx, (L, 2, G, N)),
        s5_log_dt=s5_log_dt,
        s5_b_re=nrm((L, 2, G, N, S5_GC), (2 * S5_GC) ** -0.5),
        s5_b_im=nrm((L, 2, G, N, S5_GC), (2 * S5_GC) ** -0.5),
        s5_c_re=nrm((L, 2, G, S5_GC, N), (2 * N) ** -0.5),
        s5_c_im=nrm((L, 2, G, S5_GC, N), (2 * N) ** -0.5),
        s5_d=nrm((L, MIX_GROUP)),
        s5_w_glu=nrm((L, MIX_GROUP, 2 * MIX_GROUP), MIX_GROUP ** -0.5),
        s5_b_glu=nrm((L, 2 * MIX_GROUP), 0.01),
        moe_w_group=nrm((L, D, MOE_GROUPS), D ** -0.5),
        moe_b_group=nrm((L, MOE_GROUPS), 0.01),
        moe_w_expert=nrm((L, D, MOE_EXPERTS), D ** -0.5),
        moe_b_expert=nrm((L, MOE_EXPERTS), 0.01),
        moe_w1=nrm((L, MOE_EXPERTS, D, MOE_FF), D ** -0.5),
        moe_w3=nrm((L, MOE_EXPERTS, D, MOE_FF), D ** -0.5),
        moe_w2=nrm((L, MOE_EXPERTS, MOE_FF, D), MOE_FF ** -0.5),
    )


def reference(x_prompt, x_sample, cache_mla_ckv, cache_mla_krope, state_mlstm_C, state_mlstm_n,
              state_mlstm_m, state_s5, c, c_ctx, norm1_g, norm2_g, final_g, w_mod, b_mod, w_in, w_out,
              conv_w, conv_b, conv_ln_g, conv_ln_b, mlstm_gate_b, mlstm_norm_g, mla_q_norm_g, mla_w_uq,
              mla_kv_norm_g, mla_w_ukv, s5_a_re, s5_a_im, s5_log_dt, s5_b_re, s5_b_im, s5_c_re, s5_c_im,
              s5_d, s5_w_glu, s5_b_glu, moe_w_group, moe_b_group, moe_w_expert, moe_b_expert,
              moe_w1, moe_w3, moe_w2):
    x_ctx = x_prompt
    x_lat = x_sample
    new_ckv, new_krope, new_C, new_n, new_m, new_s5 = [], [], [], [], [], []
    for l in range(DEPTH):
        p = dict(norm1_g=norm1_g[l], norm2_g=norm2_g[l], w_in=w_in[l], w_out=w_out[l],
                 conv_w=conv_w[l], conv_b=conv_b[l], conv_ln_g=conv_ln_g[l], conv_ln_b=conv_ln_b[l],
                 mlstm_gate_b=mlstm_gate_b[l], mlstm_norm_g=mlstm_norm_g[l],
                 mla_q_norm_g=mla_q_norm_g[l], mla_w_uq=mla_w_uq[l], mla_kv_norm_g=mla_kv_norm_g[l],
                 mla_w_ukv=mla_w_ukv[l], s5_a_re=s5_a_re[l], s5_a_im=s5_a_im[l], s5_log_dt=s5_log_dt[l],
                 s5_b_re=s5_b_re[l], s5_b_im=s5_b_im[l], s5_c_re=s5_c_re[l], s5_c_im=s5_c_im[l],
                 s5_d=s5_d[l], s5_w_glu=s5_w_glu[l], s5_b_glu=s5_b_glu[l],
                 moe_w_group=moe_w_group[l], moe_b_group=moe_b_group[l], moe_w_expert=moe_w_expert[l],
                 moe_b_expert=moe_b_expert[l], moe_w1=moe_w1[l], moe_w3=moe_w3[l], moe_w2=moe_w2[l])
        mod_ctx = (jax.nn.silu(c_ctx)[None, :] @ w_mod[l] + b_mod[l]).reshape(1, 6, D_MODEL)
        x_ctx, (mla_kv, mst, s5st) = trunk_layer(x_ctx, mod_ctx, p, None)
        new_ckv.append(mla_kv[0])
        new_krope.append(mla_kv[1])
        new_C.append(mst[0])
        new_n.append(mst[1])
        new_m.append(mst[2])
        new_s5.append(s5st)
        mod_lat = (jax.nn.silu(c) @ w_mod[l] + b_mod[l]).reshape(-1, 6, D_MODEL)
        ctx_l = dict(mla=(cache_mla_ckv[:, l], cache_mla_krope[:, l]),
                     mlstm=(state_mlstm_C[:, l], state_mlstm_n[:, l], state_mlstm_m[:, l]),
                     s5=state_s5[:, l])
        x_lat, _ = trunk_layer(x_lat, mod_lat, p, ctx_l)
    y_prompt = rmsnorm(x_ctx, final_g)
    y_sample = rmsnorm(x_lat, final_g)
    return (y_prompt, y_sample, jnp.stack(new_ckv, 1), jnp.stack(new_krope, 1), jnp.stack(new_C, 1),
            jnp.stack(new_n, 1), jnp.stack(new_m, 1), jnp.stack(new_s5, 1))
```

```python
import functools
import math

import jax
import jax.numpy as jnp
from jax import lax
from jax.experimental import pallas as pl
from jax.experimental.pallas import tpu as pltpu

F32 = jnp.float32
BF16 = jnp.bfloat16
EPS = 1e-6

D_MODEL = 1024
MIX = 256
CONV_WIDTH = 31
HEADS = 4
HD = 64
CHUNK = 128
ROPE = 32
KV_RANK = 128
Q_RANK = 256
GRID_W = 64
ROPE_BASE = 10000.0
S5_G = 16
S5_GC = 16
S5_N = 64
S5_STATES = S5_G * S5_N
N_EXPERT = 32
PER_GROUP = 8
N_GROUP = 4
MOE_FF = 256
LANE = 128
Z_COLS = 2560
VMEM_LIMIT = 56 * 1024 * 1024


def _cp(*sem):
    return pltpu.CompilerParams(dimension_semantics=sem, vmem_limit_bytes=VMEM_LIMIT)


def _rms(x):
    return x * lax.rsqrt(jnp.mean(x * x, axis=-1, keepdims=True) + EPS)


def _sigmoid(x):
    return 1.0 / (1.0 + jnp.exp(-x))


def _bdot(a, b):
    return jnp.dot(a.astype(BF16), b.astype(BF16), preferred_element_type=F32)


def _mod_kernel(c_ref, w_ref, b_ref, o_ref):
    c = c_ref[...]
    o_ref[0] = _bdot(c * _sigmoid(c), w_ref[0]) + b_ref[0]


def _modulation(c_all, w_mod, b_mod):
    n_layer, d, n = w_mod.shape
    tn = 1536
    return pl.pallas_call(
        _mod_kernel,
        out_shape=jax.ShapeDtypeStruct((n_layer, 16, n), F32),
        grid=(n_layer, n // tn),
        in_specs=[pl.BlockSpec((16, d), lambda l, j: (0, 0)),
                  pl.BlockSpec((1, d, tn), lambda l, j: (l, 0, j)),
                  pl.BlockSpec((1, 1, tn), lambda l, j: (l, 0, j))],
        out_specs=pl.BlockSpec((1, 16, tn), lambda l, j: (l, 0, j)),
        compiler_params=_cp("parallel", "parallel"),
        name="modulation",
    )(c_all, w_mod, b_mod.reshape(n_layer, 1, n))


def _s5_disc_kernel(are_ref, aim_ref, ldt_ref, bre_ref, bim_ref, abre_ref, abim_ref, bbre_ref, bbim_ref):
    a_re = are_ref[...]
    a_im = aim_ref[...]
    dt = jnp.exp(ldt_ref[...])
    mag = jnp.exp(a_re * dt)
    ab_re = mag * jnp.cos(a_im * dt)
    ab_im = mag * jnp.sin(a_im * dt)
    den = a_re * a_re + a_im * a_im
    f_re = ((ab_re - 1.0) * a_re + ab_im * a_im) / den
    f_im = (ab_im * a_re - (ab_re - 1.0) * a_im) / den
    b_re = bre_ref[...]
    b_im = bim_ref[...]
    abre_ref[...] = ab_re
    abim_ref[...] = ab_im
    bbre_ref[...] = f_re * b_re - f_im * b_im
    bbim_ref[...] = f_re * b_im + f_im * b_re


def _s5_discretise(a_re, a_im, log_dt, b_re, b_im):
    n_layer = a_re.shape[0]
    rows = n_layer * 2 * S5_G
    cols = S5_N * S5_GC
    rep = lambda a: jnp.repeat(a.reshape(rows, S5_N), S5_GC, axis=1)
    ldt = jnp.broadcast_to(log_dt.reshape(rows, 1), (rows, cols))
    spec = pl.BlockSpec((rows, cols), lambda: (0, 0))
    ab_re, ab_im, bb_re, bb_im = pl.pallas_call(
        _s5_disc_kernel,
        out_shape=[jax.ShapeDtypeStruct((rows, cols), F32)] * 4,
        in_specs=[spec] * 5,
        out_specs=[spec] * 4,
        name="s5_discretise",
    )(rep(a_re), rep(a_im), ldt, b_re.reshape(rows, cols), b_im.reshape(rows, cols))
    pick = lambda a: a[:, ::S5_GC].reshape(n_layer, 2, S5_STATES)
    shp = (n_layer, 2, S5_G, S5_N, S5_GC)
    return pick(ab_re), pick(ab_im), bb_re.reshape(shp), bb_im.reshape(shp)


def _pre_kernel(x_ref, mod_ref, g_ref, w_ref, za_ref, zb_ref, zg_ref, zc_ref, zd_ref):
    m = mod_ref[0]
    h = _rms(x_ref[...]) * g_ref[...] * (1.0 + m[1:2]) + m[0:1]
    hb = h.astype(BF16)
    col = 0
    for o_ref in (za_ref, zb_ref, zg_ref, zc_ref, zd_ref):
        n = o_ref.shape[-1]
        o_ref[...] = jnp.dot(hb, w_ref[:, col:col + n], preferred_element_type=F32)
        col += n


def _pre_mixer(x, mod, row_fn, norm_g, w_in_ext, tm):
    n = x.shape[0]
    widths = (512, 1024, LANE, 640, MIX)
    return pl.pallas_call(
        _pre_kernel,
        out_shape=[jax.ShapeDtypeStruct((n, w), F32) for w in widths],
        grid=(n // tm,),
        in_specs=[pl.BlockSpec((tm, D_MODEL), lambda i: (i, 0)),
                  pl.BlockSpec((1, 6, D_MODEL), lambda i: (row_fn(i), 0, 0)),
                  pl.BlockSpec((1, D_MODEL), lambda i: (0, 0)),
                  pl.BlockSpec((D_MODEL, Z_COLS), lambda i: (0, 0))],
        out_specs=[pl.BlockSpec((tm, w), lambda i: (i, 0)) for w in widths],
        compiler_params=_cp("parallel"),
        name="pre_mixer",
    )(x, mod, norm_g, w_in_ext)


_CONV_PAD = 16
_CONV_TT = 128


def _conv_kernel(za_ref, w_ref, b_ref, lg_ref, lb_ref, o_ref, hp_ref):
    t_len = o_ref.shape[1]
    u = za_ref[0]
    hp_ref[0:_CONV_PAD, :] = jnp.zeros((_CONV_PAD, MIX), F32)
    hp_ref[_CONV_PAD + t_len:2 * _CONV_PAD + t_len, :] = jnp.zeros((_CONV_PAD, MIX), F32)
    hp_ref[_CONV_PAD:_CONV_PAD + t_len, :] = u[:, :MIX] * _sigmoid(u[:, MIX:])
    w = w_ref[...]
    half = CONV_WIDTH // 2
    for t0 in range(0, t_len, _CONV_TT):
        acc = jnp.zeros((_CONV_TT, MIX), F32) + b_ref[...]
        for k in range(CONV_WIDTH):
            start = t0 + _CONV_PAD - half + k
            acc = acc + hp_ref[start:start + _CONV_TT, :] * w[k:k + 1, :]
        mu = jnp.mean(acc, axis=-1, keepdims=True)
        cen = acc - mu
        var = jnp.mean(cen * cen, axis=-1, keepdims=True)
        yn = cen * lax.rsqrt(var + EPS) * lg_ref[...] + lb_ref[...]
        o_ref[0, t0:t0 + _CONV_TT, :] = yn * _sigmoid(yn)


def _conv_module(za, w, b, ln_g, ln_b):
    bsz, t_len, _ = za.shape
    vec = pl.BlockSpec((1, MIX), lambda i: (0, 0))
    return pl.pallas_call(
        _conv_kernel,
        out_shape=jax.ShapeDtypeStruct((bsz, t_len, MIX), F32),
        grid=(bsz,),
        in_specs=[pl.BlockSpec((1, t_len, 2 * MIX), lambda i: (i, 0, 0)),
                  pl.BlockSpec((32, MIX), lambda i: (0, 0)), vec, vec, vec],
        out_specs=pl.BlockSpec((1, t_len, MIX), lambda i: (i, 0, 0)),
        scratch_shapes=[pltpu.VMEM((t_len + 2 * _CONV_PAD, MIX), F32)],
        compiler_params=_cp("parallel"),
        name="conv_module",
    )(za, w, b, ln_g, ln_b)


def _log_sigmoid(x):
    return jnp.minimum(x, 0.0) - jnp.log(1.0 + jnp.exp(-jnp.abs(x)))


def _split3_dot(tri, x):
    hi = x.astype(BF16)
    r1 = x - hi.astype(F32)
    mid = r1.astype(BF16)
    lo = (r1 - mid.astype(F32)).astype(BF16)
    dot = lambda v: jnp.dot(tri, v, preferred_element_type=F32)
    return dot(hi) + dot(mid) + dot(lo)


def _mlstm_chunk(zb_ref, zg_ref, gb_ref, c, direction, state):
    L = CHUNK
    rows = pl.ds(pl.multiple_of(c * L, L), L)
    ri = lax.broadcasted_iota(jnp.int32, (L, L), 0)
    ci = lax.broadcasted_iota(jnp.int32, (L, L), 1)
    if direction == 0:
        valid = ci <= ri
        last = L - 1
    else:
        valid = ci >= ri
        last = 0
    tri = jnp.where(valid, 1.0, 0.0).astype(BF16)
    g = zg_ref[0, rows, :] + gb_ref[...]
    bc = _split3_dot(tri, _log_sigmoid(g))
    bct = bc.T
    gt = g.T
    q = zb_ref[0, rows, 0:MIX] * (HD ** -0.5)
    k = zb_ref[0, rows, MIX:2 * MIX]
    v = zb_ref[0, rows, 2 * MIX:3 * MIX]
    kt = k.T
    cs, ns, ms = state
    new_c, new_n, new_m, hs = [], [], [], []
    for h in range(HEADS):
        li = direction * HEADS + h
        lf = 2 * HEADS + li
        b_col = bc[:, lf:lf + 1]
        b_row = bct[lf:lf + 1, :]
        i_col = g[:, li:li + 1]
        i_row = gt[li:li + 1, :]
        m_prev = ms[h]
        log_d = jnp.where(valid, b_col - b_row + i_row, -jnp.inf)
        m_inter = b_col + m_prev
        m_t = jnp.maximum(m_inter, jnp.max(log_d, axis=-1, keepdims=True))
        d_mat = jnp.exp(log_d - m_t)
        inter = jnp.exp(m_inter - m_t)
        sl = slice(h * HD, (h + 1) * HD)
        qh = q[:, sl]
        kh = k[:, sl]
        vh = v[:, sl].astype(BF16)
        qhb = qh.astype(BF16)
        s = lax.dot_general(qhb, kh.astype(BF16), (((1,), (1,)), ((), ())),
                            preferred_element_type=F32) * d_mat
        num = inter * _bdot(qhb, cs[h]) + _bdot(s, vh)
        den = inter * jnp.sum(qh * ns[h], axis=-1, keepdims=True) + jnp.sum(s, axis=-1, keepdims=True)
        hs.append(num / jnp.maximum(jnp.abs(den), jnp.exp(-m_t)))
        b_last = bc[last:last + 1, lf:lf + 1]
        lw_col = b_last - b_col + i_col
        lw_row = b_last - b_row + i_row
        m_new = jnp.maximum(b_last + m_prev, jnp.max(lw_row, axis=-1, keepdims=True))
        decay = jnp.exp(b_last + m_prev - m_new)
        w_col = jnp.exp(lw_col - m_new)
        w_row = jnp.exp(lw_row - m_new)
        new_c.append(decay * cs[h] + _bdot(kt[sl, :] * w_row, vh))
        new_n.append(decay * ns[h] + jnp.sum(kh * w_col, axis=0, keepdims=True))
        new_m.append(m_new)
    return hs, (tuple(new_c), tuple(new_n), tuple(new_m))


def _mlstm_kernel(zb_ref, zg_ref, gb_ref, ng_ref, c0_ref, n0_ref, m0_ref,
                  y_ref, c_ref, n_ref, m_ref, hf_ref):
    t_len = y_ref.shape[1]
    nc = t_len // CHUNK

    def init(direction):
        idx = [direction * HEADS + h for h in range(HEADS)]
        return (tuple(c0_ref[0, j] for j in idx),
                tuple(n0_ref[0, j:j + 1, :] for j in idx),
                tuple(m0_ref[0, j:j + 1, 0:1] for j in idx))

    def emit(direction, state):
        for h in range(HEADS):
            j = direction * HEADS + h
            c_ref[0, j] = state[0][h]
            n_ref[0, j:j + 1, :] = state[1][h]
            m_ref[0, j:j + 1, :] = jnp.broadcast_to(state[2][h], (1, LANE))

    def fwd_body(c, state):
        hs, state = _mlstm_chunk(zb_ref, zg_ref, gb_ref, c, 0, state)
        rows = pl.ds(pl.multiple_of(c * CHUNK, CHUNK), CHUNK)
        for h in range(HEADS):
            hf_ref[rows, h * HD:(h + 1) * HD] = hs[h]
        return state

    emit(0, lax.fori_loop(0, nc, fwd_body, init(0)))

    def bwd_body(i, state):
        c = nc - 1 - i
        hs, state = _mlstm_chunk(zb_ref, zg_ref, gb_ref, c, 1, state)
        rows = pl.ds(pl.multiple_of(c * CHUNK, CHUNK), CHUNK)
        o = zb_ref[0, rows, 3 * MIX:4 * MIX]
        for h in range(HEADS):
            sl = slice(h * HD, (h + 1) * HD)
            tot = hf_ref[rows, sl] + hs[h]
            y_ref[0, rows, sl] = _rms(tot) * ng_ref[:, sl] * _sigmoid(o[:, sl])
        return state

    emit(1, lax.fori_loop(0, nc, bwd_body, init(1)))


def _mlstm_mixer(zb, zg, gate_b, norm_g, c0, n0, m0):
    bsz, t_len, _ = zb.shape
    nd = 2 * HEADS
    return pl.pallas_call(
        _mlstm_kernel,
        out_shape=[jax.ShapeDtypeStruct((bsz, t_len, MIX), F32),
                   jax.ShapeDtypeStruct((bsz, nd, HD, HD), F32),
                   jax.ShapeDtypeStruct((bsz, nd, HD), F32),
                   jax.ShapeDtypeStruct((bsz, nd, LANE), F32)],
        grid=(bsz,),
        in_specs=[pl.BlockSpec((1, t_len, 4 * MIX), lambda i: (i, 0, 0)),
                  pl.BlockSpec((1, t_len, LANE), lambda i: (i, 0, 0)),
                  pl.BlockSpec((1, LANE), lambda i: (0, 0)),
                  pl.BlockSpec((1, MIX), lambda i: (0, 0)),
                  pl.BlockSpec((1, nd, HD, HD), lambda i: (i, 0, 0, 0)),
                  pl.BlockSpec((1, nd, HD), lambda i: (i, 0, 0)),
                  pl.BlockSpec((1, nd, LANE), lambda i: (i, 0, 0))],
        out_specs=[pl.BlockSpec((1, t_len, MIX), lambda i: (i, 0, 0)),
                   pl.BlockSpec((1, nd, HD, HD), lambda i: (i, 0, 0, 0)),
                   pl.BlockSpec((1, nd, HD), lambda i: (i, 0, 0)),
                   pl.BlockSpec((1, nd, LANE), lambda i: (i, 0, 0))],
        scratch_shapes=[pltpu.VMEM((t_len, MIX), F32)],
        compiler_params=_cp("parallel"),
        name="mlstm_mixer",
    )(zb, zg, gate_b, norm_g, c0, n0, m0)


_ATT_TQ = 256


def _mla_kernel(*refs, past):
    if past:
        (zc_ref, cos_ref, sin_ref, gq_ref, gkv_ref, wqa_ref, wqb_ref, wk_ref, wv_ref,
         pckv_ref, pkr_ref, y_ref, ckv_ref, q_sc, k_sc, v_sc) = refs
    else:
        (zc_ref, cos_ref, sin_ref, gq_ref, gkv_ref, wqa_ref, wqb_ref, wk_ref, wv_ref,
         y_ref, ckv_ref, q_sc, k_sc, v_sc) = refs
    t_len = y_ref.shape[1]
    cosf = cos_ref[...]
    sinf = sin_ref[...]
    cq = _rms(zc_ref[0, :, 0:Q_RANK]) * gq_ref[...]
    ckv = _rms(zc_ref[0, :, Q_RANK:Q_RANK + KV_RANK]) * gkv_ref[...]
    ckv_ref[0] = ckv
    kr = (zc_ref[0, :, Q_RANK + KV_RANK:Q_RANK + KV_RANK + LANE] * cosf
          + zc_ref[0, :, Q_RANK + KV_RANK + LANE:Q_RANK + KV_RANK + 2 * LANE] * sinf)
    cqb = cq.astype(BF16)
    ckvb = ckv.astype(BF16)
    for h in range(HEADS):
        sl = slice(h * LANE, (h + 1) * LANE)
        qa = jnp.dot(cqb, wqa_ref[:, sl], preferred_element_type=F32)
        qb = jnp.dot(cqb, wqb_ref[:, sl], preferred_element_type=F32)
        q_sc[:, sl] = (qa * cosf + qb * sinf).astype(BF16)
        kn = jnp.dot(ckvb, wk_ref[:, sl], preferred_element_type=F32)
        k_sc[past:past + t_len, sl] = (kn + kr).astype(BF16)
        v_sc[past:past + t_len, sl] = jnp.dot(ckvb, wv_ref[:, sl], preferred_element_type=F32).astype(BF16)
    if past:
        pckv = pckv_ref[0].astype(BF16)
        pkr = pkr_ref[0]
        for h in range(HEADS):
            sl = slice(h * LANE, (h + 1) * LANE)
            k_sc[0:past, sl] = (jnp.dot(pckv, wk_ref[:, sl], preferred_element_type=F32) + pkr).astype(BF16)
            v_sc[0:past, sl] = jnp.dot(pckv, wv_ref[:, sl], preferred_element_type=F32).astype(BF16)

    def q_block(i, carry):
        rows = pl.ds(pl.multiple_of(i * _ATT_TQ, _ATT_TQ), _ATT_TQ)
        outs = []
        for h in range(HEADS):
            sl = slice(h * LANE, (h + 1) * LANE)
            s = lax.dot_general(q_sc[rows, sl], k_sc[:, sl], (((1,), (1,)), ((), ())),
                                preferred_element_type=F32)
            p = jnp.exp(s - jnp.max(s, axis=-1, keepdims=True))
            o = jnp.dot(p.astype(BF16), v_sc[:, sl], preferred_element_type=F32)
            outs.append(o / jnp.sum(p, axis=-1, keepdims=True))
        y_ref[0, rows, 0:LANE] = outs[0] + outs[1]
        y_ref[0, rows, LANE:2 * LANE] = outs[2] + outs[3]
        return carry

    lax.fori_loop(0, t_len // _ATT_TQ, q_block, 0)


def _mla_mixer(zc, cosf, sinf, gq, gkv, wqa, wqb, wk, wv, past_ckv=None, past_kr=None):
    bsz, t_len, zw = zc.shape
    past = 0 if past_ckv is None else past_ckv.shape[1]
    full = lambda a: pl.BlockSpec(a.shape, lambda i: (0,) * a.ndim)
    args = [zc, cosf, sinf, gq, gkv, wqa, wqb, wk, wv]
    in_specs = [pl.BlockSpec((1, t_len, zw), lambda i: (i, 0, 0))] + [full(a) for a in args[1:]]
    if past:
        args += [past_ckv, past_kr]
        in_specs += [pl.BlockSpec((1, past, KV_RANK), lambda i: (i, 0, 0)),
                     pl.BlockSpec((1, past, LANE), lambda i: (i, 0, 0))]
    wide = HEADS * LANE
    return pl.pallas_call(
        functools.partial(_mla_kernel, past=past),
        out_shape=[jax.ShapeDtypeStruct((bsz, t_len, MIX), F32),
                   jax.ShapeDtypeStruct((bsz, t_len, KV_RANK), F32)],
        grid=(bsz,),
        in_specs=in_specs,
        out_specs=[pl.BlockSpec((1, t_len, MIX), lambda i: (i, 0, 0)),
                   pl.BlockSpec((1, t_len, KV_RANK), lambda i: (i, 0, 0))],
        scratch_shapes=[pltpu.VMEM((t_len, wide), BF16),
                        pltpu.VMEM((past + t_len, wide), BF16),
                        pltpu.VMEM((past + t_len, wide), BF16)],
        compiler_params=_cp("parallel"),
        name="mla_mixer",
    )(*args)


_S5_SEQ = 8
_S5_TT = 128


def _s5_kernel(zd_ref, bd_ref, cd_ref, are_ref, aim_ref, s0re_ref, s0im_ref,
               y_ref, fre_ref, fim_ref, bre_sc, bim_sc, sre_sc, sim_sc):
    direction = pl.program_id(0)
    k = pl.program_id(2)
    tt = _S5_TT
    rows = _S5_SEQ * tt
    nblk = S5_STATES // LANE

    @pl.when(k == 0)
    def _():
        sre_sc[...] = s0re_ref[0]
        sim_sc[...] = s0im_ref[0]

    u = zd_ref[...].reshape(rows, MIX).astype(BF16)
    for sc, base in ((bre_sc, 0), (bim_sc, S5_STATES)):
        for j in range(0, nblk, 2):
            v = jnp.dot(u, bd_ref[0, :, base + j * LANE:base + (j + 2) * LANE], preferred_element_type=F32)
            sc[j] = v[:, 0:LANE]
            sc[j + 1] = v[:, LANE:2 * LANE]
    lanes = lambda j: slice(j * LANE, (j + 1) * LANE)
    a_re = [jnp.broadcast_to(are_ref[0][:, lanes(j)], (_S5_SEQ, LANE)) for j in range(nblk)]
    a_im = [jnp.broadcast_to(aim_ref[0][:, lanes(j)], (_S5_SEQ, LANE)) for j in range(nblk)]

    def step(i, carry):
        s_re, s_im = carry
        t = i + direction * (tt - 1 - 2 * i)
        sel = pl.ds(t, _S5_SEQ, stride=tt)
        n_re, n_im = [], []
        for j in range(nblk):
            r = a_re[j] * s_re[j] - a_im[j] * s_im[j] + bre_sc[j, sel, :]
            m = a_re[j] * s_im[j] + a_im[j] * s_re[j] + bim_sc[j, sel, :]
            bre_sc[j, sel, :] = r
            bim_sc[j, sel, :] = m
            n_re.append(r)
            n_im.append(m)
        return tuple(n_re), tuple(n_im)

    init = (tuple(sre_sc[:, lanes(j)] for j in range(nblk)), tuple(sim_sc[:, lanes(j)] for j in range(nblk)))
    s_re, s_im = lax.fori_loop(0, tt, step, init)
    for j in range(nblk):
        sre_sc[:, lanes(j)] = s_re[j]
        sim_sc[:, lanes(j)] = s_im[j]
    y = jnp.zeros((rows, MIX), F32)
    for sc, base in ((bre_sc, 0), (bim_sc, S5_STATES)):
        for j in range(0, nblk, 2):
            s2 = jnp.concatenate([sc[j], sc[j + 1]], axis=-1).astype(BF16)
            y = y + jnp.dot(s2, cd_ref[0, base + j * LANE:base + (j + 2) * LANE, :], preferred_element_type=F32)
    y_ref[0] = y.reshape(_S5_SEQ, tt, MIX)

    @pl.when(k == pl.num_programs(2) - 1)
    def _():
        fre_ref[0] = sre_sc[...]
        fim_ref[0] = sim_sc[...]


def _s5_scan(zd, bd, cd, ab_re, ab_im, s0_re, s0_im):
    bsz, t_len, _ = zd.shape
    nt = t_len // _S5_TT
    tile = lambda d, k: k + d * (nt - 1 - 2 * k)
    st = pl.BlockSpec((1, _S5_SEQ, S5_STATES), lambda d, g, k: (d, g, 0))
    return pl.pallas_call(
        _s5_kernel,
        out_shape=[jax.ShapeDtypeStruct((2, bsz, t_len, MIX), F32),
                   jax.ShapeDtypeStruct((2, bsz, S5_STATES), F32),
                   jax.ShapeDtypeStruct((2, bsz, S5_STATES), F32)],
        grid=(2, bsz // _S5_SEQ, nt),
        in_specs=[pl.BlockSpec((_S5_SEQ, _S5_TT, MIX), lambda d, g, k: (g, tile(d, k), 0)),
                  pl.BlockSpec((1, MIX, 2 * S5_STATES), lambda d, g, k: (d, 0, 0)),
                  pl.BlockSpec((1, 2 * S5_STATES, MIX), lambda d, g, k: (d, 0, 0)),
                  pl.BlockSpec((1, 1, S5_STATES), lambda d, g, k: (d, 0, 0)),
                  pl.BlockSpec((1, 1, S5_STATES), lambda d, g, k: (d, 0, 0)),
                  st, st],
        out_specs=[pl.BlockSpec((1, _S5_SEQ, _S5_TT, MIX), lambda d, g, k: (d, g, tile(d, k), 0)),
                   st, st],
        scratch_shapes=[pltpu.VMEM((S5_STATES // LANE, _S5_SEQ * _S5_TT, LANE), F32),
                        pltpu.VMEM((S5_STATES // LANE, _S5_SEQ * _S5_TT, LANE), F32),
                        pltpu.VMEM((_S5_SEQ, S5_STATES), F32),
                        pltpu.VMEM((_S5_SEQ, S5_STATES), F32)],
        compiler_params=_cp("parallel", "parallel", "arbitrary"),
        name="s5_scan",
    )(zd, bd, cd, ab_re, ab_im, s0_re, s0_im)


def _gelu_tanh(x):
    return 0.5 * x * (1.0 + jnp.tanh(math.sqrt(2.0 / math.pi) * (x + 0.044715 * x * x * x)))


def _route(logits):
    lane = lax.broadcasted_iota(jnp.int32, logits.shape, 1).astype(F32)
    big = float(LANE)
    neg = -jnp.inf
    g_mask = (lane >= N_EXPERT) & (lane < N_EXPERT + N_GROUP)
    gl = jnp.where(g_mask, logits, neg)
    g_max = jnp.max(gl, axis=-1, keepdims=True)
    g_idx = jnp.min(jnp.where(gl == g_max, lane, big), axis=-1, keepdims=True) - N_EXPERT
    g_sel = 1.0 / jnp.sum(jnp.where(g_mask, jnp.exp(logits - g_max), 0.0), axis=-1, keepdims=True)
    lo = g_idx * PER_GROUP
    el = jnp.where((lane >= lo) & (lane < lo + PER_GROUP), logits, neg)
    v1 = jnp.max(el, axis=-1, keepdims=True)
    i1 = jnp.min(jnp.where(el == v1, lane, big), axis=-1, keepdims=True)
    el2 = jnp.where(lane == i1, neg, el)
    v2 = jnp.max(el2, axis=-1, keepdims=True)
    i2 = jnp.min(jnp.where(el2 == v2, lane, big), axis=-1, keepdims=True)
    e2 = jnp.exp(v2 - v1)
    w1 = g_sel / (1.0 + e2)
    return jnp.where(lane == i1, w1, jnp.where(lane == i2, w1 * e2, 0.0))


def _post_kernel(x_ref, ya_ref, yb_ref, yc_ref, ys_ref, zd_ref, mod_ref, wo_ref, d_ref, wglu_ref, bglu_ref,
                 g2_ref, wr_ref, br_ref, x1_ref, h2_ref, gate_ref):
    m = mod_ref[0]
    zd = zd_ref[...]
    ys = ys_ref[0] + ys_ref[1] + d_ref[...] * zd
    gl = _bdot(_gelu_tanh(ys), wglu_ref[...]) + bglu_ref[...]
    yd = gl[:, 0:MIX] * _sigmoid(gl[:, MIX:2 * MIX])
    mix = (_bdot(ya_ref[...], wo_ref[0:MIX, :]) + _bdot(yb_ref[...], wo_ref[MIX:2 * MIX, :])
           + _bdot(yc_ref[...], wo_ref[2 * MIX:3 * MIX, :]) + _bdot(yd, wo_ref[3 * MIX:4 * MIX, :]))
    x1 = x_ref[...] + m[2:3] * mix
    x1_ref[...] = x1
    h2 = _rms(x1) * g2_ref[...] * (1.0 + m[4:5]) + m[3:4]
    h2b = h2.astype(BF16)
    h2_ref[...] = h2b
    gate_ref[...] = _route(jnp.dot(h2b, wr_ref[...], preferred_element_type=F32) + br_ref[...])


def _post_mixer(x, ya, yb, yc, ys, zd, mod, row_fn, w_out, s5_d, w_glu, b_glu, norm2_g, w_router, b_router, tm):
    n = x.shape[0]
    tok = lambda w: pl.BlockSpec((tm, w), lambda i: (i, 0))
    full = lambda a: pl.BlockSpec(a.shape, lambda i: (0,) * a.ndim)
    return pl.pallas_call(
        _post_kernel,
        out_shape=[jax.ShapeDtypeStruct((n, D_MODEL), F32),
                   jax.ShapeDtypeStruct((n, D_MODEL), BF16),
                   jax.ShapeDtypeStruct((n, LANE), F32)],
        grid=(n // tm,),
        in_specs=[tok(D_MODEL), tok(MIX), tok(MIX), tok(MIX),
                  pl.BlockSpec((2, tm, MIX), lambda i: (0, i, 0)), tok(MIX),
                  pl.BlockSpec((1, 6, D_MODEL), lambda i: (row_fn(i), 0, 0)),
                  full(w_out), full(s5_d), full(w_glu), full(b_glu), full(norm2_g),
                  full(w_router), full(b_router)],
        out_specs=[tok(D_MODEL), tok(D_MODEL), tok(LANE)],
        compiler_params=_cp("parallel"),
        name="post_mixer",
    )(x, ya, yb, yc, ys, zd, mod, w_out, s5_d, w_glu, b_glu, norm2_g, w_router, b_router)


_MOE_EPS = 4


def _moe_kernel(h_ref, gate_ref, x_ref, mod_ref, w1_ref, w3_ref, w2_ref, o_ref, acc_ref):
    j = pl.program_id(1)

    @pl.when(j == 0)
    def _():
        acc_ref[...] = jnp.zeros_like(acc_ref)

    h = h_ref[...]
    gate = gate_ref[...]
    lane = lax.broadcasted_iota(jnp.int32, gate.shape, 1)
    for e in range(_MOE_EPS):
        ge = jnp.sum(jnp.where(lane == j * _MOE_EPS + e, gate, 0.0), axis=-1, keepdims=True)
        h1 = jnp.dot(h, w1_ref[e], preferred_element_type=F32)
        h3 = jnp.dot(h, w3_ref[e], preferred_element_type=F32)
        hid = h1 * _sigmoid(h1) * h3 * ge
        acc_ref[...] += jnp.dot(hid.astype(BF16), w2_ref[e], preferred_element_type=F32)

    @pl.when(j == pl.num_programs(1) - 1)
    def _():
        o_ref[...] = x_ref[...] + mod_ref[0][5:6] * acc_ref[...]


def _moe(h2, gate, x1, mod, row_fn, w1, w3, w2, tm):
    n = h2.shape[0]
    return pl.pallas_call(
        _moe_kernel,
        out_shape=jax.ShapeDtypeStruct((n, D_MODEL), F32),
        grid=(n // tm, N_EXPERT // _MOE_EPS),
        in_specs=[pl.BlockSpec((tm, D_MODEL), lambda i, j: (i, 0)),
                  pl.BlockSpec((tm, LANE), lambda i, j: (i, 0)),
                  pl.BlockSpec((tm, D_MODEL), lambda i, j: (i, 0)),
                  pl.BlockSpec((1, 6, D_MODEL), lambda i, j: (row_fn(i), 0, 0)),
                  pl.BlockSpec((_MOE_EPS, D_MODEL, MOE_FF), lambda i, j: (j, 0, 0)),
                  pl.BlockSpec((_MOE_EPS, D_MODEL, MOE_FF), lambda i, j: (j, 0, 0)),
                  pl.BlockSpec((_MOE_EPS, MOE_FF, D_MODEL), lambda i, j: (j, 0, 0))],
        out_specs=pl.BlockSpec((tm, D_MODEL), lambda i, j: (i, 0)),
        scratch_shapes=[pltpu.VMEM((tm, D_MODEL), F32)],
        compiler_params=_cp("parallel", "arbitrary"),
        name="moe",
    )(h2, gate, x1, mod, w1, w3, w2)


def _final_kernel(x_ref, g_ref, o_ref):
    o_ref[...] = _rms(x_ref[...]) * g_ref[...]


def _final_norm(x, g, tm):
    n = x.shape[0]
    return pl.pallas_call(
        _final_kernel,
        out_shape=jax.ShapeDtypeStruct((n, D_MODEL), F32),
        grid=(n // tm,),
        in_specs=[pl.BlockSpec((tm, D_MODEL), lambda i: (i, 0)),
                  pl.BlockSpec((1, D_MODEL), lambda i: (0, 0))],
        out_specs=pl.BlockSpec((tm, D_MODEL), lambda i: (i, 0)),
        compiler_params=_cp("parallel"),
        name="final_norm",
    )(x, g)


def _rot_cols(w):
    q = ROPE // 4
    return jnp.concatenate([-w[..., q:2 * q], w[..., 0:q], -w[..., 3 * q:4 * q], w[..., 2 * q:3 * q]], axis=-1)


def _rope_slot(w):
    pad = [(0, 0)] * (w.ndim - 1) + [(HD, LANE - HD - ROPE)]
    return jnp.pad(w, pad)


def _prep_w_in(w_in, gate_cols):
    a = w_in[..., 0:512]
    qkvo = w_in[..., 512:1536]
    g = jnp.pad(w_in[..., 1536:1536 + gate_cols], [(0, 0), (0, 0), (0, LANE - gate_cols)])
    c0 = 1536 + gate_cols
    cq = w_in[..., c0:c0 + Q_RANK]
    ckv = w_in[..., c0 + Q_RANK:c0 + Q_RANK + KV_RANK]
    kr = w_in[..., c0 + Q_RANK + KV_RANK:c0 + Q_RANK + KV_RANK + ROPE]
    d = w_in[..., c0 + Q_RANK + KV_RANK + ROPE:]
    return jnp.concatenate([a, qkvo, g, cq, ckv, _rope_slot(kr), _rope_slot(_rot_cols(kr)), d],
                           axis=-1).astype(BF16)


def _prep_mla(w_uq, w_ukv):
    n_layer = w_uq.shape[0]
    scale = (HD + ROPE) ** -0.5
    wq = w_uq.reshape(n_layer, Q_RANK, HEADS, HD + ROPE) * scale
    nope, rope = wq[..., :HD], wq[..., HD:]
    zeros_r = jnp.zeros_like(rope)
    wqa = jnp.concatenate([nope, rope, zeros_r], axis=-1).reshape(n_layer, Q_RANK, HEADS * LANE)
    wqb = jnp.concatenate([jnp.zeros_like(nope), _rot_cols(rope), zeros_r], axis=-1)
    wqb = wqb.reshape(n_layer, Q_RANK, HEADS * LANE)
    wkv = w_ukv.reshape(n_layer, KV_RANK, HEADS, 2 * HD)
    k_nope, val = wkv[..., :HD], wkv[..., HD:]
    zeros_h = jnp.zeros_like(k_nope)
    wk = jnp.concatenate([k_nope, zeros_h], axis=-1).reshape(n_layer, KV_RANK, HEADS * LANE)
    even = jnp.concatenate([val, zeros_h], axis=-1)
    odd = jnp.concatenate([zeros_h, val], axis=-1)
    is_odd = (jnp.arange(HEADS) % 2 == 1)[None, None, :, None]
    wv = jnp.where(is_odd, odd, even).reshape(n_layer, KV_RANK, HEADS * LANE)
    return wqa.astype(BF16), wqb.astype(BF16), wk.astype(BF16), wv.astype(BF16)


def _rope_tables(t_len, rotate):
    ones = jnp.ones((t_len, HD), F32)
    zeros = jnp.zeros((t_len, HD), F32)
    tail = jnp.zeros((t_len, LANE - HD - ROPE), F32)
    if not rotate:
        return (jnp.concatenate([ones, jnp.ones((t_len, ROPE), F32), tail], axis=-1),
                jnp.zeros((t_len, LANE), F32))
    rows = t_len // GRID_W
    row = jnp.repeat(jnp.arange(rows, dtype=F32), GRID_W)
    col = jnp.tile(jnp.arange(GRID_W, dtype=F32), rows)
    nf = ROPE // 4
    inv = ROPE_BASE ** (-jnp.arange(nf, dtype=F32) / nf)
    ar = row[:, None] * inv
    ac = col[:, None] * inv
    cos = jnp.concatenate([jnp.cos(ar), jnp.cos(ar), jnp.cos(ac), jnp.cos(ac)], axis=-1)
    sin = jnp.concatenate([jnp.sin(ar), jnp.sin(ar), jnp.sin(ac), jnp.sin(ac)], axis=-1)
    return (jnp.concatenate([ones, cos, tail], axis=-1), jnp.concatenate([zeros, sin, tail], axis=-1))


def _prep_s5(bb_re, bb_im, c_re, c_im):
    eye = jnp.eye(S5_G, dtype=F32)
    to_b = lambda bb: jnp.einsum("ldgnc,gh->ldgchn", bb, eye).reshape(bb.shape[0], 2, MIX, S5_STATES)
    to_c = lambda cc: jnp.einsum("ldgcn,gh->ldgnhc", cc, eye).reshape(cc.shape[0], 2, S5_STATES, MIX)
    bd = jnp.concatenate([to_b(bb_re), to_b(bb_im)], axis=-1)
    cd = jnp.concatenate([to_c(c_re.astype(F32)), -to_c(c_im.astype(F32))], axis=-2)
    return bd.astype(BF16), cd.astype(BF16)


def _layer(x, bsz, t_len, mod, row_fn, p, ctx, tm):
    za, zb, zg, zc, zd = _pre_mixer(x, mod, row_fn, p["norm1_g"], p["w_in"], tm)
    seq = lambda a: a.reshape(bsz, t_len, a.shape[-1])
    ya = _conv_module(seq(za), p["conv_w"], p["conv_b"], p["conv_ln_g"], p["conv_ln_b"])
    yb, c_fin, n_fin, m_fin = _mlstm_mixer(seq(zb), seq(zg), p["gate_b"], p["mlstm_norm_g"],
                                           ctx["mlstm_c"], ctx["mlstm_n"], ctx["mlstm_m"])
    yc, ckv = _mla_mixer(seq(zc), ctx["cos"], ctx["sin"], p["mla_q_norm_g"], p["mla_kv_norm_g"],
                         p["wqa"], p["wqb"], p["wk"], p["wv"], ctx.get("past_ckv"), ctx.get("past_kr"))
    ys, s_re, s_im = _s5_scan(seq(zd), p["s5_bd"], p["s5_cd"], p["s5_ab_re"], p["s5_ab_im"],
                              ctx["s5_re"], ctx["s5_im"])
    flat = lambda a: a.reshape(bsz * t_len, a.shape[-1])
    x1, h2, gate = _post_mixer(x, flat(ya), flat(yb), flat(yc), ys.reshape(2, bsz * t_len, MIX), zd, mod, row_fn,
                               p["w_out"], p["s5_d"], p["s5_w_glu"], p["s5_b_glu"], p["norm2_g"],
                               p["w_router"], p["b_router"], tm)
    x2 = _moe(h2, gate, x1, mod, row_fn, p["moe_w1"], p["moe_w3"], p["moe_w2"], tm)
    krope = seq(zc)[:, :, Q_RANK + KV_RANK + HD:Q_RANK + KV_RANK + HD + ROPE]
    return x2, (ckv, krope, c_fin, n_fin, m_fin, s_re, s_im)


def kernel(x_prompt, x_sample, cache_mla_ckv, cache_mla_krope, state_mlstm_C, state_mlstm_n, state_mlstm_m, state_s5, c, c_ctx, norm1_g, norm2_g, final_g, w_mod, b_mod, w_in, w_out, conv_w, conv_b, conv_ln_g, conv_ln_b, mlstm_gate_b, mlstm_norm_g, mla_q_norm_g, mla_w_uq, mla_kv_norm_g, mla_w_ukv, s5_a_re, s5_a_im, s5_log_dt, s5_b_re, s5_b_im, s5_c_re, s5_c_im, s5_d, s5_w_glu, s5_b_glu, moe_w_group, moe_b_group, moe_w_expert, moe_b_expert, moe_w1, moe_w3, moe_w2):
    n_layer = w_in.shape[0]
    b_ctx, t_ctx, d = x_prompt.shape
    b_lat, t_lat, _ = x_sample.shape
    nd = 2 * HEADS
    tm = 512
    assert b_lat + 1 <= 16 and t_lat % tm == 0 and (b_ctx * t_ctx) % tm == 0

    c_all = jnp.zeros((16, d), F32).at[0].set(c_ctx).at[1:1 + b_lat].set(c)
    mod_all = _modulation(c_all, w_mod, b_mod).reshape(n_layer, 16, 6, d)
    gate_cols = 4 * HEADS
    w_in_ext = _prep_w_in(w_in, gate_cols)
    wqa, wqb, wk, wv = _prep_mla(mla_w_uq, mla_w_ukv)
    ab_re, ab_im, bb_re, bb_im = _s5_discretise(s5_a_re, s5_a_im, s5_log_dt, s5_b_re, s5_b_im)
    s5_bd, s5_cd = _prep_s5(bb_re, bb_im, s5_c_re, s5_c_im)
    w_router = jnp.pad(jnp.concatenate([moe_w_expert, moe_w_group], axis=-1),
                       [(0, 0), (0, 0), (0, LANE - N_EXPERT - N_GROUP)]).astype(BF16)
    b_router = jnp.pad(jnp.concatenate([moe_b_expert, moe_b_group], axis=-1),
                       [(0, 0), (0, LANE - N_EXPERT - N_GROUP)])
    gate_b = jnp.pad(mlstm_gate_b.reshape(n_layer, gate_cols), [(0, 0), (0, LANE - gate_cols)])
    conv_w_p = jnp.pad(conv_w, [(0, 0), (0, 32 - CONV_WIDTH), (0, 0)])
    w_out_b = w_out.astype(BF16)
    w_glu_b = s5_w_glu.astype(BF16)
    w1_b, w3_b, w2_b = moe_w1.astype(BF16), moe_w3.astype(BF16), moe_w2.astype(BF16)
    row = lambda a, l: a[l][None, :]

    cos_ctx, sin_ctx = _rope_tables(t_ctx, rotate=False)
    cos_lat, sin_lat = _rope_tables(t_lat, rotate=True)
    zero_state = dict(
        mlstm_c=jnp.zeros((b_ctx, nd, HD, HD), F32), mlstm_n=jnp.zeros((b_ctx, nd, HD), F32),
        mlstm_m=jnp.zeros((b_ctx, nd, LANE), F32),
        s5_re=jnp.zeros((2, b_ctx, S5_STATES), F32), s5_im=jnp.zeros((2, b_ctx, S5_STATES), F32),
        cos=cos_ctx, sin=sin_ctx)

    rows_per_seq = t_lat // tm
    row_ctx = lambda i: 0
    row_lat = lambda i: 1 + i // rows_per_seq

    x_ctx = x_prompt.reshape(b_ctx * t_ctx, d)
    x_lat = x_sample.reshape(b_lat * t_lat, d)
    outs = []
    for l in range(n_layer):
        p = dict(norm1_g=row(norm1_g, l), norm2_g=row(norm2_g, l), w_in=w_in_ext[l], w_out=w_out_b[l],
                 conv_w=conv_w_p[l], conv_b=row(conv_b, l), conv_ln_g=row(conv_ln_g, l),
                 conv_ln_b=row(conv_ln_b, l), gate_b=row(gate_b, l), mlstm_norm_g=row(mlstm_norm_g, l),
                 mla_q_norm_g=row(mla_q_norm_g, l), mla_kv_norm_g=row(mla_kv_norm_g, l),
                 wqa=wqa[l], wqb=wqb[l], wk=wk[l], wv=wv[l],
                 s5_bd=s5_bd[l], s5_cd=s5_cd[l], s5_ab_re=ab_re[l][:, None, :], s5_ab_im=ab_im[l][:, None, :],
                 s5_d=row(s5_d, l), s5_w_glu=w_glu_b[l], s5_b_glu=row(s5_b_glu, l),
                 w_router=w_router[l], b_router=row(b_router, l),
                 moe_w1=w1_b[l], moe_w3=w3_b[l], moe_w2=w2_b[l])
        x_ctx, st = _layer(x_ctx, b_ctx, t_ctx, mod_all[l], row_ctx, p, zero_state, tm)
        outs.append(st)
        s5_l = state_s5[:, l].reshape(b_lat, 2, S5_STATES, 2)
        lat_state = dict(
            mlstm_c=state_mlstm_C[:, l].reshape(b_lat, nd, HD, HD),
            mlstm_n=state_mlstm_n[:, l].reshape(b_lat, nd, HD),
            mlstm_m=jnp.broadcast_to(state_mlstm_m[:, l].reshape(b_lat, nd, 1), (b_lat, nd, LANE)),
            s5_re=s5_l[..., 0].transpose(1, 0, 2), s5_im=s5_l[..., 1].transpose(1, 0, 2),
            cos=cos_lat, sin=sin_lat,
            past_ckv=cache_mla_ckv[:, l], past_kr=_rope_slot(cache_mla_krope[:, l]))
        x_lat, _ = _layer(x_lat, b_lat, t_lat, mod_all[l], row_lat, p, lat_state, tm)

    y_prompt = _final_norm(x_ctx, final_g[None, :], tm).reshape(b_ctx, t_ctx, d)
    y_sample = _final_norm(x_lat, final_g[None, :], tm).reshape(b_lat, t_lat, d)
    stack = lambda i: jnp.stack([o[i] for o in outs], axis=1)
    new_ckv = stack(0)
    new_krope = stack(1)
    new_c = stack(2).reshape(b_ctx, n_layer, 2, HEADS, HD, HD)
    new_n = stack(3).reshape(b_ctx, n_layer, 2, HEADS, HD)
    new_m = stack(4)[..., 0].reshape(b_ctx, n_layer, 2, HEADS)
    s_re = jnp.stack([o[5] for o in outs], axis=0)
    s_im = jnp.stack([o[6] for o in outs], axis=0)
    new_s5 = jnp.stack([s_re, s_im], axis=-1).transpose(2, 0, 1, 3, 4)
    new_s5 = new_s5.reshape(b_ctx, n_layer, 2, S5_G, S5_N, 2)
    return (y_prompt, y_sample, new_ckv, new_krope, new_c, new_n, new_m, new_s5)
```

```python
import functools
import math

import jax
import jax.numpy as jnp
from jax import lax
from jax.experimental import pallas as pl
from jax.experimental.pallas import tpu as pltpu

F32 = jnp.float32
BF16 = jnp.bfloat16
EPS = 1e-6

D_MODEL = 1024
MIX = 256
CONV_WIDTH = 31
HEADS = 4
HD = 64
CHUNK = 128
ROPE = 32
KV_RANK = 128
Q_RANK = 256
GRID_W = 64
ROPE_BASE = 10000.0
S5_G = 16
S5_GC = 16
S5_N = 64
S5_STATES = S5_G * S5_N
N_EXPERT = 32
PER_GROUP = 8
N_GROUP = 4
MOE_FF = 256
LANE = 128
Z_COLS = 2560
VMEM_LIMIT = 56 * 1024 * 1024


def _cp(*sem):
    return pltpu.CompilerParams(dimension_semantics=sem, vmem_limit_bytes=VMEM_LIMIT)


def _rms(x):
    return x * lax.rsqrt(jnp.mean(x * x, axis=-1, keepdims=True) + EPS)


def _sigmoid(x):
    return 1.0 / (1.0 + jnp.exp(-x))


def _bdot(a, b):
    return jnp.dot(a.astype(BF16), b.astype(BF16), preferred_element_type=F32)


def _mod_kernel(c_ref, w_ref, b_ref, o_ref):
    c = c_ref[...]
    o_ref[0] = _bdot(c * _sigmoid(c), w_ref[0]) + b_ref[0]


def _modulation(c_all, w_mod, b_mod):
    n_layer, d, n = w_mod.shape
    tn = 1536
    return pl.pallas_call(
        _mod_kernel,
        out_shape=jax.ShapeDtypeStruct((n_layer, 16, n), F32),
        grid=(n_layer, n // tn),
        in_specs=[pl.BlockSpec((16, d), lambda l, j: (0, 0)),
                  pl.BlockSpec((1, d, tn), lambda l, j: (l, 0, j)),
                  pl.BlockSpec((1, 1, tn), lambda l, j: (l, 0, j))],
        out_specs=pl.BlockSpec((1, 16, tn), lambda l, j: (l, 0, j)),
        compiler_params=_cp("parallel", "parallel"),
        name="modulation",
    )(c_all, w_mod, b_mod.reshape(n_layer, 1, n))


def _s5_disc_kernel(are_ref, aim_ref, ldt_ref, bre_ref, bim_ref, abre_ref, abim_ref, bbre_ref, bbim_ref):
    a_re = are_ref[...]
    a_im = aim_ref[...]
    dt = jnp.exp(ldt_ref[...])
    mag = jnp.exp(a_re * dt)
    ab_re = mag * jnp.cos(a_im * dt)
    ab_im = mag * jnp.sin(a_im * dt)
    den = a_re * a_re + a_im * a_im
    f_re = ((ab_re - 1.0) * a_re + ab_im * a_im) / den
    f_im = (ab_im * a_re - (ab_re - 1.0) * a_im) / den
    b_re = bre_ref[...]
    b_im = bim_ref[...]
    abre_ref[...] = ab_re
    abim_ref[...] = ab_im
    bbre_ref[...] = f_re * b_re - f_im * b_im
    bbim_ref[...] = f_re * b_im + f_im * b_re


def _s5_discretise(a_re, a_im, log_dt, b_re, b_im):
    n_layer = a_re.shape[0]
    rows = n_layer * 2 * S5_G
    cols = S5_N * S5_GC
    rep = lambda a: jnp.repeat(a.reshape(rows, S5_N), S5_GC, axis=1)
    ldt = jnp.broadcast_to(log_dt.reshape(rows, 1), (rows, cols))
    spec = pl.BlockSpec((rows, cols), lambda: (0, 0))
    ab_re, ab_im, bb_re, bb_im = pl.pallas_call(
        _s5_disc_kernel,
        out_shape=[jax.ShapeDtypeStruct((rows, cols), F32)] * 4,
        in_specs=[spec] * 5,
        out_specs=[spec] * 4,
        name="s5_discretise",
    )(rep(a_re), rep(a_im), ldt, b_re.reshape(rows, cols), b_im.reshape(rows, cols))
    pick = lambda a: a[:, ::S5_GC].reshape(n_layer, 2, S5_STATES)
    shp = (n_layer, 2, S5_G, S5_N, S5_GC)
    return pick(ab_re), pick(ab_im), bb_re.reshape(shp), bb_im.reshape(shp)


def _pre_kernel(x_ref, mod_ref, g_ref, w_ref, za_ref, zb_ref, zg_ref, zc_ref, zd_ref):
    m = mod_ref[0]
    h = _rms(x_ref[...]) * g_ref[...] * (1.0 + m[1:2]) + m[0:1]
    hb = h.astype(BF16)
    col = 0
    for o_ref in (za_ref, zb_ref, zg_ref, zc_ref, zd_ref):
        n = o_ref.shape[-1]
        o_ref[...] = jnp.dot(hb, w_ref[:, col:col + n], preferred_element_type=F32)
        col += n


def _pre_mixer(x, mod, row_fn, norm_g, w_in_ext, tm):
    n = x.shape[0]
    widths = (512, 1024, LANE, 640, MIX)
    return pl.pallas_call(
        _pre_kernel,
        out_shape=[jax.ShapeDtypeStruct((n, w), F32) for w in widths],
        grid=(n // tm,),
        in_specs=[pl.BlockSpec((tm, D_MODEL), lambda i: (i, 0)),
                  pl.BlockSpec((1, 6, D_MODEL), lambda i: (row_fn(i, tm), 0, 0)),
                  pl.BlockSpec((1, D_MODEL), lambda i: (0, 0)),
                  pl.BlockSpec((D_MODEL, Z_COLS), lambda i: (0, 0))],
        out_specs=[pl.BlockSpec((tm, w), lambda i: (i, 0)) for w in widths],
        compiler_params=_cp("parallel"),
        name="pre_mixer",
    )(x, mod, norm_g, w_in_ext)


_CONV_PAD = 16
_CONV_TT = 128


def _conv_kernel(za_ref, w_ref, b_ref, lg_ref, lb_ref, o_ref, hp_ref):
    t_len = o_ref.shape[1]
    u = za_ref[0]
    hp_ref[0:_CONV_PAD, :] = jnp.zeros((_CONV_PAD, MIX), F32)
    hp_ref[_CONV_PAD + t_len:2 * _CONV_PAD + t_len, :] = jnp.zeros((_CONV_PAD, MIX), F32)
    hp_ref[_CONV_PAD:_CONV_PAD + t_len, :] = u[:, :MIX] * _sigmoid(u[:, MIX:])
    w = w_ref[...]
    half = CONV_WIDTH // 2
    for t0 in range(0, t_len, _CONV_TT):
        acc = jnp.zeros((_CONV_TT, MIX), F32) + b_ref[...]
        for k in range(CONV_WIDTH):
            start = t0 + _CONV_PAD - half + k
            acc = acc + hp_ref[start:start + _CONV_TT, :] * w[k:k + 1, :]
        mu = jnp.mean(acc, axis=-1, keepdims=True)
        cen = acc - mu
        var = jnp.mean(cen * cen, axis=-1, keepdims=True)
        yn = cen * lax.rsqrt(var + EPS) * lg_ref[...] + lb_ref[...]
        o_ref[0, t0:t0 + _CONV_TT, :] = yn * _sigmoid(yn)


def _conv_module(za, w, b, ln_g, ln_b):
    bsz, t_len, _ = za.shape
    vec = pl.BlockSpec((1, MIX), lambda i: (0, 0))
    return pl.pallas_call(
        _conv_kernel,
        out_shape=jax.ShapeDtypeStruct((bsz, t_len, MIX), F32),
        grid=(bsz,),
        in_specs=[pl.BlockSpec((1, t_len, 2 * MIX), lambda i: (i, 0, 0)),
                  pl.BlockSpec((32, MIX), lambda i: (0, 0)), vec, vec, vec],
        out_specs=pl.BlockSpec((1, t_len, MIX), lambda i: (i, 0, 0)),
        scratch_shapes=[pltpu.VMEM((t_len + 2 * _CONV_PAD, MIX), F32)],
        compiler_params=_cp("parallel"),
        name="conv_module",
    )(za, w, b, ln_g, ln_b)


def _log_sigmoid(x):
    return jnp.minimum(x, 0.0) - jnp.log(1.0 + jnp.exp(-jnp.abs(x)))


def _split3_dot(tri, x):
    hi = x.astype(BF16)
    r1 = x - hi.astype(F32)
    mid = r1.astype(BF16)
    lo = (r1 - mid.astype(F32)).astype(BF16)
    dot = lambda v: jnp.dot(tri, v, preferred_element_type=F32)
    return dot(hi) + dot(mid) + dot(lo)


def _mlstm_chunk(zb_ref, zg_ref, gb_ref, c, direction, state):
    L = CHUNK
    rows = pl.ds(pl.multiple_of(c * L, L), L)
    ri = lax.broadcasted_iota(jnp.int32, (L, L), 0)
    ci = lax.broadcasted_iota(jnp.int32, (L, L), 1)
    if direction == 0:
        valid = ci <= ri
        last = L - 1
    else:
        valid = ci >= ri
        last = 0
    tri = jnp.where(valid, 1.0, 0.0).astype(BF16)
    g = zg_ref[0, rows, :] + gb_ref[...]
    bc = _split3_dot(tri, _log_sigmoid(g))
    bct = bc.T
    gt = g.T
    q = zb_ref[0, rows, 0:MIX] * (HD ** -0.5)
    k = zb_ref[0, rows, MIX:2 * MIX]
    v = zb_ref[0, rows, 2 * MIX:3 * MIX]
    kt = k.T
    cs, ns, ms = state
    new_c, new_n, new_m, hs = [], [], [], []
    for h in range(HEADS):
        li = direction * HEADS + h
        lf = 2 * HEADS + li
        b_col = bc[:, lf:lf + 1]
        b_row = bct[lf:lf + 1, :]
        i_col = g[:, li:li + 1]
        i_row = gt[li:li + 1, :]
        m_prev = ms[h]
        log_d = jnp.where(valid, b_col - b_row + i_row, -jnp.inf)
        m_inter = b_col + m_prev
        m_t = jnp.maximum(m_inter, jnp.max(log_d, axis=-1, keepdims=True))
        d_mat = jnp.exp(log_d - m_t)
        inter = jnp.exp(m_inter - m_t)
        sl = slice(h * HD, (h + 1) * HD)
        qh = q[:, sl]
        kh = k[:, sl]
        vh = v[:, sl].astype(BF16)
        qhb = qh.astype(BF16)
        s = lax.dot_general(qhb, kh.astype(BF16), (((1,), (1,)), ((), ())),
                            preferred_element_type=F32) * d_mat
        num = inter * _bdot(qhb, cs[h]) + _bdot(s, vh)
        den = inter * jnp.sum(qh * ns[h], axis=-1, keepdims=True) + jnp.sum(s, axis=-1, keepdims=True)
        hs.append(num / jnp.maximum(jnp.abs(den), jnp.exp(-m_t)))
        b_last = bc[last:last + 1, lf:lf + 1]
        lw_col = b_last - b_col + i_col
        lw_row = b_last - b_row + i_row
        m_new = jnp.maximum(b_last + m_prev, jnp.max(lw_row, axis=-1, keepdims=True))
        decay = jnp.exp(b_last + m_prev - m_new)
        w_col = jnp.exp(lw_col - m_new)
        w_row = jnp.exp(lw_row - m_new)
        new_c.append(decay * cs[h] + _bdot(kt[sl, :] * w_row, vh))
        new_n.append(decay * ns[h] + jnp.sum(kh * w_col, axis=0, keepdims=True))
        new_m.append(m_new)
    return hs, (tuple(new_c), tuple(new_n), tuple(new_m))


def _mlstm_kernel(zb_ref, zg_ref, gb_ref, ng_ref, c0_ref, n0_ref, m0_ref,
                  y_ref, c_ref, n_ref, m_ref, hf_ref):
    t_len = y_ref.shape[1]
    nc = t_len // CHUNK

    def init(direction):
        idx = [direction * HEADS + h for h in range(HEADS)]
        return (tuple(c0_ref[0, j] for j in idx),
                tuple(n0_ref[0, j:j + 1, :] for j in idx),
                tuple(m0_ref[0, j:j + 1, 0:1] for j in idx))

    def emit(direction, state):
        for h in range(HEADS):
            j = direction * HEADS + h
            c_ref[0, j] = state[0][h]
            n_ref[0, j:j + 1, :] = state[1][h]
            m_ref[0, j:j + 1, :] = jnp.broadcast_to(state[2][h], (1, LANE))

    def fwd_body(c, state):
        hs, state = _mlstm_chunk(zb_ref, zg_ref, gb_ref, c, 0, state)
        rows = pl.ds(pl.multiple_of(c * CHUNK, CHUNK), CHUNK)
        for h in range(HEADS):
            hf_ref[rows, h * HD:(h + 1) * HD] = hs[h]
        return state

    emit(0, lax.fori_loop(0, nc, fwd_body, init(0)))

    def bwd_body(i, state):
        c = nc - 1 - i
        hs, state = _mlstm_chunk(zb_ref, zg_ref, gb_ref, c, 1, state)
        rows = pl.ds(pl.multiple_of(c * CHUNK, CHUNK), CHUNK)
        o = zb_ref[0, rows, 3 * MIX:4 * MIX]
        for h in range(HEADS):
            sl = slice(h * HD, (h + 1) * HD)
            tot = hf_ref[rows, sl] + hs[h]
            y_ref[0, rows, sl] = _rms(tot) * ng_ref[:, sl] * _sigmoid(o[:, sl])
        return state

    emit(1, lax.fori_loop(0, nc, bwd_body, init(1)))


def _mlstm_mixer(zb, zg, gate_b, norm_g, c0, n0, m0):
    bsz, t_len, _ = zb.shape
    nd = 2 * HEADS
    return pl.pallas_call(
        _mlstm_kernel,
        out_shape=[jax.ShapeDtypeStruct((bsz, t_len, MIX), F32),
                   jax.ShapeDtypeStruct((bsz, nd, HD, HD), F32),
                   jax.ShapeDtypeStruct((bsz, nd, HD), F32),
                   jax.ShapeDtypeStruct((bsz, nd, LANE), F32)],
        grid=(bsz,),
        in_specs=[pl.BlockSpec((1, t_len, 4 * MIX), lambda i: (i, 0, 0)),
                  pl.BlockSpec((1, t_len, LANE), lambda i: (i, 0, 0)),
                  pl.BlockSpec((1, LANE), lambda i: (0, 0)),
                  pl.BlockSpec((1, MIX), lambda i: (0, 0)),
                  pl.BlockSpec((1, nd, HD, HD), lambda i: (i, 0, 0, 0)),
                  pl.BlockSpec((1, nd, HD), lambda i: (i, 0, 0)),
                  pl.BlockSpec((1, nd, LANE), lambda i: (i, 0, 0))],
        out_specs=[pl.BlockSpec((1, t_len, MIX), lambda i: (i, 0, 0)),
                   pl.BlockSpec((1, nd, HD, HD), lambda i: (i, 0, 0, 0)),
                   pl.BlockSpec((1, nd, HD), lambda i: (i, 0, 0)),
                   pl.BlockSpec((1, nd, LANE), lambda i: (i, 0, 0))],
        scratch_shapes=[pltpu.VMEM((t_len, MIX), F32)],
        compiler_params=_cp("parallel"),
        name="mlstm_mixer",
    )(zb, zg, gate_b, norm_g, c0, n0, m0)


_ATT_TQ = 256


def _mla_kernel(*refs, past):
    if past:
        (zc_ref, cos_ref, sin_ref, gq_ref, gkv_ref, wqa_ref, wqb_ref, wk_ref, wv_ref,
         pckv_ref, pkr_ref, y_ref, ckv_ref, q_sc, k_sc, v_sc) = refs
    else:
        (zc_ref, cos_ref, sin_ref, gq_ref, gkv_ref, wqa_ref, wqb_ref, wk_ref, wv_ref,
         y_ref, ckv_ref, q_sc, k_sc, v_sc) = refs
    t_len = y_ref.shape[1]
    cosf = cos_ref[...]
    sinf = sin_ref[...]
    cq = _rms(zc_ref[0, :, 0:Q_RANK]) * gq_ref[...]
    ckv = _rms(zc_ref[0, :, Q_RANK:Q_RANK + KV_RANK]) * gkv_ref[...]
    ckv_ref[0] = ckv
    kr = (zc_ref[0, :, Q_RANK + KV_RANK:Q_RANK + KV_RANK + LANE] * cosf
          + zc_ref[0, :, Q_RANK + KV_RANK + LANE:Q_RANK + KV_RANK + 2 * LANE] * sinf)
    cqb = cq.astype(BF16)
    ckvb = ckv.astype(BF16)
    for h in range(HEADS):
        sl = slice(h * LANE, (h + 1) * LANE)
        qa = jnp.dot(cqb, wqa_ref[:, sl], preferred_element_type=F32)
        qb = jnp.dot(cqb, wqb_ref[:, sl], preferred_element_type=F32)
        q_sc[:, sl] = (qa * cosf + qb * sinf).astype(BF16)
        kn = jnp.dot(ckvb, wk_ref[:, sl], preferred_element_type=F32)
        k_sc[past:past + t_len, sl] = (kn + kr).astype(BF16)
        v_sc[past:past + t_len, sl] = jnp.dot(ckvb, wv_ref[:, sl], preferred_element_type=F32).astype(BF16)
    if past:
        pckv = pckv_ref[0].astype(BF16)
        pkr = pkr_ref[0]
        for h in range(HEADS):
            sl = slice(h * LANE, (h + 1) * LANE)
            k_sc[0:past, sl] = (jnp.dot(pckv, wk_ref[:, sl], preferred_element_type=F32) + pkr).astype(BF16)
            v_sc[0:past, sl] = jnp.dot(pckv, wv_ref[:, sl], preferred_element_type=F32).astype(BF16)

    def q_block(i, carry):
        rows = pl.ds(pl.multiple_of(i * _ATT_TQ, _ATT_TQ), _ATT_TQ)
        outs = []
        for h in range(HEADS):
            sl = slice(h * LANE, (h + 1) * LANE)
            s = lax.dot_general(q_sc[rows, sl], k_sc[:, sl], (((1,), (1,)), ((), ())),
                                preferred_element_type=F32)
            p = jnp.exp(s - jnp.max(s, axis=-1, keepdims=True))
            o = jnp.dot(p.astype(BF16), v_sc[:, sl], preferred_element_type=F32)
            outs.append(o / jnp.sum(p, axis=-1, keepdims=True))
        y_ref[0, rows, 0:LANE] = outs[0] + outs[1]
        y_ref[0, rows, LANE:2 * LANE] = outs[2] + outs[3]
        return carry

    lax.fori_loop(0, t_len // _ATT_TQ, q_block, 0)


def _mla_mixer(zc, cosf, sinf, gq, gkv, wqa, wqb, wk, wv, past_ckv=None, past_kr=None):
    bsz, t_len, zw = zc.shape
    past = 0 if past_ckv is None else past_ckv.shape[1]
    full = lambda a: pl.BlockSpec(a.shape, lambda i: (0,) * a.ndim)
    args = [zc, cosf, sinf, gq, gkv, wqa, wqb, wk, wv]
    in_specs = [pl.BlockSpec((1, t_len, zw), lambda i: (i, 0, 0))] + [full(a) for a in args[1:]]
    if past:
        args += [past_ckv, past_kr]
        in_specs += [pl.BlockSpec((1, past, KV_RANK), lambda i: (i, 0, 0)),
                     pl.BlockSpec((1, past, LANE), lambda i: (i, 0, 0))]
    wide = HEADS * LANE
    return pl.pallas_call(
        functools.partial(_mla_kernel, past=past),
        out_shape=[jax.ShapeDtypeStruct((bsz, t_len, MIX), F32),
                   jax.ShapeDtypeStruct((bsz, t_len, KV_RANK), F32)],
        grid=(bsz,),
        in_specs=in_specs,
        out_specs=[pl.BlockSpec((1, t_len, MIX), lambda i: (i, 0, 0)),
                   pl.BlockSpec((1, t_len, KV_RANK), lambda i: (i, 0, 0))],
        scratch_shapes=[pltpu.VMEM((t_len, wide), BF16),
                        pltpu.VMEM((past + t_len, wide), BF16),
                        pltpu.VMEM((past + t_len, wide), BF16)],
        compiler_params=_cp("parallel"),
        name="mla_mixer",
    )(*args)


_S5_SEQ = 8
_S5_TT = 128


def _s5_kernel(zd_ref, bd_ref, cd_ref, are_ref, aim_ref, s0re_ref, s0im_ref,
               y_ref, fre_ref, fim_ref, bre_sc, bim_sc, sre_sc, sim_sc):
    direction = pl.program_id(0)
    k = pl.program_id(2)
    tt = _S5_TT
    rows = _S5_SEQ * tt

    @pl.when(k == 0)
    def _():
        sre_sc[...] = s0re_ref[0]
        sim_sc[...] = s0im_ref[0]

    u = zd_ref[...].reshape(rows, MIX).astype(BF16)
    bre_sc[...] = jnp.dot(u, bd_ref[0, :, 0:S5_STATES], preferred_element_type=F32)
    bim_sc[...] = jnp.dot(u, bd_ref[0, :, S5_STATES:2 * S5_STATES], preferred_element_type=F32)
    a_re = jnp.broadcast_to(are_ref[0], (_S5_SEQ, S5_STATES))
    a_im = jnp.broadcast_to(aim_ref[0], (_S5_SEQ, S5_STATES))

    def step(i, carry):
        s_re, s_im = carry
        t = i + direction * (tt - 1 - 2 * i)
        sel = pl.ds(pl.multiple_of(t * _S5_SEQ, _S5_SEQ), _S5_SEQ)
        n_re = a_re * s_re - a_im * s_im + bre_sc[sel, :]
        n_im = a_re * s_im + a_im * s_re + bim_sc[sel, :]
        bre_sc[sel, :] = n_re
        bim_sc[sel, :] = n_im
        return n_re, n_im

    s_re, s_im = lax.fori_loop(0, tt, step, (sre_sc[...], sim_sc[...]), unroll=2)
    sre_sc[...] = s_re
    sim_sc[...] = s_im
    y = (jnp.dot(bre_sc[...].astype(BF16), cd_ref[0, 0:S5_STATES, :], preferred_element_type=F32)
         + jnp.dot(bim_sc[...].astype(BF16), cd_ref[0, S5_STATES:2 * S5_STATES, :],
                   preferred_element_type=F32))
    y_ref[0] = y.reshape(tt, _S5_SEQ, MIX)

    @pl.when(k == pl.num_programs(2) - 1)
    def _():
        fre_ref[0] = s_re
        fim_ref[0] = s_im


def _s5_scan(zd, bd, cd, ab_re, ab_im, s0_re, s0_im):
    t_len, bsz, _ = zd.shape
    nt = t_len // _S5_TT
    tile = lambda d, k: k + d * (nt - 1 - 2 * k)
    st = pl.BlockSpec((1, _S5_SEQ, S5_STATES), lambda d, g, k: (d, g, 0))
    return pl.pallas_call(
        _s5_kernel,
        out_shape=[jax.ShapeDtypeStruct((2, t_len, bsz, MIX), F32),
                   jax.ShapeDtypeStruct((2, bsz, S5_STATES), F32),
                   jax.ShapeDtypeStruct((2, bsz, S5_STATES), F32)],
        grid=(2, bsz // _S5_SEQ, nt),
        in_specs=[pl.BlockSpec((_S5_TT, _S5_SEQ, MIX), lambda d, g, k: (tile(d, k), g, 0)),
                  pl.BlockSpec((1, MIX, 2 * S5_STATES), lambda d, g, k: (d, 0, 0)),
                  pl.BlockSpec((1, 2 * S5_STATES, MIX), lambda d, g, k: (d, 0, 0)),
                  pl.BlockSpec((1, 1, S5_STATES), lambda d, g, k: (d, 0, 0)),
                  pl.BlockSpec((1, 1, S5_STATES), lambda d, g, k: (d, 0, 0)),
                  st, st],
        out_specs=[pl.BlockSpec((1, _S5_TT, _S5_SEQ, MIX), lambda d, g, k: (d, tile(d, k), g, 0)),
                   st, st],
        scratch_shapes=[pltpu.VMEM((_S5_SEQ * _S5_TT, S5_STATES), F32),
                        pltpu.VMEM((_S5_SEQ * _S5_TT, S5_STATES), F32),
                        pltpu.VMEM((_S5_SEQ, S5_STATES), F32),
                        pltpu.VMEM((_S5_SEQ, S5_STATES), F32)],
        compiler_params=_cp("parallel", "parallel", "arbitrary"),
        name="s5_scan",
    )(zd, bd, cd, ab_re, ab_im, s0_re, s0_im)


def _gelu_tanh(x):
    return 0.5 * x * (1.0 + jnp.tanh(math.sqrt(2.0 / math.pi) * (x + 0.044715 * x * x * x)))


def _route(logits):
    lane = lax.broadcasted_iota(jnp.int32, logits.shape, 1).astype(F32)
    big = float(LANE)
    neg = -jnp.inf
    g_mask = (lane >= N_EXPERT) & (lane < N_EXPERT + N_GROUP)
    gl = jnp.where(g_mask, logits, neg)
    g_max = jnp.max(gl, axis=-1, keepdims=True)
    g_idx = jnp.min(jnp.where(gl == g_max, lane, big), axis=-1, keepdims=True) - N_EXPERT
    g_sel = 1.0 / jnp.sum(jnp.where(g_mask, jnp.exp(logits - g_max), 0.0), axis=-1, keepdims=True)
    lo = g_idx * PER_GROUP
    el = jnp.where((lane >= lo) & (lane < lo + PER_GROUP), logits, neg)
    v1 = jnp.max(el, axis=-1, keepdims=True)
    i1 = jnp.min(jnp.where(el == v1, lane, big), axis=-1, keepdims=True)
    el2 = jnp.where(lane == i1, neg, el)
    v2 = jnp.max(el2, axis=-1, keepdims=True)
    i2 = jnp.min(jnp.where(el2 == v2, lane, big), axis=-1, keepdims=True)
    e2 = jnp.exp(v2 - v1)
    w1 = g_sel / (1.0 + e2)
    return jnp.where(lane == i1, w1, jnp.where(lane == i2, w1 * e2, 0.0))


def _post_kernel(x_ref, ya_ref, yb_ref, yc_ref, ys_ref, zd_ref, mod_ref, wo_ref, d_ref, wglu_ref, bglu_ref,
                 g2_ref, wr_ref, br_ref, x1_ref, h2_ref, gate_ref):
    m = mod_ref[0]
    zd = zd_ref[...]
    ys = ys_ref[0] + ys_ref[1] + d_ref[...] * zd
    gl = _bdot(_gelu_tanh(ys), wglu_ref[...]) + bglu_ref[...]
    yd = gl[:, 0:MIX] * _sigmoid(gl[:, MIX:2 * MIX])
    mix = (_bdot(ya_ref[...], wo_ref[0:MIX, :]) + _bdot(yb_ref[...], wo_ref[MIX:2 * MIX, :])
           + _bdot(yc_ref[...], wo_ref[2 * MIX:3 * MIX, :]) + _bdot(yd, wo_ref[3 * MIX:4 * MIX, :]))
    x1 = x_ref[...] + m[2:3] * mix
    x1_ref[...] = x1
    h2 = _rms(x1) * g2_ref[...] * (1.0 + m[4:5]) + m[3:4]
    h2b = h2.astype(BF16)
    h2_ref[...] = h2b
    gate_ref[...] = _route(jnp.dot(h2b, wr_ref[...], preferred_element_type=F32) + br_ref[...])


def _post_mixer(x, ya, yb, yc, ys, zd, mod, row_fn, w_out, s5_d, w_glu, b_glu, norm2_g, w_router, b_router, tm):
    n = x.shape[0]
    tok = lambda w: pl.BlockSpec((tm, w), lambda i: (i, 0))
    full = lambda a: pl.BlockSpec(a.shape, lambda i: (0,) * a.ndim)
    return pl.pallas_call(
        _post_kernel,
        out_shape=[jax.ShapeDtypeStruct((n, D_MODEL), F32),
                   jax.ShapeDtypeStruct((n, D_MODEL), BF16),
                   jax.ShapeDtypeStruct((n, LANE), F32)],
        grid=(n // tm,),
        in_specs=[tok(D_MODEL), tok(MIX), tok(MIX), tok(MIX),
                  pl.BlockSpec((2, tm, MIX), lambda i: (0, i, 0)), tok(MIX),
                  pl.BlockSpec((1, 6, D_MODEL), lambda i: (row_fn(i, tm), 0, 0)),
                  full(w_out), full(s5_d), full(w_glu), full(b_glu), full(norm2_g),
                  full(w_router), full(b_router)],
        out_specs=[tok(D_MODEL), tok(D_MODEL), tok(LANE)],
        compiler_params=_cp("parallel"),
        name="post_mixer",
    )(x, ya, yb, yc, ys, zd, mod, w_out, s5_d, w_glu, b_glu, norm2_g, w_router, b_router)


_MOE_EPS = 4
_MOE_TM = 1024


def _moe_kernel(h_ref, gate_ref, x_ref, mod_ref, w1_ref, w3_ref, w2_ref, o_ref, acc_ref):
    j = pl.program_id(1)

    @pl.when(j == 0)
    def _():
        acc_ref[...] = jnp.zeros_like(acc_ref)

    h = h_ref[...]
    gate = gate_ref[...]
    lane = lax.broadcasted_iota(jnp.int32, gate.shape, 1)
    for e in range(_MOE_EPS):
        ge = jnp.sum(jnp.where(lane == j * _MOE_EPS + e, gate, 0.0), axis=-1, keepdims=True)
        h1 = jnp.dot(h, w1_ref[e], preferred_element_type=F32)
        h3 = jnp.dot(h, w3_ref[e], preferred_element_type=F32)
        hid = h1 * _sigmoid(h1) * h3 * ge
        acc_ref[...] += jnp.dot(hid.astype(BF16), w2_ref[e], preferred_element_type=F32)

    @pl.when(j == pl.num_programs(1) - 1)
    def _():
        o_ref[...] = x_ref[...] + mod_ref[0][5:6] * acc_ref[...]


def _moe(h2, gate, x1, mod, row_fn, w1, w3, w2, tm):
    n = h2.shape[0]
    return pl.pallas_call(
        _moe_kernel,
        out_shape=jax.ShapeDtypeStruct((n, D_MODEL), F32),
        grid=(n // tm, N_EXPERT // _MOE_EPS),
        in_specs=[pl.BlockSpec((tm, D_MODEL), lambda i, j: (i, 0)),
                  pl.BlockSpec((tm, LANE), lambda i, j: (i, 0)),
                  pl.BlockSpec((tm, D_MODEL), lambda i, j: (i, 0)),
                  pl.BlockSpec((1, 6, D_MODEL), lambda i, j: (row_fn(i, tm), 0, 0)),
                  pl.BlockSpec((_MOE_EPS, D_MODEL, MOE_FF), lambda i, j: (j, 0, 0)),
                  pl.BlockSpec((_MOE_EPS, D_MODEL, MOE_FF), lambda i, j: (j, 0, 0)),
                  pl.BlockSpec((_MOE_EPS, MOE_FF, D_MODEL), lambda i, j: (j, 0, 0))],
        out_specs=pl.BlockSpec((tm, D_MODEL), lambda i, j: (i, 0)),
        scratch_shapes=[pltpu.VMEM((tm, D_MODEL), F32)],
        compiler_params=_cp("parallel", "arbitrary"),
        name="moe",
    )(h2, gate, x1, mod, w1, w3, w2)


def _final_kernel(x_ref, g_ref, o_ref):
    o_ref[...] = _rms(x_ref[...]) * g_ref[...]


def _final_norm(x, g, tm):
    n = x.shape[0]
    return pl.pallas_call(
        _final_kernel,
        out_shape=jax.ShapeDtypeStruct((n, D_MODEL), F32),
        grid=(n // tm,),
        in_specs=[pl.BlockSpec((tm, D_MODEL), lambda i: (i, 0)),
                  pl.BlockSpec((1, D_MODEL), lambda i: (0, 0))],
        out_specs=pl.BlockSpec((tm, D_MODEL), lambda i: (i, 0)),
        compiler_params=_cp("parallel"),
        name="final_norm",
    )(x, g)


def _rot_cols(w):
    q = ROPE // 4
    return jnp.concatenate([-w[..., q:2 * q], w[..., 0:q], -w[..., 3 * q:4 * q], w[..., 2 * q:3 * q]], axis=-1)


def _rope_slot(w):
    pad = [(0, 0)] * (w.ndim - 1) + [(HD, LANE - HD - ROPE)]
    return jnp.pad(w, pad)


def _prep_w_in(w_in, gate_cols):
    a = w_in[..., 0:512]
    qkvo = w_in[..., 512:1536]
    g = jnp.pad(w_in[..., 1536:1536 + gate_cols], [(0, 0), (0, 0), (0, LANE - gate_cols)])
    c0 = 1536 + gate_cols
    cq = w_in[..., c0:c0 + Q_RANK]
    ckv = w_in[..., c0 + Q_RANK:c0 + Q_RANK + KV_RANK]
    kr = w_in[..., c0 + Q_RANK + KV_RANK:c0 + Q_RANK + KV_RANK + ROPE]
    d = w_in[..., c0 + Q_RANK + KV_RANK + ROPE:]
    return jnp.concatenate([a, qkvo, g, cq, ckv, _rope_slot(kr), _rope_slot(_rot_cols(kr)), d],
                           axis=-1).astype(BF16)


def _prep_mla(w_uq, w_ukv):
    n_layer = w_uq.shape[0]
    scale = (HD + ROPE) ** -0.5
    wq = w_uq.reshape(n_layer, Q_RANK, HEADS, HD + ROPE) * scale
    nope, rope = wq[..., :HD], wq[..., HD:]
    zeros_r = jnp.zeros_like(rope)
    wqa = jnp.concatenate([nope, rope, zeros_r], axis=-1).reshape(n_layer, Q_RANK, HEADS * LANE)
    wqb = jnp.concatenate([jnp.zeros_like(nope), _rot_cols(rope), zeros_r], axis=-1)
    wqb = wqb.reshape(n_layer, Q_RANK, HEADS * LANE)
    wkv = w_ukv.reshape(n_layer, KV_RANK, HEADS, 2 * HD)
    k_nope, val = wkv[..., :HD], wkv[..., HD:]
    zeros_h = jnp.zeros_like(k_nope)
    wk = jnp.concatenate([k_nope, zeros_h], axis=-1).reshape(n_layer, KV_RANK, HEADS * LANE)
    even = jnp.concatenate([val, zeros_h], axis=-1)
    odd = jnp.concatenate([zeros_h, val], axis=-1)
    is_odd = (jnp.arange(HEADS) % 2 == 1)[None, None, :, None]
    wv = jnp.where(is_odd, odd, even).reshape(n_layer, KV_RANK, HEADS * LANE)
    return wqa.astype(BF16), wqb.astype(BF16), wk.astype(BF16), wv.astype(BF16)


def _rope_tables(t_len, rotate):
    ones = jnp.ones((t_len, HD), F32)
    zeros = jnp.zeros((t_len, HD), F32)
    tail = jnp.zeros((t_len, LANE - HD - ROPE), F32)
    if not rotate:
        return (jnp.concatenate([ones, jnp.ones((t_len, ROPE), F32), tail], axis=-1),
                jnp.zeros((t_len, LANE), F32))
    rows = t_len // GRID_W
    row = jnp.repeat(jnp.arange(rows, dtype=F32), GRID_W)
    col = jnp.tile(jnp.arange(GRID_W, dtype=F32), rows)
    nf = ROPE // 4
    inv = ROPE_BASE ** (-jnp.arange(nf, dtype=F32) / nf)
    ar = row[:, None] * inv
    ac = col[:, None] * inv
    cos = jnp.concatenate([jnp.cos(ar), jnp.cos(ar), jnp.cos(ac), jnp.cos(ac)], axis=-1)
    sin = jnp.concatenate([jnp.sin(ar), jnp.sin(ar), jnp.sin(ac), jnp.sin(ac)], axis=-1)
    return (jnp.concatenate([ones, cos, tail], axis=-1), jnp.concatenate([zeros, sin, tail], axis=-1))


def _prep_s5(bb_re, bb_im, c_re, c_im):
    eye = jnp.eye(S5_G, dtype=F32)
    to_b = lambda bb: jnp.einsum("ldgnc,gh->ldgchn", bb, eye).reshape(bb.shape[0], 2, MIX, S5_STATES)
    to_c = lambda cc: jnp.einsum("ldgcn,gh->ldgnhc", cc, eye).reshape(cc.shape[0], 2, S5_STATES, MIX)
    bd = jnp.concatenate([to_b(bb_re), to_b(bb_im)], axis=-1)
    cd = jnp.concatenate([to_c(c_re.astype(F32)), -to_c(c_im.astype(F32))], axis=-2)
    return bd.astype(BF16), cd.astype(BF16)


def _layer(x, bsz, t_len, mod, row_fn, p, ctx, tm):
    za, zb, zg, zc, zd = _pre_mixer(x, mod, row_fn, p["norm1_g"], p["w_in"], tm)
    seq = lambda a: a.reshape(bsz, t_len, a.shape[-1])
    ya = _conv_module(seq(za), p["conv_w"], p["conv_b"], p["conv_ln_g"], p["conv_ln_b"])
    yb, c_fin, n_fin, m_fin = _mlstm_mixer(seq(zb), seq(zg), p["gate_b"], p["mlstm_norm_g"],
                                           ctx["mlstm_c"], ctx["mlstm_n"], ctx["mlstm_m"])
    yc, ckv = _mla_mixer(seq(zc), ctx["cos"], ctx["sin"], p["mla_q_norm_g"], p["mla_kv_norm_g"],
                         p["wqa"], p["wqb"], p["wk"], p["wv"], ctx.get("past_ckv"), ctx.get("past_kr"))
    ys, s_re, s_im = _s5_scan(seq(zd).transpose(1, 0, 2), p["s5_bd"], p["s5_cd"], p["s5_ab_re"], p["s5_ab_im"],
                              ctx["s5_re"], ctx["s5_im"])
    ys = ys.transpose(0, 2, 1, 3)
    flat = lambda a: a.reshape(bsz * t_len, a.shape[-1])
    x1, h2, gate = _post_mixer(x, flat(ya), flat(yb), flat(yc), ys.reshape(2, bsz * t_len, MIX), zd, mod, row_fn,
                               p["w_out"], p["s5_d"], p["s5_w_glu"], p["s5_b_glu"], p["norm2_g"],
                               p["w_router"], p["b_router"], tm)
    x2 = _moe(h2, gate, x1, mod, row_fn, p["moe_w1"], p["moe_w3"], p["moe_w2"], _MOE_TM)
    krope = seq(zc)[:, :, Q_RANK + KV_RANK + HD:Q_RANK + KV_RANK + HD + ROPE]
    return x2, (ckv, krope, c_fin, n_fin, m_fin, s_re, s_im)


def kernel(x_prompt, x_sample, cache_mla_ckv, cache_mla_krope, state_mlstm_C, state_mlstm_n, state_mlstm_m, state_s5, c, c_ctx, norm1_g, norm2_g, final_g, w_mod, b_mod, w_in, w_out, conv_w, conv_b, conv_ln_g, conv_ln_b, mlstm_gate_b, mlstm_norm_g, mla_q_norm_g, mla_w_uq, mla_kv_norm_g, mla_w_ukv, s5_a_re, s5_a_im, s5_log_dt, s5_b_re, s5_b_im, s5_c_re, s5_c_im, s5_d, s5_w_glu, s5_b_glu, moe_w_group, moe_b_group, moe_w_expert, moe_b_expert, moe_w1, moe_w3, moe_w2):
    n_layer = w_in.shape[0]
    b_ctx, t_ctx, d = x_prompt.shape
    b_lat, t_lat, _ = x_sample.shape
    nd = 2 * HEADS
    tm = 512
    assert b_lat + 1 <= 16 and t_lat % _MOE_TM == 0 and (b_ctx * t_ctx) % _MOE_TM == 0

    c_all = jnp.zeros((16, d), F32).at[0].set(c_ctx).at[1:1 + b_lat].set(c)
    mod_all = _modulation(c_all, w_mod, b_mod).reshape(n_layer, 16, 6, d)
    gate_cols = 4 * HEADS
    w_in_ext = _prep_w_in(w_in, gate_cols)
    wqa, wqb, wk, wv = _prep_mla(mla_w_uq, mla_w_ukv)
    ab_re, ab_im, bb_re, bb_im = _s5_discretise(s5_a_re, s5_a_im, s5_log_dt, s5_b_re, s5_b_im)
    s5_bd, s5_cd = _prep_s5(bb_re, bb_im, s5_c_re, s5_c_im)
    w_router = jnp.pad(jnp.concatenate([moe_w_expert, moe_w_group], axis=-1),
                       [(0, 0), (0, 0), (0, LANE - N_EXPERT - N_GROUP)]).astype(BF16)
    b_router = jnp.pad(jnp.concatenate([moe_b_expert, moe_b_group], axis=-1),
                       [(0, 0), (0, LANE - N_EXPERT - N_GROUP)])
    gate_b = jnp.pad(mlstm_gate_b.reshape(n_layer, gate_cols), [(0, 0), (0, LANE - gate_cols)])
    conv_w_p = jnp.pad(conv_w, [(0, 0), (0, 32 - CONV_WIDTH), (0, 0)])
    w_out_b = w_out.astype(BF16)
    w_glu_b = s5_w_glu.astype(BF16)
    w1_b, w3_b, w2_b = moe_w1.astype(BF16), moe_w3.astype(BF16), moe_w2.astype(BF16)
    row = lambda a, l: a[l][None, :]

    cos_ctx, sin_ctx = _rope_tables(t_ctx, rotate=False)
    cos_lat, sin_lat = _rope_tables(t_lat, rotate=True)
    zero_state = dict(
        mlstm_c=jnp.zeros((b_ctx, nd, HD, HD), F32), mlstm_n=jnp.zeros((b_ctx, nd, HD), F32),
        mlstm_m=jnp.zeros((b_ctx, nd, LANE), F32),
        s5_re=jnp.zeros((2, b_ctx, S5_STATES), F32), s5_im=jnp.zeros((2, b_ctx, S5_STATES), F32),
        cos=cos_ctx, sin=sin_ctx)

    row_ctx = lambda i, tile: 0
    row_lat = lambda i, tile: 1 + (i * tile) // t_lat

    x_ctx = x_prompt.reshape(b_ctx * t_ctx, d)
    x_lat = x_sample.reshape(b_lat * t_lat, d)
    outs = []
    for l in range(n_layer):
        p = dict(norm1_g=row(norm1_g, l), norm2_g=row(norm2_g, l), w_in=w_in_ext[l], w_out=w_out_b[l],
                 conv_w=conv_w_p[l], conv_b=row(conv_b, l), conv_ln_g=row(conv_ln_g, l),
                 conv_ln_b=row(conv_ln_b, l), gate_b=row(gate_b, l), mlstm_norm_g=row(mlstm_norm_g, l),
                 mla_q_norm_g=row(mla_q_norm_g, l), mla_kv_norm_g=row(mla_kv_norm_g, l),
                 wqa=wqa[l], wqb=wqb[l], wk=wk[l], wv=wv[l],
                 s5_bd=s5_bd[l], s5_cd=s5_cd[l], s5_ab_re=ab_re[l][:, None, :], s5_ab_im=ab_im[l][:, None, :],
                 s5_d=row(s5_d, l), s5_w_glu=w_glu_b[l], s5_b_glu=row(s5_b_glu, l),
                 w_router=w_router[l], b_router=row(b_router, l),
                 moe_w1=w1_b[l], moe_w3=w3_b[l], moe_w2=w2_b[l])
        x_ctx, st = _layer(x_ctx, b_ctx, t_ctx, mod_all[l], row_ctx, p, zero_state, tm)
        outs.append(st)
        s5_l = state_s5[:, l].reshape(b_lat, 2, S5_STATES, 2)
        lat_state = dict(
            mlstm_c=state_mlstm_C[:, l].reshape(b_lat, nd, HD, HD),
            mlstm_n=state_mlstm_n[:, l].reshape(b_lat, nd, HD),
            mlstm_m=jnp.broadcast_to(state_mlstm_m[:, l].reshape(b_lat, nd, 1), (b_lat, nd, LANE)),
            s5_re=s5_l[..., 0].transpose(1, 0, 2), s5_im=s5_l[..., 1].transpose(1, 0, 2),
            cos=cos_lat, sin=sin_lat,
            past_ckv=cache_mla_ckv[:, l], past_kr=_rope_slot(cache_mla_krope[:, l]))
        x_lat, _ = _layer(x_lat, b_lat, t_lat, mod_all[l], row_lat, p, lat_state, tm)

    y_prompt = _final_norm(x_ctx, final_g[None, :], tm).reshape(b_ctx, t_ctx, d)
    y_sample = _final_norm(x_lat, final_g[None, :], tm).reshape(b_lat, t_lat, d)
    stack = lambda i: jnp.stack([o[i] for o in outs], axis=1)
    new_ckv = stack(0)
    new_krope = stack(1)
    new_c = stack(2).reshape(b_ctx, n_layer, 2, HEADS, HD, HD)
    new_n = stack(3).reshape(b_ctx, n_layer, 2, HEADS, HD)
    new_m = stack(4)[..., 0].reshape(b_ctx, n_layer, 2, HEADS)
    s_re = jnp.stack([o[5] for o in outs], axis=0)
    s_im = jnp.stack([o[6] for o in outs], axis=0)
    new_s5 = jnp.stack([s_re, s_im], axis=-1).transpose(2, 0, 1, 3, 4)
    new_s5 = new_s5.reshape(b_ctx, n_layer, 2, S5_G, S5_N, 2)
    return (y_prompt, y_sample, new_ckv, new_krope, new_c, new_n, new_m, new_s5)
```

```python
import functools
import math

import jax
import jax.numpy as jnp
from jax import lax
from jax.experimental import pallas as pl
from jax.experimental.pallas import tpu as pltpu

F32 = jnp.float32
BF16 = jnp.bfloat16
EPS = 1e-6

D_MODEL = 1024
MIX = 256
CONV_WIDTH = 31
HEADS = 4
HD = 64
CHUNK = 128
ROPE = 32
KV_RANK = 128
Q_RANK = 256
GRID_W = 64
ROPE_BASE = 10000.0
S5_G = 16
S5_GC = 16
S5_N = 64
S5_STATES = S5_G * S5_N
N_EXPERT = 32
PER_GROUP = 8
N_GROUP = 4
MOE_FF = 256
LANE = 128
Z_COLS = 2560
VMEM_LIMIT = 56 * 1024 * 1024


def _cp(*sem):
    return pltpu.CompilerParams(dimension_semantics=sem, vmem_limit_bytes=VMEM_LIMIT)


def _rms(x):
    return x * lax.rsqrt(jnp.mean(x * x, axis=-1, keepdims=True) + EPS)


def _sigmoid(x):
    return 1.0 / (1.0 + jnp.exp(-x))


def _bdot(a, b):
    return jnp.dot(a.astype(BF16), b.astype(BF16), preferred_element_type=F32)


def _mod_kernel(c_ref, w_ref, b_ref, o_ref):
    c = c_ref[...]
    o_ref[0] = _bdot(c * _sigmoid(c), w_ref[0]) + b_ref[0]


def _modulation(c_all, w_mod, b_mod):
    n_layer, d, n = w_mod.shape
    tn = 1536
    return pl.pallas_call(
        _mod_kernel,
        out_shape=jax.ShapeDtypeStruct((n_layer, 16, n), F32),
        grid=(n_layer, n // tn),
        in_specs=[pl.BlockSpec((16, d), lambda l, j: (0, 0)),
                  pl.BlockSpec((1, d, tn), lambda l, j: (l, 0, j)),
                  pl.BlockSpec((1, 1, tn), lambda l, j: (l, 0, j))],
        out_specs=pl.BlockSpec((1, 16, tn), lambda l, j: (l, 0, j)),
        compiler_params=_cp("parallel", "parallel"),
        name="modulation",
    )(c_all, w_mod, b_mod.reshape(n_layer, 1, n))


def _s5_disc_kernel(are_ref, aim_ref, ldt_ref, bre_ref, bim_ref, abre_ref, abim_ref, bbre_ref, bbim_ref):
    a_re = are_ref[...]
    a_im = aim_ref[...]
    dt = jnp.exp(ldt_ref[...])
    mag = jnp.exp(a_re * dt)
    ab_re = mag * jnp.cos(a_im * dt)
    ab_im = mag * jnp.sin(a_im * dt)
    den = a_re * a_re + a_im * a_im
    f_re = ((ab_re - 1.0) * a_re + ab_im * a_im) / den
    f_im = (ab_im * a_re - (ab_re - 1.0) * a_im) / den
    b_re = bre_ref[...]
    b_im = bim_ref[...]
    abre_ref[...] = ab_re
    abim_ref[...] = ab_im
    bbre_ref[...] = f_re * b_re - f_im * b_im
    bbim_ref[...] = f_re * b_im + f_im * b_re


def _s5_discretise(a_re, a_im, log_dt, b_re, b_im):
    n_layer = a_re.shape[0]
    rows = n_layer * 2 * S5_G
    cols = S5_N * S5_GC
    rep = lambda a: jnp.repeat(a.reshape(rows, S5_N), S5_GC, axis=1)
    ldt = jnp.broadcast_to(log_dt.reshape(rows, 1), (rows, cols))
    spec = pl.BlockSpec((rows, cols), lambda: (0, 0))
    ab_re, ab_im, bb_re, bb_im = pl.pallas_call(
        _s5_disc_kernel,
        out_shape=[jax.ShapeDtypeStruct((rows, cols), F32)] * 4,
        in_specs=[spec] * 5,
        out_specs=[spec] * 4,
        name="s5_discretise",
    )(rep(a_re), rep(a_im), ldt, b_re.reshape(rows, cols), b_im.reshape(rows, cols))
    pick = lambda a: a[:, ::S5_GC].reshape(n_layer, 2, S5_STATES)
    shp = (n_layer, 2, S5_G, S5_N, S5_GC)
    return pick(ab_re), pick(ab_im), bb_re.reshape(shp), bb_im.reshape(shp)


def _pre_kernel(x_ref, mod_ref, g_ref, w_ref, za_ref, zb_ref, zg_ref, zc_ref, zd_ref):
    m = mod_ref[0]
    h = _rms(x_ref[...]) * g_ref[...] * (1.0 + m[1:2]) + m[0:1]
    hb = h.astype(BF16)
    col = 0
    for o_ref in (za_ref, zb_ref, zg_ref, zc_ref, zd_ref):
        n = o_ref.shape[-1]
        o_ref[...] = jnp.dot(hb, w_ref[:, col:col + n], preferred_element_type=F32)
        col += n


def _pre_mixer(x, mod, row_fn, norm_g, w_in_ext, tm):
    n = x.shape[0]
    widths = (512, 1024, LANE, 640, MIX)
    return pl.pallas_call(
        _pre_kernel,
        out_shape=[jax.ShapeDtypeStruct((n, w), F32) for w in widths],
        grid=(n // tm,),
        in_specs=[pl.BlockSpec((tm, D_MODEL), lambda i: (i, 0)),
                  pl.BlockSpec((1, 6, D_MODEL), lambda i: (row_fn(i, tm), 0, 0)),
                  pl.BlockSpec((1, D_MODEL), lambda i: (0, 0)),
                  pl.BlockSpec((D_MODEL, Z_COLS), lambda i: (0, 0))],
        out_specs=[pl.BlockSpec((tm, w), lambda i: (i, 0)) for w in widths],
        compiler_params=_cp("parallel"),
        name="pre_mixer",
    )(x, mod, norm_g, w_in_ext)


_CONV_PAD = 16
_CONV_TT = 128


def _conv_kernel(za_ref, w_ref, b_ref, lg_ref, lb_ref, o_ref, hp_ref):
    t_len = o_ref.shape[1]
    u = za_ref[0]
    hp_ref[0:_CONV_PAD, :] = jnp.zeros((_CONV_PAD, MIX), F32)
    hp_ref[_CONV_PAD + t_len:2 * _CONV_PAD + t_len, :] = jnp.zeros((_CONV_PAD, MIX), F32)
    hp_ref[_CONV_PAD:_CONV_PAD + t_len, :] = u[:, :MIX] * _sigmoid(u[:, MIX:])
    w = w_ref[...]
    half = CONV_WIDTH // 2
    for t0 in range(0, t_len, _CONV_TT):
        acc = jnp.zeros((_CONV_TT, MIX), F32) + b_ref[...]
        for k in range(CONV_WIDTH):
            start = t0 + _CONV_PAD - half + k
            acc = acc + hp_ref[start:start + _CONV_TT, :] * w[k:k + 1, :]
        mu = jnp.mean(acc, axis=-1, keepdims=True)
        cen = acc - mu
        var = jnp.mean(cen * cen, axis=-1, keepdims=True)
        yn = cen * lax.rsqrt(var + EPS) * lg_ref[...] + lb_ref[...]
        o_ref[0, t0:t0 + _CONV_TT, :] = yn * _sigmoid(yn)


def _conv_module(za, w, b, ln_g, ln_b):
    bsz, t_len, _ = za.shape
    vec = pl.BlockSpec((1, MIX), lambda i: (0, 0))
    return pl.pallas_call(
        _conv_kernel,
        out_shape=jax.ShapeDtypeStruct((bsz, t_len, MIX), F32),
        grid=(bsz,),
        in_specs=[pl.BlockSpec((1, t_len, 2 * MIX), lambda i: (i, 0, 0)),
                  pl.BlockSpec((32, MIX), lambda i: (0, 0)), vec, vec, vec],
        out_specs=pl.BlockSpec((1, t_len, MIX), lambda i: (i, 0, 0)),
        scratch_shapes=[pltpu.VMEM((t_len + 2 * _CONV_PAD, MIX), F32)],
        compiler_params=_cp("parallel"),
        name="conv_module",
    )(za, w, b, ln_g, ln_b)


def _log_sigmoid(x):
    return jnp.minimum(x, 0.0) - jnp.log(1.0 + jnp.exp(-jnp.abs(x)))


def _split3_dot(tri, x):
    hi = x.astype(BF16)
    r1 = x - hi.astype(F32)
    mid = r1.astype(BF16)
    lo = (r1 - mid.astype(F32)).astype(BF16)
    dot = lambda v: jnp.dot(tri, v, preferred_element_type=F32)
    return dot(hi) + dot(mid) + dot(lo)


def _mlstm_chunk(zb_ref, zg_ref, gb_ref, c, direction, state):
    L = CHUNK
    rows = pl.ds(pl.multiple_of(c * L, L), L)
    ri = lax.broadcasted_iota(jnp.int32, (L, L), 0)
    ci = lax.broadcasted_iota(jnp.int32, (L, L), 1)
    if direction == 0:
        valid = ci <= ri
        last = L - 1
    else:
        valid = ci >= ri
        last = 0
    tri = jnp.where(valid, 1.0, 0.0).astype(BF16)
    g = zg_ref[0, rows, :] + gb_ref[...]
    bc = _split3_dot(tri, _log_sigmoid(g))
    bct = bc.T
    gt = g.T
    q = zb_ref[0, rows, 0:MIX] * (HD ** -0.5)
    k = zb_ref[0, rows, MIX:2 * MIX]
    v = zb_ref[0, rows, 2 * MIX:3 * MIX]
    kt = k.T
    cs, ns, ms = state
    new_c, new_n, new_m, hs = [], [], [], []
    for h in range(HEADS):
        li = direction * HEADS + h
        lf = 2 * HEADS + li
        b_col = bc[:, lf:lf + 1]
        b_row = bct[lf:lf + 1, :]
        i_col = g[:, li:li + 1]
        i_row = gt[li:li + 1, :]
        m_prev = ms[h]
        log_d = jnp.where(valid, b_col - b_row + i_row, -jnp.inf)
        m_inter = b_col + m_prev
        m_t = jnp.maximum(m_inter, jnp.max(log_d, axis=-1, keepdims=True))
        d_mat = jnp.exp(log_d - m_t)
        inter = jnp.exp(m_inter - m_t)
        sl = slice(h * HD, (h + 1) * HD)
        qh = q[:, sl]
        kh = k[:, sl]
        vh = v[:, sl].astype(BF16)
        qhb = qh.astype(BF16)
        s = lax.dot_general(qhb, kh.astype(BF16), (((1,), (1,)), ((), ())),
                            preferred_element_type=F32) * d_mat
        num = inter * _bdot(qhb, cs[h]) + _bdot(s, vh)
        den = inter * jnp.sum(qh * ns[h], axis=-1, keepdims=True) + jnp.sum(s, axis=-1, keepdims=True)
        hs.append(num / jnp.maximum(jnp.abs(den), jnp.exp(-m_t)))
        b_last = bc[last:last + 1, lf:lf + 1]
        lw_col = b_last - b_col + i_col
        lw_row = b_last - b_row + i_row
        m_new = jnp.maximum(b_last + m_prev, jnp.max(lw_row, axis=-1, keepdims=True))
        decay = jnp.exp(b_last + m_prev - m_new)
        w_col = jnp.exp(lw_col - m_new)
        w_row = jnp.exp(lw_row - m_new)
        new_c.append(decay * cs[h] + _bdot(kt[sl, :] * w_row, vh))
        new_n.append(decay * ns[h] + jnp.sum(kh * w_col, axis=0, keepdims=True))
        new_m.append(m_new)
    return hs, (tuple(new_c), tuple(new_n), tuple(new_m))


def _mlstm_kernel(zb_ref, zg_ref, gb_ref, ng_ref, c0_ref, n0_ref, m0_ref,
                  y_ref, c_ref, n_ref, m_ref, hf_ref):
    t_len = y_ref.shape[1]
    nc = t_len // CHUNK

    def init(direction):
        idx = [direction * HEADS + h for h in range(HEADS)]
        return (tuple(c0_ref[0, j] for j in idx),
                tuple(n0_ref[0, j:j + 1, :] for j in idx),
                tuple(m0_ref[0, j:j + 1, 0:1] for j in idx))

    def emit(direction, state):
        for h in range(HEADS):
            j = direction * HEADS + h
            c_ref[0, j] = state[0][h]
            n_ref[0, j:j + 1, :] = state[1][h]
            m_ref[0, j:j + 1, :] = jnp.broadcast_to(state[2][h], (1, LANE))

    def fwd_body(c, state):
        hs, state = _mlstm_chunk(zb_ref, zg_ref, gb_ref, c, 0, state)
        rows = pl.ds(pl.multiple_of(c * CHUNK, CHUNK), CHUNK)
        for h in range(HEADS):
            hf_ref[rows, h * HD:(h + 1) * HD] = hs[h]
        return state

    emit(0, lax.fori_loop(0, nc, fwd_body, init(0)))

    def bwd_body(i, state):
        c = nc - 1 - i
        hs, state = _mlstm_chunk(zb_ref, zg_ref, gb_ref, c, 1, state)
        rows = pl.ds(pl.multiple_of(c * CHUNK, CHUNK), CHUNK)
        o = zb_ref[0, rows, 3 * MIX:4 * MIX]
        for h in range(HEADS):
            sl = slice(h * HD, (h + 1) * HD)
            tot = hf_ref[rows, sl] + hs[h]
            y_ref[0, rows, sl] = _rms(tot) * ng_ref[:, sl] * _sigmoid(o[:, sl])
        return state

    emit(1, lax.fori_loop(0, nc, bwd_body, init(1)))


def _mlstm_mixer(zb, zg, gate_b, norm_g, c0, n0, m0):
    bsz, t_len, _ = zb.shape
    nd = 2 * HEADS
    return pl.pallas_call(
        _mlstm_kernel,
        out_shape=[jax.ShapeDtypeStruct((bsz, t_len, MIX), F32),
                   jax.ShapeDtypeStruct((bsz, nd, HD, HD), F32),
                   jax.ShapeDtypeStruct((bsz, nd, HD), F32),
                   jax.ShapeDtypeStruct((bsz, nd, LANE), F32)],
        grid=(bsz,),
        in_specs=[pl.BlockSpec((1, t_len, 4 * MIX), lambda i: (i, 0, 0)),
                  pl.BlockSpec((1, t_len, LANE), lambda i: (i, 0, 0)),
                  pl.BlockSpec((1, LANE), lambda i: (0, 0)),
                  pl.BlockSpec((1, MIX), lambda i: (0, 0)),
                  pl.BlockSpec((1, nd, HD, HD), lambda i: (i, 0, 0, 0)),
                  pl.BlockSpec((1, nd, HD), lambda i: (i, 0, 0)),
                  pl.BlockSpec((1, nd, LANE), lambda i: (i, 0, 0))],
        out_specs=[pl.BlockSpec((1, t_len, MIX), lambda i: (i, 0, 0)),
                   pl.BlockSpec((1, nd, HD, HD), lambda i: (i, 0, 0, 0)),
                   pl.BlockSpec((1, nd, HD), lambda i: (i, 0, 0)),
                   pl.BlockSpec((1, nd, LANE), lambda i: (i, 0, 0))],
        scratch_shapes=[pltpu.VMEM((t_len, MIX), F32)],
        compiler_params=_cp("parallel"),
        name="mlstm_mixer",
    )(zb, zg, gate_b, norm_g, c0, n0, m0)


_ATT_TQ = 256


def _mla_kernel(*refs, past):
    if past:
        (zc_ref, cos_ref, sin_ref, gq_ref, gkv_ref, wqa_ref, wqb_ref, wk_ref, wv_ref,
         pckv_ref, pkr_ref, y_ref, ckv_ref, q_sc, k_sc, v_sc) = refs
    else:
        (zc_ref, cos_ref, sin_ref, gq_ref, gkv_ref, wqa_ref, wqb_ref, wk_ref, wv_ref,
         y_ref, ckv_ref, q_sc, k_sc, v_sc) = refs
    t_len = y_ref.shape[1]
    cosf = cos_ref[...]
    sinf = sin_ref[...]
    cq = _rms(zc_ref[0, :, 0:Q_RANK]) * gq_ref[...]
    ckv = _rms(zc_ref[0, :, Q_RANK:Q_RANK + KV_RANK]) * gkv_ref[...]
    ckv_ref[0] = ckv
    kr = (zc_ref[0, :, Q_RANK + KV_RANK:Q_RANK + KV_RANK + LANE] * cosf
          + zc_ref[0, :, Q_RANK + KV_RANK + LANE:Q_RANK + KV_RANK + 2 * LANE] * sinf)
    cqb = cq.astype(BF16)
    ckvb = ckv.astype(BF16)
    for h in range(HEADS):
        sl = slice(h * LANE, (h + 1) * LANE)
        qa = jnp.dot(cqb, wqa_ref[:, sl], preferred_element_type=F32)
        qb = jnp.dot(cqb, wqb_ref[:, sl], preferred_element_type=F32)
        q_sc[:, sl] = (qa * cosf + qb * sinf).astype(BF16)
        kn = jnp.dot(ckvb, wk_ref[:, sl], preferred_element_type=F32)
        k_sc[past:past + t_len, sl] = (kn + kr).astype(BF16)
        v_sc[past:past + t_len, sl] = jnp.dot(ckvb, wv_ref[:, sl], preferred_element_type=F32).astype(BF16)
    if past:
        pckv = pckv_ref[0].astype(BF16)
        pkr = pkr_ref[0]
        for h in range(HEADS):
            sl = slice(h * LANE, (h + 1) * LANE)
            k_sc[0:past, sl] = (jnp.dot(pckv, wk_ref[:, sl], preferred_element_type=F32) + pkr).astype(BF16)
            v_sc[0:past, sl] = jnp.dot(pckv, wv_ref[:, sl], preferred_element_type=F32).astype(BF16)

    def q_block(i, carry):
        rows = pl.ds(pl.multiple_of(i * _ATT_TQ, _ATT_TQ), _ATT_TQ)
        outs = []
        for h in range(HEADS):
            sl = slice(h * LANE, (h + 1) * LANE)
            s = lax.dot_general(q_sc[rows, sl], k_sc[:, sl], (((1,), (1,)), ((), ())),
                                preferred_element_type=F32)
            p = jnp.exp(s - jnp.max(s, axis=-1, keepdims=True))
            o = jnp.dot(p.astype(BF16), v_sc[:, sl], preferred_element_type=F32)
            outs.append(o / jnp.sum(p, axis=-1, keepdims=True))
        y_ref[0, rows, 0:LANE] = outs[0] + outs[1]
        y_ref[0, rows, LANE:2 * LANE] = outs[2] + outs[3]
        return carry

    lax.fori_loop(0, t_len // _ATT_TQ, q_block, 0)


def _mla_mixer(zc, cosf, sinf, gq, gkv, wqa, wqb, wk, wv, past_ckv=None, past_kr=None):
    bsz, t_len, zw = zc.shape
    past = 0 if past_ckv is None else past_ckv.shape[1]
    full = lambda a: pl.BlockSpec(a.shape, lambda i: (0,) * a.ndim)
    args = [zc, cosf, sinf, gq, gkv, wqa, wqb, wk, wv]
    in_specs = [pl.BlockSpec((1, t_len, zw), lambda i: (i, 0, 0))] + [full(a) for a in args[1:]]
    if past:
        args += [past_ckv, past_kr]
        in_specs += [pl.BlockSpec((1, past, KV_RANK), lambda i: (i, 0, 0)),
                     pl.BlockSpec((1, past, LANE), lambda i: (i, 0, 0))]
    wide = HEADS * LANE
    return pl.pallas_call(
        functools.partial(_mla_kernel, past=past),
        out_shape=[jax.ShapeDtypeStruct((bsz, t_len, MIX), F32),
                   jax.ShapeDtypeStruct((bsz, t_len, KV_RANK), F32)],
        grid=(bsz,),
        in_specs=in_specs,
        out_specs=[pl.BlockSpec((1, t_len, MIX), lambda i: (i, 0, 0)),
                   pl.BlockSpec((1, t_len, KV_RANK), lambda i: (i, 0, 0))],
        scratch_shapes=[pltpu.VMEM((t_len, wide), BF16),
                        pltpu.VMEM((past + t_len, wide), BF16),
                        pltpu.VMEM((past + t_len, wide), BF16)],
        compiler_params=_cp("parallel"),
        name="mla_mixer",
    )(*args)


_S5_SEQ = 8
_S5_TT = 128


def _s5_kernel(zd_ref, bd_ref, cd_ref, are_ref, aim_ref, s0re_ref, s0im_ref,
               y_ref, fre_ref, fim_ref, bre_sc, bim_sc, sre_sc, sim_sc):
    direction = pl.program_id(0)
    k = pl.program_id(2)
    tt = _S5_TT
    rows = _S5_SEQ * tt

    @pl.when(k == 0)
    def _():
        sre_sc[...] = s0re_ref[0]
        sim_sc[...] = s0im_ref[0]

    u = zd_ref[...].reshape(rows, MIX).astype(BF16)
    bre_sc[...] = jnp.dot(u, bd_ref[0, :, 0:S5_STATES], preferred_element_type=F32)
    bim_sc[...] = jnp.dot(u, bd_ref[0, :, S5_STATES:2 * S5_STATES], preferred_element_type=F32)
    a_re = jnp.broadcast_to(are_ref[0], (_S5_SEQ, S5_STATES))
    a_im = jnp.broadcast_to(aim_ref[0], (_S5_SEQ, S5_STATES))

    def step(i, carry):
        s_re, s_im = carry
        t = i + direction * (tt - 1 - 2 * i)
        sel = pl.ds(pl.multiple_of(t * _S5_SEQ, _S5_SEQ), _S5_SEQ)
        n_re = a_re * s_re - a_im * s_im + bre_sc[sel, :]
        n_im = a_re * s_im + a_im * s_re + bim_sc[sel, :]
        bre_sc[sel, :] = n_re
        bim_sc[sel, :] = n_im
        return n_re, n_im

    s_re, s_im = lax.fori_loop(0, tt, step, (sre_sc[...], sim_sc[...]), unroll=2)
    sre_sc[...] = s_re
    sim_sc[...] = s_im
    y = (jnp.dot(bre_sc[...].astype(BF16), cd_ref[0, 0:S5_STATES, :], preferred_element_type=F32)
         + jnp.dot(bim_sc[...].astype(BF16), cd_ref[0, S5_STATES:2 * S5_STATES, :],
                   preferred_element_type=F32))
    y_ref[0] = y.reshape(tt, _S5_SEQ, MIX)

    @pl.when(k == pl.num_programs(2) - 1)
    def _():
        fre_ref[0] = s_re
        fim_ref[0] = s_im


def _s5_scan(zd, bd, cd, ab_re, ab_im, s0_re, s0_im):
    t_len, bsz, _ = zd.shape
    nt = t_len // _S5_TT
    tile = lambda d, k: k + d * (nt - 1 - 2 * k)
    st = pl.BlockSpec((1, _S5_SEQ, S5_STATES), lambda d, g, k: (d, g, 0))
    return pl.pallas_call(
        _s5_kernel,
        out_shape=[jax.ShapeDtypeStruct((2, t_len, bsz, MIX), F32),
                   jax.ShapeDtypeStruct((2, bsz, S5_STATES), F32),
                   jax.ShapeDtypeStruct((2, bsz, S5_STATES), F32)],
        grid=(2, bsz // _S5_SEQ, nt),
        in_specs=[pl.BlockSpec((_S5_TT, _S5_SEQ, MIX), lambda d, g, k: (tile(d, k), g, 0)),
                  pl.BlockSpec((1, MIX, 2 * S5_STATES), lambda d, g, k: (d, 0, 0)),
                  pl.BlockSpec((1, 2 * S5_STATES, MIX), lambda d, g, k: (d, 0, 0)),
                  pl.BlockSpec((1, 1, S5_STATES), lambda d, g, k: (d, 0, 0)),
                  pl.BlockSpec((1, 1, S5_STATES), lambda d, g, k: (d, 0, 0)),
                  st, st],
        out_specs=[pl.BlockSpec((1, _S5_TT, _S5_SEQ, MIX), lambda d, g, k: (d, tile(d, k), g, 0)),
                   st, st],
        scratch_shapes=[pltpu.VMEM((_S5_SEQ * _S5_TT, S5_STATES), F32),
                        pltpu.VMEM((_S5_SEQ * _S5_TT, S5_STATES), F32),
                        pltpu.VMEM((_S5_SEQ, S5_STATES), F32),
                        pltpu.VMEM((_S5_SEQ, S5_STATES), F32)],
        compiler_params=_cp("parallel", "parallel", "arbitrary"),
        name="s5_scan",
    )(zd, bd, cd, ab_re, ab_im, s0_re, s0_im)


def _gelu_tanh(x):
    return 0.5 * x * (1.0 + jnp.tanh(math.sqrt(2.0 / math.pi) * (x + 0.044715 * x * x * x)))


def _route(logits):
    lane = lax.broadcasted_iota(jnp.int32, logits.shape, 1).astype(F32)
    big = float(LANE)
    neg = -jnp.inf
    g_mask = (lane >= N_EXPERT) & (lane < N_EXPERT + N_GROUP)
    gl = jnp.where(g_mask, logits, neg)
    g_max = jnp.max(gl, axis=-1, keepdims=True)
    g_idx = jnp.min(jnp.where(gl == g_max, lane, big), axis=-1, keepdims=True) - N_EXPERT
    g_sel = 1.0 / jnp.sum(jnp.where(g_mask, jnp.exp(logits - g_max), 0.0), axis=-1, keepdims=True)
    lo = g_idx * PER_GROUP
    el = jnp.where((lane >= lo) & (lane < lo + PER_GROUP), logits, neg)
    v1 = jnp.max(el, axis=-1, keepdims=True)
    i1 = jnp.min(jnp.where(el == v1, lane, big), axis=-1, keepdims=True)
    el2 = jnp.where(lane == i1, neg, el)
    v2 = jnp.max(el2, axis=-1, keepdims=True)
    i2 = jnp.min(jnp.where(el2 == v2, lane, big), axis=-1, keepdims=True)
    e2 = jnp.exp(v2 - v1)
    w1 = g_sel / (1.0 + e2)
    return jnp.where(lane == i1, w1, jnp.where(lane == i2, w1 * e2, 0.0))


def _post_kernel(x_ref, ya_ref, yb_ref, yc_ref, ys_ref, zd_ref, mod_ref, wo_ref, d_ref, wglu_ref, bglu_ref,
                 g2_ref, wr_ref, br_ref, x1_ref, h2_ref, gate_ref):
    m = mod_ref[0]
    zd = zd_ref[...]
    ys = ys_ref[0] + ys_ref[1] + d_ref[...] * zd
    gl = _bdot(_gelu_tanh(ys), wglu_ref[...]) + bglu_ref[...]
    yd = gl[:, 0:MIX] * _sigmoid(gl[:, MIX:2 * MIX])
    mix = (_bdot(ya_ref[...], wo_ref[0:MIX, :]) + _bdot(yb_ref[...], wo_ref[MIX:2 * MIX, :])
           + _bdot(yc_ref[...], wo_ref[2 * MIX:3 * MIX, :]) + _bdot(yd, wo_ref[3 * MIX:4 * MIX, :]))
    x1 = x_ref[...] + m[2:3] * mix
    x1_ref[...] = x1
    h2 = _rms(x1) * g2_ref[...] * (1.0 + m[4:5]) + m[3:4]
    h2b = h2.astype(BF16)
    h2_ref[...] = h2b
    gate_ref[...] = _route(jnp.dot(h2b, wr_ref[...], preferred_element_type=F32) + br_ref[...])


def _post_mixer(x, ya, yb, yc, ys, zd, mod, row_fn, w_out, s5_d, w_glu, b_glu, norm2_g, w_router, b_router, tm):
    n = x.shape[0]
    tok = lambda w: pl.BlockSpec((tm, w), lambda i: (i, 0))
    full = lambda a: pl.BlockSpec(a.shape, lambda i: (0,) * a.ndim)
    return pl.pallas_call(
        _post_kernel,
        out_shape=[jax.ShapeDtypeStruct((n, D_MODEL), F32),
                   jax.ShapeDtypeStruct((n, D_MODEL), BF16),
                   jax.ShapeDtypeStruct((n, LANE), F32)],
        grid=(n // tm,),
        in_specs=[tok(D_MODEL), tok(MIX), tok(MIX), tok(MIX),
                  pl.BlockSpec((2, tm, MIX), lambda i: (0, i, 0)), tok(MIX),
                  pl.BlockSpec((1, 6, D_MODEL), lambda i: (row_fn(i, tm), 0, 0)),
                  full(w_out), full(s5_d), full(w_glu), full(b_glu), full(norm2_g),
                  full(w_router), full(b_router)],
        out_specs=[tok(D_MODEL), tok(D_MODEL), tok(LANE)],
        compiler_params=_cp("parallel"),
        name="post_mixer",
    )(x, ya, yb, yc, ys, zd, mod, w_out, s5_d, w_glu, b_glu, norm2_g, w_router, b_router)


_MOE_EPS = 4
_MOE_TM = 1024


_MOE_ALIGN = 16
_MOE_ROWS = 2 * _MOE_TM + N_EXPERT * _MOE_ALIGN
_MOE_CH = 128
_MOE_GB = 512
_MOE_CB = 256


def _moe_kernel(h_ref, gate_ref, x_ref, mod_ref, w1_ref, w3_ref, w2_ref, o_ref,
                xs_sc, col_sc, start_sm, rows_sm):
    j = pl.program_id(1)
    tm = h_ref.shape[0]

    @pl.when(j == 0)
    def _dispatch():
        gate = gate_ref[...]
        cnt_row = jnp.sum(jnp.where(gate != 0.0, 1.0, 0.0), axis=0, keepdims=True)
        units_row = jnp.floor((cnt_row + (_MOE_ALIGN - 1)) * (1.0 / _MOE_ALIGN))
        li = lax.broadcasted_iota(jnp.int32, (LANE, LANE), 0)
        lj = lax.broadcasted_iota(jnp.int32, (LANE, LANE), 1)
        before = jnp.where(li < lj, 1.0, 0.0).astype(BF16)
        start_row = jnp.dot(jnp.broadcast_to(units_row, (8, LANE)).astype(BF16), before,
                            preferred_element_type=F32) * _MOE_ALIGN
        start_i = start_row.astype(jnp.int32)
        cnt_i = cnt_row.astype(jnp.int32)
        for e in range(N_EXPERT):
            start_sm[e] = start_i[0, e]
            rows_sm[e] = cnt_i[0, e]
        gt = gate.T[0:N_EXPERT, :]
        chosen = gt != 0.0
        ones = jnp.where(chosen, 1.0, 0.0)
        cnt = jnp.sum(ones, axis=-1, keepdims=True)
        units = jnp.floor((cnt + (_MOE_ALIGN - 1)) * (1.0 / _MOE_ALIGN))
        ei = lax.broadcasted_iota(jnp.int32, (N_EXPERT, N_EXPERT), 0)
        ej = lax.broadcasted_iota(jnp.int32, (N_EXPERT, N_EXPERT), 1)
        start = jnp.dot(jnp.where(ej < ei, 1.0, 0.0).astype(BF16),
                        jnp.broadcast_to(units, (N_EXPERT, LANE)).astype(BF16),
                        preferred_element_type=F32)[:, 0:1] * _MOE_ALIGN
        ti = lax.broadcasted_iota(jnp.int32, (tm, tm), 0)
        tj = lax.broadcasted_iota(jnp.int32, (tm, tm), 1)
        earlier = jnp.where(ti < tj, 1.0, 0.0).astype(BF16)
        pos = start + jnp.dot(ones.astype(BF16), earlier, preferred_element_type=F32)
        p_a = jnp.min(jnp.where(chosen, pos, float(_MOE_ROWS)), axis=0, keepdims=True)
        p_b = jnp.max(jnp.where(chosen, pos, -1.0), axis=0, keepdims=True)
        g_a = jnp.sum(jnp.where(chosen & (pos == p_a), gt, 0.0), axis=0, keepdims=True)
        g_b = jnp.sum(jnp.where(chosen & (pos == p_b) & (p_b != p_a), gt, 0.0), axis=0, keepdims=True)
        x = h_ref[...]
        for r0 in range(0, _MOE_ROWS, _MOE_GB):
            ri = (lax.broadcasted_iota(jnp.int32, (_MOE_GB, tm), 0) + r0).astype(F32)
            sel = jnp.where((ri == p_a) | (ri == p_b), 1.0, 0.0).astype(BF16)
            xs_sc[r0:r0 + _MOE_GB, :] = jnp.dot(sel, x, preferred_element_type=F32).astype(BF16)
        xs_sc[_MOE_ROWS:_MOE_ROWS + _MOE_CH, :] = jnp.zeros((_MOE_CH, D_MODEL), BF16)
        packed = jnp.concatenate([p_a, p_b, g_a, g_b, jnp.zeros((LANE - 4, tm), F32)], axis=0)
        col_sc[...] = packed.T

    for el in range(_MOE_EPS):
        e = j * _MOE_EPS + el
        seg = start_sm[e]
        n_rows = rows_sm[e]

        def chunk(i, carry, el=el, seg=seg, n_rows=n_rows):
            rows = pl.ds(pl.multiple_of(seg + i * _MOE_CH, _MOE_ALIGN), _MOE_CH)
            xs = xs_sc[rows, :]
            h1 = jnp.dot(xs, w1_ref[el], preferred_element_type=F32)
            h3 = jnp.dot(xs, w3_ref[el], preferred_element_type=F32)
            y = jnp.dot((h1 * _sigmoid(h1) * h3).astype(BF16), w2_ref[el], preferred_element_type=F32)
            live = lax.broadcasted_iota(jnp.int32, (_MOE_CH, 1), 0) + i * _MOE_CH < n_rows
            xs_sc[rows, :] = jnp.where(live, y, xs.astype(F32)).astype(BF16)
            return carry

        lax.fori_loop(0, (n_rows + _MOE_CH - 1) // _MOE_CH, chunk, 0)

    @pl.when(j == pl.num_programs(1) - 1)
    def _combine():
        g2 = mod_ref[0][5:6]
        y = xs_sc[0:_MOE_ROWS, :]
        for t0 in range(0, tm, _MOE_CB):
            c = col_sc[t0:t0 + _MOE_CB, :]
            ri = lax.broadcasted_iota(jnp.int32, (_MOE_CB, _MOE_ROWS), 1).astype(F32)
            w = (jnp.where(ri == c[:, 0:1], c[:, 2:3], 0.0)
                 + jnp.where(ri == c[:, 1:2], c[:, 3:4], 0.0)).astype(BF16)
            o_ref[t0:t0 + _MOE_CB, :] = (x_ref[t0:t0 + _MOE_CB, :]
                                         + g2 * jnp.dot(w, y, preferred_element_type=F32))


def _moe(h2, gate, x1, mod, row_fn, w1, w3, w2, tm):
    n = h2.shape[0]
    return pl.pallas_call(
        _moe_kernel,
        out_shape=jax.ShapeDtypeStruct((n, D_MODEL), F32),
        grid=(n // tm, N_EXPERT // _MOE_EPS),
        in_specs=[pl.BlockSpec((tm, D_MODEL), lambda i, j: (i, 0)),
                  pl.BlockSpec((tm, LANE), lambda i, j: (i, 0)),
                  pl.BlockSpec((tm, D_MODEL), lambda i, j: (i, 0)),
                  pl.BlockSpec((1, 6, D_MODEL), lambda i, j: (row_fn(i, tm), 0, 0)),
                  pl.BlockSpec((_MOE_EPS, D_MODEL, MOE_FF), lambda i, j: (j, 0, 0)),
                  pl.BlockSpec((_MOE_EPS, D_MODEL, MOE_FF), lambda i, j: (j, 0, 0)),
                  pl.BlockSpec((_MOE_EPS, MOE_FF, D_MODEL), lambda i, j: (j, 0, 0))],
        out_specs=pl.BlockSpec((tm, D_MODEL), lambda i, j: (i, 0)),
        scratch_shapes=[pltpu.VMEM((_MOE_ROWS + _MOE_CH, D_MODEL), BF16),
                        pltpu.VMEM((tm, LANE), F32),
                        pltpu.SMEM((N_EXPERT,), jnp.int32),
                        pltpu.SMEM((N_EXPERT,), jnp.int32)],
        compiler_params=_cp("parallel", "arbitrary"),
        name="moe",
    )(h2, gate, x1, mod, w1, w3, w2)


def _final_kernel(x_ref, g_ref, o_ref):
    o_ref[...] = _rms(x_ref[...]) * g_ref[...]


def _final_norm(x, g, tm):
    n = x.shape[0]
    return pl.pallas_call(
        _final_kernel,
        out_shape=jax.ShapeDtypeStruct((n, D_MODEL), F32),
        grid=(n // tm,),
        in_specs=[pl.BlockSpec((tm, D_MODEL), lambda i: (i, 0)),
                  pl.BlockSpec((1, D_MODEL), lambda i: (0, 0))],
        out_specs=pl.BlockSpec((tm, D_MODEL), lambda i: (i, 0)),
        compiler_params=_cp("parallel"),
        name="final_norm",
    )(x, g)


def _rot_cols(w):
    q = ROPE // 4
    return jnp.concatenate([-w[..., q:2 * q], w[..., 0:q], -w[..., 3 * q:4 * q], w[..., 2 * q:3 * q]], axis=-1)


def _rope_slot(w):
    pad = [(0, 0)] * (w.ndim - 1) + [(HD, LANE - HD - ROPE)]
    return jnp.pad(w, pad)


def _prep_w_in(w_in, gate_cols):
    a = w_in[..., 0:512]
    qkvo = w_in[..., 512:1536]
    g = jnp.pad(w_in[..., 1536:1536 + gate_cols], [(0, 0), (0, 0), (0, LANE - gate_cols)])
    c0 = 1536 + gate_cols
    cq = w_in[..., c0:c0 + Q_RANK]
    ckv = w_in[..., c0 + Q_RANK:c0 + Q_RANK + KV_RANK]
    kr = w_in[..., c0 + Q_RANK + KV_RANK:c0 + Q_RANK + KV_RANK + ROPE]
    d = w_in[..., c0 + Q_RANK + KV_RANK + ROPE:]
    return jnp.concatenate([a, qkvo, g, cq, ckv, _rope_slot(kr), _rope_slot(_rot_cols(kr)), d],
                           axis=-1).astype(BF16)


def _prep_mla(w_uq, w_ukv):
    n_layer = w_uq.shape[0]
    scale = (HD + ROPE) ** -0.5
    wq = w_uq.reshape(n_layer, Q_RANK, HEADS, HD + ROPE) * scale
    nope, rope = wq[..., :HD], wq[..., HD:]
    zeros_r = jnp.zeros_like(rope)
    wqa = jnp.concatenate([nope, rope, zeros_r], axis=-1).reshape(n_layer, Q_RANK, HEADS * LANE)
    wqb = jnp.concatenate([jnp.zeros_like(nope), _rot_cols(rope), zeros_r], axis=-1)
    wqb = wqb.reshape(n_layer, Q_RANK, HEADS * LANE)
    wkv = w_ukv.reshape(n_layer, KV_RANK, HEADS, 2 * HD)
    k_nope, val = wkv[..., :HD], wkv[..., HD:]
    zeros_h = jnp.zeros_like(k_nope)
    wk = jnp.concatenate([k_nope, zeros_h], axis=-1).reshape(n_layer, KV_RANK, HEADS * LANE)
    even = jnp.concatenate([val, zeros_h], axis=-1)
    odd = jnp.concatenate([zeros_h, val], axis=-1)
    is_odd = (jnp.arange(HEADS) % 2 == 1)[None, None, :, None]
    wv = jnp.where(is_odd, odd, even).reshape(n_layer, KV_RANK, HEADS * LANE)
    return wqa.astype(BF16), wqb.astype(BF16), wk.astype(BF16), wv.astype(BF16)


def _rope_tables(t_len, rotate):
    ones = jnp.ones((t_len, HD), F32)
    zeros = jnp.zeros((t_len, HD), F32)
    tail = jnp.zeros((t_len, LANE - HD - ROPE), F32)
    if not rotate:
        return (jnp.concatenate([ones, jnp.ones((t_len, ROPE), F32), tail], axis=-1),
                jnp.zeros((t_len, LANE), F32))
    rows = t_len // GRID_W
    row = jnp.repeat(jnp.arange(rows, dtype=F32), GRID_W)
    col = jnp.tile(jnp.arange(GRID_W, dtype=F32), rows)
    nf = ROPE // 4
    inv = ROPE_BASE ** (-jnp.arange(nf, dtype=F32) / nf)
    ar = row[:, None] * inv
    ac = col[:, None] * inv
    cos = jnp.concatenate([jnp.cos(ar), jnp.cos(ar), jnp.cos(ac), jnp.cos(ac)], axis=-1)
    sin = jnp.concatenate([jnp.sin(ar), jnp.sin(ar), jnp.sin(ac), jnp.sin(ac)], axis=-1)
    return (jnp.concatenate([ones, cos, tail], axis=-1), jnp.concatenate([zeros, sin, tail], axis=-1))


def _prep_s5(bb_re, bb_im, c_re, c_im):
    eye = jnp.eye(S5_G, dtype=F32)
    to_b = lambda bb: jnp.einsum("ldgnc,gh->ldgchn", bb, eye).reshape(bb.shape[0], 2, MIX, S5_STATES)
    to_c = lambda cc: jnp.einsum("ldgcn,gh->ldgnhc", cc, eye).reshape(cc.shape[0], 2, S5_STATES, MIX)
    bd = jnp.concatenate([to_b(bb_re), to_b(bb_im)], axis=-1)
    cd = jnp.concatenate([to_c(c_re.astype(F32)), -to_c(c_im.astype(F32))], axis=-2)
    return bd.astype(BF16), cd.astype(BF16)


def _layer(x, bsz, t_len, mod, row_fn, p, ctx, tm):
    za, zb, zg, zc, zd = _pre_mixer(x, mod, row_fn, p["norm1_g"], p["w_in"], tm)
    seq = lambda a: a.reshape(bsz, t_len, a.shape[-1])
    ya = _conv_module(seq(za), p["conv_w"], p["conv_b"], p["conv_ln_g"], p["conv_ln_b"])
    yb, c_fin, n_fin, m_fin = _mlstm_mixer(seq(zb), seq(zg), p["gate_b"], p["mlstm_norm_g"],
                                           ctx["mlstm_c"], ctx["mlstm_n"], ctx["mlstm_m"])
    yc, ckv = _mla_mixer(seq(zc), ctx["cos"], ctx["sin"], p["mla_q_norm_g"], p["mla_kv_norm_g"],
                         p["wqa"], p["wqb"], p["wk"], p["wv"], ctx.get("past_ckv"), ctx.get("past_kr"))
    ys, s_re, s_im = _s5_scan(seq(zd).transpose(1, 0, 2), p["s5_bd"], p["s5_cd"], p["s5_ab_re"], p["s5_ab_im"],
                              ctx["s5_re"], ctx["s5_im"])
    ys = ys.transpose(0, 2, 1, 3)
    flat = lambda a: a.reshape(bsz * t_len, a.shape[-1])
    x1, h2, gate = _post_mixer(x, flat(ya), flat(yb), flat(yc), ys.reshape(2, bsz * t_len, MIX), zd, mod, row_fn,
                               p["w_out"], p["s5_d"], p["s5_w_glu"], p["s5_b_glu"], p["norm2_g"],
                               p["w_router"], p["b_router"], tm)
    x2 = _moe(h2, gate, x1, mod, row_fn, p["moe_w1"], p["moe_w3"], p["moe_w2"], _MOE_TM)
    krope = seq(zc)[:, :, Q_RANK + KV_RANK + HD:Q_RANK + KV_RANK + HD + ROPE]
    return x2, (ckv, krope, c_fin, n_fin, m_fin, s_re, s_im)


def kernel(x_prompt, x_sample, cache_mla_ckv, cache_mla_krope, state_mlstm_C, state_mlstm_n, state_mlstm_m, state_s5, c, c_ctx, norm1_g, norm2_g, final_g, w_mod, b_mod, w_in, w_out, conv_w, conv_b, conv_ln_g, conv_ln_b, mlstm_gate_b, mlstm_norm_g, mla_q_norm_g, mla_w_uq, mla_kv_norm_g, mla_w_ukv, s5_a_re, s5_a_im, s5_log_dt, s5_b_re, s5_b_im, s5_c_re, s5_c_im, s5_d, s5_w_glu, s5_b_glu, moe_w_group, moe_b_group, moe_w_expert, moe_b_expert, moe_w1, moe_w3, moe_w2):
    n_layer = w_in.shape[0]
    b_ctx, t_ctx, d = x_prompt.shape
    b_lat, t_lat, _ = x_sample.shape
    nd = 2 * HEADS
    tm = 512
    assert b_lat + 1 <= 16 and t_lat % _MOE_TM == 0 and (b_ctx * t_ctx) % _MOE_TM == 0

    c_all = jnp.zeros((16, d), F32).at[0].set(c_ctx).at[1:1 + b_lat].set(c)
    mod_all = _modulation(c_all, w_mod, b_mod).reshape(n_layer, 16, 6, d)
    gate_cols = 4 * HEADS
    w_in_ext = _prep_w_in(w_in, gate_cols)
    wqa, wqb, wk, wv = _prep_mla(mla_w_uq, mla_w_ukv)
    ab_re, ab_im, bb_re, bb_im = _s5_discretise(s5_a_re, s5_a_im, s5_log_dt, s5_b_re, s5_b_im)
    s5_bd, s5_cd = _prep_s5(bb_re, bb_im, s5_c_re, s5_c_im)
    w_router = jnp.pad(jnp.concatenate([moe_w_expert, moe_w_group], axis=-1),
                       [(0, 0), (0, 0), (0, LANE - N_EXPERT - N_GROUP)]).astype(BF16)
    b_router = jnp.pad(jnp.concatenate([moe_b_expert, moe_b_group], axis=-1),
                       [(0, 0), (0, LANE - N_EXPERT - N_GROUP)])
    gate_b = jnp.pad(mlstm_gate_b.reshape(n_layer, gate_cols), [(0, 0), (0, LANE - gate_cols)])
    conv_w_p = jnp.pad(conv_w, [(0, 0), (0, 32 - CONV_WIDTH), (0, 0)])
    w_out_b = w_out.astype(BF16)
    w_glu_b = s5_w_glu.astype(BF16)
    w1_b, w3_b, w2_b = moe_w1.astype(BF16), moe_w3.astype(BF16), moe_w2.astype(BF16)
    row = lambda a, l: a[l][None, :]

    cos_ctx, sin_ctx = _rope_tables(t_ctx, rotate=False)
    cos_lat, sin_lat = _rope_tables(t_lat, rotate=True)
    zero_state = dict(
        mlstm_c=jnp.zeros((b_ctx, nd, HD, HD), F32), mlstm_n=jnp.zeros((b_ctx, nd, HD), F32),
        mlstm_m=jnp.zeros((b_ctx, nd, LANE), F32),
        s5_re=jnp.zeros((2, b_ctx, S5_STATES), F32), s5_im=jnp.zeros((2, b_ctx, S5_STATES), F32),
        cos=cos_ctx, sin=sin_ctx)

    row_ctx = lambda i, tile: 0
    row_lat = lambda i, tile: 1 + (i * tile) // t_lat

    x_ctx = x_prompt.reshape(b_ctx * t_ctx, d)
    x_lat = x_sample.reshape(b_lat * t_lat, d)
    outs = []
    for l in range(n_layer):
        p = dict(norm1_g=row(norm1_g, l), norm2_g=row(norm2_g, l), w_in=w_in_ext[l], w_out=w_out_b[l],
                 conv_w=conv_w_p[l], conv_b=row(conv_b, l), conv_ln_g=row(conv_ln_g, l),
                 conv_ln_b=row(conv_ln_b, l), gate_b=row(gate_b, l), mlstm_norm_g=row(mlstm_norm_g, l),
                 mla_q_norm_g=row(mla_q_norm_g, l), mla_kv_norm_g=row(mla_kv_norm_g, l),
                 wqa=wqa[l], wqb=wqb[l], wk=wk[l], wv=wv[l],
                 s5_bd=s5_bd[l], s5_cd=s5_cd[l], s5_ab_re=ab_re[l][:, None, :], s5_ab_im=ab_im[l][:, None, :],
                 s5_d=row(s5_d, l), s5_w_glu=w_glu_b[l], s5_b_glu=row(s5_b_glu, l),
                 w_router=w_router[l], b_router=row(b_router, l),
                 moe_w1=w1_b[l], moe_w3=w3_b[l], moe_w2=w2_b[l])
        x_ctx, st = _layer(x_ctx, b_ctx, t_ctx, mod_all[l], row_ctx, p, zero_state, tm)
        outs.append(st)
        s5_l = state_s5[:, l].reshape(b_lat, 2, S5_STATES, 2)
        lat_state = dict(
            mlstm_c=state_mlstm_C[:, l].reshape(b_lat, nd, HD, HD),
            mlstm_n=state_mlstm_n[:, l].reshape(b_lat, nd, HD),
            mlstm_m=jnp.broadcast_to(state_mlstm_m[:, l].reshape(b_lat, nd, 1), (b_lat, nd, LANE)),
            s5_re=s5_l[..., 0].transpose(1, 0, 2), s5_im=s5_l[..., 1].transpose(1, 0, 2),
            cos=cos_lat, sin=sin_lat,
            past_ckv=cache_mla_ckv[:, l], past_kr=_rope_slot(cache_mla_krope[:, l]))
        x_lat, _ = _layer(x_lat, b_lat, t_lat, mod_all[l], row_lat, p, lat_state, tm)

    y_prompt = _final_norm(x_ctx, final_g[None, :], tm).reshape(b_ctx, t_ctx, d)
    y_sample = _final_norm(x_lat, final_g[None, :], tm).reshape(b_lat, t_lat, d)
    stack = lambda i: jnp.stack([o[i] for o in outs], axis=1)
    new_ckv = stack(0)
    new_krope = stack(1)
    new_c = stack(2).reshape(b_ctx, n_layer, 2, HEADS, HD, HD)
    new_n = stack(3).reshape(b_ctx, n_layer, 2, HEADS, HD)
    new_m = stack(4)[..., 0].reshape(b_ctx, n_layer, 2, HEADS)
    s_re = jnp.stack([o[5] for o in outs], axis=0)
    s_im = jnp.stack([o[6] for o in outs], axis=0)
    new_s5 = jnp.stack([s_re, s_im], axis=-1).transpose(2, 0, 1, 3, 4)
    new_s5 = new_s5.reshape(b_ctx, n_layer, 2, S5_G, S5_N, 2)
    return (y_prompt, y_sample, new_ckv, new_krope, new_c, new_n, new_m, new_s5)
```

```python
import functools
import math

import jax
import jax.numpy as jnp
from jax import lax
from jax.experimental import pallas as pl
from jax.experimental.pallas import tpu as pltpu

F32 = jnp.float32
BF16 = jnp.bfloat16
EPS = 1e-6

D_MODEL = 1024
MIX = 256
CONV_WIDTH = 31
HEADS = 4
HD = 64
CHUNK = 128
ROPE = 32
KV_RANK = 128
Q_RANK = 256
GRID_W = 64
ROPE_BASE = 10000.0
S5_G = 16
S5_GC = 16
S5_N = 64
S5_STATES = S5_G * S5_N
N_EXPERT = 32
PER_GROUP = 8
N_GROUP = 4
MOE_FF = 256
LANE = 128
Z_COLS = 2688
VMEM_LIMIT = 56 * 1024 * 1024


def _cp(*sem):
    return pltpu.CompilerParams(dimension_semantics=sem, vmem_limit_bytes=VMEM_LIMIT)


def _rms(x):
    return x * lax.rsqrt(jnp.mean(x * x, axis=-1, keepdims=True) + EPS)


def _sigmoid(x):
    return 1.0 / (1.0 + jnp.exp(-x))


def _bdot(a, b):
    return jnp.dot(a.astype(BF16), b.astype(BF16), preferred_element_type=F32)


def _mod_kernel(c_ref, w_ref, b_ref, o_ref):
    c = c_ref[...]
    o_ref[0] = _bdot(c * _sigmoid(c), w_ref[0]) + b_ref[0]


def _modulation(c_all, w_mod, b_mod):
    n_layer, d, n = w_mod.shape
    tn = 1536
    return pl.pallas_call(
        _mod_kernel,
        out_shape=jax.ShapeDtypeStruct((n_layer, 16, n), F32),
        grid=(n_layer, n // tn),
        in_specs=[pl.BlockSpec((16, d), lambda l, j: (0, 0)),
                  pl.BlockSpec((1, d, tn), lambda l, j: (l, 0, j)),
                  pl.BlockSpec((1, 1, tn), lambda l, j: (l, 0, j))],
        out_specs=pl.BlockSpec((1, 16, tn), lambda l, j: (l, 0, j)),
        compiler_params=_cp("parallel", "parallel"),
        name="modulation",
    )(c_all, w_mod, b_mod.reshape(n_layer, 1, n))


def _s5_disc_kernel(are_ref, aim_ref, ldt_ref, bre_ref, bim_ref, abre_ref, abim_ref, bbre_ref, bbim_ref):
    a_re = are_ref[...]
    a_im = aim_ref[...]
    dt = jnp.exp(ldt_ref[...])
    mag = jnp.exp(a_re * dt)
    ab_re = mag * jnp.cos(a_im * dt)
    ab_im = mag * jnp.sin(a_im * dt)
    den = a_re * a_re + a_im * a_im
    f_re = ((ab_re - 1.0) * a_re + ab_im * a_im) / den
    f_im = (ab_im * a_re - (ab_re - 1.0) * a_im) / den
    b_re = bre_ref[...]
    b_im = bim_ref[...]
    abre_ref[...] = ab_re
    abim_ref[...] = ab_im
    bbre_ref[...] = f_re * b_re - f_im * b_im
    bbim_ref[...] = f_re * b_im + f_im * b_re


def _s5_discretise(a_re, a_im, log_dt, b_re, b_im):
    n_layer = a_re.shape[0]
    rows = n_layer * 2 * S5_G
    cols = S5_N * S5_GC
    rep = lambda a: jnp.repeat(a.reshape(rows, S5_N), S5_GC, axis=1)
    ldt = jnp.broadcast_to(log_dt.reshape(rows, 1), (rows, cols))
    spec = pl.BlockSpec((rows, cols), lambda: (0, 0))
    ab_re, ab_im, bb_re, bb_im = pl.pallas_call(
        _s5_disc_kernel,
        out_shape=[jax.ShapeDtypeStruct((rows, cols), F32)] * 4,
        in_specs=[spec] * 5,
        out_specs=[spec] * 4,
        name="s5_discretise",
    )(rep(a_re), rep(a_im), ldt, b_re.reshape(rows, cols), b_im.reshape(rows, cols))
    pick = lambda a: a[:, ::S5_GC].reshape(n_layer, 2, S5_STATES)
    shp = (n_layer, 2, S5_G, S5_N, S5_GC)
    return pick(ab_re), pick(ab_im), bb_re.reshape(shp), bb_im.reshape(shp)


def _pre_kernel(x_ref, mod_ref, g_ref, w_ref, za_ref, zb_ref, zg_ref, zc_ref, zd_ref):
    m = mod_ref[0]
    h = _rms(x_ref[...]) * g_ref[...] * (1.0 + m[1:2]) + m[0:1]
    hb = h.astype(BF16)
    col = 0
    for o_ref in (za_ref, zb_ref, zg_ref, zc_ref, zd_ref):
        n = o_ref.shape[-1]
        o_ref[...] = jnp.dot(hb, w_ref[:, col:col + n], preferred_element_type=F32)
        col += n


def _pre_mixer(x, mod, row_fn, norm_g, w_in_ext, tm):
    n = x.shape[0]
    widths = (512, 1024, 2 * LANE, 640, MIX)
    return pl.pallas_call(
        _pre_kernel,
        out_shape=[jax.ShapeDtypeStruct((n, w), F32) for w in widths],
        grid=(n // tm,),
        in_specs=[pl.BlockSpec((tm, D_MODEL), lambda i: (i, 0)),
                  pl.BlockSpec((1, 6, D_MODEL), lambda i: (row_fn(i, tm), 0, 0)),
                  pl.BlockSpec((1, D_MODEL), lambda i: (0, 0)),
                  pl.BlockSpec((D_MODEL, Z_COLS), lambda i: (0, 0))],
        out_specs=[pl.BlockSpec((tm, w), lambda i: (i, 0)) for w in widths],
        compiler_params=_cp("parallel"),
        name="pre_mixer",
    )(x, mod, norm_g, w_in_ext)


_CONV_PAD = 16
_CONV_TT = 128


def _conv_kernel(za_ref, w_ref, b_ref, lg_ref, lb_ref, o_ref, hp_ref):
    t_len = o_ref.shape[1]
    u = za_ref[0]
    hp_ref[0:_CONV_PAD, :] = jnp.zeros((_CONV_PAD, MIX), F32)
    hp_ref[_CONV_PAD + t_len:2 * _CONV_PAD + t_len, :] = jnp.zeros((_CONV_PAD, MIX), F32)
    hp_ref[_CONV_PAD:_CONV_PAD + t_len, :] = u[:, :MIX] * _sigmoid(u[:, MIX:])
    w = w_ref[...]
    half = CONV_WIDTH // 2
    for t0 in range(0, t_len, _CONV_TT):
        acc = jnp.zeros((_CONV_TT, MIX), F32) + b_ref[...]
        for k in range(CONV_WIDTH):
            start = t0 + _CONV_PAD - half + k
            acc = acc + hp_ref[start:start + _CONV_TT, :] * w[k:k + 1, :]
        mu = jnp.mean(acc, axis=-1, keepdims=True)
        cen = acc - mu
        var = jnp.mean(cen * cen, axis=-1, keepdims=True)
        yn = cen * lax.rsqrt(var + EPS) * lg_ref[...] + lb_ref[...]
        o_ref[0, t0:t0 + _CONV_TT, :] = yn * _sigmoid(yn)


def _conv_module(za, w, b, ln_g, ln_b):
    bsz, t_len, _ = za.shape
    vec = pl.BlockSpec((1, MIX), lambda i: (0, 0))
    return pl.pallas_call(
        _conv_kernel,
        out_shape=jax.ShapeDtypeStruct((bsz, t_len, MIX), F32),
        grid=(bsz,),
        in_specs=[pl.BlockSpec((1, t_len, 2 * MIX), lambda i: (i, 0, 0)),
                  pl.BlockSpec((32, MIX), lambda i: (0, 0)), vec, vec, vec],
        out_specs=pl.BlockSpec((1, t_len, MIX), lambda i: (i, 0, 0)),
        scratch_shapes=[pltpu.VMEM((t_len + 2 * _CONV_PAD, MIX), F32)],
        compiler_params=_cp("parallel"),
        name="conv_module",
    )(za, w, b, ln_g, ln_b)


def _log_sigmoid(x):
    return jnp.minimum(x, 0.0) - jnp.log(1.0 + jnp.exp(-jnp.abs(x)))


def _split3_dot(tri, x):
    hi = x.astype(BF16)
    r1 = x - hi.astype(F32)
    mid = r1.astype(BF16)
    lo = (r1 - mid.astype(F32)).astype(BF16)
    dot = lambda v: jnp.dot(tri, v, preferred_element_type=F32)
    return dot(hi) + dot(mid) + dot(lo)


def _split2(x):
    hi = x.astype(BF16)
    return hi, (x - hi.astype(F32)).astype(BF16)


def _mlstm_chunk(zb_ref, zg_ref, gb_ref, qt_sc, vt_sc, c, direction, state):
    L = CHUNK
    rows = pl.ds(pl.multiple_of(c * L, L), L)
    si = lax.broadcasted_iota(jnp.int32, (L, L), 0)
    ti = lax.broadcasted_iota(jnp.int32, (L, L), 1)
    if direction == 0:
        valid = si <= ti
        tri = jnp.where(ti <= si, 1.0, 0.0).astype(BF16)
        last = L - 1
    else:
        valid = si >= ti
        tri = jnp.where(ti >= si, 1.0, 0.0).astype(BF16)
        last = 0
    g_in = zg_ref[0, rows, 0:LANE] + gb_ref[:, 0:LANE]
    g_f = zg_ref[0, rows, LANE:2 * LANE] + gb_ref[:, LANE:2 * LANE]
    bc = _split3_dot(tri, _log_sigmoid(g_f))
    r = g_in - bc
    g_in_t = g_in.T
    bct = bc.T
    k = zb_ref[0, rows, MIX:2 * MIX]
    k_hi, k_lo = _split2(k)
    cts, ns, ms = state
    new_c, new_n, new_m, hts = [], [], [], []
    for h in range(HEADS):
        j = direction * HEADS + h
        sl = slice(h * HD, (h + 1) * HD)
        qt = qt_sc[c, sl, :]
        vt = vt_sc[c, sl, :]
        i_row = g_in_t[j:j + 1, :]
        b_row = bct[j:j + 1, :]
        m_prev = ms[h]
        rb = jnp.where(valid, jnp.broadcast_to(r[:, j:j + 1], (L, L)), -jnp.inf)
        c_row = jnp.maximum(m_prev, jnp.max(rb, axis=0, keepdims=True))
        inter = jnp.exp(m_prev - c_row)
        st = jnp.dot(k_hi[:, sl], qt, preferred_element_type=F32) * jnp.exp(rb - c_row)
        num = (inter * jnp.dot(cts[h].astype(BF16), qt, preferred_element_type=F32)
               + jnp.dot(vt.astype(BF16), st.astype(BF16), preferred_element_type=F32))
        n_hi, n_lo = _split2(jnp.broadcast_to(ns[h], (8, HD)))
        qn = (jnp.dot(n_hi, qt, preferred_element_type=F32)
              + jnp.dot(n_lo, qt, preferred_element_type=F32))[0:1, :]
        den = inter * qn + jnp.sum(st, axis=0, keepdims=True)
        hts.append(num / jnp.maximum(jnp.abs(den), jnp.exp(-(b_row + c_row))))
        b_last = b_row[:, last:last + 1]
        lw = b_last - b_row + i_row
        m_new = jnp.maximum(b_last + m_prev, jnp.max(lw, axis=-1, keepdims=True))
        decay = jnp.exp(b_last + m_prev - m_new)
        w = jnp.exp(lw - m_new)
        new_c.append(decay * cts[h] + jnp.dot((vt * w).astype(BF16), k_hi[:, sl], preferred_element_type=F32))
        w_hi, w_lo = _split2(jnp.broadcast_to(w, (8, L)))
        wk = (jnp.dot(w_hi, k_hi[:, sl], preferred_element_type=F32)
              + jnp.dot(w_lo, k_hi[:, sl], preferred_element_type=F32)
              + jnp.dot(w_hi, k_lo[:, sl], preferred_element_type=F32))[0:1, :]
        new_n.append(decay * ns[h] + wk)
        new_m.append(m_new)
    return hts, (tuple(new_c), tuple(new_n), tuple(new_m))


def _mlstm_kernel(zb_ref, zg_ref, gb_ref, ng_ref, c0_ref, n0_ref, m0_ref,
                  y_ref, c_ref, n_ref, m_ref, qt_sc, vt_sc, ht_sc):
    t_len = y_ref.shape[1]
    nc = t_len // CHUNK

    def transpose_chunk(c, carry):
        rows = pl.ds(pl.multiple_of(c * CHUNK, CHUNK), CHUNK)
        qt_sc[c] = (zb_ref[0, rows, 0:MIX] * (HD ** -0.5)).T.astype(BF16)
        vt_sc[c] = zb_ref[0, rows, 2 * MIX:3 * MIX].T
        return carry

    lax.fori_loop(0, nc, transpose_chunk, 0)

    def init(direction):
        idx = [direction * HEADS + h for h in range(HEADS)]
        return (tuple(c0_ref[0, j] for j in idx),
                tuple(n0_ref[0, j:j + 1, :] for j in idx),
                tuple(m0_ref[0, j:j + 1, 0:1] for j in idx))

    def emit(direction, state):
        for h in range(HEADS):
            j = direction * HEADS + h
            c_ref[0, j] = state[0][h]
            n_ref[0, j:j + 1, :] = state[1][h]
            m_ref[0, j:j + 1, :] = jnp.broadcast_to(state[2][h], (1, LANE))

    def scan_body(i, states):
        cb = nc - 1 - i
        hts_f, state_f = _mlstm_chunk(zb_ref, zg_ref, gb_ref, qt_sc, vt_sc, i, 0, states[0])
        hts_b, state_b = _mlstm_chunk(zb_ref, zg_ref, gb_ref, qt_sc, vt_sc, cb, 1, states[1])
        for h in range(HEADS):
            ht_sc[0, i, h * HD:(h + 1) * HD, :] = hts_f[h]
            ht_sc[1, cb, h * HD:(h + 1) * HD, :] = hts_b[h]
        return state_f, state_b

    state_f, state_b = lax.fori_loop(0, nc, scan_body, (init(0), init(1)))
    emit(0, state_f)
    emit(1, state_b)

    def out_body(c, carry):
        rows = pl.ds(pl.multiple_of(c * CHUNK, CHUNK), CHUNK)
        normed = []
        for h in range(HEADS):
            tot = ht_sc[0, c, h * HD:(h + 1) * HD, :] + ht_sc[1, c, h * HD:(h + 1) * HD, :]
            normed.append(tot * lax.rsqrt(jnp.mean(tot * tot, axis=0, keepdims=True) + EPS))
        hn = jnp.concatenate(normed, axis=0).T
        y_ref[0, rows, :] = hn * ng_ref[...] * _sigmoid(zb_ref[0, rows, 3 * MIX:4 * MIX])
        return carry

    lax.fori_loop(0, nc, out_body, 0)


def _mlstm_mixer(zb, zg, gate_b, norm_g, c0, n0, m0):
    bsz, t_len, _ = zb.shape
    nd = 2 * HEADS
    return pl.pallas_call(
        _mlstm_kernel,
        out_shape=[jax.ShapeDtypeStruct((bsz, t_len, MIX), F32),
                   jax.ShapeDtypeStruct((bsz, nd, HD, HD), F32),
                   jax.ShapeDtypeStruct((bsz, nd, HD), F32),
                   jax.ShapeDtypeStruct((bsz, nd, LANE), F32)],
        grid=(bsz,),
        in_specs=[pl.BlockSpec((1, t_len, 4 * MIX), lambda i: (i, 0, 0)),
                  pl.BlockSpec((1, t_len, 2 * LANE), lambda i: (i, 0, 0)),
                  pl.BlockSpec((1, 2 * LANE), lambda i: (0, 0)),
                  pl.BlockSpec((1, MIX), lambda i: (0, 0)),
                  pl.BlockSpec((1, nd, HD, HD), lambda i: (i, 0, 0, 0)),
                  pl.BlockSpec((1, nd, HD), lambda i: (i, 0, 0)),
                  pl.BlockSpec((1, nd, LANE), lambda i: (i, 0, 0))],
        out_specs=[pl.BlockSpec((1, t_len, MIX), lambda i: (i, 0, 0)),
                   pl.BlockSpec((1, nd, HD, HD), lambda i: (i, 0, 0, 0)),
                   pl.BlockSpec((1, nd, HD), lambda i: (i, 0, 0)),
                   pl.BlockSpec((1, nd, LANE), lambda i: (i, 0, 0))],
        scratch_shapes=[pltpu.VMEM((t_len // CHUNK, MIX, CHUNK), BF16),
                        pltpu.VMEM((t_len // CHUNK, MIX, CHUNK), F32),
                        pltpu.VMEM((2, t_len // CHUNK, MIX, CHUNK), F32)],
        compiler_params=_cp("parallel"),
        name="mlstm_mixer",
    )(zb, zg, gate_b, norm_g, c0, n0, m0)


_ATT_TQ = 256


def _mla_kernel(*refs, past):
    if past:
        (zc_ref, cos_ref, sin_ref, gq_ref, gkv_ref, wqa_ref, wqb_ref, wk_ref, wv_ref,
         pckv_ref, pkr_ref, y_ref, ckv_ref, q_sc, k_sc, v_sc) = refs
    else:
        (zc_ref, cos_ref, sin_ref, gq_ref, gkv_ref, wqa_ref, wqb_ref, wk_ref, wv_ref,
         y_ref, ckv_ref, q_sc, k_sc, v_sc) = refs
    t_len = y_ref.shape[1]
    cosf = cos_ref[...]
    sinf = sin_ref[...]
    cq = _rms(zc_ref[0, :, 0:Q_RANK]) * gq_ref[...]
    ckv = _rms(zc_ref[0, :, Q_RANK:Q_RANK + KV_RANK]) * gkv_ref[...]
    ckv_ref[0] = ckv
    kr = (zc_ref[0, :, Q_RANK + KV_RANK:Q_RANK + KV_RANK + LANE] * cosf
          + zc_ref[0, :, Q_RANK + KV_RANK + LANE:Q_RANK + KV_RANK + 2 * LANE] * sinf)
    cqb = cq.astype(BF16)
    ckvb = ckv.astype(BF16)
    for h in range(HEADS):
        sl = slice(h * LANE, (h + 1) * LANE)
        qa = jnp.dot(cqb, wqa_ref[:, sl], preferred_element_type=F32)
        qb = jnp.dot(cqb, wqb_ref[:, sl], preferred_element_type=F32)
        q_sc[:, sl] = (qa * cosf + qb * sinf).astype(BF16)
        kn = jnp.dot(ckvb, wk_ref[:, sl], preferred_element_type=F32)
        k_sc[past:past + t_len, sl] = (kn + kr).astype(BF16)
        v_sc[past:past + t_len, sl] = jnp.dot(ckvb, wv_ref[:, sl], preferred_element_type=F32).astype(BF16)
    if past:
        pckv = pckv_ref[0].astype(BF16)
        pkr = pkr_ref[0]
        for h in range(HEADS):
            sl = slice(h * LANE, (h + 1) * LANE)
            k_sc[0:past, sl] = (jnp.dot(pckv, wk_ref[:, sl], preferred_element_type=F32) + pkr).astype(BF16)
            v_sc[0:past, sl] = jnp.dot(pckv, wv_ref[:, sl], preferred_element_type=F32).astype(BF16)

    def q_block(i, carry):
        rows = pl.ds(pl.multiple_of(i * _ATT_TQ, _ATT_TQ), _ATT_TQ)
        outs = []
        for h in range(HEADS):
            sl = slice(h * LANE, (h + 1) * LANE)
            s = lax.dot_general(q_sc[rows, sl], k_sc[:, sl], (((1,), (1,)), ((), ())),
                                preferred_element_type=F32)
            p = jnp.exp(s - jnp.max(s, axis=-1, keepdims=True))
            o = jnp.dot(p.astype(BF16), v_sc[:, sl], preferred_element_type=F32)
            outs.append(o / jnp.sum(p, axis=-1, keepdims=True))
        y_ref[0, rows, 0:LANE] = outs[0] + outs[1]
        y_ref[0, rows, LANE:2 * LANE] = outs[2] + outs[3]
        return carry

    lax.fori_loop(0, t_len // _ATT_TQ, q_block, 0)


def _mla_mixer(zc, cosf, sinf, gq, gkv, wqa, wqb, wk, wv, past_ckv=None, past_kr=None):
    bsz, t_len, zw = zc.shape
    past = 0 if past_ckv is None else past_ckv.shape[1]
    full = lambda a: pl.BlockSpec(a.shape, lambda i: (0,) * a.ndim)
    args = [zc, cosf, sinf, gq, gkv, wqa, wqb, wk, wv]
    in_specs = [pl.BlockSpec((1, t_len, zw), lambda i: (i, 0, 0))] + [full(a) for a in args[1:]]
    if past:
        args += [past_ckv, past_kr]
        in_specs += [pl.BlockSpec((1, past, KV_RANK), lambda i: (i, 0, 0)),
                     pl.BlockSpec((1, past, LANE), lambda i: (i, 0, 0))]
    wide = HEADS * LANE
    return pl.pallas_call(
        functools.partial(_mla_kernel, past=past),
        out_shape=[jax.ShapeDtypeStruct((bsz, t_len, MIX), F32),
                   jax.ShapeDtypeStruct((bsz, t_len, KV_RANK), F32)],
        grid=(bsz,),
        in_specs=in_specs,
        out_specs=[pl.BlockSpec((1, t_len, MIX), lambda i: (i, 0, 0)),
                   pl.BlockSpec((1, t_len, KV_RANK), lambda i: (i, 0, 0))],
        scratch_shapes=[pltpu.VMEM((t_len, wide), BF16),
                        pltpu.VMEM((past + t_len, wide), BF16),
                        pltpu.VMEM((past + t_len, wide), BF16)],
        compiler_params=_cp("parallel"),
        name="mla_mixer",
    )(*args)


_S5_SEQ = 8
_S5_TT = 128


def _s5_kernel(zd_ref, bd_ref, cd_ref, are_ref, aim_ref, s0re_ref, s0im_ref,
               y_ref, fre_ref, fim_ref, bre_sc, bim_sc, sre_sc, sim_sc):
    direction = pl.program_id(0)
    k = pl.program_id(2)
    tt = _S5_TT
    rows = _S5_SEQ * tt

    @pl.when(k == 0)
    def _():
        sre_sc[...] = s0re_ref[0]
        sim_sc[...] = s0im_ref[0]

    u = zd_ref[...].reshape(rows, MIX).astype(BF16)
    bre_sc[...] = jnp.dot(u, bd_ref[0, :, 0:S5_STATES], preferred_element_type=F32)
    bim_sc[...] = jnp.dot(u, bd_ref[0, :, S5_STATES:2 * S5_STATES], preferred_element_type=F32)
    a_re = jnp.broadcast_to(are_ref[0], (_S5_SEQ, S5_STATES))
    a_im = jnp.broadcast_to(aim_ref[0], (_S5_SEQ, S5_STATES))

    def step(i, carry):
        s_re, s_im = carry
        t = i + direction * (tt - 1 - 2 * i)
        sel = pl.ds(pl.multiple_of(t * _S5_SEQ, _S5_SEQ), _S5_SEQ)
        n_re = a_re * s_re - a_im * s_im + bre_sc[sel, :]
        n_im = a_re * s_im + a_im * s_re + bim_sc[sel, :]
        bre_sc[sel, :] = n_re
        bim_sc[sel, :] = n_im
        return n_re, n_im

    s_re, s_im = lax.fori_loop(0, tt, step, (sre_sc[...], sim_sc[...]), unroll=2)
    sre_sc[...] = s_re
    sim_sc[...] = s_im
    y = (jnp.dot(bre_sc[...].astype(BF16), cd_ref[0, 0:S5_STATES, :], preferred_element_type=F32)
         + jnp.dot(bim_sc[...].astype(BF16), cd_ref[0, S5_STATES:2 * S5_STATES, :],
                   preferred_element_type=F32))
    y_ref[0] = y.reshape(tt, _S5_SEQ, MIX)

    @pl.when(k == pl.num_programs(2) - 1)
    def _():
        fre_ref[0] = s_re
        fim_ref[0] = s_im


def _s5_scan(zd, bd, cd, ab_re, ab_im, s0_re, s0_im):
    t_len, bsz, _ = zd.shape
    nt = t_len // _S5_TT
    tile = lambda d, k: k + d * (nt - 1 - 2 * k)
    st = pl.BlockSpec((1, _S5_SEQ, S5_STATES), lambda d, g, k: (d, g, 0))
    return pl.pallas_call(
        _s5_kernel,
        out_shape=[jax.ShapeDtypeStruct((2, t_len, bsz, MIX), F32),
                   jax.ShapeDtypeStruct((2, bsz, S5_STATES), F32),
                   jax.ShapeDtypeStruct((2, bsz, S5_STATES), F32)],
        grid=(2, bsz // _S5_SEQ, nt),
        in_specs=[pl.BlockSpec((_S5_TT, _S5_SEQ, MIX), lambda d, g, k: (tile(d, k), g, 0)),
                  pl.BlockSpec((1, MIX, 2 * S5_STATES), lambda d, g, k: (d, 0, 0)),
                  pl.BlockSpec((1, 2 * S5_STATES, MIX), lambda d, g, k: (d, 0, 0)),
                  pl.BlockSpec((1, 1, S5_STATES), lambda d, g, k: (d, 0, 0)),
                  pl.BlockSpec((1, 1, S5_STATES), lambda d, g, k: (d, 0, 0)),
                  st, st],
        out_specs=[pl.BlockSpec((1, _S5_TT, _S5_SEQ, MIX), lambda d, g, k: (d, tile(d, k), g, 0)),
                   st, st],
        scratch_shapes=[pltpu.VMEM((_S5_SEQ * _S5_TT, S5_STATES), F32),
                        pltpu.VMEM((_S5_SEQ * _S5_TT, S5_STATES), F32),
                        pltpu.VMEM((_S5_SEQ, S5_STATES), F32),
                        pltpu.VMEM((_S5_SEQ, S5_STATES), F32)],
        compiler_params=_cp("parallel", "parallel", "arbitrary"),
        name="s5_scan",
    )(zd, bd, cd, ab_re, ab_im, s0_re, s0_im)


def _gelu_tanh(x):
    return 0.5 * x * (1.0 + jnp.tanh(math.sqrt(2.0 / math.pi) * (x + 0.044715 * x * x * x)))


def _route(logits):
    lane = lax.broadcasted_iota(jnp.int32, logits.shape, 1).astype(F32)
    big = float(LANE)
    neg = -jnp.inf
    g_mask = (lane >= N_EXPERT) & (lane < N_EXPERT + N_GROUP)
    gl = jnp.where(g_mask, logits, neg)
    g_max = jnp.max(gl, axis=-1, keepdims=True)
    g_idx = jnp.min(jnp.where(gl == g_max, lane, big), axis=-1, keepdims=True) - N_EXPERT
    g_sel = 1.0 / jnp.sum(jnp.where(g_mask, jnp.exp(logits - g_max), 0.0), axis=-1, keepdims=True)
    lo = g_idx * PER_GROUP
    el = jnp.where((lane >= lo) & (lane < lo + PER_GROUP), logits, neg)
    v1 = jnp.max(el, axis=-1, keepdims=True)
    i1 = jnp.min(jnp.where(el == v1, lane, big), axis=-1, keepdims=True)
    el2 = jnp.where(lane == i1, neg, el)
    v2 = jnp.max(el2, axis=-1, keepdims=True)
    i2 = jnp.min(jnp.where(el2 == v2, lane, big), axis=-1, keepdims=True)
    e2 = jnp.exp(v2 - v1)
    w1 = g_sel / (1.0 + e2)
    return jnp.where(lane == i1, w1, jnp.where(lane == i2, w1 * e2, 0.0))


def _post_kernel(x_ref, ya_ref, yb_ref, yc_ref, ys_ref, zd_ref, mod_ref, wo_ref, d_ref, wglu_ref, bglu_ref,
                 g2_ref, wr_ref, br_ref, x1_ref, h2_ref, gate_ref):
    m = mod_ref[0]
    zd = zd_ref[...]
    ys = ys_ref[0] + ys_ref[1] + d_ref[...] * zd
    gl = _bdot(_gelu_tanh(ys), wglu_ref[...]) + bglu_ref[...]
    yd = gl[:, 0:MIX] * _sigmoid(gl[:, MIX:2 * MIX])
    mix = (_bdot(ya_ref[...], wo_ref[0:MIX, :]) + _bdot(yb_ref[...], wo_ref[MIX:2 * MIX, :])
           + _bdot(yc_ref[...], wo_ref[2 * MIX:3 * MIX, :]) + _bdot(yd, wo_ref[3 * MIX:4 * MIX, :]))
    x1 = x_ref[...] + m[2:3] * mix
    x1_ref[...] = x1
    h2 = _rms(x1) * g2_ref[...] * (1.0 + m[4:5]) + m[3:4]
    h2b = h2.astype(BF16)
    h2_ref[...] = h2b
    gate_ref[...] = _route(jnp.dot(h2b, wr_ref[...], preferred_element_type=F32) + br_ref[...])


def _post_mixer(x, ya, yb, yc, ys, zd, mod, row_fn, w_out, s5_d, w_glu, b_glu, norm2_g, w_router, b_router, tm):
    n = x.shape[0]
    tok = lambda w: pl.BlockSpec((tm, w), lambda i: (i, 0))
    full = lambda a: pl.BlockSpec(a.shape, lambda i: (0,) * a.ndim)
    return pl.pallas_call(
        _post_kernel,
        out_shape=[jax.ShapeDtypeStruct((n, D_MODEL), F32),
                   jax.ShapeDtypeStruct((n, D_MODEL), BF16),
                   jax.ShapeDtypeStruct((n, LANE), F32)],
        grid=(n // tm,),
        in_specs=[tok(D_MODEL), tok(MIX), tok(MIX), tok(MIX),
                  pl.BlockSpec((2, tm, MIX), lambda i: (0, i, 0)), tok(MIX),
                  pl.BlockSpec((1, 6, D_MODEL), lambda i: (row_fn(i, tm), 0, 0)),
                  full(w_out), full(s5_d), full(w_glu), full(b_glu), full(norm2_g),
                  full(w_router), full(b_router)],
        out_specs=[tok(D_MODEL), tok(D_MODEL), tok(LANE)],
        compiler_params=_cp("parallel"),
        name="post_mixer",
    )(x, ya, yb, yc, ys, zd, mod, w_out, s5_d, w_glu, b_glu, norm2_g, w_router, b_router)


_MOE_EPS = 4
_MOE_TM = 1024


_MOE_ALIGN = 16
_MOE_ROWS = 2 * _MOE_TM + N_EXPERT * _MOE_ALIGN
_MOE_CH = 128
_MOE_GB = 512
_MOE_CB = 256


def _moe_kernel(h_ref, gate_ref, x_ref, mod_ref, w1_ref, w3_ref, w2_ref, o_ref,
                xs_sc, col_sc, start_sm, rows_sm):
    j = pl.program_id(1)
    tm = h_ref.shape[0]

    @pl.when(j == 0)
    def _dispatch():
        gate = gate_ref[...]
        cnt_row = jnp.sum(jnp.where(gate != 0.0, 1.0, 0.0), axis=0, keepdims=True)
        units_row = jnp.floor((cnt_row + (_MOE_ALIGN - 1)) * (1.0 / _MOE_ALIGN))
        li = lax.broadcasted_iota(jnp.int32, (LANE, LANE), 0)
        lj = lax.broadcasted_iota(jnp.int32, (LANE, LANE), 1)
        before = jnp.where(li < lj, 1.0, 0.0).astype(BF16)
        start_row = jnp.dot(jnp.broadcast_to(units_row, (8, LANE)).astype(BF16), before,
                            preferred_element_type=F32) * _MOE_ALIGN
        start_i = start_row.astype(jnp.int32)
        cnt_i = cnt_row.astype(jnp.int32)
        for e in range(N_EXPERT):
            start_sm[e] = start_i[0, e]
            rows_sm[e] = cnt_i[0, e]
        gt = gate.T[0:N_EXPERT, :]
        chosen = gt != 0.0
        ones = jnp.where(chosen, 1.0, 0.0)
        cnt = jnp.sum(ones, axis=-1, keepdims=True)
        units = jnp.floor((cnt + (_MOE_ALIGN - 1)) * (1.0 / _MOE_ALIGN))
        ei = lax.broadcasted_iota(jnp.int32, (N_EXPERT, N_EXPERT), 0)
        ej = lax.broadcasted_iota(jnp.int32, (N_EXPERT, N_EXPERT), 1)
        start = jnp.dot(jnp.where(ej < ei, 1.0, 0.0).astype(BF16),
                        jnp.broadcast_to(units, (N_EXPERT, LANE)).astype(BF16),
                        preferred_element_type=F32)[:, 0:1] * _MOE_ALIGN
        ti = lax.broadcasted_iota(jnp.int32, (tm, tm), 0)
        tj = lax.broadcasted_iota(jnp.int32, (tm, tm), 1)
        earlier = jnp.where(ti < tj, 1.0, 0.0).astype(BF16)
        pos = start + jnp.dot(ones.astype(BF16), earlier, preferred_element_type=F32)
        p_a = jnp.min(jnp.where(chosen, pos, float(_MOE_ROWS)), axis=0, keepdims=True)
        p_b = jnp.max(jnp.where(chosen, pos, -1.0), axis=0, keepdims=True)
        g_a = jnp.sum(jnp.where(chosen & (pos == p_a), gt, 0.0), axis=0, keepdims=True)
        g_b = jnp.sum(jnp.where(chosen & (pos == p_b) & (p_b != p_a), gt, 0.0), axis=0, keepdims=True)
        x = h_ref[...]
        for r0 in range(0, _MOE_ROWS, _MOE_GB):
            ri = (lax.broadcasted_iota(jnp.int32, (_MOE_GB, tm), 0) + r0).astype(F32)
            sel = jnp.where((ri == p_a) | (ri == p_b), 1.0, 0.0).astype(BF16)
            xs_sc[r0:r0 + _MOE_GB, :] = jnp.dot(sel, x, preferred_element_type=F32).astype(BF16)
        xs_sc[_MOE_ROWS:_MOE_ROWS + _MOE_CH, :] = jnp.zeros((_MOE_CH, D_MODEL), BF16)
        packed = jnp.concatenate([p_a, p_b, g_a, g_b, jnp.zeros((LANE - 4, tm), F32)], axis=0)
        col_sc[...] = packed.T

    for el in range(_MOE_EPS):
        e = j * _MOE_EPS + el
        seg = start_sm[e]
        n_rows = rows_sm[e]

        def chunk(i, carry, el=el, seg=seg, n_rows=n_rows):
            rows = pl.ds(pl.multiple_of(seg + i * _MOE_CH, _MOE_ALIGN), _MOE_CH)
            xs = xs_sc[rows, :]
            h1 = jnp.dot(xs, w1_ref[el], preferred_element_type=F32)
            h3 = jnp.dot(xs, w3_ref[el], preferred_element_type=F32)
            y = jnp.dot((h1 * _sigmoid(h1) * h3).astype(BF16), w2_ref[el], preferred_element_type=F32)
            live = lax.broadcasted_iota(jnp.int32, (_MOE_CH, 1), 0) + i * _MOE_CH < n_rows
            xs_sc[rows, :] = jnp.where(live, y, xs.astype(F32)).astype(BF16)
            return carry

        lax.fori_loop(0, (n_rows + _MOE_CH - 1) // _MOE_CH, chunk, 0)

    @pl.when(j == pl.num_programs(1) - 1)
    def _combine():
        g2 = mod_ref[0][5:6]
        y = xs_sc[0:_MOE_ROWS, :]
        for t0 in range(0, tm, _MOE_CB):
            c = col_sc[t0:t0 + _MOE_CB, :]
            ri = lax.broadcasted_iota(jnp.int32, (_MOE_CB, _MOE_ROWS), 1).astype(F32)
            w = (jnp.where(ri == c[:, 0:1], c[:, 2:3], 0.0)
                 + jnp.where(ri == c[:, 1:2], c[:, 3:4], 0.0)).astype(BF16)
            o_ref[t0:t0 + _MOE_CB, :] = (x_ref[t0:t0 + _MOE_CB, :]
                                         + g2 * jnp.dot(w, y, preferred_element_type=F32))


def _moe(h2, gate, x1, mod, row_fn, w1, w3, w2, tm):
    n = h2.shape[0]
    return pl.pallas_call(
        _moe_kernel,
        out_shape=jax.ShapeDtypeStruct((n, D_MODEL), F32),
        grid=(n // tm, N_EXPERT // _MOE_EPS),
        in_specs=[pl.BlockSpec((tm, D_MODEL), lambda i, j: (i, 0)),
                  pl.BlockSpec((tm, LANE), lambda i, j: (i, 0)),
                  pl.BlockSpec((tm, D_MODEL), lambda i, j: (i, 0)),
                  pl.BlockSpec((1, 6, D_MODEL), lambda i, j: (row_fn(i, tm), 0, 0)),
                  pl.BlockSpec((_MOE_EPS, D_MODEL, MOE_FF), lambda i, j: (j, 0, 0)),
                  pl.BlockSpec((_MOE_EPS, D_MODEL, MOE_FF), lambda i, j: (j, 0, 0)),
                  pl.BlockSpec((_MOE_EPS, MOE_FF, D_MODEL), lambda i, j: (j, 0, 0))],
        out_specs=pl.BlockSpec((tm, D_MODEL), lambda i, j: (i, 0)),
        scratch_shapes=[pltpu.VMEM((_MOE_ROWS + _MOE_CH, D_MODEL), BF16),
                        pltpu.VMEM((tm, LANE), F32),
                        pltpu.SMEM((N_EXPERT,), jnp.int32),
                        pltpu.SMEM((N_EXPERT,), jnp.int32)],
        compiler_params=_cp("parallel", "arbitrary"),
        name="moe",
    )(h2, gate, x1, mod, w1, w3, w2)


def _final_kernel(x_ref, g_ref, o_ref):
    o_ref[...] = _rms(x_ref[...]) * g_ref[...]


def _final_norm(x, g, tm):
    n = x.shape[0]
    return pl.pallas_call(
        _final_kernel,
        out_shape=jax.ShapeDtypeStruct((n, D_MODEL), F32),
        grid=(n // tm,),
        in_specs=[pl.BlockSpec((tm, D_MODEL), lambda i: (i, 0)),
                  pl.BlockSpec((1, D_MODEL), lambda i: (0, 0))],
        out_specs=pl.BlockSpec((tm, D_MODEL), lambda i: (i, 0)),
        compiler_params=_cp("parallel"),
        name="final_norm",
    )(x, g)


def _rot_cols(w):
    q = ROPE // 4
    return jnp.concatenate([-w[..., q:2 * q], w[..., 0:q], -w[..., 3 * q:4 * q], w[..., 2 * q:3 * q]], axis=-1)


def _rope_slot(w):
    pad = [(0, 0)] * (w.ndim - 1) + [(HD, LANE - HD - ROPE)]
    return jnp.pad(w, pad)


def _gate_lanes(g):
    n = g.shape[-1] // 2
    pad = [(0, 0)] * (g.ndim - 1) + [(0, LANE - n)]
    return jnp.concatenate([jnp.pad(g[..., :n], pad), jnp.pad(g[..., n:], pad)], axis=-1)


def _prep_w_in(w_in, gate_cols):
    a = w_in[..., 0:512]
    qkvo = w_in[..., 512:1536]
    g = _gate_lanes(w_in[..., 1536:1536 + gate_cols])
    c0 = 1536 + gate_cols
    cq = w_in[..., c0:c0 + Q_RANK]
    ckv = w_in[..., c0 + Q_RANK:c0 + Q_RANK + KV_RANK]
    kr = w_in[..., c0 + Q_RANK + KV_RANK:c0 + Q_RANK + KV_RANK + ROPE]
    d = w_in[..., c0 + Q_RANK + KV_RANK + ROPE:]
    return jnp.concatenate([a, qkvo, g, cq, ckv, _rope_slot(kr), _rope_slot(_rot_cols(kr)), d],
                           axis=-1).astype(BF16)


def _prep_mla(w_uq, w_ukv):
    n_layer = w_uq.shape[0]
    scale = (HD + ROPE) ** -0.5
    wq = w_uq.reshape(n_layer, Q_RANK, HEADS, HD + ROPE) * scale
    nope, rope = wq[..., :HD], wq[..., HD:]
    zeros_r = jnp.zeros_like(rope)
    wqa = jnp.concatenate([nope, rope, zeros_r], axis=-1).reshape(n_layer, Q_RANK, HEADS * LANE)
    wqb = jnp.concatenate([jnp.zeros_like(nope), _rot_cols(rope), zeros_r], axis=-1)
    wqb = wqb.reshape(n_layer, Q_RANK, HEADS * LANE)
    wkv = w_ukv.reshape(n_layer, KV_RANK, HEADS, 2 * HD)
    k_nope, val = wkv[..., :HD], wkv[..., HD:]
    zeros_h = jnp.zeros_like(k_nope)
    wk = jnp.concatenate([k_nope, zeros_h], axis=-1).reshape(n_layer, KV_RANK, HEADS * LANE)
    even = jnp.concatenate([val, zeros_h], axis=-1)
    odd = jnp.concatenate([zeros_h, val], axis=-1)
    is_odd = (jnp.arange(HEADS) % 2 == 1)[None, None, :, None]
    wv = jnp.where(is_odd, odd, even).reshape(n_layer, KV_RANK, HEADS * LANE)
    return wqa.astype(BF16), wqb.astype(BF16), wk.astype(BF16), wv.astype(BF16)


def _rope_tables(t_len, rotate):
    ones = jnp.ones((t_len, HD), F32)
    zeros = jnp.zeros((t_len, HD), F32)
    tail = jnp.zeros((t_len, LANE - HD - ROPE), F32)
    if not rotate:
        return (jnp.concatenate([ones, jnp.ones((t_len, ROPE), F32), tail], axis=-1),
                jnp.zeros((t_len, LANE), F32))
    rows = t_len // GRID_W
    row = jnp.repeat(jnp.arange(rows, dtype=F32), GRID_W)
    col = jnp.tile(jnp.arange(GRID_W, dtype=F32), rows)
    nf = ROPE // 4
    inv = ROPE_BASE ** (-jnp.arange(nf, dtype=F32) / nf)
    ar = row[:, None] * inv
    ac = col[:, None] * inv
    cos = jnp.concatenate([jnp.cos(ar), jnp.cos(ar), jnp.cos(ac), jnp.cos(ac)], axis=-1)
    sin = jnp.concatenate([jnp.sin(ar), jnp.sin(ar), jnp.sin(ac), jnp.sin(ac)], axis=-1)
    return (jnp.concatenate([ones, cos, tail], axis=-1), jnp.concatenate([zeros, sin, tail], axis=-1))


def _prep_s5(bb_re, bb_im, c_re, c_im):
    eye = jnp.eye(S5_G, dtype=F32)
    to_b = lambda bb: jnp.einsum("ldgnc,gh->ldgchn", bb, eye).reshape(bb.shape[0], 2, MIX, S5_STATES)
    to_c = lambda cc: jnp.einsum("ldgcn,gh->ldgnhc", cc, eye).reshape(cc.shape[0], 2, S5_STATES, MIX)
    bd = jnp.concatenate([to_b(bb_re), to_b(bb_im)], axis=-1)
    cd = jnp.concatenate([to_c(c_re.astype(F32)), -to_c(c_im.astype(F32))], axis=-2)
    return bd.astype(BF16), cd.astype(BF16)


def _layer(x, bsz, t_len, mod, row_fn, p, ctx, tm):
    za, zb, zg, zc, zd = _pre_mixer(x, mod, row_fn, p["norm1_g"], p["w_in"], tm)
    seq = lambda a: a.reshape(bsz, t_len, a.shape[-1])
    ya = _conv_module(seq(za), p["conv_w"], p["conv_b"], p["conv_ln_g"], p["conv_ln_b"])
    yb, c_fin, n_fin, m_fin = _mlstm_mixer(seq(zb), seq(zg), p["gate_b"], p["mlstm_norm_g"],
                                           ctx["mlstm_c"], ctx["mlstm_n"], ctx["mlstm_m"])
    yc, ckv = _mla_mixer(seq(zc), ctx["cos"], ctx["sin"], p["mla_q_norm_g"], p["mla_kv_norm_g"],
                         p["wqa"], p["wqb"], p["wk"], p["wv"], ctx.get("past_ckv"), ctx.get("past_kr"))
    ys, s_re, s_im = _s5_scan(seq(zd).transpose(1, 0, 2), p["s5_bd"], p["s5_cd"], p["s5_ab_re"], p["s5_ab_im"],
                              ctx["s5_re"], ctx["s5_im"])
    ys = ys.transpose(0, 2, 1, 3)
    flat = lambda a: a.reshape(bsz * t_len, a.shape[-1])
    x1, h2, gate = _post_mixer(x, flat(ya), flat(yb), flat(yc), ys.reshape(2, bsz * t_len, MIX), zd, mod, row_fn,
                               p["w_out"], p["s5_d"], p["s5_w_glu"], p["s5_b_glu"], p["norm2_g"],
                               p["w_router"], p["b_router"], tm)
    x2 = _moe(h2, gate, x1, mod, row_fn, p["moe_w1"], p["moe_w3"], p["moe_w2"], _MOE_TM)
    krope = seq(zc)[:, :, Q_RANK + KV_RANK + HD:Q_RANK + KV_RANK + HD + ROPE]
    return x2, (ckv, krope, c_fin, n_fin, m_fin, s_re, s_im)


def kernel(x_prompt, x_sample, cache_mla_ckv, cache_mla_krope, state_mlstm_C, state_mlstm_n, state_mlstm_m, state_s5, c, c_ctx, norm1_g, norm2_g, final_g, w_mod, b_mod, w_in, w_out, conv_w, conv_b, conv_ln_g, conv_ln_b, mlstm_gate_b, mlstm_norm_g, mla_q_norm_g, mla_w_uq, mla_kv_norm_g, mla_w_ukv, s5_a_re, s5_a_im, s5_log_dt, s5_b_re, s5_b_im, s5_c_re, s5_c_im, s5_d, s5_w_glu, s5_b_glu, moe_w_group, moe_b_group, moe_w_expert, moe_b_expert, moe_w1, moe_w3, moe_w2):
    n_layer = w_in.shape[0]
    b_ctx, t_ctx, d = x_prompt.shape
    b_lat, t_lat, _ = x_sample.shape
    nd = 2 * HEADS
    tm = 512
    assert b_lat + 1 <= 16 and t_lat % _MOE_TM == 0 and (b_ctx * t_ctx) % _MOE_TM == 0

    c_all = jnp.zeros((16, d), F32).at[0].set(c_ctx).at[1:1 + b_lat].set(c)
    mod_all = _modulation(c_all, w_mod, b_mod).reshape(n_layer, 16, 6, d)
    gate_cols = 4 * HEADS
    w_in_ext = _prep_w_in(w_in, gate_cols)
    wqa, wqb, wk, wv = _prep_mla(mla_w_uq, mla_w_ukv)
    ab_re, ab_im, bb_re, bb_im = _s5_discretise(s5_a_re, s5_a_im, s5_log_dt, s5_b_re, s5_b_im)
    s5_bd, s5_cd = _prep_s5(bb_re, bb_im, s5_c_re, s5_c_im)
    w_router = jnp.pad(jnp.concatenate([moe_w_expert, moe_w_group], axis=-1),
                       [(0, 0), (0, 0), (0, LANE - N_EXPERT - N_GROUP)]).astype(BF16)
    b_router = jnp.pad(jnp.concatenate([moe_b_expert, moe_b_group], axis=-1),
                       [(0, 0), (0, LANE - N_EXPERT - N_GROUP)])
    gate_b = _gate_lanes(mlstm_gate_b.reshape(n_layer, gate_cols))
    conv_w_p = jnp.pad(conv_w, [(0, 0), (0, 32 - CONV_WIDTH), (0, 0)])
    w_out_b = w_out.astype(BF16)
    w_glu_b = s5_w_glu.astype(BF16)
    w1_b, w3_b, w2_b = moe_w1.astype(BF16), moe_w3.astype(BF16), moe_w2.astype(BF16)
    row = lambda a, l: a[l][None, :]

    cos_ctx, sin_ctx = _rope_tables(t_ctx, rotate=False)
    cos_lat, sin_lat = _rope_tables(t_lat, rotate=True)
    zero_state = dict(
        mlstm_c=jnp.zeros((b_ctx, nd, HD, HD), F32), mlstm_n=jnp.zeros((b_ctx, nd, HD), F32),
        mlstm_m=jnp.zeros((b_ctx, nd, LANE), F32),
        s5_re=jnp.zeros((2, b_ctx, S5_STATES), F32), s5_im=jnp.zeros((2, b_ctx, S5_STATES), F32),
        cos=cos_ctx, sin=sin_ctx)

    row_ctx = lambda i, tile: 0
    row_lat = lambda i, tile: 1 + (i * tile) // t_lat

    x_ctx = x_prompt.reshape(b_ctx * t_ctx, d)
    x_lat = x_sample.reshape(b_lat * t_lat, d)
    outs = []
    for l in range(n_layer):
        p = dict(norm1_g=row(norm1_g, l), norm2_g=row(norm2_g, l), w_in=w_in_ext[l], w_out=w_out_b[l],
                 conv_w=conv_w_p[l], conv_b=row(conv_b, l), conv_ln_g=row(conv_ln_g, l),
                 conv_ln_b=row(conv_ln_b, l), gate_b=row(gate_b, l), mlstm_norm_g=row(mlstm_norm_g, l),
                 mla_q_norm_g=row(mla_q_norm_g, l), mla_kv_norm_g=row(mla_kv_norm_g, l),
                 wqa=wqa[l], wqb=wqb[l], wk=wk[l], wv=wv[l],
                 s5_bd=s5_bd[l], s5_cd=s5_cd[l], s5_ab_re=ab_re[l][:, None, :], s5_ab_im=ab_im[l][:, None, :],
                 s5_d=row(s5_d, l), s5_w_glu=w_glu_b[l], s5_b_glu=row(s5_b_glu, l),
                 w_router=w_router[l], b_router=row(b_router, l),
                 moe_w1=w1_b[l], moe_w3=w3_b[l], moe_w2=w2_b[l])
        x_ctx, st = _layer(x_ctx, b_ctx, t_ctx, mod_all[l], row_ctx, p, zero_state, tm)
        outs.append(st)
        s5_l = state_s5[:, l].reshape(b_lat, 2, S5_STATES, 2)
        lat_state = dict(
            mlstm_c=jnp.swapaxes(state_mlstm_C[:, l].reshape(b_lat, nd, HD, HD), -1, -2),
            mlstm_n=state_mlstm_n[:, l].reshape(b_lat, nd, HD),
            mlstm_m=jnp.broadcast_to(state_mlstm_m[:, l].reshape(b_lat, nd, 1), (b_lat, nd, LANE)),
            s5_re=s5_l[..., 0].transpose(1, 0, 2), s5_im=s5_l[..., 1].transpose(1, 0, 2),
            cos=cos_lat, sin=sin_lat,
            past_ckv=cache_mla_ckv[:, l], past_kr=_rope_slot(cache_mla_krope[:, l]))
        x_lat, _ = _layer(x_lat, b_lat, t_lat, mod_all[l], row_lat, p, lat_state, tm)

    y_prompt = _final_norm(x_ctx, final_g[None, :], tm).reshape(b_ctx, t_ctx, d)
    y_sample = _final_norm(x_lat, final_g[None, :], tm).reshape(b_lat, t_lat, d)
    stack = lambda i: jnp.stack([o[i] for o in outs], axis=1)
    new_ckv = stack(0)
    new_krope = stack(1)
    new_c = jnp.swapaxes(stack(2), -1, -2).reshape(b_ctx, n_layer, 2, HEADS, HD, HD)
    new_n = stack(3).reshape(b_ctx, n_layer, 2, HEADS, HD)
    new_m = stack(4)[..., 0].reshape(b_ctx, n_layer, 2, HEADS)
    s_re = jnp.stack([o[5] for o in outs], axis=0)
    s_im = jnp.stack([o[6] for o in outs], axis=0)
    new_s5 = jnp.stack([s_re, s_im], axis=-1).transpose(2, 0, 1, 3, 4)
    new_s5 = new_s5.reshape(b_ctx, n_layer, 2, S5_G, S5_N, 2)
    return (y_prompt, y_sample, new_ckv, new_krope, new_c, new_n, new_m, new_s5)
```

```python
import functools
import math

import jax
import jax.numpy as jnp
from jax import lax
from jax.experimental import pallas as pl
from jax.experimental.pallas import tpu as pltpu

F32 = jnp.float32
BF16 = jnp.bfloat16
EPS = 1e-6

D_MODEL = 1024
MIX = 256
CONV_WIDTH = 31
HEADS = 4
HD = 64
CHUNK = 128
ROPE = 32
KV_RANK = 128
Q_RANK = 256
GRID_W = 64
ROPE_BASE = 10000.0
S5_G = 16
S5_GC = 16
S5_N = 64
S5_STATES = S5_G * S5_N
N_EXPERT = 32
PER_GROUP = 8
N_GROUP = 4
MOE_FF = 256
LANE = 128
Z_COLS = 2688
VMEM_LIMIT = 56 * 1024 * 1024


def _cp(*sem):
    return pltpu.CompilerParams(dimension_semantics=sem, vmem_limit_bytes=VMEM_LIMIT)


def _rms(x):
    return x * lax.rsqrt(jnp.mean(x * x, axis=-1, keepdims=True) + EPS)


def _sigmoid(x):
    return 1.0 / (1.0 + jnp.exp(-x))


def _bdot(a, b):
    return jnp.dot(a.astype(BF16), b.astype(BF16), preferred_element_type=F32)


def _mod_kernel(c_ref, w_ref, b_ref, o_ref):
    c = c_ref[...]
    o_ref[0] = _bdot(c * _sigmoid(c), w_ref[0]) + b_ref[0]


def _modulation(c_all, w_mod, b_mod):
    n_layer, d, n = w_mod.shape
    tn = 1536
    return pl.pallas_call(
        _mod_kernel,
        out_shape=jax.ShapeDtypeStruct((n_layer, 16, n), F32),
        grid=(n_layer, n // tn),
        in_specs=[pl.BlockSpec((16, d), lambda l, j: (0, 0)),
                  pl.BlockSpec((1, d, tn), lambda l, j: (l, 0, j)),
                  pl.BlockSpec((1, 1, tn), lambda l, j: (l, 0, j))],
        out_specs=pl.BlockSpec((1, 16, tn), lambda l, j: (l, 0, j)),
        compiler_params=_cp("parallel", "parallel"),
        name="modulation",
    )(c_all, w_mod, b_mod.reshape(n_layer, 1, n))


def _s5_disc_kernel(are_ref, aim_ref, ldt_ref, bre_ref, bim_ref, abre_ref, abim_ref, bbre_ref, bbim_ref):
    a_re = are_ref[...]
    a_im = aim_ref[...]
    dt = jnp.exp(ldt_ref[...])
    mag = jnp.exp(a_re * dt)
    ab_re = mag * jnp.cos(a_im * dt)
    ab_im = mag * jnp.sin(a_im * dt)
    den = a_re * a_re + a_im * a_im
    f_re = ((ab_re - 1.0) * a_re + ab_im * a_im) / den
    f_im = (ab_im * a_re - (ab_re - 1.0) * a_im) / den
    b_re = bre_ref[...]
    b_im = bim_ref[...]
    abre_ref[...] = ab_re
    abim_ref[...] = ab_im
    bbre_ref[...] = f_re * b_re - f_im * b_im
    bbim_ref[...] = f_re * b_im + f_im * b_re


def _s5_discretise(a_re, a_im, log_dt, b_re, b_im):
    n_layer = a_re.shape[0]
    rows = n_layer * 2 * S5_G
    cols = S5_N * S5_GC
    rep = lambda a: jnp.repeat(a.reshape(rows, S5_N), S5_GC, axis=1)
    ldt = jnp.broadcast_to(log_dt.reshape(rows, 1), (rows, cols))
    spec = pl.BlockSpec((rows, cols), lambda: (0, 0))
    ab_re, ab_im, bb_re, bb_im = pl.pallas_call(
        _s5_disc_kernel,
        out_shape=[jax.ShapeDtypeStruct((rows, cols), F32)] * 4,
        in_specs=[spec] * 5,
        out_specs=[spec] * 4,
        name="s5_discretise",
    )(rep(a_re), rep(a_im), ldt, b_re.reshape(rows, cols), b_im.reshape(rows, cols))
    pick = lambda a: a[:, ::S5_GC].reshape(n_layer, 2, S5_STATES)
    shp = (n_layer, 2, S5_G, S5_N, S5_GC)
    return pick(ab_re), pick(ab_im), bb_re.reshape(shp), bb_im.reshape(shp)


def _pre_kernel(x_ref, mod_ref, g_ref, w_ref, za_ref, zb_ref, zg_ref, zc_ref, zd_ref):
    m = mod_ref[0]
    h = _rms(x_ref[...]) * g_ref[...] * (1.0 + m[1:2]) + m[0:1]
    hb = h.astype(BF16)
    col = 0
    for o_ref in (za_ref, zb_ref, zg_ref, zc_ref, zd_ref):
        n = o_ref.shape[-1]
        o_ref[...] = jnp.dot(hb, w_ref[:, col:col + n], preferred_element_type=F32)
        col += n


def _pre_mixer(x, mod, row_fn, norm_g, w_in_ext, tm):
    n = x.shape[0]
    widths = (512, 1024, 2 * LANE, 640, MIX)
    return pl.pallas_call(
        _pre_kernel,
        out_shape=[jax.ShapeDtypeStruct((n, w), F32) for w in widths],
        grid=(n // tm,),
        in_specs=[pl.BlockSpec((tm, D_MODEL), lambda i: (i, 0)),
                  pl.BlockSpec((1, 6, D_MODEL), lambda i: (row_fn(i, tm), 0, 0)),
                  pl.BlockSpec((1, D_MODEL), lambda i: (0, 0)),
                  pl.BlockSpec((D_MODEL, Z_COLS), lambda i: (0, 0))],
        out_specs=[pl.BlockSpec((tm, w), lambda i: (i, 0)) for w in widths],
        compiler_params=_cp("parallel"),
        name="pre_mixer",
    )(x, mod, norm_g, w_in_ext)


_CONV_PAD = 16
_CONV_TT = 128


def _conv_kernel(za_ref, w_ref, b_ref, lg_ref, lb_ref, o_ref, hp_ref):
    t_len = o_ref.shape[1]
    u = za_ref[0]
    hp_ref[0:_CONV_PAD, :] = jnp.zeros((_CONV_PAD, MIX), F32)
    hp_ref[_CONV_PAD + t_len:2 * _CONV_PAD + t_len, :] = jnp.zeros((_CONV_PAD, MIX), F32)
    hp_ref[_CONV_PAD:_CONV_PAD + t_len, :] = u[:, :MIX] * _sigmoid(u[:, MIX:])
    w = w_ref[...]
    half = CONV_WIDTH // 2
    for t0 in range(0, t_len, _CONV_TT):
        acc = jnp.zeros((_CONV_TT, MIX), F32) + b_ref[...]
        for k in range(CONV_WIDTH):
            start = t0 + _CONV_PAD - half + k
            acc = acc + hp_ref[start:start + _CONV_TT, :] * w[k:k + 1, :]
        mu = jnp.mean(acc, axis=-1, keepdims=True)
        cen = acc - mu
        var = jnp.mean(cen * cen, axis=-1, keepdims=True)
        yn = cen * lax.rsqrt(var + EPS) * lg_ref[...] + lb_ref[...]
        o_ref[0, t0:t0 + _CONV_TT, :] = yn * _sigmoid(yn)


def _conv_module(za, w, b, ln_g, ln_b):
    bsz, t_len, _ = za.shape
    vec = pl.BlockSpec((1, MIX), lambda i: (0, 0))
    return pl.pallas_call(
        _conv_kernel,
        out_shape=jax.ShapeDtypeStruct((bsz, t_len, MIX), F32),
        grid=(bsz,),
        in_specs=[pl.BlockSpec((1, t_len, 2 * MIX), lambda i: (i, 0, 0)),
                  pl.BlockSpec((32, MIX), lambda i: (0, 0)), vec, vec, vec],
        out_specs=pl.BlockSpec((1, t_len, MIX), lambda i: (i, 0, 0)),
        scratch_shapes=[pltpu.VMEM((t_len + 2 * _CONV_PAD, MIX), F32)],
        compiler_params=_cp("parallel"),
        name="conv_module",
    )(za, w, b, ln_g, ln_b)


def _log_sigmoid(x):
    return jnp.minimum(x, 0.0) - jnp.log(1.0 + jnp.exp(-jnp.abs(x)))


def _split3_dot(tri, x):
    hi = x.astype(BF16)
    r1 = x - hi.astype(F32)
    mid = r1.astype(BF16)
    lo = (r1 - mid.astype(F32)).astype(BF16)
    dot = lambda v: jnp.dot(tri, v, preferred_element_type=F32)
    return dot(hi) + dot(mid) + dot(lo)


def _split2(x):
    hi = x.astype(BF16)
    return hi, (x - hi.astype(F32)).astype(BF16)


def _mlstm_chunk(zb_ref, zg_ref, gb_ref, qt_sc, qbd_sc, vt_sc, vbd_sc, ct_sc, nb_sc, c, direction, ms):
    L = CHUNK
    rows = pl.ds(pl.multiple_of(c * L, L), L)
    si = lax.broadcasted_iota(jnp.int32, (L, L), 0)
    ti = lax.broadcasted_iota(jnp.int32, (L, L), 1)
    if direction == 0:
        valid = si <= ti
        tri = jnp.where(ti <= si, 1.0, 0.0).astype(BF16)
        last = L - 1
    else:
        valid = si >= ti
        tri = jnp.where(ti >= si, 1.0, 0.0).astype(BF16)
        last = 0
    g_in = zg_ref[0, rows, 0:LANE] + gb_ref[:, 0:LANE]
    g_f = zg_ref[0, rows, LANE:2 * LANE] + gb_ref[:, LANE:2 * LANE]
    bc = _split3_dot(tri, _log_sigmoid(g_f))
    r = g_in - bc
    g_in_t = g_in.T
    bct = bc.T
    k = zb_ref[0, rows, MIX:2 * MIX]
    k_hi, k_lo = _split2(k)
    qt = qt_sc[c]
    st_all = jnp.dot(k_hi, qbd_sc[c], preferred_element_type=F32)
    sts, inters, floors, colsums, ws, decays, new_m = [], [], [], [], [], [], []
    for h in range(HEADS):
        j = direction * HEADS + h
        i_row = g_in_t[j:j + 1, :]
        b_row = bct[j:j + 1, :]
        m_prev = ms[h]
        rb = jnp.where(valid, jnp.broadcast_to(r[:, j:j + 1], (L, L)), -jnp.inf)
        c_row = jnp.maximum(m_prev, jnp.max(rb, axis=0, keepdims=True))
        st = st_all[:, h * L:(h + 1) * L] * jnp.exp(rb - c_row)
        sts.append(st.astype(BF16))
        colsums.append(jnp.sum(st, axis=0, keepdims=True))
        inters.append(jnp.exp(m_prev - c_row))
        floors.append(jnp.exp(-(b_row + c_row)))
        b_last = b_row[:, last:last + 1]
        lw = b_last - b_row + i_row
        m_new = jnp.maximum(b_last + m_prev, jnp.max(lw, axis=-1, keepdims=True))
        decays.append(jnp.exp(b_last + m_prev - m_new))
        ws.append(jnp.exp(lw - m_new))
        new_m.append(m_new)
    per_head_rows = lambda vs: jnp.concatenate([jnp.broadcast_to(v, (HD, L)) for v in vs], axis=0)
    ct = ct_sc[direction]
    nb = nb_sc[direction]
    num = (per_head_rows(inters) * jnp.dot(ct.astype(BF16), qt, preferred_element_type=F32)
           + jnp.dot(vbd_sc[c], jnp.concatenate(sts, axis=0), preferred_element_type=F32))
    n_hi, n_lo = _split2(nb)
    qn = jnp.dot(n_hi, qt, preferred_element_type=F32) + jnp.dot(n_lo, qt, preferred_element_type=F32)
    dens = [jnp.maximum(jnp.abs(inters[h] * qn[h:h + 1, :] + colsums[h]), floors[h]) for h in range(HEADS)]
    ht = num / per_head_rows(dens)
    lane = lax.broadcasted_iota(jnp.int32, (1, MIX), 1) // HD
    dec_row = sum(jnp.where(lane == h, decays[h], 0.0) for h in range(HEADS))
    upd = jnp.dot((vt_sc[c] * per_head_rows(ws)).astype(BF16), k_hi, preferred_element_type=F32)
    same_head = (lax.broadcasted_iota(jnp.int32, (MIX, MIX), 0) // HD
                 == lax.broadcasted_iota(jnp.int32, (MIX, MIX), 1) // HD)
    ct_sc[direction] = jnp.where(same_head, ct * dec_row + upd, 0.0)
    w_hi, w_lo = _split2(jnp.concatenate(ws + [jnp.zeros((8 - HEADS, L), F32)], axis=0))
    wk = (jnp.dot(w_hi, k_hi, preferred_element_type=F32) + jnp.dot(w_lo, k_hi, preferred_element_type=F32)
          + jnp.dot(w_hi, k_lo, preferred_element_type=F32))
    own = lax.broadcasted_iota(jnp.int32, (8, MIX), 0) == lax.broadcasted_iota(jnp.int32, (8, MIX), 1) // HD
    nb_sc[direction] = jnp.where(own, nb * dec_row + wk, 0.0)
    return ht, tuple(new_m)


def _mlstm_kernel(zb_ref, zg_ref, gb_ref, ng_ref, c0_ref, n0_ref, m0_ref,
                  y_ref, c_ref, n_ref, m_ref, qt_sc, qbd_sc, vt_sc, vbd_sc, ht_sc, ct_sc, nb_sc):
    t_len = y_ref.shape[1]
    nc = t_len // CHUNK
    L = CHUNK
    qbd_sc[...] = jnp.zeros_like(qbd_sc)
    vbd_sc[...] = jnp.zeros_like(vbd_sc)
    ct_sc[...] = jnp.zeros_like(ct_sc)
    nb_sc[...] = jnp.zeros_like(nb_sc)

    def transpose_chunk(c, carry):
        rows = pl.ds(pl.multiple_of(c * L, L), L)
        qt = (zb_ref[0, rows, 0:MIX] * (HD ** -0.5)).T.astype(BF16)
        vt = zb_ref[0, rows, 2 * MIX:3 * MIX].T
        qt_sc[c] = qt
        vt_sc[c] = vt
        for h in range(HEADS):
            sl = slice(h * HD, (h + 1) * HD)
            qbd_sc[c, sl, h * L:(h + 1) * L] = qt[sl, :]
            vbd_sc[c, sl, h * L:(h + 1) * L] = vt[sl, :].astype(BF16)
        return carry

    lax.fori_loop(0, nc, transpose_chunk, 0)

    for j in range(2 * HEADS):
        d, h = divmod(j, HEADS)
        sl = slice(h * HD, (h + 1) * HD)
        ct_sc[d, sl, sl] = c0_ref[0, j]
        nb_sc[d, h:h + 1, sl] = n0_ref[0, j:j + 1, :]
    m_init = tuple(tuple(m0_ref[0, j:j + 1, 0:1] for j in range(d * HEADS, (d + 1) * HEADS)) for d in range(2))

    def scan_body(i, ms):
        cb = nc - 1 - i
        scr = (zb_ref, zg_ref, gb_ref, qt_sc, qbd_sc, vt_sc, vbd_sc, ct_sc, nb_sc)
        ht_sc[0, i], ms_f = _mlstm_chunk(*scr, i, 0, ms[0])
        ht_sc[1, cb], ms_b = _mlstm_chunk(*scr, cb, 1, ms[1])
        return ms_f, ms_b

    m_fin = lax.fori_loop(0, nc, scan_body, m_init)
    for j in range(2 * HEADS):
        d, h = divmod(j, HEADS)
        sl = slice(h * HD, (h + 1) * HD)
        c_ref[0, j] = ct_sc[d, sl, sl]
        n_ref[0, j:j + 1, :] = nb_sc[d, h:h + 1, sl]
        m_ref[0, j:j + 1, :] = jnp.broadcast_to(m_fin[d][h], (1, LANE))

    def out_body(c, carry):
        rows = pl.ds(pl.multiple_of(c * CHUNK, CHUNK), CHUNK)
        normed = []
        for h in range(HEADS):
            tot = ht_sc[0, c, h * HD:(h + 1) * HD, :] + ht_sc[1, c, h * HD:(h + 1) * HD, :]
            normed.append(tot * lax.rsqrt(jnp.mean(tot * tot, axis=0, keepdims=True) + EPS))
        hn = jnp.concatenate(normed, axis=0).T
        y_ref[0, rows, :] = hn * ng_ref[...] * _sigmoid(zb_ref[0, rows, 3 * MIX:4 * MIX])
        return carry

    lax.fori_loop(0, nc, out_body, 0)


def _mlstm_mixer(zb, zg, gate_b, norm_g, c0, n0, m0):
    bsz, t_len, _ = zb.shape
    nd = 2 * HEADS
    return pl.pallas_call(
        _mlstm_kernel,
        out_shape=[jax.ShapeDtypeStruct((bsz, t_len, MIX), F32),
                   jax.ShapeDtypeStruct((bsz, nd, HD, HD), F32),
                   jax.ShapeDtypeStruct((bsz, nd, HD), F32),
                   jax.ShapeDtypeStruct((bsz, nd, LANE), F32)],
        grid=(bsz,),
        in_specs=[pl.BlockSpec((1, t_len, 4 * MIX), lambda i: (i, 0, 0)),
                  pl.BlockSpec((1, t_len, 2 * LANE), lambda i: (i, 0, 0)),
                  pl.BlockSpec((1, 2 * LANE), lambda i: (0, 0)),
                  pl.BlockSpec((1, MIX), lambda i: (0, 0)),
                  pl.BlockSpec((1, nd, HD, HD), lambda i: (i, 0, 0, 0)),
                  pl.BlockSpec((1, nd, HD), lambda i: (i, 0, 0)),
                  pl.BlockSpec((1, nd, LANE), lambda i: (i, 0, 0))],
        out_specs=[pl.BlockSpec((1, t_len, MIX), lambda i: (i, 0, 0)),
                   pl.BlockSpec((1, nd, HD, HD), lambda i: (i, 0, 0, 0)),
                   pl.BlockSpec((1, nd, HD), lambda i: (i, 0, 0)),
                   pl.BlockSpec((1, nd, LANE), lambda i: (i, 0, 0))],
        scratch_shapes=[pltpu.VMEM((t_len // CHUNK, MIX, CHUNK), BF16),
                        pltpu.VMEM((t_len // CHUNK, MIX, HEADS * CHUNK), BF16),
                        pltpu.VMEM((t_len // CHUNK, MIX, CHUNK), F32),
                        pltpu.VMEM((t_len // CHUNK, MIX, HEADS * CHUNK), BF16),
                        pltpu.VMEM((2, t_len // CHUNK, MIX, CHUNK), F32),
                        pltpu.VMEM((2, MIX, MIX), F32),
                        pltpu.VMEM((2, 8, MIX), F32)],
        compiler_params=_cp("parallel"),
        name="mlstm_mixer",
    )(zb, zg, gate_b, norm_g, c0, n0, m0)


_ATT_TQ = 256


def _mla_kernel(*refs, past):
    if past:
        (zc_ref, cos_ref, sin_ref, gq_ref, gkv_ref, wqa_ref, wqb_ref, wk_ref, wv_ref,
         pckv_ref, pkr_ref, y_ref, ckv_ref, q_sc, k_sc, v_sc) = refs
    else:
        (zc_ref, cos_ref, sin_ref, gq_ref, gkv_ref, wqa_ref, wqb_ref, wk_ref, wv_ref,
         y_ref, ckv_ref, q_sc, k_sc, v_sc) = refs
    t_len = y_ref.shape[1]
    cosf = cos_ref[...]
    sinf = sin_ref[...]
    cq = _rms(zc_ref[0, :, 0:Q_RANK]) * gq_ref[...]
    ckv = _rms(zc_ref[0, :, Q_RANK:Q_RANK + KV_RANK]) * gkv_ref[...]
    ckv_ref[0] = ckv
    kr = (zc_ref[0, :, Q_RANK + KV_RANK:Q_RANK + KV_RANK + LANE] * cosf
          + zc_ref[0, :, Q_RANK + KV_RANK + LANE:Q_RANK + KV_RANK + 2 * LANE] * sinf)
    cqb = cq.astype(BF16)
    ckvb = ckv.astype(BF16)
    for h in range(HEADS):
        sl = slice(h * LANE, (h + 1) * LANE)
        qa = jnp.dot(cqb, wqa_ref[:, sl], preferred_element_type=F32)
        qb = jnp.dot(cqb, wqb_ref[:, sl], preferred_element_type=F32)
        q_sc[:, sl] = (qa * cosf + qb * sinf).astype(BF16)
        kn = jnp.dot(ckvb, wk_ref[:, sl], preferred_element_type=F32)
        k_sc[past:past + t_len, sl] = (kn + kr).astype(BF16)
        v_sc[past:past + t_len, sl] = jnp.dot(ckvb, wv_ref[:, sl], preferred_element_type=F32).astype(BF16)
    if past:
        pckv = pckv_ref[0].astype(BF16)
        pkr = pkr_ref[0]
        for h in range(HEADS):
            sl = slice(h * LANE, (h + 1) * LANE)
            k_sc[0:past, sl] = (jnp.dot(pckv, wk_ref[:, sl], preferred_element_type=F32) + pkr).astype(BF16)
            v_sc[0:past, sl] = jnp.dot(pckv, wv_ref[:, sl], preferred_element_type=F32).astype(BF16)

    def q_block(i, carry):
        rows = pl.ds(pl.multiple_of(i * _ATT_TQ, _ATT_TQ), _ATT_TQ)
        outs = []
        for h in range(HEADS):
            sl = slice(h * LANE, (h + 1) * LANE)
            s = lax.dot_general(q_sc[rows, sl], k_sc[:, sl], (((1,), (1,)), ((), ())),
                                preferred_element_type=F32)
            p = jnp.exp(s - jnp.max(s, axis=-1, keepdims=True))
            o = jnp.dot(p.astype(BF16), v_sc[:, sl], preferred_element_type=F32)
            outs.append(o / jnp.sum(p, axis=-1, keepdims=True))
        y_ref[0, rows, 0:LANE] = outs[0] + outs[1]
        y_ref[0, rows, LANE:2 * LANE] = outs[2] + outs[3]
        return carry

    lax.fori_loop(0, t_len // _ATT_TQ, q_block, 0)


def _mla_mixer(zc, cosf, sinf, gq, gkv, wqa, wqb, wk, wv, past_ckv=None, past_kr=None):
    bsz, t_len, zw = zc.shape
    past = 0 if past_ckv is None else past_ckv.shape[1]
    full = lambda a: pl.BlockSpec(a.shape, lambda i: (0,) * a.ndim)
    args = [zc, cosf, sinf, gq, gkv, wqa, wqb, wk, wv]
    in_specs = [pl.BlockSpec((1, t_len, zw), lambda i: (i, 0, 0))] + [full(a) for a in args[1:]]
    if past:
        args += [past_ckv, past_kr]
        in_specs += [pl.BlockSpec((1, past, KV_RANK), lambda i: (i, 0, 0)),
                     pl.BlockSpec((1, past, LANE), lambda i: (i, 0, 0))]
    wide = HEADS * LANE
    return pl.pallas_call(
        functools.partial(_mla_kernel, past=past),
        out_shape=[jax.ShapeDtypeStruct((bsz, t_len, MIX), F32),
                   jax.ShapeDtypeStruct((bsz, t_len, KV_RANK), F32)],
        grid=(bsz,),
        in_specs=in_specs,
        out_specs=[pl.BlockSpec((1, t_len, MIX), lambda i: (i, 0, 0)),
                   pl.BlockSpec((1, t_len, KV_RANK), lambda i: (i, 0, 0))],
        scratch_shapes=[pltpu.VMEM((t_len, wide), BF16),
                        pltpu.VMEM((past + t_len, wide), BF16),
                        pltpu.VMEM((past + t_len, wide), BF16)],
        compiler_params=_cp("parallel"),
        name="mla_mixer",
    )(*args)


_S5_SEQ = 8
_S5_TT = 128


def _s5_kernel(zd_ref, bd_ref, cd_ref, are_ref, aim_ref, s0re_ref, s0im_ref,
               y_ref, fre_ref, fim_ref, bre_sc, bim_sc, sre_sc, sim_sc):
    direction = pl.program_id(0)
    k = pl.program_id(2)
    tt = _S5_TT
    rows = _S5_SEQ * tt

    @pl.when(k == 0)
    def _():
        sre_sc[...] = s0re_ref[0]
        sim_sc[...] = s0im_ref[0]

    u = zd_ref[...].reshape(rows, MIX).astype(BF16)
    bre_sc[...] = jnp.dot(u, bd_ref[0, :, 0:S5_STATES], preferred_element_type=F32)
    bim_sc[...] = jnp.dot(u, bd_ref[0, :, S5_STATES:2 * S5_STATES], preferred_element_type=F32)
    a_re = jnp.broadcast_to(are_ref[0], (_S5_SEQ, S5_STATES))
    a_im = jnp.broadcast_to(aim_ref[0], (_S5_SEQ, S5_STATES))

    def step(i, carry):
        s_re, s_im = carry
        t = i + direction * (tt - 1 - 2 * i)
        sel = pl.ds(pl.multiple_of(t * _S5_SEQ, _S5_SEQ), _S5_SEQ)
        n_re = a_re * s_re - a_im * s_im + bre_sc[sel, :]
        n_im = a_re * s_im + a_im * s_re + bim_sc[sel, :]
        bre_sc[sel, :] = n_re
        bim_sc[sel, :] = n_im
        return n_re, n_im

    s_re, s_im = lax.fori_loop(0, tt, step, (sre_sc[...], sim_sc[...]), unroll=2)
    sre_sc[...] = s_re
    sim_sc[...] = s_im
    y = (jnp.dot(bre_sc[...].astype(BF16), cd_ref[0, 0:S5_STATES, :], preferred_element_type=F32)
         + jnp.dot(bim_sc[...].astype(BF16), cd_ref[0, S5_STATES:2 * S5_STATES, :],
                   preferred_element_type=F32))
    y_ref[0] = y.reshape(tt, _S5_SEQ, MIX)

    @pl.when(k == pl.num_programs(2) - 1)
    def _():
        fre_ref[0] = s_re
        fim_ref[0] = s_im


def _s5_scan(zd, bd, cd, ab_re, ab_im, s0_re, s0_im):
    t_len, bsz, _ = zd.shape
    nt = t_len // _S5_TT
    tile = lambda d, k: k + d * (nt - 1 - 2 * k)
    st = pl.BlockSpec((1, _S5_SEQ, S5_STATES), lambda d, g, k: (d, g, 0))
    return pl.pallas_call(
        _s5_kernel,
        out_shape=[jax.ShapeDtypeStruct((2, t_len, bsz, MIX), F32),
                   jax.ShapeDtypeStruct((2, bsz, S5_STATES), F32),
                   jax.ShapeDtypeStruct((2, bsz, S5_STATES), F32)],
        grid=(2, bsz // _S5_SEQ, nt),
        in_specs=[pl.BlockSpec((_S5_TT, _S5_SEQ, MIX), lambda d, g, k: (tile(d, k), g, 0)),
                  pl.BlockSpec((1, MIX, 2 * S5_STATES), lambda d, g, k: (d, 0, 0)),
                  pl.BlockSpec((1, 2 * S5_STATES, MIX), lambda d, g, k: (d, 0, 0)),
                  pl.BlockSpec((1, 1, S5_STATES), lambda d, g, k: (d, 0, 0)),
                  pl.BlockSpec((1, 1, S5_STATES), lambda d, g, k: (d, 0, 0)),
                  st, st],
        out_specs=[pl.BlockSpec((1, _S5_TT, _S5_SEQ, MIX), lambda d, g, k: (d, tile(d, k), g, 0)),
                   st, st],
        scratch_shapes=[pltpu.VMEM((_S5_SEQ * _S5_TT, S5_STATES), F32),
                        pltpu.VMEM((_S5_SEQ * _S5_TT, S5_STATES), F32),
                        pltpu.VMEM((_S5_SEQ, S5_STATES), F32),
                        pltpu.VMEM((_S5_SEQ, S5_STATES), F32)],
        compiler_params=_cp("parallel", "parallel", "arbitrary"),
        name="s5_scan",
    )(zd, bd, cd, ab_re, ab_im, s0_re, s0_im)


def _gelu_tanh(x):
    return 0.5 * x * (1.0 + jnp.tanh(math.sqrt(2.0 / math.pi) * (x + 0.044715 * x * x * x)))


def _route(logits):
    lane = lax.broadcasted_iota(jnp.int32, logits.shape, 1).astype(F32)
    big = float(LANE)
    neg = -jnp.inf
    g_mask = (lane >= N_EXPERT) & (lane < N_EXPERT + N_GROUP)
    gl = jnp.where(g_mask, logits, neg)
    g_max = jnp.max(gl, axis=-1, keepdims=True)
    g_idx = jnp.min(jnp.where(gl == g_max, lane, big), axis=-1, keepdims=True) - N_EXPERT
    g_sel = 1.0 / jnp.sum(jnp.where(g_mask, jnp.exp(logits - g_max), 0.0), axis=-1, keepdims=True)
    lo = g_idx * PER_GROUP
    el = jnp.where((lane >= lo) & (lane < lo + PER_GROUP), logits, neg)
    v1 = jnp.max(el, axis=-1, keepdims=True)
    i1 = jnp.min(jnp.where(el == v1, lane, big), axis=-1, keepdims=True)
    el2 = jnp.where(lane == i1, neg, el)
    v2 = jnp.max(el2, axis=-1, keepdims=True)
    i2 = jnp.min(jnp.where(el2 == v2, lane, big), axis=-1, keepdims=True)
    e2 = jnp.exp(v2 - v1)
    w1 = g_sel / (1.0 + e2)
    return jnp.where(lane == i1, w1, jnp.where(lane == i2, w1 * e2, 0.0))


def _post_kernel(x_ref, ya_ref, yb_ref, yc_ref, ys_ref, zd_ref, mod_ref, wo_ref, d_ref, wglu_ref, bglu_ref,
                 g2_ref, wr_ref, br_ref, x1_ref, h2_ref, gate_ref):
    m = mod_ref[0]
    zd = zd_ref[...]
    ys = ys_ref[0] + ys_ref[1] + d_ref[...] * zd
    gl = _bdot(_gelu_tanh(ys), wglu_ref[...]) + bglu_ref[...]
    yd = gl[:, 0:MIX] * _sigmoid(gl[:, MIX:2 * MIX])
    mix = (_bdot(ya_ref[...], wo_ref[0:MIX, :]) + _bdot(yb_ref[...], wo_ref[MIX:2 * MIX, :])
           + _bdot(yc_ref[...], wo_ref[2 * MIX:3 * MIX, :]) + _bdot(yd, wo_ref[3 * MIX:4 * MIX, :]))
    x1 = x_ref[...] + m[2:3] * mix
    x1_ref[...] = x1
    h2 = _rms(x1) * g2_ref[...] * (1.0 + m[4:5]) + m[3:4]
    h2b = h2.astype(BF16)
    h2_ref[...] = h2b
    gate_ref[...] = _route(jnp.dot(h2b, wr_ref[...], preferred_element_type=F32) + br_ref[...])


def _post_mixer(x, ya, yb, yc, ys, zd, mod, row_fn, w_out, s5_d, w_glu, b_glu, norm2_g, w_router, b_router, tm):
    n = x.shape[0]
    tok = lambda w: pl.BlockSpec((tm, w), lambda i: (i, 0))
    full = lambda a: pl.BlockSpec(a.shape, lambda i: (0,) * a.ndim)
    return pl.pallas_call(
        _post_kernel,
        out_shape=[jax.ShapeDtypeStruct((n, D_MODEL), F32),
                   jax.ShapeDtypeStruct((n, D_MODEL), BF16),
                   jax.ShapeDtypeStruct((n, LANE), F32)],
        grid=(n // tm,),
        in_specs=[tok(D_MODEL), tok(MIX), tok(MIX), tok(MIX),
                  pl.BlockSpec((2, tm, MIX), lambda i: (0, i, 0)), tok(MIX),
                  pl.BlockSpec((1, 6, D_MODEL), lambda i: (row_fn(i, tm), 0, 0)),
                  full(w_out), full(s5_d), full(w_glu), full(b_glu), full(norm2_g),
                  full(w_router), full(b_router)],
        out_specs=[tok(D_MODEL), tok(D_MODEL), tok(LANE)],
        compiler_params=_cp("parallel"),
        name="post_mixer",
    )(x, ya, yb, yc, ys, zd, mod, w_out, s5_d, w_glu, b_glu, norm2_g, w_router, b_router)


_MOE_EPS = 4
_MOE_TM = 1024


_MOE_ALIGN = 16
_MOE_ROWS = 2 * _MOE_TM + N_EXPERT * _MOE_ALIGN
_MOE_CH = 128
_MOE_GB = 512
_MOE_CB = 256


def _moe_kernel(h_ref, gate_ref, x_ref, mod_ref, w1_ref, w3_ref, w2_ref, o_ref,
                xs_sc, ys_sc, col_sc, row_sc, start_sm, rows_sm, done_sm):
    j = pl.program_id(1)
    tm = h_ref.shape[0]

    @pl.when(j == 0)
    def _dispatch():
        gate = gate_ref[...]
        cnt_row = jnp.sum(jnp.where(gate != 0.0, 1.0, 0.0), axis=0, keepdims=True)
        units_row = jnp.floor((cnt_row + (_MOE_ALIGN - 1)) * (1.0 / _MOE_ALIGN))
        li = lax.broadcasted_iota(jnp.int32, (LANE, LANE), 0)
        lj = lax.broadcasted_iota(jnp.int32, (LANE, LANE), 1)
        before = jnp.where(li < lj, 1.0, 0.0).astype(BF16)
        start_row = jnp.dot(jnp.broadcast_to(units_row, (8, LANE)).astype(BF16), before,
                            preferred_element_type=F32) * _MOE_ALIGN
        start_i = start_row.astype(jnp.int32)
        cnt_i = cnt_row.astype(jnp.int32)
        for e in range(N_EXPERT):
            start_sm[e] = start_i[0, e]
            rows_sm[e] = cnt_i[0, e]
        gt = gate.T[0:N_EXPERT, :]
        chosen = gt != 0.0
        ones = jnp.where(chosen, 1.0, 0.0)
        cnt = jnp.sum(ones, axis=-1, keepdims=True)
        units = jnp.floor((cnt + (_MOE_ALIGN - 1)) * (1.0 / _MOE_ALIGN))
        ei = lax.broadcasted_iota(jnp.int32, (N_EXPERT, N_EXPERT), 0)
        ej = lax.broadcasted_iota(jnp.int32, (N_EXPERT, N_EXPERT), 1)
        start = jnp.dot(jnp.where(ej < ei, 1.0, 0.0).astype(BF16),
                        jnp.broadcast_to(units, (N_EXPERT, LANE)).astype(BF16),
                        preferred_element_type=F32)[:, 0:1] * _MOE_ALIGN
        ti = lax.broadcasted_iota(jnp.int32, (tm, tm), 0)
        tj = lax.broadcasted_iota(jnp.int32, (tm, tm), 1)
        earlier = jnp.where(ti < tj, 1.0, 0.0).astype(BF16)
        pos = start + jnp.dot(ones.astype(BF16), earlier, preferred_element_type=F32)
        p_a = jnp.min(jnp.where(chosen, pos, float(_MOE_ROWS)), axis=0, keepdims=True)
        p_b = jnp.max(jnp.where(chosen, pos, -1.0), axis=0, keepdims=True)
        g_a = jnp.sum(jnp.where(chosen & (pos == p_a), gt, 0.0), axis=0, keepdims=True)
        g_b = jnp.sum(jnp.where(chosen & (pos == p_b) & (p_b != p_a), gt, 0.0), axis=0, keepdims=True)
        row_sc[...] = jnp.concatenate([p_a, p_b, jnp.zeros((6, tm), F32)], axis=0)
        packed = jnp.concatenate([p_a, p_b, g_a, g_b, jnp.zeros((LANE - 4, tm), F32)], axis=0)
        col_sc[...] = packed.T
        xs_sc[_MOE_ROWS:_MOE_ROWS + _MOE_CH, :] = jnp.zeros((_MOE_CH, D_MODEL), BF16)
        ys_sc[...] = jnp.zeros_like(ys_sc)
        o_ref[...] = x_ref[...]
        done_sm[0] = 0
        done_sm[1] = 0

    n_blocks = _MOE_ROWS // _MOE_GB
    e_last = j * _MOE_EPS + (_MOE_EPS - 1)
    seg_end = start_sm[e_last] + rows_sm[e_last]
    last_step = j == pl.num_programs(1) - 1

    def gather_block(b, carry):
        r0 = pl.multiple_of(b * _MOE_GB, _MOE_GB)
        ri = (lax.broadcasted_iota(jnp.int32, (_MOE_GB, tm), 0) + r0).astype(F32)
        sel = jnp.where((ri == row_sc[0:1, :]) | (ri == row_sc[1:2, :]), 1.0, 0.0).astype(BF16)
        xs_sc[pl.ds(r0, _MOE_GB), :] = jnp.dot(sel, h_ref[...], preferred_element_type=F32).astype(BF16)
        return carry

    gathered = jnp.minimum((seg_end + _MOE_CH + _MOE_GB - 1) // _MOE_GB, n_blocks)
    lax.fori_loop(done_sm[0], gathered, gather_block, 0)
    done_sm[0] = gathered

    def expert_rows(el, i):
        rows = pl.ds(pl.multiple_of(start_sm[j * _MOE_EPS + el] + i * _MOE_CH, _MOE_ALIGN), _MOE_CH)
        xs = xs_sc[rows, :]
        h1 = jnp.dot(xs, w1_ref[el], preferred_element_type=F32)
        h3 = jnp.dot(xs, w3_ref[el], preferred_element_type=F32)
        return rows, jnp.dot((h1 * _sigmoid(h1) * h3).astype(BF16), w2_ref[el], preferred_element_type=F32)

    for el in range(_MOE_EPS):
        rows, y = expert_rows(el, 0)
        ys_sc[rows, :] = y.astype(BF16)
    for el in range(_MOE_EPS):
        n_rows = rows_sm[j * _MOE_EPS + el]

        def chunk(i, carry, el=el, n_rows=n_rows):
            rows, y = expert_rows(el, i)
            live = lax.broadcasted_iota(jnp.int32, (_MOE_CH, 1), 0) + i * _MOE_CH < n_rows
            ys_sc[rows, :] = jnp.where(live, y, ys_sc[rows, :].astype(F32)).astype(BF16)
            return carry

        lax.fori_loop(1, (n_rows + _MOE_CH - 1) // _MOE_CH, chunk, 0)

    g2 = mod_ref[0][5:6]

    def combine_block(b, carry):
        r0 = pl.multiple_of(b * _MOE_GB, _MOE_GB)
        y = ys_sc[pl.ds(r0, _MOE_GB), :]
        for t0 in range(0, tm, _MOE_CB):
            c = col_sc[t0:t0 + _MOE_CB, :]
            ri = (lax.broadcasted_iota(jnp.int32, (_MOE_CB, _MOE_GB), 1) + r0).astype(F32)
            w = (jnp.where(ri == c[:, 0:1], c[:, 2:3], 0.0)
                 + jnp.where(ri == c[:, 1:2], c[:, 3:4], 0.0)).astype(BF16)
            o_ref[t0:t0 + _MOE_CB, :] += g2 * jnp.dot(w, y, preferred_element_type=F32)
        return carry

    final_rows = (seg_end + _MOE_ALIGN - 1) // _MOE_ALIGN * _MOE_ALIGN
    combined = jnp.where(last_step, n_blocks, final_rows // _MOE_GB)
    lax.fori_loop(done_sm[1], combined, combine_block, 0)
    done_sm[1] = combined


def _moe(h2, gate, x1, mod, row_fn, w1, w3, w2, tm):
    n = h2.shape[0]
    return pl.pallas_call(
        _moe_kernel,
        out_shape=jax.ShapeDtypeStruct((n, D_MODEL), F32),
        grid=(n // tm, N_EXPERT // _MOE_EPS),
        in_specs=[pl.BlockSpec((tm, D_MODEL), lambda i, j: (i, 0)),
                  pl.BlockSpec((tm, LANE), lambda i, j: (i, 0)),
                  pl.BlockSpec((tm, D_MODEL), lambda i, j: (i, 0)),
                  pl.BlockSpec((1, 6, D_MODEL), lambda i, j: (row_fn(i, tm), 0, 0)),
                  pl.BlockSpec((_MOE_EPS, D_MODEL, MOE_FF), lambda i, j: (j, 0, 0)),
                  pl.BlockSpec((_MOE_EPS, D_MODEL, MOE_FF), lambda i, j: (j, 0, 0)),
                  pl.BlockSpec((_MOE_EPS, MOE_FF, D_MODEL), lambda i, j: (j, 0, 0))],
        out_specs=pl.BlockSpec((tm, D_MODEL), lambda i, j: (i, 0)),
        scratch_shapes=[pltpu.VMEM((_MOE_ROWS + _MOE_CH, D_MODEL), BF16),
                        pltpu.VMEM((_MOE_ROWS + _MOE_CH, D_MODEL), BF16),
                        pltpu.VMEM((tm, LANE), F32),
                        pltpu.VMEM((8, tm), F32),
                        pltpu.SMEM((N_EXPERT,), jnp.int32),
                        pltpu.SMEM((N_EXPERT,), jnp.int32),
                        pltpu.SMEM((2,), jnp.int32)],
        compiler_params=_cp("parallel", "arbitrary"),
        name="moe",
    )(h2, gate, x1, mod, w1, w3, w2)


def _final_kernel(x_ref, g_ref, o_ref):
    o_ref[...] = _rms(x_ref[...]) * g_ref[...]


def _final_norm(x, g, tm):
    n = x.shape[0]
    return pl.pallas_call(
        _final_kernel,
        out_shape=jax.ShapeDtypeStruct((n, D_MODEL), F32),
        grid=(n // tm,),
        in_specs=[pl.BlockSpec((tm, D_MODEL), lambda i: (i, 0)),
                  pl.BlockSpec((1, D_MODEL), lambda i: (0, 0))],
        out_specs=pl.BlockSpec((tm, D_MODEL), lambda i: (i, 0)),
        compiler_params=_cp("parallel"),
        name="final_norm",
    )(x, g)


def _rot_cols(w):
    q = ROPE // 4
    return jnp.concatenate([-w[..., q:2 * q], w[..., 0:q], -w[..., 3 * q:4 * q], w[..., 2 * q:3 * q]], axis=-1)


def _rope_slot(w):
    pad = [(0, 0)] * (w.ndim - 1) + [(HD, LANE - HD - ROPE)]
    return jnp.pad(w, pad)


def _gate_lanes(g):
    n = g.shape[-1] // 2
    pad = [(0, 0)] * (g.ndim - 1) + [(0, LANE - n)]
    return jnp.concatenate([jnp.pad(g[..., :n], pad), jnp.pad(g[..., n:], pad)], axis=-1)


def _prep_w_in(w_in, gate_cols):
    a = w_in[..., 0:512]
    qkvo = w_in[..., 512:1536]
    g = _gate_lanes(w_in[..., 1536:1536 + gate_cols])
    c0 = 1536 + gate_cols
    cq = w_in[..., c0:c0 + Q_RANK]
    ckv = w_in[..., c0 + Q_RANK:c0 + Q_RANK + KV_RANK]
    kr = w_in[..., c0 + Q_RANK + KV_RANK:c0 + Q_RANK + KV_RANK + ROPE]
    d = w_in[..., c0 + Q_RANK + KV_RANK + ROPE:]
    return jnp.concatenate([a, qkvo, g, cq, ckv, _rope_slot(kr), _rope_slot(_rot_cols(kr)), d],
                           axis=-1).astype(BF16)


def _prep_mla(w_uq, w_ukv):
    n_layer = w_uq.shape[0]
    scale = (HD + ROPE) ** -0.5
    wq = w_uq.reshape(n_layer, Q_RANK, HEADS, HD + ROPE) * scale
    nope, rope = wq[..., :HD], wq[..., HD:]
    zeros_r = jnp.zeros_like(rope)
    wqa = jnp.concatenate([nope, rope, zeros_r], axis=-1).reshape(n_layer, Q_RANK, HEADS * LANE)
    wqb = jnp.concatenate([jnp.zeros_like(nope), _rot_cols(rope), zeros_r], axis=-1)
    wqb = wqb.reshape(n_layer, Q_RANK, HEADS * LANE)
    wkv = w_ukv.reshape(n_layer, KV_RANK, HEADS, 2 * HD)
    k_nope, val = wkv[..., :HD], wkv[..., HD:]
    zeros_h = jnp.zeros_like(k_nope)
    wk = jnp.concatenate([k_nope, zeros_h], axis=-1).reshape(n_layer, KV_RANK, HEADS * LANE)
    even = jnp.concatenate([val, zeros_h], axis=-1)
    odd = jnp.concatenate([zeros_h, val], axis=-1)
    is_odd = (jnp.arange(HEADS) % 2 == 1)[None, None, :, None]
    wv = jnp.where(is_odd, odd, even).reshape(n_layer, KV_RANK, HEADS * LANE)
    return wqa.astype(BF16), wqb.astype(BF16), wk.astype(BF16), wv.astype(BF16)


def _rope_tables(t_len, rotate):
    ones = jnp.ones((t_len, HD), F32)
    zeros = jnp.zeros((t_len, HD), F32)
    tail = jnp.zeros((t_len, LANE - HD - ROPE), F32)
    if not rotate:
        return (jnp.concatenate([ones, jnp.ones((t_len, ROPE), F32), tail], axis=-1),
                jnp.zeros((t_len, LANE), F32))
    rows = t_len // GRID_W
    row = jnp.repeat(jnp.arange(rows, dtype=F32), GRID_W)
    col = jnp.tile(jnp.arange(GRID_W, dtype=F32), rows)
    nf = ROPE // 4
    inv = ROPE_BASE ** (-jnp.arange(nf, dtype=F32) / nf)
    ar = row[:, None] * inv
    ac = col[:, None] * inv
    cos = jnp.concatenate([jnp.cos(ar), jnp.cos(ar), jnp.cos(ac), jnp.cos(ac)], axis=-1)
    sin = jnp.concatenate([jnp.sin(ar), jnp.sin(ar), jnp.sin(ac), jnp.sin(ac)], axis=-1)
    return (jnp.concatenate([ones, cos, tail], axis=-1), jnp.concatenate([zeros, sin, tail], axis=-1))


def _prep_s5(bb_re, bb_im, c_re, c_im):
    eye = jnp.eye(S5_G, dtype=F32)
    to_b = lambda bb: jnp.einsum("ldgnc,gh->ldgchn", bb, eye).reshape(bb.shape[0], 2, MIX, S5_STATES)
    to_c = lambda cc: jnp.einsum("ldgcn,gh->ldgnhc", cc, eye).reshape(cc.shape[0], 2, S5_STATES, MIX)
    bd = jnp.concatenate([to_b(bb_re), to_b(bb_im)], axis=-1)
    cd = jnp.concatenate([to_c(c_re.astype(F32)), -to_c(c_im.astype(F32))], axis=-2)
    return bd.astype(BF16), cd.astype(BF16)


def _layer(x, bsz, t_len, mod, row_fn, p, ctx, tm):
    za, zb, zg, zc, zd = _pre_mixer(x, mod, row_fn, p["norm1_g"], p["w_in"], tm)
    seq = lambda a: a.reshape(bsz, t_len, a.shape[-1])
    ya = _conv_module(seq(za), p["conv_w"], p["conv_b"], p["conv_ln_g"], p["conv_ln_b"])
    yb, c_fin, n_fin, m_fin = _mlstm_mixer(seq(zb), seq(zg), p["gate_b"], p["mlstm_norm_g"],
                                           ctx["mlstm_c"], ctx["mlstm_n"], ctx["mlstm_m"])
    yc, ckv = _mla_mixer(seq(zc), ctx["cos"], ctx["sin"], p["mla_q_norm_g"], p["mla_kv_norm_g"],
                         p["wqa"], p["wqb"], p["wk"], p["wv"], ctx.get("past_ckv"), ctx.get("past_kr"))
    ys, s_re, s_im = _s5_scan(seq(zd).transpose(1, 0, 2), p["s5_bd"], p["s5_cd"], p["s5_ab_re"], p["s5_ab_im"],
                              ctx["s5_re"], ctx["s5_im"])
    ys = ys.transpose(0, 2, 1, 3)
    flat = lambda a: a.reshape(bsz * t_len, a.shape[-1])
    x1, h2, gate = _post_mixer(x, flat(ya), flat(yb), flat(yc), ys.reshape(2, bsz * t_len, MIX), zd, mod, row_fn,
                               p["w_out"], p["s5_d"], p["s5_w_glu"], p["s5_b_glu"], p["norm2_g"],
                               p["w_router"], p["b_router"], tm)
    x2 = _moe(h2, gate, x1, mod, row_fn, p["moe_w1"], p["moe_w3"], p["moe_w2"], _MOE_TM)
    krope = seq(zc)[:, :, Q_RANK + KV_RANK + HD:Q_RANK + KV_RANK + HD + ROPE]
    return x2, (ckv, krope, c_fin, n_fin, m_fin, s_re, s_im)


def kernel(x_prompt, x_sample, cache_mla_ckv, cache_mla_krope, state_mlstm_C, state_mlstm_n, state_mlstm_m, state_s5, c, c_ctx, norm1_g, norm2_g, final_g, w_mod, b_mod, w_in, w_out, conv_w, conv_b, conv_ln_g, conv_ln_b, mlstm_gate_b, mlstm_norm_g, mla_q_norm_g, mla_w_uq, mla_kv_norm_g, mla_w_ukv, s5_a_re, s5_a_im, s5_log_dt, s5_b_re, s5_b_im, s5_c_re, s5_c_im, s5_d, s5_w_glu, s5_b_glu, moe_w_group, moe_b_group, moe_w_expert, moe_b_expert, moe_w1, moe_w3, moe_w2):
    n_layer = w_in.shape[0]
    b_ctx, t_ctx, d = x_prompt.shape
    b_lat, t_lat, _ = x_sample.shape
    nd = 2 * HEADS
    tm = 512
    assert b_lat + 1 <= 16 and t_lat % _MOE_TM == 0 and (b_ctx * t_ctx) % _MOE_TM == 0

    c_all = jnp.zeros((16, d), F32).at[0].set(c_ctx).at[1:1 + b_lat].set(c)
    mod_all = _modulation(c_all, w_mod, b_mod).reshape(n_layer, 16, 6, d)
    gate_cols = 4 * HEADS
    w_in_ext = _prep_w_in(w_in, gate_cols)
    wqa, wqb, wk, wv = _prep_mla(mla_w_uq, mla_w_ukv)
    ab_re, ab_im, bb_re, bb_im = _s5_discretise(s5_a_re, s5_a_im, s5_log_dt, s5_b_re, s5_b_im)
    s5_bd, s5_cd = _prep_s5(bb_re, bb_im, s5_c_re, s5_c_im)
    w_router = jnp.pad(jnp.concatenate([moe_w_expert, moe_w_group], axis=-1),
                       [(0, 0), (0, 0), (0, LANE - N_EXPERT - N_GROUP)]).astype(BF16)
    b_router = jnp.pad(jnp.concatenate([moe_b_expert, moe_b_group], axis=-1),
                       [(0, 0), (0, LANE - N_EXPERT - N_GROUP)])
    gate_b = _gate_lanes(mlstm_gate_b.reshape(n_layer, gate_cols))
    conv_w_p = jnp.pad(conv_w, [(0, 0), (0, 32 - CONV_WIDTH), (0, 0)])
    w_out_b = w_out.astype(BF16)
    w_glu_b = s5_w_glu.astype(BF16)
    w1_b, w3_b, w2_b = moe_w1.astype(BF16), moe_w3.astype(BF16), moe_w2.astype(BF16)
    row = lambda a, l: a[l][None, :]

    cos_ctx, sin_ctx = _rope_tables(t_ctx, rotate=False)
    cos_lat, sin_lat = _rope_tables(t_lat, rotate=True)
    zero_state = dict(
        mlstm_c=jnp.zeros((b_ctx, nd, HD, HD), F32), mlstm_n=jnp.zeros((b_ctx, nd, HD), F32),
        mlstm_m=jnp.zeros((b_ctx, nd, LANE), F32),
        s5_re=jnp.zeros((2, b_ctx, S5_STATES), F32), s5_im=jnp.zeros((2, b_ctx, S5_STATES), F32),
        cos=cos_ctx, sin=sin_ctx)

    row_ctx = lambda i, tile: 0
    row_lat = lambda i, tile: 1 + (i * tile) // t_lat

    x_ctx = x_prompt.reshape(b_ctx * t_ctx, d)
    x_lat = x_sample.reshape(b_lat * t_lat, d)
    outs = []
    for l in range(n_layer):
        p = dict(norm1_g=row(norm1_g, l), norm2_g=row(norm2_g, l), w_in=w_in_ext[l], w_out=w_out_b[l],
                 conv_w=conv_w_p[l], conv_b=row(conv_b, l), conv_ln_g=row(conv_ln_g, l),
                 conv_ln_b=row(conv_ln_b, l), gate_b=row(gate_b, l), mlstm_norm_g=row(mlstm_norm_g, l),
                 mla_q_norm_g=row(mla_q_norm_g, l), mla_kv_norm_g=row(mla_kv_norm_g, l),
                 wqa=wqa[l], wqb=wqb[l], wk=wk[l], wv=wv[l],
                 s5_bd=s5_bd[l], s5_cd=s5_cd[l], s5_ab_re=ab_re[l][:, None, :], s5_ab_im=ab_im[l][:, None, :],
                 s5_d=row(s5_d, l), s5_w_glu=w_glu_b[l], s5_b_glu=row(s5_b_glu, l),
                 w_router=w_router[l], b_router=row(b_router, l),
                 moe_w1=w1_b[l], moe_w3=w3_b[l], moe_w2=w2_b[l])
        x_ctx, st = _layer(x_ctx, b_ctx, t_ctx, mod_all[l], row_ctx, p, zero_state, tm)
        outs.append(st)
        s5_l = state_s5[:, l].reshape(b_lat, 2, S5_STATES, 2)
        lat_state = dict(
            mlstm_c=jnp.swapaxes(state_mlstm_C[:, l].reshape(b_lat, nd, HD, HD), -1, -2),
            mlstm_n=state_mlstm_n[:, l].reshape(b_lat, nd, HD),
            mlstm_m=jnp.broadcast_to(state_mlstm_m[:, l].reshape(b_lat, nd, 1), (b_lat, nd, LANE)),
            s5_re=s5_l[..., 0].transpose(1, 0, 2), s5_im=s5_l[..., 1].transpose(1, 0, 2),
            cos=cos_lat, sin=sin_lat,
            past_ckv=cache_mla_ckv[:, l], past_kr=_rope_slot(cache_mla_krope[:, l]))
        x_lat, _ = _layer(x_lat, b_lat, t_lat, mod_all[l], row_lat, p, lat_state, tm)

    y_prompt = _final_norm(x_ctx, final_g[None, :], tm).reshape(b_ctx, t_ctx, d)
    y_sample = _final_norm(x_lat, final_g[None, :], tm).reshape(b_lat, t_lat, d)
    stack = lambda i: jnp.stack([o[i] for o in outs], axis=1)
    new_ckv = stack(0)
    new_krope = stack(1)
    new_c = jnp.swapaxes(stack(2), -1, -2).reshape(b_ctx, n_layer, 2, HEADS, HD, HD)
    new_n = stack(3).reshape(b_ctx, n_layer, 2, HEADS, HD)
    new_m = stack(4)[..., 0].reshape(b_ctx, n_layer, 2, HEADS)
    s_re = jnp.stack([o[5] for o in outs], axis=0)
    s_im = jnp.stack([o[6] for o in outs], axis=0)
    new_s5 = jnp.stack([s_re, s_im], axis=-1).transpose(2, 0, 1, 3, 4)
    new_s5 = new_s5.reshape(b_ctx, n_layer, 2, S5_G, S5_N, 2)
    return (y_prompt, y_sample, new_ckv, new_krope, new_c, new_n, new_m, new_s5)
```

```python
import functools
import math

import jax
import jax.numpy as jnp
from jax import lax
from jax.experimental import pallas as pl
from jax.experimental.pallas import tpu as pltpu

F32 = jnp.float32
BF16 = jnp.bfloat16
EPS = 1e-6

D_MODEL = 1024
MIX = 256
CONV_WIDTH = 31
HEADS = 4
HD = 64
CHUNK = 128
ROPE = 32
KV_RANK = 128
Q_RANK = 256
GRID_W = 64
ROPE_BASE = 10000.0
S5_G = 16
S5_GC = 16
S5_N = 64
S5_STATES = S5_G * S5_N
N_EXPERT = 32
PER_GROUP = 8
N_GROUP = 4
MOE_FF = 256
LANE = 128
Z_COLS = 2688
VMEM_LIMIT = 56 * 1024 * 1024


def _cp(*sem):
    return pltpu.CompilerParams(dimension_semantics=sem, vmem_limit_bytes=VMEM_LIMIT)


def _rms(x):
    return x * lax.rsqrt(jnp.mean(x * x, axis=-1, keepdims=True) + EPS)


def _sigmoid(x):
    return 1.0 / (1.0 + jnp.exp(-x))


def _bdot(a, b):
    return jnp.dot(a.astype(BF16), b.astype(BF16), preferred_element_type=F32)


def _mod_kernel(c_ref, w_ref, b_ref, o_ref):
    c = c_ref[...]
    o_ref[0] = _bdot(c * _sigmoid(c), w_ref[0]) + b_ref[0]


def _modulation(c_all, w_mod, b_mod):
    n_layer, d, n = w_mod.shape
    tn = 1536
    return pl.pallas_call(
        _mod_kernel,
        out_shape=jax.ShapeDtypeStruct((n_layer, 16, n), F32),
        grid=(n_layer, n // tn),
        in_specs=[pl.BlockSpec((16, d), lambda l, j: (0, 0)),
                  pl.BlockSpec((1, d, tn), lambda l, j: (l, 0, j)),
                  pl.BlockSpec((1, 1, tn), lambda l, j: (l, 0, j))],
        out_specs=pl.BlockSpec((1, 16, tn), lambda l, j: (l, 0, j)),
        compiler_params=_cp("parallel", "parallel"),
        name="modulation",
    )(c_all, w_mod, b_mod.reshape(n_layer, 1, n))


def _s5_disc_kernel(are_ref, aim_ref, ldt_ref, bre_ref, bim_ref, abre_ref, abim_ref, bbre_ref, bbim_ref):
    a_re = are_ref[...]
    a_im = aim_ref[...]
    dt = jnp.exp(ldt_ref[...])
    mag = jnp.exp(a_re * dt)
    ab_re = mag * jnp.cos(a_im * dt)
    ab_im = mag * jnp.sin(a_im * dt)
    den = a_re * a_re + a_im * a_im
    f_re = ((ab_re - 1.0) * a_re + ab_im * a_im) / den
    f_im = (ab_im * a_re - (ab_re - 1.0) * a_im) / den
    b_re = bre_ref[...]
    b_im = bim_ref[...]
    abre_ref[...] = ab_re
    abim_ref[...] = ab_im
    bbre_ref[...] = f_re * b_re - f_im * b_im
    bbim_ref[...] = f_re * b_im + f_im * b_re


def _s5_discretise(a_re, a_im, log_dt, b_re, b_im):
    n_layer = a_re.shape[0]
    rows = n_layer * 2 * S5_G
    cols = S5_N * S5_GC
    rep = lambda a: jnp.repeat(a.reshape(rows, S5_N), S5_GC, axis=1)
    ldt = jnp.broadcast_to(log_dt.reshape(rows, 1), (rows, cols))
    spec = pl.BlockSpec((rows, cols), lambda: (0, 0))
    ab_re, ab_im, bb_re, bb_im = pl.pallas_call(
        _s5_disc_kernel,
        out_shape=[jax.ShapeDtypeStruct((rows, cols), F32)] * 4,
        in_specs=[spec] * 5,
        out_specs=[spec] * 4,
        name="s5_discretise",
    )(rep(a_re), rep(a_im), ldt, b_re.reshape(rows, cols), b_im.reshape(rows, cols))
    pick = lambda a: a[:, ::S5_GC].reshape(n_layer, 2, S5_STATES)
    shp = (n_layer, 2, S5_G, S5_N, S5_GC)
    return pick(ab_re), pick(ab_im), bb_re.reshape(shp), bb_im.reshape(shp)


def _pre_kernel(x_ref, mod_ref, g_ref, w_ref, za_ref, zb_ref, zg_ref, zc_ref, zd_ref):
    m = mod_ref[0]
    h = _rms(x_ref[...]) * g_ref[...] * (1.0 + m[1:2]) + m[0:1]
    hb = h.astype(BF16)
    col = 0
    for o_ref in (za_ref, zb_ref, zg_ref, zc_ref, zd_ref):
        n = o_ref.shape[-1]
        o_ref[...] = jnp.dot(hb, w_ref[:, col:col + n], preferred_element_type=F32)
        col += n


def _pre_mixer(x, mod, row_fn, norm_g, w_in_ext, tm):
    n = x.shape[0]
    widths = (512, 1024, 2 * LANE, 640, MIX)
    return pl.pallas_call(
        _pre_kernel,
        out_shape=[jax.ShapeDtypeStruct((n, w), F32) for w in widths],
        grid=(n // tm,),
        in_specs=[pl.BlockSpec((tm, D_MODEL), lambda i: (i, 0)),
                  pl.BlockSpec((1, 6, D_MODEL), lambda i: (row_fn(i, tm), 0, 0)),
                  pl.BlockSpec((1, D_MODEL), lambda i: (0, 0)),
                  pl.BlockSpec((D_MODEL, Z_COLS), lambda i: (0, 0))],
        out_specs=[pl.BlockSpec((tm, w), lambda i: (i, 0)) for w in widths],
        compiler_params=_cp("parallel"),
        name="pre_mixer",
    )(x, mod, norm_g, w_in_ext)


_CONV_PAD = 16
_CONV_TT = 128


def _conv_kernel(za_ref, w_ref, b_ref, lg_ref, lb_ref, o_ref, hp_ref, sh_ref):
    t_len = o_ref.shape[1]
    u = za_ref[0]
    hp_ref[0:_CONV_PAD, :] = jnp.zeros((_CONV_PAD, MIX), F32)
    hp_ref[_CONV_PAD + t_len:2 * _CONV_PAD + t_len, :] = jnp.zeros((_CONV_PAD, MIX), F32)
    hp_ref[_CONV_PAD:_CONV_PAD + t_len, :] = u[:, :MIX] * _sigmoid(u[:, MIX:])
    span = t_len + 2 * _CONV_PAD - 8
    for off in range(8):
        sh_ref[off, 0:span, :] = hp_ref[off:off + span, :]
    w = w_ref[...]
    half = CONV_WIDTH // 2
    for t0 in range(0, t_len, _CONV_TT):
        acc = jnp.zeros((_CONV_TT, MIX), F32) + b_ref[...]
        for k in range(CONV_WIDTH):
            start = t0 + _CONV_PAD - half + k
            aligned = start // 8 * 8
            acc = acc + sh_ref[start - aligned, aligned:aligned + _CONV_TT, :] * w[k:k + 1, :]
        mu = jnp.mean(acc, axis=-1, keepdims=True)
        cen = acc - mu
        var = jnp.mean(cen * cen, axis=-1, keepdims=True)
        yn = cen * lax.rsqrt(var + EPS) * lg_ref[...] + lb_ref[...]
        o_ref[0, t0:t0 + _CONV_TT, :] = yn * _sigmoid(yn)


def _conv_module(za, w, b, ln_g, ln_b):
    bsz, t_len, _ = za.shape
    vec = pl.BlockSpec((1, MIX), lambda i: (0, 0))
    return pl.pallas_call(
        _conv_kernel,
        out_shape=jax.ShapeDtypeStruct((bsz, t_len, MIX), F32),
        grid=(bsz,),
        in_specs=[pl.BlockSpec((1, t_len, 2 * MIX), lambda i: (i, 0, 0)),
                  pl.BlockSpec((32, MIX), lambda i: (0, 0)), vec, vec, vec],
        out_specs=pl.BlockSpec((1, t_len, MIX), lambda i: (i, 0, 0)),
        scratch_shapes=[pltpu.VMEM((t_len + 2 * _CONV_PAD, MIX), F32),
                        pltpu.VMEM((8, t_len + 2 * _CONV_PAD, MIX), F32)],
        compiler_params=_cp("parallel"),
        name="conv_module",
    )(za, w, b, ln_g, ln_b)


def _log_sigmoid(x):
    return jnp.minimum(x, 0.0) - jnp.log(1.0 + jnp.exp(-jnp.abs(x)))


def _split3_dot(tri, x):
    hi = x.astype(BF16)
    r1 = x - hi.astype(F32)
    mid = r1.astype(BF16)
    lo = (r1 - mid.astype(F32)).astype(BF16)
    dot = lambda v: jnp.dot(tri, v, preferred_element_type=F32)
    return dot(hi) + dot(mid) + dot(lo)


def _split2(x):
    hi = x.astype(BF16)
    return hi, (x - hi.astype(F32)).astype(BF16)


def _mlstm_chunk(zb_ref, zg_ref, gb_ref, qt_sc, qbd_sc, vt_sc, vbd_sc, ct_sc, nb_sc, c, direction, ms):
    L = CHUNK
    rows = pl.ds(pl.multiple_of(c * L, L), L)
    si = lax.broadcasted_iota(jnp.int32, (L, L), 0)
    ti = lax.broadcasted_iota(jnp.int32, (L, L), 1)
    if direction == 0:
        valid = si <= ti
        tri = jnp.where(ti <= si, 1.0, 0.0).astype(BF16)
        last = L - 1
    else:
        valid = si >= ti
        tri = jnp.where(ti >= si, 1.0, 0.0).astype(BF16)
        last = 0
    g_in = zg_ref[0, rows, 0:LANE] + gb_ref[:, 0:LANE]
    g_f = zg_ref[0, rows, LANE:2 * LANE] + gb_ref[:, LANE:2 * LANE]
    bc = _split3_dot(tri, _log_sigmoid(g_f))
    r = g_in - bc
    g_in_t = g_in.T
    bct = bc.T
    k = zb_ref[0, rows, MIX:2 * MIX]
    k_hi, k_lo = _split2(k)
    qt = qt_sc[c]
    st_all = jnp.dot(k_hi, qbd_sc[c], preferred_element_type=F32)
    sts, inters, floors, colsums, ws, decays, new_m = [], [], [], [], [], [], []
    for h in range(HEADS):
        j = direction * HEADS + h
        i_row = g_in_t[j:j + 1, :]
        b_row = bct[j:j + 1, :]
        m_prev = ms[h]
        rb = jnp.where(valid, jnp.broadcast_to(r[:, j:j + 1], (L, L)), -jnp.inf)
        c_row = jnp.maximum(m_prev, jnp.max(rb, axis=0, keepdims=True))
        st = st_all[:, h * L:(h + 1) * L] * jnp.exp(rb - c_row)
        sts.append(st.astype(BF16))
        colsums.append(jnp.sum(st, axis=0, keepdims=True))
        inters.append(jnp.exp(m_prev - c_row))
        floors.append(jnp.exp(-(b_row + c_row)))
        b_last = b_row[:, last:last + 1]
        lw = b_last - b_row + i_row
        m_new = jnp.maximum(b_last + m_prev, jnp.max(lw, axis=-1, keepdims=True))
        decays.append(jnp.exp(b_last + m_prev - m_new))
        ws.append(jnp.exp(lw - m_new))
        new_m.append(m_new)
    per_head_rows = lambda vs: jnp.concatenate([jnp.broadcast_to(v, (HD, L)) for v in vs], axis=0)
    ct = ct_sc[direction]
    nb = nb_sc[direction]
    num = (per_head_rows(inters) * jnp.dot(ct.astype(BF16), qt, preferred_element_type=F32)
           + jnp.dot(vbd_sc[c], jnp.concatenate(sts, axis=0), preferred_element_type=F32))
    n_hi, n_lo = _split2(nb)
    qn = jnp.dot(n_hi, qt, preferred_element_type=F32) + jnp.dot(n_lo, qt, preferred_element_type=F32)
    dens = [jnp.maximum(jnp.abs(inters[h] * qn[h:h + 1, :] + colsums[h]), floors[h]) for h in range(HEADS)]
    ht = num / per_head_rows(dens)
    lane = lax.broadcasted_iota(jnp.int32, (1, MIX), 1) // HD
    dec_row = sum(jnp.where(lane == h, decays[h], 0.0) for h in range(HEADS))
    upd = jnp.dot((vt_sc[c] * per_head_rows(ws)).astype(BF16), k_hi, preferred_element_type=F32)
    same_head = (lax.broadcasted_iota(jnp.int32, (MIX, MIX), 0) // HD
                 == lax.broadcasted_iota(jnp.int32, (MIX, MIX), 1) // HD)
    ct_sc[direction] = jnp.where(same_head, ct * dec_row + upd, 0.0)
    w_hi, w_lo = _split2(jnp.concatenate(ws + [jnp.zeros((8 - HEADS, L), F32)], axis=0))
    wk = (jnp.dot(w_hi, k_hi, preferred_element_type=F32) + jnp.dot(w_lo, k_hi, preferred_element_type=F32)
          + jnp.dot(w_hi, k_lo, preferred_element_type=F32))
    own = lax.broadcasted_iota(jnp.int32, (8, MIX), 0) == lax.broadcasted_iota(jnp.int32, (8, MIX), 1) // HD
    nb_sc[direction] = jnp.where(own, nb * dec_row + wk, 0.0)
    return ht, tuple(new_m)


def _mlstm_kernel(zb_ref, zg_ref, gb_ref, ng_ref, c0_ref, n0_ref, m0_ref,
                  y_ref, c_ref, n_ref, m_ref, qt_sc, qbd_sc, vt_sc, vbd_sc, ht_sc, ct_sc, nb_sc):
    t_len = y_ref.shape[1]
    nc = t_len // CHUNK
    L = CHUNK
    qbd_sc[...] = jnp.zeros_like(qbd_sc)
    vbd_sc[...] = jnp.zeros_like(vbd_sc)
    ct_sc[...] = jnp.zeros_like(ct_sc)
    nb_sc[...] = jnp.zeros_like(nb_sc)

    def transpose_chunk(c, carry):
        rows = pl.ds(pl.multiple_of(c * L, L), L)
        qt = (zb_ref[0, rows, 0:MIX] * (HD ** -0.5)).T.astype(BF16)
        vt = zb_ref[0, rows, 2 * MIX:3 * MIX].T
        qt_sc[c] = qt
        vt_sc[c] = vt
        for h in range(HEADS):
            sl = slice(h * HD, (h + 1) * HD)
            qbd_sc[c, sl, h * L:(h + 1) * L] = qt[sl, :]
            vbd_sc[c, sl, h * L:(h + 1) * L] = vt[sl, :].astype(BF16)
        return carry

    lax.fori_loop(0, nc, transpose_chunk, 0)

    for j in range(2 * HEADS):
        d, h = divmod(j, HEADS)
        sl = slice(h * HD, (h + 1) * HD)
        ct_sc[d, sl, sl] = c0_ref[0, j]
        nb_sc[d, h:h + 1, sl] = n0_ref[0, j:j + 1, :]
    m_init = tuple(tuple(m0_ref[0, j:j + 1, 0:1] for j in range(d * HEADS, (d + 1) * HEADS)) for d in range(2))

    def scan_body(i, ms):
        cb = nc - 1 - i
        scr = (zb_ref, zg_ref, gb_ref, qt_sc, qbd_sc, vt_sc, vbd_sc, ct_sc, nb_sc)
        ht_sc[0, i], ms_f = _mlstm_chunk(*scr, i, 0, ms[0])
        ht_sc[1, cb], ms_b = _mlstm_chunk(*scr, cb, 1, ms[1])
        return ms_f, ms_b

    m_fin = lax.fori_loop(0, nc, scan_body, m_init)
    for j in range(2 * HEADS):
        d, h = divmod(j, HEADS)
        sl = slice(h * HD, (h + 1) * HD)
        c_ref[0, j] = ct_sc[d, sl, sl]
        n_ref[0, j:j + 1, :] = nb_sc[d, h:h + 1, sl]
        m_ref[0, j:j + 1, :] = jnp.broadcast_to(m_fin[d][h], (1, LANE))

    def out_body(c, carry):
        rows = pl.ds(pl.multiple_of(c * CHUNK, CHUNK), CHUNK)
        normed = []
        for h in range(HEADS):
            tot = ht_sc[0, c, h * HD:(h + 1) * HD, :] + ht_sc[1, c, h * HD:(h + 1) * HD, :]
            normed.append(tot * lax.rsqrt(jnp.mean(tot * tot, axis=0, keepdims=True) + EPS))
        hn = jnp.concatenate(normed, axis=0).T
        y_ref[0, rows, :] = hn * ng_ref[...] * _sigmoid(zb_ref[0, rows, 3 * MIX:4 * MIX])
        return carry

    lax.fori_loop(0, nc, out_body, 0)


def _mlstm_mixer(zb, zg, gate_b, norm_g, c0, n0, m0):
    bsz, t_len, _ = zb.shape
    nd = 2 * HEADS
    return pl.pallas_call(
        _mlstm_kernel,
        out_shape=[jax.ShapeDtypeStruct((bsz, t_len, MIX), F32),
                   jax.ShapeDtypeStruct((bsz, nd, HD, HD), F32),
                   jax.ShapeDtypeStruct((bsz, nd, HD), F32),
                   jax.ShapeDtypeStruct((bsz, nd, LANE), F32)],
        grid=(bsz,),
        in_specs=[pl.BlockSpec((1, t_len, 4 * MIX), lambda i: (i, 0, 0)),
                  pl.BlockSpec((1, t_len, 2 * LANE), lambda i: (i, 0, 0)),
                  pl.BlockSpec((1, 2 * LANE), lambda i: (0, 0)),
                  pl.BlockSpec((1, MIX), lambda i: (0, 0)),
                  pl.BlockSpec((1, nd, HD, HD), lambda i: (i, 0, 0, 0)),
                  pl.BlockSpec((1, nd, HD), lambda i: (i, 0, 0)),
                  pl.BlockSpec((1, nd, LANE), lambda i: (i, 0, 0))],
        out_specs=[pl.BlockSpec((1, t_len, MIX), lambda i: (i, 0, 0)),
                   pl.BlockSpec((1, nd, HD, HD), lambda i: (i, 0, 0, 0)),
                   pl.BlockSpec((1, nd, HD), lambda i: (i, 0, 0)),
                   pl.BlockSpec((1, nd, LANE), lambda i: (i, 0, 0))],
        scratch_shapes=[pltpu.VMEM((t_len // CHUNK, MIX, CHUNK), BF16),
                        pltpu.VMEM((t_len // CHUNK, MIX, HEADS * CHUNK), BF16),
                        pltpu.VMEM((t_len // CHUNK, MIX, CHUNK), F32),
                        pltpu.VMEM((t_len // CHUNK, MIX, HEADS * CHUNK), BF16),
                        pltpu.VMEM((2, t_len // CHUNK, MIX, CHUNK), F32),
                        pltpu.VMEM((2, MIX, MIX), F32),
                        pltpu.VMEM((2, 8, MIX), F32)],
        compiler_params=_cp("parallel"),
        name="mlstm_mixer",
    )(zb, zg, gate_b, norm_g, c0, n0, m0)


_ATT_TQ = 256


def _mla_kernel(*refs, past):
    if past:
        (zc_ref, cos_ref, sin_ref, gq_ref, gkv_ref, wqa_ref, wqb_ref, wk_ref, wv_ref,
         pckv_ref, pkr_ref, y_ref, ckv_ref, q_sc, k_sc, v_sc) = refs
    else:
        (zc_ref, cos_ref, sin_ref, gq_ref, gkv_ref, wqa_ref, wqb_ref, wk_ref, wv_ref,
         y_ref, ckv_ref, q_sc, k_sc, v_sc) = refs
    t_len = y_ref.shape[1]
    cosf = cos_ref[...]
    sinf = sin_ref[...]
    cq = _rms(zc_ref[0, :, 0:Q_RANK]) * gq_ref[...]
    ckv = _rms(zc_ref[0, :, Q_RANK:Q_RANK + KV_RANK]) * gkv_ref[...]
    ckv_ref[0] = ckv
    kr = (zc_ref[0, :, Q_RANK + KV_RANK:Q_RANK + KV_RANK + LANE] * cosf
          + zc_ref[0, :, Q_RANK + KV_RANK + LANE:Q_RANK + KV_RANK + 2 * LANE] * sinf)
    cqb = cq.astype(BF16)
    ckvb = ckv.astype(BF16)
    for h in range(HEADS):
        sl = slice(h * LANE, (h + 1) * LANE)
        qa = jnp.dot(cqb, wqa_ref[:, sl], preferred_element_type=F32)
        qb = jnp.dot(cqb, wqb_ref[:, sl], preferred_element_type=F32)
        q_sc[:, sl] = (qa * cosf + qb * sinf).astype(BF16)
        kn = jnp.dot(ckvb, wk_ref[:, sl], preferred_element_type=F32)
        k_sc[past:past + t_len, sl] = (kn + kr).astype(BF16)
        v_sc[past:past + t_len, sl] = jnp.dot(ckvb, wv_ref[:, sl], preferred_element_type=F32).astype(BF16)
    if past:
        pckv = pckv_ref[0].astype(BF16)
        pkr = pkr_ref[0]
        for h in range(HEADS):
            sl = slice(h * LANE, (h + 1) * LANE)
            k_sc[0:past, sl] = (jnp.dot(pckv, wk_ref[:, sl], preferred_element_type=F32) + pkr).astype(BF16)
            v_sc[0:past, sl] = jnp.dot(pckv, wv_ref[:, sl], preferred_element_type=F32).astype(BF16)

    def q_block(i, carry):
        rows = pl.ds(pl.multiple_of(i * _ATT_TQ, _ATT_TQ), _ATT_TQ)
        outs = []
        for h in range(HEADS):
            sl = slice(h * LANE, (h + 1) * LANE)
            s = lax.dot_general(q_sc[rows, sl], k_sc[:, sl], (((1,), (1,)), ((), ())),
                                preferred_element_type=F32)
            p = jnp.exp(s - jnp.max(s, axis=-1, keepdims=True))
            o = jnp.dot(p.astype(BF16), v_sc[:, sl], preferred_element_type=F32)
            outs.append(o / jnp.sum(p, axis=-1, keepdims=True))
        y_ref[0, rows, 0:LANE] = outs[0] + outs[1]
        y_ref[0, rows, LANE:2 * LANE] = outs[2] + outs[3]
        return carry

    lax.fori_loop(0, t_len // _ATT_TQ, q_block, 0)


def _mla_mixer(zc, cosf, sinf, gq, gkv, wqa, wqb, wk, wv, past_ckv=None, past_kr=None):
    bsz, t_len, zw = zc.shape
    past = 0 if past_ckv is None else past_ckv.shape[1]
    full = lambda a: pl.BlockSpec(a.shape, lambda i: (0,) * a.ndim)
    args = [zc, cosf, sinf, gq, gkv, wqa, wqb, wk, wv]
    in_specs = [pl.BlockSpec((1, t_len, zw), lambda i: (i, 0, 0))] + [full(a) for a in args[1:]]
    if past:
        args += [past_ckv, past_kr]
        in_specs += [pl.BlockSpec((1, past, KV_RANK), lambda i: (i, 0, 0)),
                     pl.BlockSpec((1, past, LANE), lambda i: (i, 0, 0))]
    wide = HEADS * LANE
    return pl.pallas_call(
        functools.partial(_mla_kernel, past=past),
        out_shape=[jax.ShapeDtypeStruct((bsz, t_len, MIX), F32),
                   jax.ShapeDtypeStruct((bsz, t_len, KV_RANK), F32)],
        grid=(bsz,),
        in_specs=in_specs,
        out_specs=[pl.BlockSpec((1, t_len, MIX), lambda i: (i, 0, 0)),
                   pl.BlockSpec((1, t_len, KV_RANK), lambda i: (i, 0, 0))],
        scratch_shapes=[pltpu.VMEM((t_len, wide), BF16),
                        pltpu.VMEM((past + t_len, wide), BF16),
                        pltpu.VMEM((past + t_len, wide), BF16)],
        compiler_params=_cp("parallel"),
        name="mla_mixer",
    )(*args)


_S5_SEQ = 8
_S5_TT = 128


def _s5_kernel(zd_ref, bd_ref, cd_ref, are_ref, aim_ref, s0re_ref, s0im_ref,
               y_ref, fre_ref, fim_ref, bre_sc, bim_sc, sre_sc, sim_sc):
    direction = pl.program_id(0)
    k = pl.program_id(2)
    tt = _S5_TT
    rows = _S5_SEQ * tt

    @pl.when(k == 0)
    def _():
        sre_sc[...] = s0re_ref[0]
        sim_sc[...] = s0im_ref[0]

    u = zd_ref[...].reshape(rows, MIX).astype(BF16)
    bre_sc[...] = jnp.dot(u, bd_ref[0, :, 0:S5_STATES], preferred_element_type=F32)
    bim_sc[...] = jnp.dot(u, bd_ref[0, :, S5_STATES:2 * S5_STATES], preferred_element_type=F32)
    a_re = jnp.broadcast_to(are_ref[0], (_S5_SEQ, S5_STATES))
    a_im = jnp.broadcast_to(aim_ref[0], (_S5_SEQ, S5_STATES))

    def step(i, carry):
        s_re, s_im = carry
        t = i + direction * (tt - 1 - 2 * i)
        sel = pl.ds(pl.multiple_of(t * _S5_SEQ, _S5_SEQ), _S5_SEQ)
        n_re = a_re * s_re - a_im * s_im + bre_sc[sel, :]
        n_im = a_re * s_im + a_im * s_re + bim_sc[sel, :]
        bre_sc[sel, :] = n_re
        bim_sc[sel, :] = n_im
        return n_re, n_im

    s_re, s_im = lax.fori_loop(0, tt, step, (sre_sc[...], sim_sc[...]), unroll=2)
    sre_sc[...] = s_re
    sim_sc[...] = s_im
    y = (jnp.dot(bre_sc[...].astype(BF16), cd_ref[0, 0:S5_STATES, :], preferred_element_type=F32)
         + jnp.dot(bim_sc[...].astype(BF16), cd_ref[0, S5_STATES:2 * S5_STATES, :],
                   preferred_element_type=F32))
    y_ref[0] = y.reshape(tt, _S5_SEQ, MIX)

    @pl.when(k == pl.num_programs(2) - 1)
    def _():
        fre_ref[0] = s_re
        fim_ref[0] = s_im


def _s5_scan(zd, bd, cd, ab_re, ab_im, s0_re, s0_im):
    t_len, bsz, _ = zd.shape
    nt = t_len // _S5_TT
    tile = lambda d, k: k + d * (nt - 1 - 2 * k)
    st = pl.BlockSpec((1, _S5_SEQ, S5_STATES), lambda d, g, k: (d, g, 0))
    return pl.pallas_call(
        _s5_kernel,
        out_shape=[jax.ShapeDtypeStruct((2, t_len, bsz, MIX), F32),
                   jax.ShapeDtypeStruct((2, bsz, S5_STATES), F32),
                   jax.ShapeDtypeStruct((2, bsz, S5_STATES), F32)],
        grid=(2, bsz // _S5_SEQ, nt),
        in_specs=[pl.BlockSpec((_S5_TT, _S5_SEQ, MIX), lambda d, g, k: (tile(d, k), g, 0)),
                  pl.BlockSpec((1, MIX, 2 * S5_STATES), lambda d, g, k: (d, 0, 0)),
                  pl.BlockSpec((1, 2 * S5_STATES, MIX), lambda d, g, k: (d, 0, 0)),
                  pl.BlockSpec((1, 1, S5_STATES), lambda d, g, k: (d, 0, 0)),
                  pl.BlockSpec((1, 1, S5_STATES), lambda d, g, k: (d, 0, 0)),
                  st, st],
        out_specs=[pl.BlockSpec((1, _S5_TT, _S5_SEQ, MIX), lambda d, g, k: (d, tile(d, k), g, 0)),
                   st, st],
        scratch_shapes=[pltpu.VMEM((_S5_SEQ * _S5_TT, S5_STATES), F32),
                        pltpu.VMEM((_S5_SEQ * _S5_TT, S5_STATES), F32),
                        pltpu.VMEM((_S5_SEQ, S5_STATES), F32),
                        pltpu.VMEM((_S5_SEQ, S5_STATES), F32)],
        compiler_params=_cp("parallel", "parallel", "arbitrary"),
        name="s5_scan",
    )(zd, bd, cd, ab_re, ab_im, s0_re, s0_im)


def _gelu_tanh(x):
    return 0.5 * x * (1.0 + jnp.tanh(math.sqrt(2.0 / math.pi) * (x + 0.044715 * x * x * x)))


def _route(logits):
    lane = lax.broadcasted_iota(jnp.int32, logits.shape, 1).astype(F32)
    big = float(LANE)
    neg = -jnp.inf
    g_mask = (lane >= N_EXPERT) & (lane < N_EXPERT + N_GROUP)
    gl = jnp.where(g_mask, logits, neg)
    g_max = jnp.max(gl, axis=-1, keepdims=True)
    g_idx = jnp.min(jnp.where(gl == g_max, lane, big), axis=-1, keepdims=True) - N_EXPERT
    g_sel = 1.0 / jnp.sum(jnp.where(g_mask, jnp.exp(logits - g_max), 0.0), axis=-1, keepdims=True)
    lo = g_idx * PER_GROUP
    el = jnp.where((lane >= lo) & (lane < lo + PER_GROUP), logits, neg)
    v1 = jnp.max(el, axis=-1, keepdims=True)
    i1 = jnp.min(jnp.where(el == v1, lane, big), axis=-1, keepdims=True)
    el2 = jnp.where(lane == i1, neg, el)
    v2 = jnp.max(el2, axis=-1, keepdims=True)
    i2 = jnp.min(jnp.where(el2 == v2, lane, big), axis=-1, keepdims=True)
    e2 = jnp.exp(v2 - v1)
    w1 = g_sel / (1.0 + e2)
    return jnp.where(lane == i1, w1, jnp.where(lane == i2, w1 * e2, 0.0))


def _post_kernel(x_ref, ya_ref, yb_ref, yc_ref, ys_ref, zd_ref, mod_ref, wo_ref, d_ref, wglu_ref, bglu_ref,
                 g2_ref, wr_ref, br_ref, x1_ref, h2_ref, gate_ref):
    m = mod_ref[0]
    zd = zd_ref[...]
    ys = ys_ref[0] + ys_ref[1] + d_ref[...] * zd
    gl = _bdot(_gelu_tanh(ys), wglu_ref[...]) + bglu_ref[...]
    yd = gl[:, 0:MIX] * _sigmoid(gl[:, MIX:2 * MIX])
    mix = (_bdot(ya_ref[...], wo_ref[0:MIX, :]) + _bdot(yb_ref[...], wo_ref[MIX:2 * MIX, :])
           + _bdot(yc_ref[...], wo_ref[2 * MIX:3 * MIX, :]) + _bdot(yd, wo_ref[3 * MIX:4 * MIX, :]))
    x1 = x_ref[...] + m[2:3] * mix
    x1_ref[...] = x1
    h2 = _rms(x1) * g2_ref[...] * (1.0 + m[4:5]) + m[3:4]
    h2b = h2.astype(BF16)
    h2_ref[...] = h2b
    gate_ref[...] = _route(jnp.dot(h2b, wr_ref[...], preferred_element_type=F32) + br_ref[...])


def _post_mixer(x, ya, yb, yc, ys, zd, mod, row_fn, w_out, s5_d, w_glu, b_glu, norm2_g, w_router, b_router, tm):
    n = x.shape[0]
    tok = lambda w: pl.BlockSpec((tm, w), lambda i: (i, 0))
    full = lambda a: pl.BlockSpec(a.shape, lambda i: (0,) * a.ndim)
    return pl.pallas_call(
        _post_kernel,
        out_shape=[jax.ShapeDtypeStruct((n, D_MODEL), F32),
                   jax.ShapeDtypeStruct((n, D_MODEL), BF16),
                   jax.ShapeDtypeStruct((n, LANE), F32)],
        grid=(n // tm,),
        in_specs=[tok(D_MODEL), tok(MIX), tok(MIX), tok(MIX),
                  pl.BlockSpec((2, tm, MIX), lambda i: (0, i, 0)), tok(MIX),
                  pl.BlockSpec((1, 6, D_MODEL), lambda i: (row_fn(i, tm), 0, 0)),
                  full(w_out), full(s5_d), full(w_glu), full(b_glu), full(norm2_g),
                  full(w_router), full(b_router)],
        out_specs=[tok(D_MODEL), tok(D_MODEL), tok(LANE)],
        compiler_params=_cp("parallel"),
        name="post_mixer",
    )(x, ya, yb, yc, ys, zd, mod, w_out, s5_d, w_glu, b_glu, norm2_g, w_router, b_router)


_MOE_EPS = 4
_MOE_TM = 1024


_MOE_ALIGN = 16
_MOE_ROWS = 2 * _MOE_TM + N_EXPERT * _MOE_ALIGN
_MOE_CH = 128
_MOE_GB = 512
_MOE_CB = 256


def _moe_kernel(h_ref, gate_ref, x_ref, mod_ref, w1_ref, w3_ref, w2_ref, o_ref,
                xs_sc, ys_sc, col_sc, row_sc, start_sm, rows_sm, done_sm):
    j = pl.program_id(1)
    tm = h_ref.shape[0]

    @pl.when(j == 0)
    def _dispatch():
        gate = gate_ref[...]
        cnt_row = jnp.sum(jnp.where(gate != 0.0, 1.0, 0.0), axis=0, keepdims=True)
        units_row = jnp.floor((cnt_row + (_MOE_ALIGN - 1)) * (1.0 / _MOE_ALIGN))
        li = lax.broadcasted_iota(jnp.int32, (LANE, LANE), 0)
        lj = lax.broadcasted_iota(jnp.int32, (LANE, LANE), 1)
        before = jnp.where(li < lj, 1.0, 0.0).astype(BF16)
        start_row = jnp.dot(jnp.broadcast_to(units_row, (8, LANE)).astype(BF16), before,
                            preferred_element_type=F32) * _MOE_ALIGN
        start_i = start_row.astype(jnp.int32)
        cnt_i = cnt_row.astype(jnp.int32)
        for e in range(N_EXPERT):
            start_sm[e] = start_i[0, e]
            rows_sm[e] = cnt_i[0, e]
        gt = gate.T[0:N_EXPERT, :]
        chosen = gt != 0.0
        ones = jnp.where(chosen, 1.0, 0.0)
        cnt = jnp.sum(ones, axis=-1, keepdims=True)
        units = jnp.floor((cnt + (_MOE_ALIGN - 1)) * (1.0 / _MOE_ALIGN))
        ei = lax.broadcasted_iota(jnp.int32, (N_EXPERT, N_EXPERT), 0)
        ej = lax.broadcasted_iota(jnp.int32, (N_EXPERT, N_EXPERT), 1)
        start = jnp.dot(jnp.where(ej < ei, 1.0, 0.0).astype(BF16),
                        jnp.broadcast_to(units, (N_EXPERT, LANE)).astype(BF16),
                        preferred_element_type=F32)[:, 0:1] * _MOE_ALIGN
        ti = lax.broadcasted_iota(jnp.int32, (tm, tm), 0)
        tj = lax.broadcasted_iota(jnp.int32, (tm, tm), 1)
        earlier = jnp.where(ti < tj, 1.0, 0.0).astype(BF16)
        pos = start + jnp.dot(ones.astype(BF16), earlier, preferred_element_type=F32)
        p_a = jnp.min(jnp.where(chosen, pos, float(_MOE_ROWS)), axis=0, keepdims=True)
        p_b = jnp.max(jnp.where(chosen, pos, -1.0), axis=0, keepdims=True)
        g_a = jnp.sum(jnp.where(chosen & (pos == p_a), gt, 0.0), axis=0, keepdims=True)
        g_b = jnp.sum(jnp.where(chosen & (pos == p_b) & (p_b != p_a), gt, 0.0), axis=0, keepdims=True)
        row_sc[...] = jnp.concatenate([p_a, p_b, jnp.zeros((6, tm), F32)], axis=0)
        packed = jnp.concatenate([p_a, p_b, g_a, g_b, jnp.zeros((LANE - 4, tm), F32)], axis=0)
        col_sc[...] = packed.T
        xs_sc[_MOE_ROWS:_MOE_ROWS + _MOE_CH, :] = jnp.zeros((_MOE_CH, D_MODEL), BF16)
        ys_sc[...] = jnp.zeros_like(ys_sc)
        o_ref[...] = x_ref[...]
        done_sm[0] = 0
        done_sm[1] = 0

    n_blocks = _MOE_ROWS // _MOE_GB
    e_last = j * _MOE_EPS + (_MOE_EPS - 1)
    seg_end = start_sm[e_last] + rows_sm[e_last]
    last_step = j == pl.num_programs(1) - 1

    def gather_block(b, carry):
        r0 = pl.multiple_of(b * _MOE_GB, _MOE_GB)
        ri = (lax.broadcasted_iota(jnp.int32, (_MOE_GB, tm), 0) + r0).astype(F32)
        sel = jnp.where((ri == row_sc[0:1, :]) | (ri == row_sc[1:2, :]), 1.0, 0.0).astype(BF16)
        xs_sc[pl.ds(r0, _MOE_GB), :] = jnp.dot(sel, h_ref[...], preferred_element_type=F32).astype(BF16)
        return carry

    gathered = jnp.minimum((seg_end + _MOE_CH + _MOE_GB - 1) // _MOE_GB, n_blocks)
    lax.fori_loop(done_sm[0], gathered, gather_block, 0)
    done_sm[0] = gathered

    def expert_rows(el, i):
        rows = pl.ds(pl.multiple_of(start_sm[j * _MOE_EPS + el] + i * _MOE_CH, _MOE_ALIGN), _MOE_CH)
        xs = xs_sc[rows, :]
        h1 = jnp.dot(xs, w1_ref[el], preferred_element_type=F32)
        h3 = jnp.dot(xs, w3_ref[el], preferred_element_type=F32)
        return rows, jnp.dot((h1 * _sigmoid(h1) * h3).astype(BF16), w2_ref[el], preferred_element_type=F32)

    first = [expert_rows(el, 0) for el in range(_MOE_EPS)]
    for rows, y in first:
        ys_sc[rows, :] = y.astype(BF16)
    for el in range(_MOE_EPS):
        n_rows = rows_sm[j * _MOE_EPS + el]

        def chunk(i, carry, el=el, n_rows=n_rows):
            rows, y = expert_rows(el, i)
            live = lax.broadcasted_iota(jnp.int32, (_MOE_CH, 1), 0) + i * _MOE_CH < n_rows
            ys_sc[rows, :] = jnp.where(live, y, ys_sc[rows, :].astype(F32)).astype(BF16)
            return carry

        lax.fori_loop(1, (n_rows + _MOE_CH - 1) // _MOE_CH, chunk, 0)

    g2 = mod_ref[0][5:6]

    def combine_block(b, carry):
        r0 = pl.multiple_of(b * _MOE_GB, _MOE_GB)
        y = ys_sc[pl.ds(r0, _MOE_GB), :]
        for t0 in range(0, tm, _MOE_CB):
            c = col_sc[t0:t0 + _MOE_CB, :]
            ri = (lax.broadcasted_iota(jnp.int32, (_MOE_CB, _MOE_GB), 1) + r0).astype(F32)
            w = (jnp.where(ri == c[:, 0:1], c[:, 2:3], 0.0)
                 + jnp.where(ri == c[:, 1:2], c[:, 3:4], 0.0)).astype(BF16)
            o_ref[t0:t0 + _MOE_CB, :] += g2 * jnp.dot(w, y, preferred_element_type=F32)
        return carry

    final_rows = (seg_end + _MOE_ALIGN - 1) // _MOE_ALIGN * _MOE_ALIGN
    combined = jnp.where(last_step, n_blocks, final_rows // _MOE_GB)
    lax.fori_loop(done_sm[1], combined, combine_block, 0)
    done_sm[1] = combined


def _moe(h2, gate, x1, mod, row_fn, w1, w3, w2, layer, tm):
    n = h2.shape[0]
    first_blk = layer * (N_EXPERT // _MOE_EPS)
    return pl.pallas_call(
        _moe_kernel,
        out_shape=jax.ShapeDtypeStruct((n, D_MODEL), F32),
        grid=(n // tm, N_EXPERT // _MOE_EPS),
        in_specs=[pl.BlockSpec((tm, D_MODEL), lambda i, j: (i, 0)),
                  pl.BlockSpec((tm, LANE), lambda i, j: (i, 0)),
                  pl.BlockSpec((tm, D_MODEL), lambda i, j: (i, 0)),
                  pl.BlockSpec((1, 6, D_MODEL), lambda i, j: (row_fn(i, tm), 0, 0)),
                  pl.BlockSpec((_MOE_EPS, D_MODEL, MOE_FF), lambda i, j: (first_blk + j, 0, 0)),
                  pl.BlockSpec((_MOE_EPS, D_MODEL, MOE_FF), lambda i, j: (first_blk + j, 0, 0)),
                  pl.BlockSpec((_MOE_EPS, MOE_FF, D_MODEL), lambda i, j: (first_blk + j, 0, 0))],
        out_specs=pl.BlockSpec((tm, D_MODEL), lambda i, j: (i, 0)),
        scratch_shapes=[pltpu.VMEM((_MOE_ROWS + _MOE_CH, D_MODEL), BF16),
                        pltpu.VMEM((_MOE_ROWS + _MOE_CH, D_MODEL), BF16),
                        pltpu.VMEM((tm, LANE), F32),
                        pltpu.VMEM((8, tm), F32),
                        pltpu.SMEM((N_EXPERT,), jnp.int32),
                        pltpu.SMEM((N_EXPERT,), jnp.int32),
                        pltpu.SMEM((2,), jnp.int32)],
        compiler_params=_cp("parallel", "arbitrary"),
        name="moe",
    )(h2, gate, x1, mod, w1, w3, w2)


def _final_kernel(x_ref, g_ref, o_ref):
    o_ref[...] = _rms(x_ref[...]) * g_ref[...]


def _final_norm(x, g, tm):
    n = x.shape[0]
    return pl.pallas_call(
        _final_kernel,
        out_shape=jax.ShapeDtypeStruct((n, D_MODEL), F32),
        grid=(n // tm,),
        in_specs=[pl.BlockSpec((tm, D_MODEL), lambda i: (i, 0)),
                  pl.BlockSpec((1, D_MODEL), lambda i: (0, 0))],
        out_specs=pl.BlockSpec((tm, D_MODEL), lambda i: (i, 0)),
        compiler_params=_cp("parallel"),
        name="final_norm",
    )(x, g)


def _rot_cols(w):
    q = ROPE // 4
    return jnp.concatenate([-w[..., q:2 * q], w[..., 0:q], -w[..., 3 * q:4 * q], w[..., 2 * q:3 * q]], axis=-1)


def _rope_slot(w):
    pad = [(0, 0)] * (w.ndim - 1) + [(HD, LANE - HD - ROPE)]
    return jnp.pad(w, pad)


def _gate_lanes(g):
    n = g.shape[-1] // 2
    pad = [(0, 0)] * (g.ndim - 1) + [(0, LANE - n)]
    return jnp.concatenate([jnp.pad(g[..., :n], pad), jnp.pad(g[..., n:], pad)], axis=-1)


def _prep_w_in(w_in, gate_cols):
    a = w_in[..., 0:512]
    qkvo = w_in[..., 512:1536]
    g = _gate_lanes(w_in[..., 1536:1536 + gate_cols])
    c0 = 1536 + gate_cols
    cq = w_in[..., c0:c0 + Q_RANK]
    ckv = w_in[..., c0 + Q_RANK:c0 + Q_RANK + KV_RANK]
    kr = w_in[..., c0 + Q_RANK + KV_RANK:c0 + Q_RANK + KV_RANK + ROPE]
    d = w_in[..., c0 + Q_RANK + KV_RANK + ROPE:]
    return jnp.concatenate([a, qkvo, g, cq, ckv, _rope_slot(kr), _rope_slot(_rot_cols(kr)), d],
                           axis=-1).astype(BF16)


def _prep_mla(w_uq, w_ukv):
    n_layer = w_uq.shape[0]
    scale = (HD + ROPE) ** -0.5
    wq = w_uq.reshape(n_layer, Q_RANK, HEADS, HD + ROPE) * scale
    nope, rope = wq[..., :HD], wq[..., HD:]
    zeros_r = jnp.zeros_like(rope)
    wqa = jnp.concatenate([nope, rope, zeros_r], axis=-1).reshape(n_layer, Q_RANK, HEADS * LANE)
    wqb = jnp.concatenate([jnp.zeros_like(nope), _rot_cols(rope), zeros_r], axis=-1)
    wqb = wqb.reshape(n_layer, Q_RANK, HEADS * LANE)
    wkv = w_ukv.reshape(n_layer, KV_RANK, HEADS, 2 * HD)
    k_nope, val = wkv[..., :HD], wkv[..., HD:]
    zeros_h = jnp.zeros_like(k_nope)
    wk = jnp.concatenate([k_nope, zeros_h], axis=-1).reshape(n_layer, KV_RANK, HEADS * LANE)
    even = jnp.concatenate([val, zeros_h], axis=-1)
    odd = jnp.concatenate([zeros_h, val], axis=-1)
    is_odd = (jnp.arange(HEADS) % 2 == 1)[None, None, :, None]
    wv = jnp.where(is_odd, odd, even).reshape(n_layer, KV_RANK, HEADS * LANE)
    return wqa.astype(BF16), wqb.astype(BF16), wk.astype(BF16), wv.astype(BF16)


def _rope_tables(t_len, rotate):
    ones = jnp.ones((t_len, HD), F32)
    zeros = jnp.zeros((t_len, HD), F32)
    tail = jnp.zeros((t_len, LANE - HD - ROPE), F32)
    if not rotate:
        return (jnp.concatenate([ones, jnp.ones((t_len, ROPE), F32), tail], axis=-1),
                jnp.zeros((t_len, LANE), F32))
    rows = t_len // GRID_W
    row = jnp.repeat(jnp.arange(rows, dtype=F32), GRID_W)
    col = jnp.tile(jnp.arange(GRID_W, dtype=F32), rows)
    nf = ROPE // 4
    inv = ROPE_BASE ** (-jnp.arange(nf, dtype=F32) / nf)
    ar = row[:, None] * inv
    ac = col[:, None] * inv
    cos = jnp.concatenate([jnp.cos(ar), jnp.cos(ar), jnp.cos(ac), jnp.cos(ac)], axis=-1)
    sin = jnp.concatenate([jnp.sin(ar), jnp.sin(ar), jnp.sin(ac), jnp.sin(ac)], axis=-1)
    return (jnp.concatenate([ones, cos, tail], axis=-1), jnp.concatenate([zeros, sin, tail], axis=-1))


def _prep_s5(bb_re, bb_im, c_re, c_im):
    eye = jnp.eye(S5_G, dtype=F32)
    to_b = lambda bb: jnp.einsum("ldgnc,gh->ldgchn", bb, eye).reshape(bb.shape[0], 2, MIX, S5_STATES)
    to_c = lambda cc: jnp.einsum("ldgcn,gh->ldgnhc", cc, eye).reshape(cc.shape[0], 2, S5_STATES, MIX)
    bd = jnp.concatenate([to_b(bb_re), to_b(bb_im)], axis=-1)
    cd = jnp.concatenate([to_c(c_re.astype(F32)), -to_c(c_im.astype(F32))], axis=-2)
    return bd.astype(BF16), cd.astype(BF16)


def _layer(x, bsz, t_len, mod, row_fn, p, ctx, tm):
    za, zb, zg, zc, zd = _pre_mixer(x, mod, row_fn, p["norm1_g"], p["w_in"], tm)
    seq = lambda a: a.reshape(bsz, t_len, a.shape[-1])
    ya = _conv_module(seq(za), p["conv_w"], p["conv_b"], p["conv_ln_g"], p["conv_ln_b"])
    yb, c_fin, n_fin, m_fin = _mlstm_mixer(seq(zb), seq(zg), p["gate_b"], p["mlstm_norm_g"],
                                           ctx["mlstm_c"], ctx["mlstm_n"], ctx["mlstm_m"])
    yc, ckv = _mla_mixer(seq(zc), ctx["cos"], ctx["sin"], p["mla_q_norm_g"], p["mla_kv_norm_g"],
                         p["wqa"], p["wqb"], p["wk"], p["wv"], ctx.get("past_ckv"), ctx.get("past_kr"))
    ys, s_re, s_im = _s5_scan(seq(zd).transpose(1, 0, 2), p["s5_bd"], p["s5_cd"], p["s5_ab_re"], p["s5_ab_im"],
                              ctx["s5_re"], ctx["s5_im"])
    ys = ys.transpose(0, 2, 1, 3)
    flat = lambda a: a.reshape(bsz * t_len, a.shape[-1])
    x1, h2, gate = _post_mixer(x, flat(ya), flat(yb), flat(yc), ys.reshape(2, bsz * t_len, MIX), zd, mod, row_fn,
                               p["w_out"], p["s5_d"], p["s5_w_glu"], p["s5_b_glu"], p["norm2_g"],
                               p["w_router"], p["b_router"], tm)
    x2 = _moe(h2, gate, x1, mod, row_fn, p["moe_w1"], p["moe_w3"], p["moe_w2"], p["layer"], _MOE_TM)
    krope = seq(zc)[:, :, Q_RANK + KV_RANK + HD:Q_RANK + KV_RANK + HD + ROPE]
    return x2, (ckv, krope, c_fin, n_fin, m_fin, s_re, s_im)


def kernel(x_prompt, x_sample, cache_mla_ckv, cache_mla_krope, state_mlstm_C, state_mlstm_n, state_mlstm_m, state_s5, c, c_ctx, norm1_g, norm2_g, final_g, w_mod, b_mod, w_in, w_out, conv_w, conv_b, conv_ln_g, conv_ln_b, mlstm_gate_b, mlstm_norm_g, mla_q_norm_g, mla_w_uq, mla_kv_norm_g, mla_w_ukv, s5_a_re, s5_a_im, s5_log_dt, s5_b_re, s5_b_im, s5_c_re, s5_c_im, s5_d, s5_w_glu, s5_b_glu, moe_w_group, moe_b_group, moe_w_expert, moe_b_expert, moe_w1, moe_w3, moe_w2):
    n_layer = w_in.shape[0]
    b_ctx, t_ctx, d = x_prompt.shape
    b_lat, t_lat, _ = x_sample.shape
    nd = 2 * HEADS
    tm = 512
    assert b_lat + 1 <= 16 and t_lat % _MOE_TM == 0 and (b_ctx * t_ctx) % _MOE_TM == 0

    c_all = jnp.zeros((16, d), F32).at[0].set(c_ctx).at[1:1 + b_lat].set(c)
    mod_all = _modulation(c_all, w_mod, b_mod).reshape(n_layer, 16, 6, d)
    gate_cols = 4 * HEADS
    w_in_ext = _prep_w_in(w_in, gate_cols)
    wqa, wqb, wk, wv = _prep_mla(mla_w_uq, mla_w_ukv)
    ab_re, ab_im, bb_re, bb_im = _s5_discretise(s5_a_re, s5_a_im, s5_log_dt, s5_b_re, s5_b_im)
    s5_bd, s5_cd = _prep_s5(bb_re, bb_im, s5_c_re, s5_c_im)
    w_router = jnp.pad(jnp.concatenate([moe_w_expert, moe_w_group], axis=-1),
                       [(0, 0), (0, 0), (0, LANE - N_EXPERT - N_GROUP)]).astype(BF16)
    b_router = jnp.pad(jnp.concatenate([moe_b_expert, moe_b_group], axis=-1),
                       [(0, 0), (0, LANE - N_EXPERT - N_GROUP)])
    gate_b = _gate_lanes(mlstm_gate_b.reshape(n_layer, gate_cols))
    conv_w_p = jnp.pad(conv_w, [(0, 0), (0, 32 - CONV_WIDTH), (0, 0)])
    w_out_b = w_out.astype(BF16)
    w_glu_b = s5_w_glu.astype(BF16)
    flat_experts = lambda w: w.astype(BF16).reshape((n_layer * N_EXPERT,) + w.shape[2:])
    w1_b, w3_b, w2_b = flat_experts(moe_w1), flat_experts(moe_w3), flat_experts(moe_w2)
    row = lambda a, l: a[l][None, :]

    cos_ctx, sin_ctx = _rope_tables(t_ctx, rotate=False)
    cos_lat, sin_lat = _rope_tables(t_lat, rotate=True)
    zero_state = dict(
        mlstm_c=jnp.zeros((b_ctx, nd, HD, HD), F32), mlstm_n=jnp.zeros((b_ctx, nd, HD), F32),
        mlstm_m=jnp.zeros((b_ctx, nd, LANE), F32),
        s5_re=jnp.zeros((2, b_ctx, S5_STATES), F32), s5_im=jnp.zeros((2, b_ctx, S5_STATES), F32),
        cos=cos_ctx, sin=sin_ctx)

    row_ctx = lambda i, tile: 0
    row_lat = lambda i, tile: 1 + (i * tile) // t_lat

    x_ctx = x_prompt.reshape(b_ctx * t_ctx, d)
    x_lat = x_sample.reshape(b_lat * t_lat, d)
    outs = []
    for l in range(n_layer):
        p = dict(norm1_g=row(norm1_g, l), norm2_g=row(norm2_g, l), w_in=w_in_ext[l], w_out=w_out_b[l],
                 conv_w=conv_w_p[l], conv_b=row(conv_b, l), conv_ln_g=row(conv_ln_g, l),
                 conv_ln_b=row(conv_ln_b, l), gate_b=row(gate_b, l), mlstm_norm_g=row(mlstm_norm_g, l),
                 mla_q_norm_g=row(mla_q_norm_g, l), mla_kv_norm_g=row(mla_kv_norm_g, l),
                 wqa=wqa[l], wqb=wqb[l], wk=wk[l], wv=wv[l],
                 s5_bd=s5_bd[l], s5_cd=s5_cd[l], s5_ab_re=ab_re[l][:, None, :], s5_ab_im=ab_im[l][:, None, :],
                 s5_d=row(s5_d, l), s5_w_glu=w_glu_b[l], s5_b_glu=row(s5_b_glu, l),
                 w_router=w_router[l], b_router=row(b_router, l),
                 moe_w1=w1_b, moe_w3=w3_b, moe_w2=w2_b, layer=l)
        x_ctx, st = _layer(x_ctx, b_ctx, t_ctx, mod_all[l], row_ctx, p, zero_state, tm)
        outs.append(st)
        s5_l = state_s5[:, l].reshape(b_lat, 2, S5_STATES, 2)
        lat_state = dict(
            mlstm_c=jnp.swapaxes(state_mlstm_C[:, l].reshape(b_lat, nd, HD, HD), -1, -2),
            mlstm_n=state_mlstm_n[:, l].reshape(b_lat, nd, HD),
            mlstm_m=jnp.broadcast_to(state_mlstm_m[:, l].reshape(b_lat, nd, 1), (b_lat, nd, LANE)),
            s5_re=s5_l[..., 0].transpose(1, 0, 2), s5_im=s5_l[..., 1].transpose(1, 0, 2),
            cos=cos_lat, sin=sin_lat,
            past_ckv=cache_mla_ckv[:, l], past_kr=_rope_slot(cache_mla_krope[:, l]))
        x_lat, _ = _layer(x_lat, b_lat, t_lat, mod_all[l], row_lat, p, lat_state, tm)

    y_prompt = _final_norm(x_ctx, final_g[None, :], tm).reshape(b_ctx, t_ctx, d)
    y_sample = _final_norm(x_lat, final_g[None, :], tm).reshape(b_lat, t_lat, d)
    stack = lambda i: jnp.stack([o[i] for o in outs], axis=1)
    new_ckv = stack(0)
    new_krope = stack(1)
    new_c = jnp.swapaxes(stack(2), -1, -2).reshape(b_ctx, n_layer, 2, HEADS, HD, HD)
    new_n = stack(3).reshape(b_ctx, n_layer, 2, HEADS, HD)
    new_m = stack(4)[..., 0].reshape(b_ctx, n_layer, 2, HEADS)
    s_re = jnp.stack([o[5] for o in outs], axis=0)
    s_im = jnp.stack([o[6] for o in outs], axis=0)
    new_s5 = jnp.stack([s_re, s_im], axis=-1).transpose(2, 0, 1, 3, 4)
    new_s5 = new_s5.reshape(b_ctx, n_layer, 2, S5_G, S5_N, 2)
    return (y_prompt, y_sample, new_ckv, new_krope, new_c, new_n, new_m, new_s5)
```

```python
import functools
import math

import jax
import jax.numpy as jnp
from jax import lax
from jax.experimental import pallas as pl
from jax.experimental.pallas import tpu as pltpu

F32 = jnp.float32
BF16 = jnp.bfloat16
EPS = 1e-6

D_MODEL = 1024
MIX = 256
CONV_WIDTH = 31
HEADS = 4
HD = 64
CHUNK = 128
ROPE = 32
KV_RANK = 128
Q_RANK = 256
GRID_W = 64
ROPE_BASE = 10000.0
S5_G = 16
S5_GC = 16
S5_N = 64
S5_STATES = S5_G * S5_N
N_EXPERT = 32
PER_GROUP = 8
N_GROUP = 4
MOE_FF = 256
LANE = 128
Z_COLS = 2688
VMEM_LIMIT = 56 * 1024 * 1024


def _cp(*sem):
    return pltpu.CompilerParams(dimension_semantics=sem, vmem_limit_bytes=VMEM_LIMIT)


def _rms(x):
    return x * lax.rsqrt(jnp.mean(x * x, axis=-1, keepdims=True) + EPS)


def _sigmoid(x):
    return 1.0 / (1.0 + jnp.exp(-x))


def _bdot(a, b):
    return jnp.dot(a.astype(BF16), b.astype(BF16), preferred_element_type=F32)


def _mod_kernel(c_ref, w_ref, b_ref, o_ref):
    c = c_ref[...]
    o_ref[0] = _bdot(c * _sigmoid(c), w_ref[0]) + b_ref[0]


def _modulation(c_all, w_mod, b_mod):
    n_layer, d, n = w_mod.shape
    tn = 1536
    return pl.pallas_call(
        _mod_kernel,
        out_shape=jax.ShapeDtypeStruct((n_layer, 16, n), F32),
        grid=(n_layer, n // tn),
        in_specs=[pl.BlockSpec((16, d), lambda l, j: (0, 0)),
                  pl.BlockSpec((1, d, tn), lambda l, j: (l, 0, j)),
                  pl.BlockSpec((1, 1, tn), lambda l, j: (l, 0, j))],
        out_specs=pl.BlockSpec((1, 16, tn), lambda l, j: (l, 0, j)),
        compiler_params=_cp("parallel", "parallel"),
        name="modulation",
    )(c_all, w_mod, b_mod.reshape(n_layer, 1, n))


def _s5_disc_kernel(are_ref, aim_ref, ldt_ref, bre_ref, bim_ref, abre_ref, abim_ref, bbre_ref, bbim_ref):
    a_re = are_ref[...]
    a_im = aim_ref[...]
    dt = jnp.exp(ldt_ref[...])
    mag = jnp.exp(a_re * dt)
    ab_re = mag * jnp.cos(a_im * dt)
    ab_im = mag * jnp.sin(a_im * dt)
    den = a_re * a_re + a_im * a_im
    f_re = ((ab_re - 1.0) * a_re + ab_im * a_im) / den
    f_im = (ab_im * a_re - (ab_re - 1.0) * a_im) / den
    b_re = bre_ref[...]
    b_im = bim_ref[...]
    abre_ref[...] = ab_re
    abim_ref[...] = ab_im
    bbre_ref[...] = f_re * b_re - f_im * b_im
    bbim_ref[...] = f_re * b_im + f_im * b_re


def _s5_discretise(a_re, a_im, log_dt, b_re, b_im):
    n_layer = a_re.shape[0]
    rows = n_layer * 2 * S5_G
    cols = S5_N * S5_GC
    rep = lambda a: jnp.repeat(a.reshape(rows, S5_N), S5_GC, axis=1)
    ldt = jnp.broadcast_to(log_dt.reshape(rows, 1), (rows, cols))
    spec = pl.BlockSpec((rows, cols), lambda: (0, 0))
    ab_re, ab_im, bb_re, bb_im = pl.pallas_call(
        _s5_disc_kernel,
        out_shape=[jax.ShapeDtypeStruct((rows, cols), F32)] * 4,
        in_specs=[spec] * 5,
        out_specs=[spec] * 4,
        name="s5_discretise",
    )(rep(a_re), rep(a_im), ldt, b_re.reshape(rows, cols), b_im.reshape(rows, cols))
    pick = lambda a: a[:, ::S5_GC].reshape(n_layer, 2, S5_STATES)
    shp = (n_layer, 2, S5_G, S5_N, S5_GC)
    return pick(ab_re), pick(ab_im), bb_re.reshape(shp), bb_im.reshape(shp)


def _pre_kernel(x_ref, mod_ref, g_ref, w_ref, za_ref, zb_ref, zg_ref, zc_ref, zd_ref):
    m = mod_ref[0]
    h = _rms(x_ref[...]) * g_ref[...] * (1.0 + m[1:2]) + m[0:1]
    hb = h.astype(BF16)
    col = 0
    for o_ref in (za_ref, zb_ref, zg_ref, zc_ref, zd_ref):
        n = o_ref.shape[-1]
        o_ref[...] = jnp.dot(hb, w_ref[:, col:col + n], preferred_element_type=F32).astype(o_ref.dtype)
        col += n


def _pre_mixer(x, mod, row_fn, norm_g, w_in_ext, tm):
    n = x.shape[0]
    widths = (512, 1024, 2 * LANE, 640, MIX)
    return pl.pallas_call(
        _pre_kernel,
        out_shape=[jax.ShapeDtypeStruct((n, w), dt) for w, dt in zip(widths, (BF16, F32, F32, BF16, F32))],
        grid=(n // tm,),
        in_specs=[pl.BlockSpec((tm, D_MODEL), lambda i: (i, 0)),
                  pl.BlockSpec((1, 6, D_MODEL), lambda i: (row_fn(i, tm), 0, 0)),
                  pl.BlockSpec((1, D_MODEL), lambda i: (0, 0)),
                  pl.BlockSpec((D_MODEL, Z_COLS), lambda i: (0, 0))],
        out_specs=[pl.BlockSpec((tm, w), lambda i: (i, 0)) for w in widths],
        compiler_params=_cp("parallel"),
        name="pre_mixer",
    )(x, mod, norm_g, w_in_ext)


_CONV_PAD = 16
_CONV_TT = 128


def _conv_kernel(za_ref, w_ref, b_ref, lg_ref, lb_ref, o_ref, hp_ref, sh_ref):
    t_len = o_ref.shape[1]
    u = za_ref[0].astype(F32)
    hp_ref[0:_CONV_PAD, :] = jnp.zeros((_CONV_PAD, MIX), F32)
    hp_ref[_CONV_PAD + t_len:2 * _CONV_PAD + t_len, :] = jnp.zeros((_CONV_PAD, MIX), F32)
    hp_ref[_CONV_PAD:_CONV_PAD + t_len, :] = u[:, :MIX] * _sigmoid(u[:, MIX:])
    span = t_len + 2 * _CONV_PAD - 8
    for off in range(8):
        sh_ref[off, 0:span, :] = hp_ref[off:off + span, :]
    w = w_ref[...]
    half = CONV_WIDTH // 2
    for t0 in range(0, t_len, _CONV_TT):
        acc = jnp.zeros((_CONV_TT, MIX), F32) + b_ref[...]
        for k in range(CONV_WIDTH):
            start = t0 + _CONV_PAD - half + k
            aligned = start // 8 * 8
            acc = acc + sh_ref[start - aligned, aligned:aligned + _CONV_TT, :] * w[k:k + 1, :]
        mu = jnp.mean(acc, axis=-1, keepdims=True)
        cen = acc - mu
        var = jnp.mean(cen * cen, axis=-1, keepdims=True)
        yn = cen * lax.rsqrt(var + EPS) * lg_ref[...] + lb_ref[...]
        o_ref[0, t0:t0 + _CONV_TT, :] = (yn * _sigmoid(yn)).astype(o_ref.dtype)


def _conv_module(za, w, b, ln_g, ln_b):
    bsz, t_len, _ = za.shape
    vec = pl.BlockSpec((1, MIX), lambda i: (0, 0))
    return pl.pallas_call(
        _conv_kernel,
        out_shape=jax.ShapeDtypeStruct((bsz, t_len, MIX), BF16),
        grid=(bsz,),
        in_specs=[pl.BlockSpec((1, t_len, 2 * MIX), lambda i: (i, 0, 0)),
                  pl.BlockSpec((32, MIX), lambda i: (0, 0)), vec, vec, vec],
        out_specs=pl.BlockSpec((1, t_len, MIX), lambda i: (i, 0, 0)),
        scratch_shapes=[pltpu.VMEM((t_len + 2 * _CONV_PAD, MIX), F32),
                        pltpu.VMEM((8, t_len + 2 * _CONV_PAD, MIX), F32)],
        compiler_params=_cp("parallel"),
        name="conv_module",
    )(za, w, b, ln_g, ln_b)


def _log_sigmoid(x):
    return jnp.minimum(x, 0.0) - jnp.log(1.0 + jnp.exp(-jnp.abs(x)))


def _split3_dot(tri, x):
    hi = x.astype(BF16)
    r1 = x - hi.astype(F32)
    mid = r1.astype(BF16)
    lo = (r1 - mid.astype(F32)).astype(BF16)
    dot = lambda v: jnp.dot(tri, v, preferred_element_type=F32)
    return dot(hi) + dot(mid) + dot(lo)


def _split2(x):
    hi = x.astype(BF16)
    return hi, (x - hi.astype(F32)).astype(BF16)


def _mlstm_chunk(zb_ref, zg_ref, gb_ref, qt_sc, qbd_sc, vt_sc, vbd_sc, ct_sc, nb_sc, c, direction, ms):
    L = CHUNK
    rows = pl.ds(pl.multiple_of(c * L, L), L)
    si = lax.broadcasted_iota(jnp.int32, (L, L), 0)
    ti = lax.broadcasted_iota(jnp.int32, (L, L), 1)
    if direction == 0:
        valid = si <= ti
        tri = jnp.where(ti <= si, 1.0, 0.0).astype(BF16)
        last = L - 1
    else:
        valid = si >= ti
        tri = jnp.where(ti >= si, 1.0, 0.0).astype(BF16)
        last = 0
    g_in = zg_ref[0, rows, 0:LANE] + gb_ref[:, 0:LANE]
    g_f = zg_ref[0, rows, LANE:2 * LANE] + gb_ref[:, LANE:2 * LANE]
    bc = _split3_dot(tri, _log_sigmoid(g_f))
    r = g_in - bc
    g_in_t = g_in.T
    bct = bc.T
    k = zb_ref[0, rows, MIX:2 * MIX]
    k_hi, k_lo = _split2(k)
    qt = qt_sc[c]
    st_all = jnp.dot(k_hi, qbd_sc[c], preferred_element_type=F32)
    sts, inters, floors, colsums, ws, decays, new_m = [], [], [], [], [], [], []
    for h in range(HEADS):
        j = direction * HEADS + h
        i_row = g_in_t[j:j + 1, :]
        b_row = bct[j:j + 1, :]
        m_prev = ms[h]
        rb = jnp.where(valid, jnp.broadcast_to(r[:, j:j + 1], (L, L)), -jnp.inf)
        c_row = jnp.maximum(m_prev, jnp.max(rb, axis=0, keepdims=True))
        st = st_all[:, h * L:(h + 1) * L] * jnp.exp(rb - c_row)
        sts.append(st.astype(BF16))
        colsums.append(jnp.sum(st, axis=0, keepdims=True))
        inters.append(jnp.exp(m_prev - c_row))
        floors.append(jnp.exp(-(b_row + c_row)))
        b_last = b_row[:, last:last + 1]
        lw = b_last - b_row + i_row
        m_new = jnp.maximum(b_last + m_prev, jnp.max(lw, axis=-1, keepdims=True))
        decays.append(jnp.exp(b_last + m_prev - m_new))
        ws.append(jnp.exp(lw - m_new))
        new_m.append(m_new)
    per_head_rows = lambda vs: jnp.concatenate([jnp.broadcast_to(v, (HD, L)) for v in vs], axis=0)
    ct = ct_sc[direction]
    nb = nb_sc[direction]
    num = (per_head_rows(inters) * jnp.dot(ct.astype(BF16), qt, preferred_element_type=F32)
           + jnp.dot(vbd_sc[c], jnp.concatenate(sts, axis=0), preferred_element_type=F32))
    n_hi, n_lo = _split2(nb)
    qn = jnp.dot(n_hi, qt, preferred_element_type=F32) + jnp.dot(n_lo, qt, preferred_element_type=F32)
    dens = [jnp.maximum(jnp.abs(inters[h] * qn[h:h + 1, :] + colsums[h]), floors[h]) for h in range(HEADS)]
    ht = num / per_head_rows(dens)
    lane = lax.broadcasted_iota(jnp.int32, (1, MIX), 1) // HD
    dec_row = sum(jnp.where(lane == h, decays[h], 0.0) for h in range(HEADS))
    upd = jnp.dot((vt_sc[c] * per_head_rows(ws)).astype(BF16), k_hi, preferred_element_type=F32)
    same_head = (lax.broadcasted_iota(jnp.int32, (MIX, MIX), 0) // HD
                 == lax.broadcasted_iota(jnp.int32, (MIX, MIX), 1) // HD)
    ct_sc[direction] = jnp.where(same_head, ct * dec_row + upd, 0.0)
    w_hi, w_lo = _split2(jnp.concatenate(ws + [jnp.zeros((8 - HEADS, L), F32)], axis=0))
    wk = (jnp.dot(w_hi, k_hi, preferred_element_type=F32) + jnp.dot(w_lo, k_hi, preferred_element_type=F32)
          + jnp.dot(w_hi, k_lo, preferred_element_type=F32))
    own = lax.broadcasted_iota(jnp.int32, (8, MIX), 0) == lax.broadcasted_iota(jnp.int32, (8, MIX), 1) // HD
    nb_sc[direction] = jnp.where(own, nb * dec_row + wk, 0.0)
    return ht, tuple(new_m)


def _mlstm_kernel(zb_ref, zg_ref, gb_ref, ng_ref, c0_ref, n0_ref, m0_ref,
                  y_ref, c_ref, n_ref, m_ref, qt_sc, qbd_sc, vt_sc, vbd_sc, ht_sc, ct_sc, nb_sc):
    t_len = y_ref.shape[1]
    nc = t_len // CHUNK
    L = CHUNK
    qbd_sc[...] = jnp.zeros_like(qbd_sc)
    vbd_sc[...] = jnp.zeros_like(vbd_sc)
    ct_sc[...] = jnp.zeros_like(ct_sc)
    nb_sc[...] = jnp.zeros_like(nb_sc)

    def transpose_chunk(c, carry):
        rows = pl.ds(pl.multiple_of(c * L, L), L)
        qt = (zb_ref[0, rows, 0:MIX] * (HD ** -0.5)).T.astype(BF16)
        vt = zb_ref[0, rows, 2 * MIX:3 * MIX].T
        qt_sc[c] = qt
        vt_sc[c] = vt
        for h in range(HEADS):
            sl = slice(h * HD, (h + 1) * HD)
            qbd_sc[c, sl, h * L:(h + 1) * L] = qt[sl, :]
            vbd_sc[c, sl, h * L:(h + 1) * L] = vt[sl, :].astype(BF16)
        return carry

    lax.fori_loop(0, nc, transpose_chunk, 0)

    for j in range(2 * HEADS):
        d, h = divmod(j, HEADS)
        sl = slice(h * HD, (h + 1) * HD)
        ct_sc[d, sl, sl] = c0_ref[0, j]
        nb_sc[d, h:h + 1, sl] = n0_ref[0, j:j + 1, :]
    m_init = tuple(tuple(m0_ref[0, j:j + 1, 0:1] for j in range(d * HEADS, (d + 1) * HEADS)) for d in range(2))

    def scan_body(i, ms):
        cb = nc - 1 - i
        scr = (zb_ref, zg_ref, gb_ref, qt_sc, qbd_sc, vt_sc, vbd_sc, ct_sc, nb_sc)
        ht_sc[0, i], ms_f = _mlstm_chunk(*scr, i, 0, ms[0])
        ht_sc[1, cb], ms_b = _mlstm_chunk(*scr, cb, 1, ms[1])
        return ms_f, ms_b

    m_fin = lax.fori_loop(0, nc, scan_body, m_init)
    for j in range(2 * HEADS):
        d, h = divmod(j, HEADS)
        sl = slice(h * HD, (h + 1) * HD)
        c_ref[0, j] = ct_sc[d, sl, sl]
        n_ref[0, j:j + 1, :] = nb_sc[d, h:h + 1, sl]
        m_ref[0, j:j + 1, :] = jnp.broadcast_to(m_fin[d][h], (1, LANE))

    def out_body(c, carry):
        rows = pl.ds(pl.multiple_of(c * CHUNK, CHUNK), CHUNK)
        normed = []
        for h in range(HEADS):
            tot = ht_sc[0, c, h * HD:(h + 1) * HD, :] + ht_sc[1, c, h * HD:(h + 1) * HD, :]
            normed.append(tot * lax.rsqrt(jnp.mean(tot * tot, axis=0, keepdims=True) + EPS))
        hn = jnp.concatenate(normed, axis=0).T
        y_ref[0, rows, :] = (hn * ng_ref[...] * _sigmoid(zb_ref[0, rows, 3 * MIX:4 * MIX])).astype(y_ref.dtype)
        return carry

    lax.fori_loop(0, nc, out_body, 0)


def _mlstm_mixer(zb, zg, gate_b, norm_g, c0, n0, m0):
    bsz, t_len, _ = zb.shape
    nd = 2 * HEADS
    return pl.pallas_call(
        _mlstm_kernel,
        out_shape=[jax.ShapeDtypeStruct((bsz, t_len, MIX), BF16),
                   jax.ShapeDtypeStruct((bsz, nd, HD, HD), F32),
                   jax.ShapeDtypeStruct((bsz, nd, HD), F32),
                   jax.ShapeDtypeStruct((bsz, nd, LANE), F32)],
        grid=(bsz,),
        in_specs=[pl.BlockSpec((1, t_len, 4 * MIX), lambda i: (i, 0, 0)),
                  pl.BlockSpec((1, t_len, 2 * LANE), lambda i: (i, 0, 0)),
                  pl.BlockSpec((1, 2 * LANE), lambda i: (0, 0)),
                  pl.BlockSpec((1, MIX), lambda i: (0, 0)),
                  pl.BlockSpec((1, nd, HD, HD), lambda i: (i, 0, 0, 0)),
                  pl.BlockSpec((1, nd, HD), lambda i: (i, 0, 0)),
                  pl.BlockSpec((1, nd, LANE), lambda i: (i, 0, 0))],
        out_specs=[pl.BlockSpec((1, t_len, MIX), lambda i: (i, 0, 0)),
                   pl.BlockSpec((1, nd, HD, HD), lambda i: (i, 0, 0, 0)),
                   pl.BlockSpec((1, nd, HD), lambda i: (i, 0, 0)),
                   pl.BlockSpec((1, nd, LANE), lambda i: (i, 0, 0))],
        scratch_shapes=[pltpu.VMEM((t_len // CHUNK, MIX, CHUNK), BF16),
                        pltpu.VMEM((t_len // CHUNK, MIX, HEADS * CHUNK), BF16),
                        pltpu.VMEM((t_len // CHUNK, MIX, CHUNK), F32),
                        pltpu.VMEM((t_len // CHUNK, MIX, HEADS * CHUNK), BF16),
                        pltpu.VMEM((2, t_len // CHUNK, MIX, CHUNK), F32),
                        pltpu.VMEM((2, MIX, MIX), F32),
                        pltpu.VMEM((2, 8, MIX), F32)],
        compiler_params=_cp("parallel"),
        name="mlstm_mixer",
    )(zb, zg, gate_b, norm_g, c0, n0, m0)


_ATT_TQ = 256


def _mla_kernel(*refs, past):
    if past:
        (zc_ref, cos_ref, sin_ref, gq_ref, gkv_ref, wqa_ref, wqb_ref, wk_ref, wv_ref,
         pckv_ref, pkr_ref, y_ref, ckv_ref, q_sc, k_sc, v_sc) = refs
    else:
        (zc_ref, cos_ref, sin_ref, gq_ref, gkv_ref, wqa_ref, wqb_ref, wk_ref, wv_ref,
         y_ref, ckv_ref, q_sc, k_sc, v_sc) = refs
    t_len = y_ref.shape[1]
    cosf = cos_ref[...]
    sinf = sin_ref[...]
    zc = lambda a, b: zc_ref[0, :, a:b].astype(F32)
    cq = _rms(zc(0, Q_RANK)) * gq_ref[...]
    ckv = _rms(zc(Q_RANK, Q_RANK + KV_RANK)) * gkv_ref[...]
    ckv_ref[0] = ckv
    kr = (zc(Q_RANK + KV_RANK, Q_RANK + KV_RANK + LANE) * cosf
          + zc(Q_RANK + KV_RANK + LANE, Q_RANK + KV_RANK + 2 * LANE) * sinf)
    cqb = cq.astype(BF16)
    ckvb = ckv.astype(BF16)
    pair = lambda a: jnp.concatenate([a, a], axis=-1)
    cos2, sin2, kr2 = pair(cosf), pair(sinf), pair(kr)
    for hp in range(HEADS // 2):
        sl = slice(hp * 2 * LANE, (hp + 1) * 2 * LANE)
        qa = jnp.dot(cqb, wqa_ref[:, sl], preferred_element_type=F32)
        qb = jnp.dot(cqb, wqb_ref[:, sl], preferred_element_type=F32)
        q_sc[:, sl] = (qa * cos2 + qb * sin2).astype(BF16)
        kn = jnp.dot(ckvb, wk_ref[:, sl], preferred_element_type=F32)
        k_sc[past:past + t_len, sl] = (kn + kr2).astype(BF16)
        v_sc[past:past + t_len, sl] = jnp.dot(ckvb, wv_ref[:, sl], preferred_element_type=F32).astype(BF16)
    if past:
        pckv = pckv_ref[0].astype(BF16)
        pkr2 = pair(pkr_ref[0])
        for hp in range(HEADS // 2):
            sl = slice(hp * 2 * LANE, (hp + 1) * 2 * LANE)
            k_sc[0:past, sl] = (jnp.dot(pckv, wk_ref[:, sl], preferred_element_type=F32) + pkr2).astype(BF16)
            v_sc[0:past, sl] = jnp.dot(pckv, wv_ref[:, sl], preferred_element_type=F32).astype(BF16)

    def q_block(i, carry):
        rows = pl.ds(pl.multiple_of(i * _ATT_TQ, _ATT_TQ), _ATT_TQ)
        outs = []
        for h in range(HEADS):
            sl = slice(h * LANE, (h + 1) * LANE)
            s = lax.dot_general(q_sc[rows, sl], k_sc[:, sl], (((1,), (1,)), ((), ())),
                                preferred_element_type=F32)
            p = jnp.exp(s - jnp.max(s, axis=-1, keepdims=True))
            o = jnp.dot(p.astype(BF16), v_sc[:, sl], preferred_element_type=F32)
            outs.append(o / jnp.sum(p, axis=-1, keepdims=True))
        y_ref[0, rows, 0:LANE] = (outs[0] + outs[1]).astype(y_ref.dtype)
        y_ref[0, rows, LANE:2 * LANE] = (outs[2] + outs[3]).astype(y_ref.dtype)
        return carry

    lax.fori_loop(0, t_len // _ATT_TQ, q_block, 0)


def _mla_mixer(zc, cosf, sinf, gq, gkv, wqa, wqb, wk, wv, past_ckv=None, past_kr=None):
    bsz, t_len, zw = zc.shape
    past = 0 if past_ckv is None else past_ckv.shape[1]
    full = lambda a: pl.BlockSpec(a.shape, lambda i: (0,) * a.ndim)
    args = [zc, cosf, sinf, gq, gkv, wqa, wqb, wk, wv]
    in_specs = [pl.BlockSpec((1, t_len, zw), lambda i: (i, 0, 0))] + [full(a) for a in args[1:]]
    if past:
        args += [past_ckv, past_kr]
        in_specs += [pl.BlockSpec((1, past, KV_RANK), lambda i: (i, 0, 0)),
                     pl.BlockSpec((1, past, LANE), lambda i: (i, 0, 0))]
    wide = HEADS * LANE
    return pl.pallas_call(
        functools.partial(_mla_kernel, past=past),
        out_shape=[jax.ShapeDtypeStruct((bsz, t_len, MIX), BF16),
                   jax.ShapeDtypeStruct((bsz, t_len, KV_RANK), F32)],
        grid=(bsz,),
        in_specs=in_specs,
        out_specs=[pl.BlockSpec((1, t_len, MIX), lambda i: (i, 0, 0)),
                   pl.BlockSpec((1, t_len, KV_RANK), lambda i: (i, 0, 0))],
        scratch_shapes=[pltpu.VMEM((t_len, wide), BF16),
                        pltpu.VMEM((past + t_len, wide), BF16),
                        pltpu.VMEM((past + t_len, wide), BF16)],
        compiler_params=_cp("parallel"),
        name="mla_mixer",
    )(*args)


_S5_SEQ = 8
_S5_TT = 128


def _s5_kernel(zd_ref, bd_ref, cd_ref, are_ref, aim_ref, s0re_ref, s0im_ref,
               y_ref, fre_ref, fim_ref, bre_sc, bim_sc, sre_sc, sim_sc):
    direction = pl.program_id(0)
    k = pl.program_id(2)
    tt = _S5_TT
    rows = _S5_SEQ * tt

    @pl.when(k == 0)
    def _():
        sre_sc[...] = s0re_ref[0]
        sim_sc[...] = s0im_ref[0]

    u = zd_ref[...].reshape(rows, MIX).astype(BF16)
    bre_sc[...] = jnp.dot(u, bd_ref[0, :, 0:S5_STATES], preferred_element_type=F32)
    bim_sc[...] = jnp.dot(u, bd_ref[0, :, S5_STATES:2 * S5_STATES], preferred_element_type=F32)
    a_re = jnp.broadcast_to(are_ref[0], (_S5_SEQ, S5_STATES))
    a_im = jnp.broadcast_to(aim_ref[0], (_S5_SEQ, S5_STATES))

    def step(i, carry):
        s_re, s_im = carry
        t = i + direction * (tt - 1 - 2 * i)
        sel = pl.ds(pl.multiple_of(t * _S5_SEQ, _S5_SEQ), _S5_SEQ)
        n_re = a_re * s_re - a_im * s_im + bre_sc[sel, :]
        n_im = a_re * s_im + a_im * s_re + bim_sc[sel, :]
        bre_sc[sel, :] = n_re
        bim_sc[sel, :] = n_im
        return n_re, n_im

    s_re, s_im = lax.fori_loop(0, tt, step, (sre_sc[...], sim_sc[...]), unroll=2)
    sre_sc[...] = s_re
    sim_sc[...] = s_im
    y = (jnp.dot(bre_sc[...].astype(BF16), cd_ref[0, 0:S5_STATES, :], preferred_element_type=F32)
         + jnp.dot(bim_sc[...].astype(BF16), cd_ref[0, S5_STATES:2 * S5_STATES, :],
                   preferred_element_type=F32))
    y_ref[0] = y.reshape(tt, _S5_SEQ, MIX)

    @pl.when(k == pl.num_programs(2) - 1)
    def _():
        fre_ref[0] = s_re
        fim_ref[0] = s_im


def _s5_scan(zd, bd, cd, ab_re, ab_im, s0_re, s0_im):
    t_len, bsz, _ = zd.shape
    nt = t_len // _S5_TT
    tile = lambda d, k: k + d * (nt - 1 - 2 * k)
    st = pl.BlockSpec((1, _S5_SEQ, S5_STATES), lambda d, g, k: (d, g, 0))
    return pl.pallas_call(
        _s5_kernel,
        out_shape=[jax.ShapeDtypeStruct((2, t_len, bsz, MIX), F32),
                   jax.ShapeDtypeStruct((2, bsz, S5_STATES), F32),
                   jax.ShapeDtypeStruct((2, bsz, S5_STATES), F32)],
        grid=(2, bsz // _S5_SEQ, nt),
        in_specs=[pl.BlockSpec((_S5_TT, _S5_SEQ, MIX), lambda d, g, k: (tile(d, k), g, 0)),
                  pl.BlockSpec((1, MIX, 2 * S5_STATES), lambda d, g, k: (d, 0, 0)),
                  pl.BlockSpec((1, 2 * S5_STATES, MIX), lambda d, g, k: (d, 0, 0)),
                  pl.BlockSpec((1, 1, S5_STATES), lambda d, g, k: (d, 0, 0)),
                  pl.BlockSpec((1, 1, S5_STATES), lambda d, g, k: (d, 0, 0)),
                  st, st],
        out_specs=[pl.BlockSpec((1, _S5_TT, _S5_SEQ, MIX), lambda d, g, k: (d, tile(d, k), g, 0)),
                   st, st],
        scratch_shapes=[pltpu.VMEM((_S5_SEQ * _S5_TT, S5_STATES), F32),
                        pltpu.VMEM((_S5_SEQ * _S5_TT, S5_STATES), F32),
                        pltpu.VMEM((_S5_SEQ, S5_STATES), F32),
                        pltpu.VMEM((_S5_SEQ, S5_STATES), F32)],
        compiler_params=_cp("parallel", "parallel", "arbitrary"),
        name="s5_scan",
    )(zd, bd, cd, ab_re, ab_im, s0_re, s0_im)


def _gelu_tanh(x):
    return 0.5 * x * (1.0 + jnp.tanh(math.sqrt(2.0 / math.pi) * (x + 0.044715 * x * x * x)))


def _route(logits):
    lane = lax.broadcasted_iota(jnp.int32, logits.shape, 1).astype(F32)
    big = float(LANE)
    neg = -jnp.inf
    g_mask = (lane >= N_EXPERT) & (lane < N_EXPERT + N_GROUP)
    gl = jnp.where(g_mask, logits, neg)
    g_max = jnp.max(gl, axis=-1, keepdims=True)
    g_idx = jnp.min(jnp.where(gl == g_max, lane, big), axis=-1, keepdims=True) - N_EXPERT
    g_sel = 1.0 / jnp.sum(jnp.where(g_mask, jnp.exp(logits - g_max), 0.0), axis=-1, keepdims=True)
    lo = g_idx * PER_GROUP
    el = jnp.where((lane >= lo) & (lane < lo + PER_GROUP), logits, neg)
    v1 = jnp.max(el, axis=-1, keepdims=True)
    i1 = jnp.min(jnp.where(el == v1, lane, big), axis=-1, keepdims=True)
    el2 = jnp.where(lane == i1, neg, el)
    v2 = jnp.max(el2, axis=-1, keepdims=True)
    i2 = jnp.min(jnp.where(el2 == v2, lane, big), axis=-1, keepdims=True)
    e2 = jnp.exp(v2 - v1)
    w1 = g_sel / (1.0 + e2)
    return jnp.where(lane == i1, w1, jnp.where(lane == i2, w1 * e2, 0.0))


def _post_kernel(x_ref, ya_ref, yb_ref, yc_ref, ys_ref, zd_ref, mod_ref, wo_ref, d_ref, wglu_ref, bglu_ref,
                 g2_ref, wr_ref, br_ref, x1_ref, h2_ref, gate_ref):
    m = mod_ref[0]
    zd = zd_ref[...]
    ys = ys_ref[0] + ys_ref[1] + d_ref[...] * zd
    gl = _bdot(_gelu_tanh(ys), wglu_ref[...]) + bglu_ref[...]
    yd = gl[:, 0:MIX] * _sigmoid(gl[:, MIX:2 * MIX])
    mix = (_bdot(ya_ref[...], wo_ref[0:MIX, :]) + _bdot(yb_ref[...], wo_ref[MIX:2 * MIX, :])
           + _bdot(yc_ref[...], wo_ref[2 * MIX:3 * MIX, :]) + _bdot(yd, wo_ref[3 * MIX:4 * MIX, :]))
    x1 = x_ref[...] + m[2:3] * mix
    x1_ref[...] = x1
    h2 = _rms(x1) * g2_ref[...] * (1.0 + m[4:5]) + m[3:4]
    h2b = h2.astype(BF16)
    h2_ref[...] = h2b
    gate_ref[...] = _route(jnp.dot(h2b, wr_ref[...], preferred_element_type=F32) + br_ref[...])


def _post_mixer(x, ya, yb, yc, ys, zd, mod, row_fn, w_out, s5_d, w_glu, b_glu, norm2_g, w_router, b_router, tm):
    n = x.shape[0]
    tok = lambda w: pl.BlockSpec((tm, w), lambda i: (i, 0))
    full = lambda a: pl.BlockSpec(a.shape, lambda i: (0,) * a.ndim)
    return pl.pallas_call(
        _post_kernel,
        out_shape=[jax.ShapeDtypeStruct((n, D_MODEL), F32),
                   jax.ShapeDtypeStruct((n, D_MODEL), BF16),
                   jax.ShapeDtypeStruct((n, LANE), F32)],
        grid=(n // tm,),
        in_specs=[tok(D_MODEL), tok(MIX), tok(MIX), tok(MIX),
                  pl.BlockSpec((2, tm, MIX), lambda i: (0, i, 0)), tok(MIX),
                  pl.BlockSpec((1, 6, D_MODEL), lambda i: (row_fn(i, tm), 0, 0)),
                  full(w_out), full(s5_d), full(w_glu), full(b_glu), full(norm2_g),
                  full(w_router), full(b_router)],
        out_specs=[tok(D_MODEL), tok(D_MODEL), tok(LANE)],
        compiler_params=_cp("parallel"),
        name="post_mixer",
    )(x, ya, yb, yc, ys, zd, mod, w_out, s5_d, w_glu, b_glu, norm2_g, w_router, b_router)


_MOE_EPS = 4
_MOE_TM = 1024


_MOE_ALIGN = 16
_MOE_ROWS = 2 * _MOE_TM + N_EXPERT * _MOE_ALIGN
_MOE_CH = 128
_MOE_GB = 512
_MOE_CB = 256


def _moe_kernel(h_ref, gate_ref, x_ref, mod_ref, w1_ref, w3_ref, w2_ref, o_ref,
                xs_sc, ys_sc, col_sc, row_sc, start_sm, rows_sm, done_sm):
    j = pl.program_id(1)
    tm = h_ref.shape[0]

    @pl.when(j == 0)
    def _dispatch():
        gate = gate_ref[...]
        cnt_row = jnp.sum(jnp.where(gate != 0.0, 1.0, 0.0), axis=0, keepdims=True)
        units_row = jnp.floor((cnt_row + (_MOE_ALIGN - 1)) * (1.0 / _MOE_ALIGN))
        li = lax.broadcasted_iota(jnp.int32, (LANE, LANE), 0)
        lj = lax.broadcasted_iota(jnp.int32, (LANE, LANE), 1)
        before = jnp.where(li < lj, 1.0, 0.0).astype(BF16)
        start_row = jnp.dot(jnp.broadcast_to(units_row, (8, LANE)).astype(BF16), before,
                            preferred_element_type=F32) * _MOE_ALIGN
        start_i = start_row.astype(jnp.int32)
        cnt_i = cnt_row.astype(jnp.int32)
        for e in range(N_EXPERT):
            start_sm[e] = start_i[0, e]
            rows_sm[e] = cnt_i[0, e]
        gt = gate.T[0:N_EXPERT, :]
        chosen = gt != 0.0
        ones = jnp.where(chosen, 1.0, 0.0)
        cnt = jnp.sum(ones, axis=-1, keepdims=True)
        units = jnp.floor((cnt + (_MOE_ALIGN - 1)) * (1.0 / _MOE_ALIGN))
        ei = lax.broadcasted_iota(jnp.int32, (N_EXPERT, N_EXPERT), 0)
        ej = lax.broadcasted_iota(jnp.int32, (N_EXPERT, N_EXPERT), 1)
        start = jnp.dot(jnp.where(ej < ei, 1.0, 0.0).astype(BF16),
                        jnp.broadcast_to(units, (N_EXPERT, LANE)).astype(BF16),
                        preferred_element_type=F32)[:, 0:1] * _MOE_ALIGN
        ti = lax.broadcasted_iota(jnp.int32, (tm, tm), 0)
        tj = lax.broadcasted_iota(jnp.int32, (tm, tm), 1)
        earlier = jnp.where(ti < tj, 1.0, 0.0).astype(BF16)
        pos = start + jnp.dot(ones.astype(BF16), earlier, preferred_element_type=F32)
        p_a = jnp.min(jnp.where(chosen, pos, float(_MOE_ROWS)), axis=0, keepdims=True)
        p_b = jnp.max(jnp.where(chosen, pos, -1.0), axis=0, keepdims=True)
        g_a = jnp.sum(jnp.where(chosen & (pos == p_a), gt, 0.0), axis=0, keepdims=True)
        g_b = jnp.sum(jnp.where(chosen & (pos == p_b) & (p_b != p_a), gt, 0.0), axis=0, keepdims=True)
        row_sc[...] = jnp.concatenate([p_a, p_b, jnp.zeros((6, tm), F32)], axis=0)
        packed = jnp.concatenate([p_a, p_b, g_a, g_b, jnp.zeros((LANE - 4, tm), F32)], axis=0)
        col_sc[...] = packed.T
        xs_sc[_MOE_ROWS:_MOE_ROWS + _MOE_CH, :] = jnp.zeros((_MOE_CH, D_MODEL), BF16)
        ys_sc[...] = jnp.zeros_like(ys_sc)
        o_ref[...] = x_ref[...]
        done_sm[0] = 0
        done_sm[1] = 0

    n_blocks = _MOE_ROWS // _MOE_GB
    e_last = j * _MOE_EPS + (_MOE_EPS - 1)
    seg_end = start_sm[e_last] + rows_sm[e_last]
    last_step = j == pl.num_programs(1) - 1

    def gather_block(b, carry):
        r0 = pl.multiple_of(b * _MOE_GB, _MOE_GB)
        ri = (lax.broadcasted_iota(jnp.int32, (_MOE_GB, tm), 0) + r0).astype(F32)
        sel = jnp.where((ri == row_sc[0:1, :]) | (ri == row_sc[1:2, :]), 1.0, 0.0).astype(BF16)
        xs_sc[pl.ds(r0, _MOE_GB), :] = jnp.dot(sel, h_ref[...], preferred_element_type=F32).astype(BF16)
        return carry

    gathered = jnp.minimum((seg_end + _MOE_CH + _MOE_GB - 1) // _MOE_GB, n_blocks)
    lax.fori_loop(done_sm[0], gathered, gather_block, 0)
    done_sm[0] = gathered

    def expert_rows(el, i):
        rows = pl.ds(pl.multiple_of(start_sm[j * _MOE_EPS + el] + i * _MOE_CH, _MOE_ALIGN), _MOE_CH)
        xs = xs_sc[rows, :]
        h1 = jnp.dot(xs, w1_ref[el], preferred_element_type=F32)
        h3 = jnp.dot(xs, w3_ref[el], preferred_element_type=F32)
        return rows, jnp.dot((h1 * _sigmoid(h1) * h3).astype(BF16), w2_ref[el], preferred_element_type=F32)

    first = [expert_rows(el, 0) for el in range(_MOE_EPS)]
    for rows, y in first:
        ys_sc[rows, :] = y.astype(BF16)
    for el in range(_MOE_EPS):
        n_rows = rows_sm[j * _MOE_EPS + el]

        def chunk(i, carry, el=el, n_rows=n_rows):
            rows, y = expert_rows(el, i)
            live = lax.broadcasted_iota(jnp.int32, (_MOE_CH, 1), 0) + i * _MOE_CH < n_rows
            ys_sc[rows, :] = jnp.where(live, y, ys_sc[rows, :].astype(F32)).astype(BF16)
            return carry

        lax.fori_loop(1, (n_rows + _MOE_CH - 1) // _MOE_CH, chunk, 0)

    g2 = mod_ref[0][5:6]

    def combine_block(b, carry):
        r0 = pl.multiple_of(b * _MOE_GB, _MOE_GB)
        y = ys_sc[pl.ds(r0, _MOE_GB), :]
        for t0 in range(0, tm, _MOE_CB):
            c = col_sc[t0:t0 + _MOE_CB, :]
            ri = (lax.broadcasted_iota(jnp.int32, (_MOE_CB, _MOE_GB), 1) + r0).astype(F32)
            w = (jnp.where(ri == c[:, 0:1], c[:, 2:3], 0.0)
                 + jnp.where(ri == c[:, 1:2], c[:, 3:4], 0.0)).astype(BF16)
            o_ref[t0:t0 + _MOE_CB, :] += g2 * jnp.dot(w, y, preferred_element_type=F32)
        return carry

    final_rows = (seg_end + _MOE_ALIGN - 1) // _MOE_ALIGN * _MOE_ALIGN
    combined = jnp.where(last_step, n_blocks, final_rows // _MOE_GB)
    lax.fori_loop(done_sm[1], combined, combine_block, 0)
    done_sm[1] = combined


def _moe(h2, gate, x1, mod, row_fn, w1, w3, w2, layer, tm):
    n = h2.shape[0]
    first_blk = layer * (N_EXPERT // _MOE_EPS)
    return pl.pallas_call(
        _moe_kernel,
        out_shape=jax.ShapeDtypeStruct((n, D_MODEL), F32),
        grid=(n // tm, N_EXPERT // _MOE_EPS),
        in_specs=[pl.BlockSpec((tm, D_MODEL), lambda i, j: (i, 0)),
                  pl.BlockSpec((tm, LANE), lambda i, j: (i, 0)),
                  pl.BlockSpec((tm, D_MODEL), lambda i, j: (i, 0)),
                  pl.BlockSpec((1, 6, D_MODEL), lambda i, j: (row_fn(i, tm), 0, 0)),
                  pl.BlockSpec((_MOE_EPS, D_MODEL, MOE_FF), lambda i, j: (first_blk + j, 0, 0)),
                  pl.BlockSpec((_MOE_EPS, D_MODEL, MOE_FF), lambda i, j: (first_blk + j, 0, 0)),
                  pl.BlockSpec((_MOE_EPS, MOE_FF, D_MODEL), lambda i, j: (first_blk + j, 0, 0))],
        out_specs=pl.BlockSpec((tm, D_MODEL), lambda i, j: (i, 0)),
        scratch_shapes=[pltpu.VMEM((_MOE_ROWS + _MOE_CH, D_MODEL), BF16),
                        pltpu.VMEM((_MOE_ROWS + _MOE_CH, D_MODEL), BF16),
                        pltpu.VMEM((tm, LANE), F32),
                        pltpu.VMEM((8, tm), F32),
                        pltpu.SMEM((N_EXPERT,), jnp.int32),
                        pltpu.SMEM((N_EXPERT,), jnp.int32),
                        pltpu.SMEM((2,), jnp.int32)],
        compiler_params=_cp("parallel", "arbitrary"),
        name="moe",
    )(h2, gate, x1, mod, w1, w3, w2)


def _final_kernel(x_ref, g_ref, o_ref):
    o_ref[...] = _rms(x_ref[...]) * g_ref[...]


def _final_norm(x, g, tm):
    n = x.shape[0]
    return pl.pallas_call(
        _final_kernel,
        out_shape=jax.ShapeDtypeStruct((n, D_MODEL), F32),
        grid=(n // tm,),
        in_specs=[pl.BlockSpec((tm, D_MODEL), lambda i: (i, 0)),
                  pl.BlockSpec((1, D_MODEL), lambda i: (0, 0))],
        out_specs=pl.BlockSpec((tm, D_MODEL), lambda i: (i, 0)),
        compiler_params=_cp("parallel"),
        name="final_norm",
    )(x, g)


def _rot_cols(w):
    q = ROPE // 4
    return jnp.concatenate([-w[..., q:2 * q], w[..., 0:q], -w[..., 3 * q:4 * q], w[..., 2 * q:3 * q]], axis=-1)


def _rope_slot(w):
    pad = [(0, 0)] * (w.ndim - 1) + [(HD, LANE - HD - ROPE)]
    return jnp.pad(w, pad)


def _gate_lanes(g):
    n = g.shape[-1] // 2
    pad = [(0, 0)] * (g.ndim - 1) + [(0, LANE - n)]
    return jnp.concatenate([jnp.pad(g[..., :n], pad), jnp.pad(g[..., n:], pad)], axis=-1)


def _prep_w_in(w_in, gate_cols):
    a = w_in[..., 0:512]
    qkvo = w_in[..., 512:1536]
    g = _gate_lanes(w_in[..., 1536:1536 + gate_cols])
    c0 = 1536 + gate_cols
    cq = w_in[..., c0:c0 + Q_RANK]
    ckv = w_in[..., c0 + Q_RANK:c0 + Q_RANK + KV_RANK]
    kr = w_in[..., c0 + Q_RANK + KV_RANK:c0 + Q_RANK + KV_RANK + ROPE]
    d = w_in[..., c0 + Q_RANK + KV_RANK + ROPE:]
    return jnp.concatenate([a, qkvo, g, cq, ckv, _rope_slot(kr), _rope_slot(_rot_cols(kr)), d],
                           axis=-1).astype(BF16)


def _prep_mla(w_uq, w_ukv):
    n_layer = w_uq.shape[0]
    scale = (HD + ROPE) ** -0.5
    wq = w_uq.reshape(n_layer, Q_RANK, HEADS, HD + ROPE) * scale
    nope, rope = wq[..., :HD], wq[..., HD:]
    zeros_r = jnp.zeros_like(rope)
    wqa = jnp.concatenate([nope, rope, zeros_r], axis=-1).reshape(n_layer, Q_RANK, HEADS * LANE)
    wqb = jnp.concatenate([jnp.zeros_like(nope), _rot_cols(rope), zeros_r], axis=-1)
    wqb = wqb.reshape(n_layer, Q_RANK, HEADS * LANE)
    wkv = w_ukv.reshape(n_layer, KV_RANK, HEADS, 2 * HD)
    k_nope, val = wkv[..., :HD], wkv[..., HD:]
    zeros_h = jnp.zeros_like(k_nope)
    wk = jnp.concatenate([k_nope, zeros_h], axis=-1).reshape(n_layer, KV_RANK, HEADS * LANE)
    even = jnp.concatenate([val, zeros_h], axis=-1)
    odd = jnp.concatenate([zeros_h, val], axis=-1)
    is_odd = (jnp.arange(HEADS) % 2 == 1)[None, None, :, None]
    wv = jnp.where(is_odd, odd, even).reshape(n_layer, KV_RANK, HEADS * LANE)
    return wqa.astype(BF16), wqb.astype(BF16), wk.astype(BF16), wv.astype(BF16)


def _rope_tables(t_len, rotate):
    ones = jnp.ones((t_len, HD), F32)
    zeros = jnp.zeros((t_len, HD), F32)
    tail = jnp.zeros((t_len, LANE - HD - ROPE), F32)
    if not rotate:
        return (jnp.concatenate([ones, jnp.ones((t_len, ROPE), F32), tail], axis=-1),
                jnp.zeros((t_len, LANE), F32))
    rows = t_len // GRID_W
    row = jnp.repeat(jnp.arange(rows, dtype=F32), GRID_W)
    col = jnp.tile(jnp.arange(GRID_W, dtype=F32), rows)
    nf = ROPE // 4
    inv = ROPE_BASE ** (-jnp.arange(nf, dtype=F32) / nf)
    ar = row[:, None] * inv
    ac = col[:, None] * inv
    cos = jnp.concatenate([jnp.cos(ar), jnp.cos(ar), jnp.cos(ac), jnp.cos(ac)], axis=-1)
    sin = jnp.concatenate([jnp.sin(ar), jnp.sin(ar), jnp.sin(ac), jnp.sin(ac)], axis=-1)
    return (jnp.concatenate([ones, cos, tail], axis=-1), jnp.concatenate([zeros, sin, tail], axis=-1))


def _prep_s5(bb_re, bb_im, c_re, c_im):
    eye = jnp.eye(S5_G, dtype=F32)
    to_b = lambda bb: jnp.einsum("ldgnc,gh->ldgchn", bb, eye).reshape(bb.shape[0], 2, MIX, S5_STATES)
    to_c = lambda cc: jnp.einsum("ldgcn,gh->ldgnhc", cc, eye).reshape(cc.shape[0], 2, S5_STATES, MIX)
    bd = jnp.concatenate([to_b(bb_re), to_b(bb_im)], axis=-1)
    cd = jnp.concatenate([to_c(c_re.astype(F32)), -to_c(c_im.astype(F32))], axis=-2)
    return bd.astype(BF16), cd.astype(BF16)


def _layer(x, bsz, t_len, mod, row_fn, p, ctx, tm):
    za, zb, zg, zc, zd = _pre_mixer(x, mod, row_fn, p["norm1_g"], p["w_in"], tm)
    seq = lambda a: a.reshape(bsz, t_len, a.shape[-1])
    ya = _conv_module(seq(za), p["conv_w"], p["conv_b"], p["conv_ln_g"], p["conv_ln_b"])
    yb, c_fin, n_fin, m_fin = _mlstm_mixer(seq(zb), seq(zg), p["gate_b"], p["mlstm_norm_g"],
                                           ctx["mlstm_c"], ctx["mlstm_n"], ctx["mlstm_m"])
    yc, ckv = _mla_mixer(seq(zc), ctx["cos"], ctx["sin"], p["mla_q_norm_g"], p["mla_kv_norm_g"],
                         p["wqa"], p["wqb"], p["wk"], p["wv"], ctx.get("past_ckv"), ctx.get("past_kr"))
    ys, s_re, s_im = _s5_scan(seq(zd).transpose(1, 0, 2), p["s5_bd"], p["s5_cd"], p["s5_ab_re"], p["s5_ab_im"],
                              ctx["s5_re"], ctx["s5_im"])
    ys = ys.transpose(0, 2, 1, 3)
    flat = lambda a: a.reshape(bsz * t_len, a.shape[-1])
    x1, h2, gate = _post_mixer(x, flat(ya), flat(yb), flat(yc), ys.reshape(2, bsz * t_len, MIX), zd, mod, row_fn,
                               p["w_out"], p["s5_d"], p["s5_w_glu"], p["s5_b_glu"], p["norm2_g"],
                               p["w_router"], p["b_router"], tm)
    x2 = _moe(h2, gate, x1, mod, row_fn, p["moe_w1"], p["moe_w3"], p["moe_w2"], p["layer"], _MOE_TM)
    krope = seq(zc)[:, :, Q_RANK + KV_RANK + HD:Q_RANK + KV_RANK + HD + ROPE].astype(F32)
    return x2, (ckv, krope, c_fin, n_fin, m_fin, s_re, s_im)


def kernel(x_prompt, x_sample, cache_mla_ckv, cache_mla_krope, state_mlstm_C, state_mlstm_n, state_mlstm_m, state_s5, c, c_ctx, norm1_g, norm2_g, final_g, w_mod, b_mod, w_in, w_out, conv_w, conv_b, conv_ln_g, conv_ln_b, mlstm_gate_b, mlstm_norm_g, mla_q_norm_g, mla_w_uq, mla_kv_norm_g, mla_w_ukv, s5_a_re, s5_a_im, s5_log_dt, s5_b_re, s5_b_im, s5_c_re, s5_c_im, s5_d, s5_w_glu, s5_b_glu, moe_w_group, moe_b_group, moe_w_expert, moe_b_expert, moe_w1, moe_w3, moe_w2):
    n_layer = w_in.shape[0]
    b_ctx, t_ctx, d = x_prompt.shape
    b_lat, t_lat, _ = x_sample.shape
    nd = 2 * HEADS
    tm = 512
    assert b_lat + 1 <= 16 and t_lat % _MOE_TM == 0 and (b_ctx * t_ctx) % _MOE_TM == 0

    c_all = jnp.zeros((16, d), F32).at[0].set(c_ctx).at[1:1 + b_lat].set(c)
    mod_all = _modulation(c_all, w_mod, b_mod).reshape(n_layer, 16, 6, d)
    gate_cols = 4 * HEADS
    w_in_ext = _prep_w_in(w_in, gate_cols)
    wqa, wqb, wk, wv = _prep_mla(mla_w_uq, mla_w_ukv)
    ab_re, ab_im, bb_re, bb_im = _s5_discretise(s5_a_re, s5_a_im, s5_log_dt, s5_b_re, s5_b_im)
    s5_bd, s5_cd = _prep_s5(bb_re, bb_im, s5_c_re, s5_c_im)
    w_router = jnp.pad(jnp.concatenate([moe_w_expert, moe_w_group], axis=-1),
                       [(0, 0), (0, 0), (0, LANE - N_EXPERT - N_GROUP)]).astype(BF16)
    b_router = jnp.pad(jnp.concatenate([moe_b_expert, moe_b_group], axis=-1),
                       [(0, 0), (0, LANE - N_EXPERT - N_GROUP)])
    gate_b = _gate_lanes(mlstm_gate_b.reshape(n_layer, gate_cols))
    conv_w_p = jnp.pad(conv_w, [(0, 0), (0, 32 - CONV_WIDTH), (0, 0)])
    w_out_b = w_out.astype(BF16)
    w_glu_b = s5_w_glu.astype(BF16)
    flat_experts = lambda w: w.astype(BF16).reshape((n_layer * N_EXPERT,) + w.shape[2:])
    w1_b, w3_b, w2_b = flat_experts(moe_w1), flat_experts(moe_w3), flat_experts(moe_w2)
    row = lambda a, l: a[l][None, :]

    cos_ctx, sin_ctx = _rope_tables(t_ctx, rotate=False)
    cos_lat, sin_lat = _rope_tables(t_lat, rotate=True)
    zero_state = dict(
        mlstm_c=jnp.zeros((b_ctx, nd, HD, HD), F32), mlstm_n=jnp.zeros((b_ctx, nd, HD), F32),
        mlstm_m=jnp.zeros((b_ctx, nd, LANE), F32),
        s5_re=jnp.zeros((2, b_ctx, S5_STATES), F32), s5_im=jnp.zeros((2, b_ctx, S5_STATES), F32),
        cos=cos_ctx, sin=sin_ctx)

    row_ctx = lambda i, tile: 0
    row_lat = lambda i, tile: 1 + (i * tile) // t_lat

    x_ctx = x_prompt.reshape(b_ctx * t_ctx, d)
    x_lat = x_sample.reshape(b_lat * t_lat, d)
    outs = []
    for l in range(n_layer):
        p = dict(norm1_g=row(norm1_g, l), norm2_g=row(norm2_g, l), w_in=w_in_ext[l], w_out=w_out_b[l],
                 conv_w=conv_w_p[l], conv_b=row(conv_b, l), conv_ln_g=row(conv_ln_g, l),
                 conv_ln_b=row(conv_ln_b, l), gate_b=row(gate_b, l), mlstm_norm_g=row(mlstm_norm_g, l),
                 mla_q_norm_g=row(mla_q_norm_g, l), mla_kv_norm_g=row(mla_kv_norm_g, l),
                 wqa=wqa[l], wqb=wqb[l], wk=wk[l], wv=wv[l],
                 s5_bd=s5_bd[l], s5_cd=s5_cd[l], s5_ab_re=ab_re[l][:, None, :], s5_ab_im=ab_im[l][:, None, :],
                 s5_d=row(s5_d, l), s5_w_glu=w_glu_b[l], s5_b_glu=row(s5_b_glu, l),
                 w_router=w_router[l], b_router=row(b_router, l),
                 moe_w1=w1_b, moe_w3=w3_b, moe_w2=w2_b, layer=l)
        x_ctx, st = _layer(x_ctx, b_ctx, t_ctx, mod_all[l], row_ctx, p, zero_state, tm)
        outs.append(st)
        s5_l = state_s5[:, l].reshape(b_lat, 2, S5_STATES, 2)
        lat_state = dict(
            mlstm_c=jnp.swapaxes(state_mlstm_C[:, l].reshape(b_lat, nd, HD, HD), -1, -2),
            mlstm_n=state_mlstm_n[:, l].reshape(b_lat, nd, HD),
            mlstm_m=jnp.broadcast_to(state_mlstm_m[:, l].reshape(b_lat, nd, 1), (b_lat, nd, LANE)),
            s5_re=s5_l[..., 0].transpose(1, 0, 2), s5_im=s5_l[..., 1].transpose(1, 0, 2),
            cos=cos_lat, sin=sin_lat,
            past_ckv=cache_mla_ckv[:, l], past_kr=_rope_slot(cache_mla_krope[:, l]))
        x_lat, _ = _layer(x_lat, b_lat, t_lat, mod_all[l], row_lat, p, lat_state, tm)

    y_prompt = _final_norm(x_ctx, final_g[None, :], tm).reshape(b_ctx, t_ctx, d)
    y_sample = _final_norm(x_lat, final_g[None, :], tm).reshape(b_lat, t_lat, d)
    stack = lambda i: jnp.stack([o[i] for o in outs], axis=1)
    new_ckv = stack(0)
    new_krope = stack(1)
    new_c = jnp.swapaxes(stack(2), -1, -2).reshape(b_ctx, n_layer, 2, HEADS, HD, HD)
    new_n = stack(3).reshape(b_ctx, n_layer, 2, HEADS, HD)
    new_m = stack(4)[..., 0].reshape(b_ctx, n_layer, 2, HEADS)
    s_re = jnp.stack([o[5] for o in outs], axis=0)
    s_im = jnp.stack([o[6] for o in outs], axis=0)
    new_s5 = jnp.stack([s_re, s_im], axis=-1).transpose(2, 0, 1, 3, 4)
    new_s5 = new_s5.reshape(b_ctx, n_layer, 2, S5_G, S5_N, 2)
    return (y_prompt, y_sample, new_ckv, new_krope, new_c, new_n, new_m, new_s5)
```

```python
import functools
import math

import jax
import jax.numpy as jnp
from jax import lax
from jax.experimental import pallas as pl
from jax.experimental.pallas import tpu as pltpu

F32 = jnp.float32
BF16 = jnp.bfloat16
EPS = 1e-6

D_MODEL = 1024
MIX = 256
CONV_WIDTH = 31
HEADS = 4
HD = 64
CHUNK = 128
ROPE = 32
KV_RANK = 128
Q_RANK = 256
GRID_W = 64
ROPE_BASE = 10000.0
S5_G = 16
S5_GC = 16
S5_N = 64
S5_STATES = S5_G * S5_N
N_EXPERT = 32
PER_GROUP = 8
N_GROUP = 4
MOE_FF = 256
LANE = 128
Z_COLS = 2688
VMEM_LIMIT = 56 * 1024 * 1024


def _cp(*sem):
    return pltpu.CompilerParams(dimension_semantics=sem, vmem_limit_bytes=VMEM_LIMIT)


def _rms(x):
    return x * lax.rsqrt(jnp.mean(x * x, axis=-1, keepdims=True) + EPS)


def _sigmoid(x):
    return 1.0 / (1.0 + jnp.exp(-x))


def _bdot(a, b):
    return jnp.dot(a.astype(BF16), b.astype(BF16), preferred_element_type=F32)


def _mod_kernel(c_ref, w_ref, b_ref, o_ref):
    c = c_ref[...]
    o_ref[0] = _bdot(c * _sigmoid(c), w_ref[0]) + b_ref[0]


def _modulation(c_all, w_mod, b_mod):
    n_layer, d, n = w_mod.shape
    tn = 1536
    return pl.pallas_call(
        _mod_kernel,
        out_shape=jax.ShapeDtypeStruct((n_layer, 16, n), F32),
        grid=(n_layer, n // tn),
        in_specs=[pl.BlockSpec((16, d), lambda l, j: (0, 0)),
                  pl.BlockSpec((1, d, tn), lambda l, j: (l, 0, j)),
                  pl.BlockSpec((1, 1, tn), lambda l, j: (l, 0, j))],
        out_specs=pl.BlockSpec((1, 16, tn), lambda l, j: (l, 0, j)),
        compiler_params=_cp("parallel", "parallel"),
        name="modulation",
    )(c_all, w_mod, b_mod.reshape(n_layer, 1, n))


def _s5_disc_kernel(are_ref, aim_ref, ldt_ref, bre_ref, bim_ref, abre_ref, abim_ref, bbre_ref, bbim_ref):
    a_re = are_ref[...]
    a_im = aim_ref[...]
    dt = jnp.exp(ldt_ref[...])
    mag = jnp.exp(a_re * dt)
    ab_re = mag * jnp.cos(a_im * dt)
    ab_im = mag * jnp.sin(a_im * dt)
    den = a_re * a_re + a_im * a_im
    f_re = ((ab_re - 1.0) * a_re + ab_im * a_im) / den
    f_im = (ab_im * a_re - (ab_re - 1.0) * a_im) / den
    b_re = bre_ref[...]
    b_im = bim_ref[...]
    abre_ref[...] = ab_re
    abim_ref[...] = ab_im
    bbre_ref[...] = f_re * b_re - f_im * b_im
    bbim_ref[...] = f_re * b_im + f_im * b_re


def _s5_discretise(a_re, a_im, log_dt, b_re, b_im):
    n_layer = a_re.shape[0]
    rows = n_layer * 2 * S5_G
    cols = S5_N * S5_GC
    rep = lambda a: jnp.repeat(a.reshape(rows, S5_N), S5_GC, axis=1)
    ldt = jnp.broadcast_to(log_dt.reshape(rows, 1), (rows, cols))
    spec = pl.BlockSpec((rows, cols), lambda: (0, 0))
    ab_re, ab_im, bb_re, bb_im = pl.pallas_call(
        _s5_disc_kernel,
        out_shape=[jax.ShapeDtypeStruct((rows, cols), F32)] * 4,
        in_specs=[spec] * 5,
        out_specs=[spec] * 4,
        name="s5_discretise",
    )(rep(a_re), rep(a_im), ldt, b_re.reshape(rows, cols), b_im.reshape(rows, cols))
    pick = lambda a: a[:, ::S5_GC].reshape(n_layer, 2, S5_STATES)
    shp = (n_layer, 2, S5_G, S5_N, S5_GC)
    return pick(ab_re), pick(ab_im), bb_re.reshape(shp), bb_im.reshape(shp)


def _pre_kernel(x_ref, mod_ref, g_ref, w_ref, za_ref, zb_ref, zg_ref, zc_ref, zd_ref):
    m = mod_ref[0]
    h = _rms(x_ref[...]) * g_ref[...] * (1.0 + m[1:2]) + m[0:1]
    hb = h.astype(BF16)
    col = 0
    for o_ref in (za_ref, zb_ref, zg_ref, zc_ref, zd_ref):
        n = o_ref.shape[-1]
        o_ref[...] = jnp.dot(hb, w_ref[:, col:col + n], preferred_element_type=F32).astype(o_ref.dtype)
        col += n


def _pre_mixer(x, mod, row_fn, norm_g, w_in_ext, tm):
    n = x.shape[0]
    widths = (512, 1024, 2 * LANE, 640, MIX)
    return pl.pallas_call(
        _pre_kernel,
        out_shape=[jax.ShapeDtypeStruct((n, w), dt) for w, dt in zip(widths, (BF16, F32, F32, BF16, F32))],
        grid=(n // tm,),
        in_specs=[pl.BlockSpec((tm, D_MODEL), lambda i: (i, 0)),
                  pl.BlockSpec((1, 6, D_MODEL), lambda i: (row_fn(i, tm), 0, 0)),
                  pl.BlockSpec((1, D_MODEL), lambda i: (0, 0)),
                  pl.BlockSpec((D_MODEL, Z_COLS), lambda i: (0, 0))],
        out_specs=[pl.BlockSpec((tm, w), lambda i: (i, 0)) for w in widths],
        compiler_params=_cp("parallel"),
        name="pre_mixer",
    )(x, mod, norm_g, w_in_ext)


_CONV_PAD = 16
_CONV_TT = 128


def _conv_kernel(za_ref, w_ref, b_ref, lg_ref, lb_ref, o_ref, hp_ref, sh_ref):
    t_len = o_ref.shape[1]
    u = za_ref[0].astype(F32)
    hp_ref[0:_CONV_PAD, :] = jnp.zeros((_CONV_PAD, MIX), F32)
    hp_ref[_CONV_PAD + t_len:2 * _CONV_PAD + t_len, :] = jnp.zeros((_CONV_PAD, MIX), F32)
    hp_ref[_CONV_PAD:_CONV_PAD + t_len, :] = u[:, :MIX] * _sigmoid(u[:, MIX:])
    span = t_len + 2 * _CONV_PAD - 8
    for off in range(8):
        sh_ref[off, 0:span, :] = hp_ref[off:off + span, :]
    w = w_ref[...]
    half = CONV_WIDTH // 2
    for t0 in range(0, t_len, _CONV_TT):
        acc = jnp.zeros((_CONV_TT, MIX), F32) + b_ref[...]
        for k in range(CONV_WIDTH):
            start = t0 + _CONV_PAD - half + k
            aligned = start // 8 * 8
            acc = acc + sh_ref[start - aligned, aligned:aligned + _CONV_TT, :] * w[k:k + 1, :]
        mu = jnp.mean(acc, axis=-1, keepdims=True)
        cen = acc - mu
        var = jnp.mean(cen * cen, axis=-1, keepdims=True)
        yn = cen * lax.rsqrt(var + EPS) * lg_ref[...] + lb_ref[...]
        o_ref[0, t0:t0 + _CONV_TT, :] = (yn * _sigmoid(yn)).astype(o_ref.dtype)


def _conv_module(za, w, b, ln_g, ln_b):
    bsz, t_len, _ = za.shape
    vec = pl.BlockSpec((1, MIX), lambda i: (0, 0))
    return pl.pallas_call(
        _conv_kernel,
        out_shape=jax.ShapeDtypeStruct((bsz, t_len, MIX), BF16),
        grid=(bsz,),
        in_specs=[pl.BlockSpec((1, t_len, 2 * MIX), lambda i: (i, 0, 0)),
                  pl.BlockSpec((32, MIX), lambda i: (0, 0)), vec, vec, vec],
        out_specs=pl.BlockSpec((1, t_len, MIX), lambda i: (i, 0, 0)),
        scratch_shapes=[pltpu.VMEM((t_len + 2 * _CONV_PAD, MIX), F32),
                        pltpu.VMEM((8, t_len + 2 * _CONV_PAD, MIX), F32)],
        compiler_params=_cp("parallel"),
        name="conv_module",
    )(za, w, b, ln_g, ln_b)


def _log_sigmoid(x):
    return jnp.minimum(x, 0.0) - jnp.log(1.0 + jnp.exp(-jnp.abs(x)))


def _split3_dot(tri, x):
    hi = x.astype(BF16)
    r1 = x - hi.astype(F32)
    mid = r1.astype(BF16)
    lo = (r1 - mid.astype(F32)).astype(BF16)
    dot = lambda v: jnp.dot(tri, v, preferred_element_type=F32)
    return dot(hi) + dot(mid) + dot(lo)


def _split2(x):
    hi = x.astype(BF16)
    return hi, (x - hi.astype(F32)).astype(BF16)


def _mlstm_chunk(zb_ref, zg_ref, gb_ref, qt_sc, qbd_sc, vt_sc, vbd_sc, ct_sc, nb_sc, c, direction, ms):
    L = CHUNK
    rows = pl.ds(pl.multiple_of(c * L, L), L)
    si = lax.broadcasted_iota(jnp.int32, (L, L), 0)
    ti = lax.broadcasted_iota(jnp.int32, (L, L), 1)
    if direction == 0:
        valid = si <= ti
        tri = jnp.where(ti <= si, 1.0, 0.0).astype(BF16)
        last = L - 1
    else:
        valid = si >= ti
        tri = jnp.where(ti >= si, 1.0, 0.0).astype(BF16)
        last = 0
    g_in = zg_ref[0, rows, 0:LANE] + gb_ref[:, 0:LANE]
    g_f = zg_ref[0, rows, LANE:2 * LANE] + gb_ref[:, LANE:2 * LANE]
    bc = _split3_dot(tri, _log_sigmoid(g_f))
    r = g_in - bc
    g_in_t = g_in.T
    bct = bc.T
    k = zb_ref[0, rows, MIX:2 * MIX]
    k_hi, k_lo = _split2(k)
    qt = qt_sc[c]
    st_all = jnp.dot(k_hi, qbd_sc[c], preferred_element_type=F32)
    sts, inters, floors, colsums, ws, decays, new_m = [], [], [], [], [], [], []
    for h in range(HEADS):
        j = direction * HEADS + h
        i_row = g_in_t[j:j + 1, :]
        b_row = bct[j:j + 1, :]
        m_prev = ms[h]
        rb = jnp.where(valid, jnp.broadcast_to(r[:, j:j + 1], (L, L)), -jnp.inf)
        c_row = jnp.maximum(m_prev, jnp.max(rb, axis=0, keepdims=True))
        st = st_all[:, h * L:(h + 1) * L] * jnp.exp(rb - c_row)
        sts.append(st.astype(BF16))
        colsums.append(jnp.sum(st, axis=0, keepdims=True))
        inters.append(jnp.exp(m_prev - c_row))
        floors.append(jnp.exp(-(b_row + c_row)))
        b_last = b_row[:, last:last + 1]
        lw = b_last - b_row + i_row
        m_new = jnp.maximum(b_last + m_prev, jnp.max(lw, axis=-1, keepdims=True))
        decays.append(jnp.exp(b_last + m_prev - m_new))
        ws.append(jnp.exp(lw - m_new))
        new_m.append(m_new)
    per_head_rows = lambda vs: jnp.concatenate([jnp.broadcast_to(v, (HD, L)) for v in vs], axis=0)
    ct = ct_sc[direction]
    nb = nb_sc[direction]
    num = (per_head_rows(inters) * jnp.dot(ct.astype(BF16), qt, preferred_element_type=F32)
           + jnp.dot(vbd_sc[c], jnp.concatenate(sts, axis=0), preferred_element_type=F32))
    n_hi, n_lo = _split2(nb)
    qn = jnp.dot(n_hi, qt, preferred_element_type=F32) + jnp.dot(n_lo, qt, preferred_element_type=F32)
    dens = [jnp.maximum(jnp.abs(inters[h] * qn[h:h + 1, :] + colsums[h]), floors[h]) for h in range(HEADS)]
    ht = num / per_head_rows(dens)
    lane = lax.broadcasted_iota(jnp.int32, (1, MIX), 1) // HD
    dec_row = sum(jnp.where(lane == h, decays[h], 0.0) for h in range(HEADS))
    upd = jnp.dot((vt_sc[c] * per_head_rows(ws)).astype(BF16), k_hi, preferred_element_type=F32)
    same_head = (lax.broadcasted_iota(jnp.int32, (MIX, MIX), 0) // HD
                 == lax.broadcasted_iota(jnp.int32, (MIX, MIX), 1) // HD)
    ct_sc[direction] = jnp.where(same_head, ct * dec_row + upd, 0.0)
    w_hi, w_lo = _split2(jnp.concatenate(ws + [jnp.zeros((8 - HEADS, L), F32)], axis=0))
    wk = (jnp.dot(w_hi, k_hi, preferred_element_type=F32) + jnp.dot(w_lo, k_hi, preferred_element_type=F32)
          + jnp.dot(w_hi, k_lo, preferred_element_type=F32))
    own = lax.broadcasted_iota(jnp.int32, (8, MIX), 0) == lax.broadcasted_iota(jnp.int32, (8, MIX), 1) // HD
    nb_sc[direction] = jnp.where(own, nb * dec_row + wk, 0.0)
    return ht, tuple(new_m)


def _mlstm_kernel(zb_ref, zg_ref, gb_ref, ng_ref, c0_ref, n0_ref, m0_ref,
                  y_ref, c_ref, n_ref, m_ref, qt_sc, qbd_sc, vt_sc, vbd_sc, ht_sc, ct_sc, nb_sc):
    t_len = y_ref.shape[1]
    nc = t_len // CHUNK
    L = CHUNK
    qbd_sc[...] = jnp.zeros_like(qbd_sc)
    vbd_sc[...] = jnp.zeros_like(vbd_sc)
    ct_sc[...] = jnp.zeros_like(ct_sc)
    nb_sc[...] = jnp.zeros_like(nb_sc)

    def transpose_chunk(c, carry):
        rows = pl.ds(pl.multiple_of(c * L, L), L)
        qt = (zb_ref[0, rows, 0:MIX] * (HD ** -0.5)).T.astype(BF16)
        vt = zb_ref[0, rows, 2 * MIX:3 * MIX].T
        qt_sc[c] = qt
        vt_sc[c] = vt
        for h in range(HEADS):
            sl = slice(h * HD, (h + 1) * HD)
            qbd_sc[c, sl, h * L:(h + 1) * L] = qt[sl, :]
            vbd_sc[c, sl, h * L:(h + 1) * L] = vt[sl, :].astype(BF16)
        return carry

    lax.fori_loop(0, nc, transpose_chunk, 0)

    for j in range(2 * HEADS):
        d, h = divmod(j, HEADS)
        sl = slice(h * HD, (h + 1) * HD)
        ct_sc[d, sl, sl] = c0_ref[0, j]
        nb_sc[d, h:h + 1, sl] = n0_ref[0, j:j + 1, :]
    m_init = tuple(tuple(m0_ref[0, j:j + 1, 0:1] for j in range(d * HEADS, (d + 1) * HEADS)) for d in range(2))

    def scan_body(i, ms):
        cb = nc - 1 - i
        scr = (zb_ref, zg_ref, gb_ref, qt_sc, qbd_sc, vt_sc, vbd_sc, ct_sc, nb_sc)
        ht_sc[0, i], ms_f = _mlstm_chunk(*scr, i, 0, ms[0])
        ht_sc[1, cb], ms_b = _mlstm_chunk(*scr, cb, 1, ms[1])
        return ms_f, ms_b

    m_fin = lax.fori_loop(0, nc, scan_body, m_init)
    for j in range(2 * HEADS):
        d, h = divmod(j, HEADS)
        sl = slice(h * HD, (h + 1) * HD)
        c_ref[0, j] = ct_sc[d, sl, sl]
        n_ref[0, j:j + 1, :] = nb_sc[d, h:h + 1, sl]
        m_ref[0, j:j + 1, :] = jnp.broadcast_to(m_fin[d][h], (1, LANE))

    def out_body(c, carry):
        rows = pl.ds(pl.multiple_of(c * CHUNK, CHUNK), CHUNK)
        normed = []
        for h in range(HEADS):
            tot = ht_sc[0, c, h * HD:(h + 1) * HD, :] + ht_sc[1, c, h * HD:(h + 1) * HD, :]
            normed.append(tot * lax.rsqrt(jnp.mean(tot * tot, axis=0, keepdims=True) + EPS))
        hn = jnp.concatenate(normed, axis=0).T
        y_ref[0, rows, :] = (hn * ng_ref[...] * _sigmoid(zb_ref[0, rows, 3 * MIX:4 * MIX])).astype(y_ref.dtype)
        return carry

    lax.fori_loop(0, nc, out_body, 0)


def _mlstm_mixer(zb, zg, gate_b, norm_g, c0, n0, m0):
    bsz, t_len, _ = zb.shape
    nd = 2 * HEADS
    return pl.pallas_call(
        _mlstm_kernel,
        out_shape=[jax.ShapeDtypeStruct((bsz, t_len, MIX), BF16),
                   jax.ShapeDtypeStruct((bsz, nd, HD, HD), F32),
                   jax.ShapeDtypeStruct((bsz, nd, HD), F32),
                   jax.ShapeDtypeStruct((bsz, nd, LANE), F32)],
        grid=(bsz,),
        in_specs=[pl.BlockSpec((1, t_len, 4 * MIX), lambda i: (i, 0, 0)),
                  pl.BlockSpec((1, t_len, 2 * LANE), lambda i: (i, 0, 0)),
                  pl.BlockSpec((1, 2 * LANE), lambda i: (0, 0)),
                  pl.BlockSpec((1, MIX), lambda i: (0, 0)),
                  pl.BlockSpec((1, nd, HD, HD), lambda i: (i, 0, 0, 0)),
                  pl.BlockSpec((1, nd, HD), lambda i: (i, 0, 0)),
                  pl.BlockSpec((1, nd, LANE), lambda i: (i, 0, 0))],
        out_specs=[pl.BlockSpec((1, t_len, MIX), lambda i: (i, 0, 0)),
                   pl.BlockSpec((1, nd, HD, HD), lambda i: (i, 0, 0, 0)),
                   pl.BlockSpec((1, nd, HD), lambda i: (i, 0, 0)),
                   pl.BlockSpec((1, nd, LANE), lambda i: (i, 0, 0))],
        scratch_shapes=[pltpu.VMEM((t_len // CHUNK, MIX, CHUNK), BF16),
                        pltpu.VMEM((t_len // CHUNK, MIX, HEADS * CHUNK), BF16),
                        pltpu.VMEM((t_len // CHUNK, MIX, CHUNK), F32),
                        pltpu.VMEM((t_len // CHUNK, MIX, HEADS * CHUNK), BF16),
                        pltpu.VMEM((2, t_len // CHUNK, MIX, CHUNK), F32),
                        pltpu.VMEM((2, MIX, MIX), F32),
                        pltpu.VMEM((2, 8, MIX), F32)],
        compiler_params=_cp("parallel"),
        name="mlstm_mixer",
    )(zb, zg, gate_b, norm_g, c0, n0, m0)


_ATT_TQ = 256


def _mla_kernel(*refs, past):
    if past:
        (zc_ref, cos_ref, sin_ref, gq_ref, gkv_ref, wqa_ref, wqb_ref, wk_ref, wv_ref,
         pckv_ref, pkr_ref, y_ref, ckv_ref, q_sc, k_sc, v_sc) = refs
    else:
        (zc_ref, cos_ref, sin_ref, gq_ref, gkv_ref, wqa_ref, wqb_ref, wk_ref, wv_ref,
         y_ref, ckv_ref, q_sc, k_sc, v_sc) = refs
    t_len = y_ref.shape[1]
    cosf = cos_ref[...]
    sinf = sin_ref[...]
    zc = lambda a, b: zc_ref[0, :, a:b].astype(F32)
    cq = _rms(zc(0, Q_RANK)) * gq_ref[...]
    ckv = _rms(zc(Q_RANK, Q_RANK + KV_RANK)) * gkv_ref[...]
    ckv_ref[0] = ckv
    kr = (zc(Q_RANK + KV_RANK, Q_RANK + KV_RANK + LANE) * cosf
          + zc(Q_RANK + KV_RANK + LANE, Q_RANK + KV_RANK + 2 * LANE) * sinf)
    cqb = cq.astype(BF16)
    ckvb = ckv.astype(BF16)
    pair = lambda a: jnp.concatenate([a, a], axis=-1)
    cos2, sin2, kr2 = pair(cosf), pair(sinf), pair(kr)
    for hp in range(HEADS // 2):
        sl = slice(hp * 2 * LANE, (hp + 1) * 2 * LANE)
        qa = jnp.dot(cqb, wqa_ref[:, sl], preferred_element_type=F32)
        qb = jnp.dot(cqb, wqb_ref[:, sl], preferred_element_type=F32)
        q_sc[:, sl] = (qa * cos2 + qb * sin2).astype(BF16)
        kn = jnp.dot(ckvb, wk_ref[:, sl], preferred_element_type=F32)
        k_sc[past:past + t_len, sl] = (kn + kr2).astype(BF16)
        v_sc[past:past + t_len, sl] = jnp.dot(ckvb, wv_ref[:, sl], preferred_element_type=F32).astype(BF16)
    if past:
        pckv = pckv_ref[0].astype(BF16)
        pkr2 = pair(pkr_ref[0])
        for hp in range(HEADS // 2):
            sl = slice(hp * 2 * LANE, (hp + 1) * 2 * LANE)
            k_sc[0:past, sl] = (jnp.dot(pckv, wk_ref[:, sl], preferred_element_type=F32) + pkr2).astype(BF16)
            v_sc[0:past, sl] = jnp.dot(pckv, wv_ref[:, sl], preferred_element_type=F32).astype(BF16)

    def q_block(i, carry):
        rows = pl.ds(pl.multiple_of(i * _ATT_TQ, _ATT_TQ), _ATT_TQ)
        outs = []
        for h in range(HEADS):
            sl = slice(h * LANE, (h + 1) * LANE)
            s = lax.dot_general(q_sc[rows, sl], k_sc[:, sl], (((1,), (1,)), ((), ())),
                                preferred_element_type=F32)
            p = jnp.exp(s - jnp.max(s, axis=-1, keepdims=True))
            o = jnp.dot(p.astype(BF16), v_sc[:, sl], preferred_element_type=F32)
            outs.append(o / jnp.sum(p, axis=-1, keepdims=True))
        y_ref[0, rows, 0:LANE] = (outs[0] + outs[1]).astype(y_ref.dtype)
        y_ref[0, rows, LANE:2 * LANE] = (outs[2] + outs[3]).astype(y_ref.dtype)
        return carry

    lax.fori_loop(0, t_len // _ATT_TQ, q_block, 0)


def _mla_mixer(zc, cosf, sinf, gq, gkv, wqa, wqb, wk, wv, past_ckv=None, past_kr=None):
    bsz, t_len, zw = zc.shape
    past = 0 if past_ckv is None else past_ckv.shape[1]
    full = lambda a: pl.BlockSpec(a.shape, lambda i: (0,) * a.ndim)
    args = [zc, cosf, sinf, gq, gkv, wqa, wqb, wk, wv]
    in_specs = [pl.BlockSpec((1, t_len, zw), lambda i: (i, 0, 0))] + [full(a) for a in args[1:]]
    if past:
        args += [past_ckv, past_kr]
        in_specs += [pl.BlockSpec((1, past, KV_RANK), lambda i: (i, 0, 0)),
                     pl.BlockSpec((1, past, LANE), lambda i: (i, 0, 0))]
    wide = HEADS * LANE
    return pl.pallas_call(
        functools.partial(_mla_kernel, past=past),
        out_shape=[jax.ShapeDtypeStruct((bsz, t_len, MIX), BF16),
                   jax.ShapeDtypeStruct((bsz, t_len, KV_RANK), F32)],
        grid=(bsz,),
        in_specs=in_specs,
        out_specs=[pl.BlockSpec((1, t_len, MIX), lambda i: (i, 0, 0)),
                   pl.BlockSpec((1, t_len, KV_RANK), lambda i: (i, 0, 0))],
        scratch_shapes=[pltpu.VMEM((t_len, wide), BF16),
                        pltpu.VMEM((past + t_len, wide), BF16),
                        pltpu.VMEM((past + t_len, wide), BF16)],
        compiler_params=_cp("parallel"),
        name="mla_mixer",
    )(*args)


_S5_SEQ = 8
_S5_TT = 128
_S5_PARTS = 4


def _s5_kernel(zd_ref, bd_ref, cd_ref, are_ref, aim_ref, s0re_ref, s0im_ref,
               y_ref, fre_ref, fim_ref, bre_sc, bim_sc, sre_sc, sim_sc):
    direction = pl.program_id(0)
    k = pl.program_id(2)
    tt = _S5_TT
    rows = _S5_SEQ * tt

    @pl.when(k == 0)
    def _():
        sre_sc[...] = s0re_ref[0]
        sim_sc[...] = s0im_ref[0]

    part_steps = tt // _S5_PARTS
    part_rows = part_steps * _S5_SEQ

    def run(backward):
        parts = list(range(_S5_PARTS))[::-1] if backward else list(range(_S5_PARTS))
        a_re = jnp.broadcast_to(are_ref[0], (_S5_SEQ, S5_STATES))
        a_im = jnp.broadcast_to(aim_ref[0], (_S5_SEQ, S5_STATES))
        for p in parts:
            r = slice(p * part_rows, (p + 1) * part_rows)
            u = zd_ref[p * part_steps:(p + 1) * part_steps].reshape(part_rows, MIX).astype(BF16)
            bre_sc[r, :] = jnp.dot(u, bd_ref[0, :, 0:S5_STATES], preferred_element_type=F32)
            bim_sc[r, :] = jnp.dot(u, bd_ref[0, :, S5_STATES:2 * S5_STATES], preferred_element_type=F32)
        s_re, s_im = sre_sc[...], sim_sc[...]
        for p in parts:
            steps = range(p * part_steps, (p + 1) * part_steps)
            for t in (reversed(steps) if backward else steps):
                sel = slice(t * _S5_SEQ, (t + 1) * _S5_SEQ)
                s_re, s_im = (a_re * s_re - a_im * s_im + bre_sc[sel, :],
                              a_re * s_im + a_im * s_re + bim_sc[sel, :])
                bre_sc[sel, :] = s_re
                bim_sc[sel, :] = s_im
            r = slice(p * part_rows, (p + 1) * part_rows)
            y = (jnp.dot(bre_sc[r, :].astype(BF16), cd_ref[0, 0:S5_STATES, :], preferred_element_type=F32)
                 + jnp.dot(bim_sc[r, :].astype(BF16), cd_ref[0, S5_STATES:2 * S5_STATES, :],
                           preferred_element_type=F32))
            y_ref[0, p * part_steps:(p + 1) * part_steps] = y.reshape(part_steps, _S5_SEQ, MIX)
        sre_sc[...] = s_re
        sim_sc[...] = s_im

    pl.when(direction == 0)(lambda: run(False))
    pl.when(direction == 1)(lambda: run(True))

    @pl.when(k == pl.num_programs(2) - 1)
    def _():
        fre_ref[0] = sre_sc[...]
        fim_ref[0] = sim_sc[...]


def _s5_scan(zd, bd, cd, ab_re, ab_im, s0_re, s0_im):
    t_len, bsz, _ = zd.shape
    nt = t_len // _S5_TT
    tile = lambda d, k: k + d * (nt - 1 - 2 * k)
    st = pl.BlockSpec((1, _S5_SEQ, S5_STATES), lambda d, g, k: (d, g, 0))
    return pl.pallas_call(
        _s5_kernel,
        out_shape=[jax.ShapeDtypeStruct((2, t_len, bsz, MIX), F32),
                   jax.ShapeDtypeStruct((2, bsz, S5_STATES), F32),
                   jax.ShapeDtypeStruct((2, bsz, S5_STATES), F32)],
        grid=(2, bsz // _S5_SEQ, nt),
        in_specs=[pl.BlockSpec((_S5_TT, _S5_SEQ, MIX), lambda d, g, k: (tile(d, k), g, 0)),
                  pl.BlockSpec((1, MIX, 2 * S5_STATES), lambda d, g, k: (d, 0, 0)),
                  pl.BlockSpec((1, 2 * S5_STATES, MIX), lambda d, g, k: (d, 0, 0)),
                  pl.BlockSpec((1, 1, S5_STATES), lambda d, g, k: (d, 0, 0)),
                  pl.BlockSpec((1, 1, S5_STATES), lambda d, g, k: (d, 0, 0)),
                  st, st],
        out_specs=[pl.BlockSpec((1, _S5_TT, _S5_SEQ, MIX), lambda d, g, k: (d, tile(d, k), g, 0)),
                   st, st],
        scratch_shapes=[pltpu.VMEM((_S5_SEQ * _S5_TT, S5_STATES), F32),
                        pltpu.VMEM((_S5_SEQ * _S5_TT, S5_STATES), F32),
                        pltpu.VMEM((_S5_SEQ, S5_STATES), F32),
                        pltpu.VMEM((_S5_SEQ, S5_STATES), F32)],
        compiler_params=_cp("parallel", "parallel", "arbitrary"),
        name="s5_scan",
    )(zd, bd, cd, ab_re, ab_im, s0_re, s0_im)


def _gelu_tanh(x):
    return 0.5 * x * (1.0 + jnp.tanh(math.sqrt(2.0 / math.pi) * (x + 0.044715 * x * x * x)))


def _route(logits):
    lane = lax.broadcasted_iota(jnp.int32, logits.shape, 1).astype(F32)
    big = float(LANE)
    neg = -jnp.inf
    g_mask = (lane >= N_EXPERT) & (lane < N_EXPERT + N_GROUP)
    gl = jnp.where(g_mask, logits, neg)
    g_max = jnp.max(gl, axis=-1, keepdims=True)
    g_idx = jnp.min(jnp.where(gl == g_max, lane, big), axis=-1, keepdims=True) - N_EXPERT
    g_sel = 1.0 / jnp.sum(jnp.where(g_mask, jnp.exp(logits - g_max), 0.0), axis=-1, keepdims=True)
    lo = g_idx * PER_GROUP
    el = jnp.where((lane >= lo) & (lane < lo + PER_GROUP), logits, neg)
    v1 = jnp.max(el, axis=-1, keepdims=True)
    i1 = jnp.min(jnp.where(el == v1, lane, big), axis=-1, keepdims=True)
    el2 = jnp.where(lane == i1, neg, el)
    v2 = jnp.max(el2, axis=-1, keepdims=True)
    i2 = jnp.min(jnp.where(el2 == v2, lane, big), axis=-1, keepdims=True)
    e2 = jnp.exp(v2 - v1)
    w1 = g_sel / (1.0 + e2)
    return jnp.where(lane == i1, w1, jnp.where(lane == i2, w1 * e2, 0.0))


def _post_kernel(x_ref, ya_ref, yb_ref, yc_ref, ys_ref, zd_ref, mod_ref, wo_ref, d_ref, wglu_ref, bglu_ref,
                 g2_ref, wr_ref, br_ref, x1_ref, h2_ref, gate_ref):
    m = mod_ref[0]
    zd = zd_ref[...]
    ys = ys_ref[0] + ys_ref[1] + d_ref[...] * zd
    gl = _bdot(_gelu_tanh(ys), wglu_ref[...]) + bglu_ref[...]
    yd = gl[:, 0:MIX] * _sigmoid(gl[:, MIX:2 * MIX])
    mix = (_bdot(ya_ref[...], wo_ref[0:MIX, :]) + _bdot(yb_ref[...], wo_ref[MIX:2 * MIX, :])
           + _bdot(yc_ref[...], wo_ref[2 * MIX:3 * MIX, :]) + _bdot(yd, wo_ref[3 * MIX:4 * MIX, :]))
    x1 = x_ref[...] + m[2:3] * mix
    x1_ref[...] = x1
    h2 = _rms(x1) * g2_ref[...] * (1.0 + m[4:5]) + m[3:4]
    h2b = h2.astype(BF16)
    h2_ref[...] = h2b
    gate_ref[...] = _route(jnp.dot(h2b, wr_ref[...], preferred_element_type=F32) + br_ref[...])


def _post_mixer(x, ya, yb, yc, ys, zd, mod, row_fn, w_out, s5_d, w_glu, b_glu, norm2_g, w_router, b_router, tm):
    n = x.shape[0]
    tok = lambda w: pl.BlockSpec((tm, w), lambda i: (i, 0))
    full = lambda a: pl.BlockSpec(a.shape, lambda i: (0,) * a.ndim)
    return pl.pallas_call(
        _post_kernel,
        out_shape=[jax.ShapeDtypeStruct((n, D_MODEL), F32),
                   jax.ShapeDtypeStruct((n, D_MODEL), BF16),
                   jax.ShapeDtypeStruct((n, LANE), F32)],
        grid=(n // tm,),
        in_specs=[tok(D_MODEL), tok(MIX), tok(MIX), tok(MIX),
                  pl.BlockSpec((2, tm, MIX), lambda i: (0, i, 0)), tok(MIX),
                  pl.BlockSpec((1, 6, D_MODEL), lambda i: (row_fn(i, tm), 0, 0)),
                  full(w_out), full(s5_d), full(w_glu), full(b_glu), full(norm2_g),
                  full(w_router), full(b_router)],
        out_specs=[tok(D_MODEL), tok(D_MODEL), tok(LANE)],
        compiler_params=_cp("parallel"),
        name="post_mixer",
    )(x, ya, yb, yc, ys, zd, mod, w_out, s5_d, w_glu, b_glu, norm2_g, w_router, b_router)


_MOE_EPS = 4
_MOE_TM = 1024


_MOE_ALIGN = 16
_MOE_ROWS = 2 * _MOE_TM + N_EXPERT * _MOE_ALIGN
_MOE_CH = 128
_MOE_GB = 512
_MOE_CB = 256


def _moe_kernel(h_ref, gate_ref, x_ref, mod_ref, w1_ref, w3_ref, w2_ref, o_ref,
                xs_sc, ys_sc, col_sc, row_sc, start_sm, rows_sm, done_sm):
    j = pl.program_id(1)
    tm = h_ref.shape[0]

    @pl.when(j == 0)
    def _dispatch():
        gate = gate_ref[...]
        cnt_row = jnp.sum(jnp.where(gate != 0.0, 1.0, 0.0), axis=0, keepdims=True)
        units_row = jnp.floor((cnt_row + (_MOE_ALIGN - 1)) * (1.0 / _MOE_ALIGN))
        li = lax.broadcasted_iota(jnp.int32, (LANE, LANE), 0)
        lj = lax.broadcasted_iota(jnp.int32, (LANE, LANE), 1)
        before = jnp.where(li < lj, 1.0, 0.0).astype(BF16)
        start_row = jnp.dot(jnp.broadcast_to(units_row, (8, LANE)).astype(BF16), before,
                            preferred_element_type=F32) * _MOE_ALIGN
        start_i = start_row.astype(jnp.int32)
        cnt_i = cnt_row.astype(jnp.int32)
        for e in range(N_EXPERT):
            start_sm[e] = start_i[0, e]
            rows_sm[e] = cnt_i[0, e]
        gt = gate.T[0:N_EXPERT, :]
        chosen = gt != 0.0
        ones = jnp.where(chosen, 1.0, 0.0)
        cnt = jnp.sum(ones, axis=-1, keepdims=True)
        units = jnp.floor((cnt + (_MOE_ALIGN - 1)) * (1.0 / _MOE_ALIGN))
        ei = lax.broadcasted_iota(jnp.int32, (N_EXPERT, N_EXPERT), 0)
        ej = lax.broadcasted_iota(jnp.int32, (N_EXPERT, N_EXPERT), 1)
        start = jnp.dot(jnp.where(ej < ei, 1.0, 0.0).astype(BF16),
                        jnp.broadcast_to(units, (N_EXPERT, LANE)).astype(BF16),
                        preferred_element_type=F32)[:, 0:1] * _MOE_ALIGN
        ti = lax.broadcasted_iota(jnp.int32, (tm, tm), 0)
        tj = lax.broadcasted_iota(jnp.int32, (tm, tm), 1)
        earlier = jnp.where(ti < tj, 1.0, 0.0).astype(BF16)
        pos = start + jnp.dot(ones.astype(BF16), earlier, preferred_element_type=F32)
        p_a = jnp.min(jnp.where(chosen, pos, float(_MOE_ROWS)), axis=0, keepdims=True)
        p_b = jnp.max(jnp.where(chosen, pos, -1.0), axis=0, keepdims=True)
        g_a = jnp.sum(jnp.where(chosen & (pos == p_a), gt, 0.0), axis=0, keepdims=True)
        g_b = jnp.sum(jnp.where(chosen & (pos == p_b) & (p_b != p_a), gt, 0.0), axis=0, keepdims=True)
        row_sc[...] = jnp.concatenate([p_a, p_b, jnp.zeros((6, tm), F32)], axis=0)
        packed = jnp.concatenate([p_a, p_b, g_a, g_b, jnp.zeros((LANE - 4, tm), F32)], axis=0)
        col_sc[...] = packed.T
        xs_sc[_MOE_ROWS:_MOE_ROWS + _MOE_CH, :] = jnp.zeros((_MOE_CH, D_MODEL), BF16)
        ys_sc[...] = jnp.zeros_like(ys_sc)
        o_ref[...] = x_ref[...]
        done_sm[0] = 0
        done_sm[1] = 0

    n_blocks = _MOE_ROWS // _MOE_GB
    e_last = j * _MOE_EPS + (_MOE_EPS - 1)
    seg_end = start_sm[e_last] + rows_sm[e_last]
    last_step = j == pl.num_programs(1) - 1

    def gather_block(b, carry):
        r0 = pl.multiple_of(b * _MOE_GB, _MOE_GB)
        ri = (lax.broadcasted_iota(jnp.int32, (_MOE_GB, tm), 0) + r0).astype(F32)
        sel = jnp.where((ri == row_sc[0:1, :]) | (ri == row_sc[1:2, :]), 1.0, 0.0).astype(BF16)
        xs_sc[pl.ds(r0, _MOE_GB), :] = jnp.dot(sel, h_ref[...], preferred_element_type=F32).astype(BF16)
        return carry

    gathered = jnp.minimum((seg_end + _MOE_CH + _MOE_GB - 1) // _MOE_GB, n_blocks)
    lax.fori_loop(done_sm[0], gathered, gather_block, 0)
    done_sm[0] = gathered

    def expert_rows(el, i):
        rows = pl.ds(pl.multiple_of(start_sm[j * _MOE_EPS + el] + i * _MOE_CH, _MOE_ALIGN), _MOE_CH)
        xs = xs_sc[rows, :]
        h1 = jnp.dot(xs, w1_ref[el], preferred_element_type=F32)
        h3 = jnp.dot(xs, w3_ref[el], preferred_element_type=F32)
        return rows, jnp.dot((h1 * _sigmoid(h1) * h3).astype(BF16), w2_ref[el], preferred_element_type=F32)

    first = [expert_rows(el, 0) for el in range(_MOE_EPS)]
    for rows, y in first:
        ys_sc[rows, :] = y.astype(BF16)
    for el in range(_MOE_EPS):
        n_rows = rows_sm[j * _MOE_EPS + el]

        def chunk(i, carry, el=el, n_rows=n_rows):
            rows, y = expert_rows(el, i)
            live = lax.broadcasted_iota(jnp.int32, (_MOE_CH, 1), 0) + i * _MOE_CH < n_rows
            ys_sc[rows, :] = jnp.where(live, y, ys_sc[rows, :].astype(F32)).astype(BF16)
            return carry

        lax.fori_loop(1, (n_rows + _MOE_CH - 1) // _MOE_CH, chunk, 0)

    g2 = mod_ref[0][5:6]

    def combine_block(b, carry):
        r0 = pl.multiple_of(b * _MOE_GB, _MOE_GB)
        y = ys_sc[pl.ds(r0, _MOE_GB), :]
        for t0 in range(0, tm, _MOE_CB):
            c = col_sc[t0:t0 + _MOE_CB, :]
            ri = (lax.broadcasted_iota(jnp.int32, (_MOE_CB, _MOE_GB), 1) + r0).astype(F32)
            w = (jnp.where(ri == c[:, 0:1], c[:, 2:3], 0.0)
                 + jnp.where(ri == c[:, 1:2], c[:, 3:4], 0.0)).astype(BF16)
            o_ref[t0:t0 + _MOE_CB, :] += g2 * jnp.dot(w, y, preferred_element_type=F32)
        return carry

    final_rows = (seg_end + _MOE_ALIGN - 1) // _MOE_ALIGN * _MOE_ALIGN
    combined = jnp.where(last_step, n_blocks, final_rows // _MOE_GB)
    lax.fori_loop(done_sm[1], combined, combine_block, 0)
    done_sm[1] = combined


def _moe(h2, gate, x1, mod, row_fn, w1, w3, w2, layer, tm):
    n = h2.shape[0]
    first_blk = layer * (N_EXPERT // _MOE_EPS)
    return pl.pallas_call(
        _moe_kernel,
        out_shape=jax.ShapeDtypeStruct((n, D_MODEL), F32),
        grid=(n // tm, N_EXPERT // _MOE_EPS),
        in_specs=[pl.BlockSpec((tm, D_MODEL), lambda i, j: (i, 0)),
                  pl.BlockSpec((tm, LANE), lambda i, j: (i, 0)),
                  pl.BlockSpec((tm, D_MODEL), lambda i, j: (i, 0)),
                  pl.BlockSpec((1, 6, D_MODEL), lambda i, j: (row_fn(i, tm), 0, 0)),
                  pl.BlockSpec((_MOE_EPS, D_MODEL, MOE_FF), lambda i, j: (first_blk + j, 0, 0)),
                  pl.BlockSpec((_MOE_EPS, D_MODEL, MOE_FF), lambda i, j: (first_blk + j, 0, 0)),
                  pl.BlockSpec((_MOE_EPS, MOE_FF, D_MODEL), lambda i, j: (first_blk + j, 0, 0))],
        out_specs=pl.BlockSpec((tm, D_MODEL), lambda i, j: (i, 0)),
        scratch_shapes=[pltpu.VMEM((_MOE_ROWS + _MOE_CH, D_MODEL), BF16),
                        pltpu.VMEM((_MOE_ROWS + _MOE_CH, D_MODEL), BF16),
                        pltpu.VMEM((tm, LANE), F32),
                        pltpu.VMEM((8, tm), F32),
                        pltpu.SMEM((N_EXPERT,), jnp.int32),
                        pltpu.SMEM((N_EXPERT,), jnp.int32),
                        pltpu.SMEM((2,), jnp.int32)],
        compiler_params=_cp("parallel", "arbitrary"),
        name="moe",
    )(h2, gate, x1, mod, w1, w3, w2)


def _final_kernel(x_ref, g_ref, o_ref):
    o_ref[...] = _rms(x_ref[...]) * g_ref[...]


def _final_norm(x, g, tm):
    n = x.shape[0]
    return pl.pallas_call(
        _final_kernel,
        out_shape=jax.ShapeDtypeStruct((n, D_MODEL), F32),
        grid=(n // tm,),
        in_specs=[pl.BlockSpec((tm, D_MODEL), lambda i: (i, 0)),
                  pl.BlockSpec((1, D_MODEL), lambda i: (0, 0))],
        out_specs=pl.BlockSpec((tm, D_MODEL), lambda i: (i, 0)),
        compiler_params=_cp("parallel"),
        name="final_norm",
    )(x, g)


def _rot_cols(w):
    q = ROPE // 4
    return jnp.concatenate([-w[..., q:2 * q], w[..., 0:q], -w[..., 3 * q:4 * q], w[..., 2 * q:3 * q]], axis=-1)


def _rope_slot(w):
    pad = [(0, 0)] * (w.ndim - 1) + [(HD, LANE - HD - ROPE)]
    return jnp.pad(w, pad)


def _gate_lanes(g):
    n = g.shape[-1] // 2
    pad = [(0, 0)] * (g.ndim - 1) + [(0, LANE - n)]
    return jnp.concatenate([jnp.pad(g[..., :n], pad), jnp.pad(g[..., n:], pad)], axis=-1)


def _prep_w_in(w_in, gate_cols):
    a = w_in[..., 0:512]
    qkvo = w_in[..., 512:1536]
    g = _gate_lanes(w_in[..., 1536:1536 + gate_cols])
    c0 = 1536 + gate_cols
    cq = w_in[..., c0:c0 + Q_RANK]
    ckv = w_in[..., c0 + Q_RANK:c0 + Q_RANK + KV_RANK]
    kr = w_in[..., c0 + Q_RANK + KV_RANK:c0 + Q_RANK + KV_RANK + ROPE]
    d = w_in[..., c0 + Q_RANK + KV_RANK + ROPE:]
    return jnp.concatenate([a, qkvo, g, cq, ckv, _rope_slot(kr), _rope_slot(_rot_cols(kr)), d],
                           axis=-1).astype(BF16)


def _prep_mla(w_uq, w_ukv):
    n_layer = w_uq.shape[0]
    scale = (HD + ROPE) ** -0.5
    wq = w_uq.reshape(n_layer, Q_RANK, HEADS, HD + ROPE) * scale
    nope, rope = wq[..., :HD], wq[..., HD:]
    zeros_r = jnp.zeros_like(rope)
    wqa = jnp.concatenate([nope, rope, zeros_r], axis=-1).reshape(n_layer, Q_RANK, HEADS * LANE)
    wqb = jnp.concatenate([jnp.zeros_like(nope), _rot_cols(rope), zeros_r], axis=-1)
    wqb = wqb.reshape(n_layer, Q_RANK, HEADS * LANE)
    wkv = w_ukv.reshape(n_layer, KV_RANK, HEADS, 2 * HD)
    k_nope, val = wkv[..., :HD], wkv[..., HD:]
    zeros_h = jnp.zeros_like(k_nope)
    wk = jnp.concatenate([k_nope, zeros_h], axis=-1).reshape(n_layer, KV_RANK, HEADS * LANE)
    even = jnp.concatenate([val, zeros_h], axis=-1)
    odd = jnp.concatenate([zeros_h, val], axis=-1)
    is_odd = (jnp.arange(HEADS) % 2 == 1)[None, None, :, None]
    wv = jnp.where(is_odd, odd, even).reshape(n_layer, KV_RANK, HEADS * LANE)
    return wqa.astype(BF16), wqb.astype(BF16), wk.astype(BF16), wv.astype(BF16)


def _rope_tables(t_len, rotate):
    ones = jnp.ones((t_len, HD), F32)
    zeros = jnp.zeros((t_len, HD), F32)
    tail = jnp.zeros((t_len, LANE - HD - ROPE), F32)
    if not rotate:
        return (jnp.concatenate([ones, jnp.ones((t_len, ROPE), F32), tail], axis=-1),
                jnp.zeros((t_len, LANE), F32))
    rows = t_len // GRID_W
    row = jnp.repeat(jnp.arange(rows, dtype=F32), GRID_W)
    col = jnp.tile(jnp.arange(GRID_W, dtype=F32), rows)
    nf = ROPE // 4
    inv = ROPE_BASE ** (-jnp.arange(nf, dtype=F32) / nf)
    ar = row[:, None] * inv
    ac = col[:, None] * inv
    cos = jnp.concatenate([jnp.cos(ar), jnp.cos(ar), jnp.cos(ac), jnp.cos(ac)], axis=-1)
    sin = jnp.concatenate([jnp.sin(ar), jnp.sin(ar), jnp.sin(ac), jnp.sin(ac)], axis=-1)
    return (jnp.concatenate([ones, cos, tail], axis=-1), jnp.concatenate([zeros, sin, tail], axis=-1))


def _prep_s5(bb_re, bb_im, c_re, c_im):
    eye = jnp.eye(S5_G, dtype=F32)
    to_b = lambda bb: jnp.einsum("ldgnc,gh->ldgchn", bb, eye).reshape(bb.shape[0], 2, MIX, S5_STATES)
    to_c = lambda cc: jnp.einsum("ldgcn,gh->ldgnhc", cc, eye).reshape(cc.shape[0], 2, S5_STATES, MIX)
    bd = jnp.concatenate([to_b(bb_re), to_b(bb_im)], axis=-1)
    cd = jnp.concatenate([to_c(c_re.astype(F32)), -to_c(c_im.astype(F32))], axis=-2)
    return bd.astype(BF16), cd.astype(BF16)


def _layer(x, bsz, t_len, mod, row_fn, p, ctx, tm):
    za, zb, zg, zc, zd = _pre_mixer(x, mod, row_fn, p["norm1_g"], p["w_in"], tm)
    seq = lambda a: a.reshape(bsz, t_len, a.shape[-1])
    ya = _conv_module(seq(za), p["conv_w"], p["conv_b"], p["conv_ln_g"], p["conv_ln_b"])
    yb, c_fin, n_fin, m_fin = _mlstm_mixer(seq(zb), seq(zg), p["gate_b"], p["mlstm_norm_g"],
                                           ctx["mlstm_c"], ctx["mlstm_n"], ctx["mlstm_m"])
    yc, ckv = _mla_mixer(seq(zc), ctx["cos"], ctx["sin"], p["mla_q_norm_g"], p["mla_kv_norm_g"],
                         p["wqa"], p["wqb"], p["wk"], p["wv"], ctx.get("past_ckv"), ctx.get("past_kr"))
    ys, s_re, s_im = _s5_scan(seq(zd).transpose(1, 0, 2), p["s5_bd"], p["s5_cd"], p["s5_ab_re"], p["s5_ab_im"],
                              ctx["s5_re"], ctx["s5_im"])
    ys = ys.transpose(0, 2, 1, 3)
    flat = lambda a: a.reshape(bsz * t_len, a.shape[-1])
    x1, h2, gate = _post_mixer(x, flat(ya), flat(yb), flat(yc), ys.reshape(2, bsz * t_len, MIX), zd, mod, row_fn,
                               p["w_out"], p["s5_d"], p["s5_w_glu"], p["s5_b_glu"], p["norm2_g"],
                               p["w_router"], p["b_router"], tm)
    x2 = _moe(h2, gate, x1, mod, row_fn, p["moe_w1"], p["moe_w3"], p["moe_w2"], p["layer"], _MOE_TM)
    krope = seq(zc)[:, :, Q_RANK + KV_RANK + HD:Q_RANK + KV_RANK + HD + ROPE].astype(F32)
    return x2, (ckv, krope, c_fin, n_fin, m_fin, s_re, s_im)


def kernel(x_prompt, x_sample, cache_mla_ckv, cache_mla_krope, state_mlstm_C, state_mlstm_n, state_mlstm_m, state_s5, c, c_ctx, norm1_g, norm2_g, final_g, w_mod, b_mod, w_in, w_out, conv_w, conv_b, conv_ln_g, conv_ln_b, mlstm_gate_b, mlstm_norm_g, mla_q_norm_g, mla_w_uq, mla_kv_norm_g, mla_w_ukv, s5_a_re, s5_a_im, s5_log_dt, s5_b_re, s5_b_im, s5_c_re, s5_c_im, s5_d, s5_w_glu, s5_b_glu, moe_w_group, moe_b_group, moe_w_expert, moe_b_expert, moe_w1, moe_w3, moe_w2):
    n_layer = w_in.shape[0]
    b_ctx, t_ctx, d = x_prompt.shape
    b_lat, t_lat, _ = x_sample.shape
    nd = 2 * HEADS
    tm = 512
    assert b_lat + 1 <= 16 and t_lat % _MOE_TM == 0 and (b_ctx * t_ctx) % _MOE_TM == 0

    c_all = jnp.zeros((16, d), F32).at[0].set(c_ctx).at[1:1 + b_lat].set(c)
    mod_all = _modulation(c_all, w_mod, b_mod).reshape(n_layer, 16, 6, d)
    gate_cols = 4 * HEADS
    w_in_ext = _prep_w_in(w_in, gate_cols)
    wqa, wqb, wk, wv = _prep_mla(mla_w_uq, mla_w_ukv)
    ab_re, ab_im, bb_re, bb_im = _s5_discretise(s5_a_re, s5_a_im, s5_log_dt, s5_b_re, s5_b_im)
    s5_bd, s5_cd = _prep_s5(bb_re, bb_im, s5_c_re, s5_c_im)
    w_router = jnp.pad(jnp.concatenate([moe_w_expert, moe_w_group], axis=-1),
                       [(0, 0), (0, 0), (0, LANE - N_EXPERT - N_GROUP)]).astype(BF16)
    b_router = jnp.pad(jnp.concatenate([moe_b_expert, moe_b_group], axis=-1),
                       [(0, 0), (0, LANE - N_EXPERT - N_GROUP)])
    gate_b = _gate_lanes(mlstm_gate_b.reshape(n_layer, gate_cols))
    conv_w_p = jnp.pad(conv_w, [(0, 0), (0, 32 - CONV_WIDTH), (0, 0)])
    w_out_b = w_out.astype(BF16)
    w_glu_b = s5_w_glu.astype(BF16)
    flat_experts = lambda w: w.astype(BF16).reshape((n_layer * N_EXPERT,) + w.shape[2:])
    w1_b, w3_b, w2_b = flat_experts(moe_w1), flat_experts(moe_w3), flat_experts(moe_w2)
    row = lambda a, l: a[l][None, :]

    cos_ctx, sin_ctx = _rope_tables(t_ctx, rotate=False)
    cos_lat, sin_lat = _rope_tables(t_lat, rotate=True)
    zero_state = dict(
        mlstm_c=jnp.zeros((b_ctx, nd, HD, HD), F32), mlstm_n=jnp.zeros((b_ctx, nd, HD), F32),
        mlstm_m=jnp.zeros((b_ctx, nd, LANE), F32),
        s5_re=jnp.zeros((2, b_ctx, S5_STATES), F32), s5_im=jnp.zeros((2, b_ctx, S5_STATES), F32),
        cos=cos_ctx, sin=sin_ctx)

    row_ctx = lambda i, tile: 0
    row_lat = lambda i, tile: 1 + (i * tile) // t_lat

    x_ctx = x_prompt.reshape(b_ctx * t_ctx, d)
    x_lat = x_sample.reshape(b_lat * t_lat, d)
    outs = []
    for l in range(n_layer):
        p = dict(norm1_g=row(norm1_g, l), norm2_g=row(norm2_g, l), w_in=w_in_ext[l], w_out=w_out_b[l],
                 conv_w=conv_w_p[l], conv_b=row(conv_b, l), conv_ln_g=row(conv_ln_g, l),
                 conv_ln_b=row(conv_ln_b, l), gate_b=row(gate_b, l), mlstm_norm_g=row(mlstm_norm_g, l),
                 mla_q_norm_g=row(mla_q_norm_g, l), mla_kv_norm_g=row(mla_kv_norm_g, l),
                 wqa=wqa[l], wqb=wqb[l], wk=wk[l], wv=wv[l],
                 s5_bd=s5_bd[l], s5_cd=s5_cd[l], s5_ab_re=ab_re[l][:, None, :], s5_ab_im=ab_im[l][:, None, :],
                 s5_d=row(s5_d, l), s5_w_glu=w_glu_b[l], s5_b_glu=row(s5_b_glu, l),
                 w_router=w_router[l], b_router=row(b_router, l),
                 moe_w1=w1_b, moe_w3=w3_b, moe_w2=w2_b, layer=l)
        x_ctx, st = _layer(x_ctx, b_ctx, t_ctx, mod_all[l], row_ctx, p, zero_state, tm)
        outs.append(st)
        s5_l = state_s5[:, l].reshape(b_lat, 2, S5_STATES, 2)
        lat_state = dict(
            mlstm_c=jnp.swapaxes(state_mlstm_C[:, l].reshape(b_lat, nd, HD, HD), -1, -2),
            mlstm_n=state_mlstm_n[:, l].reshape(b_lat, nd, HD),
            mlstm_m=jnp.broadcast_to(state_mlstm_m[:, l].reshape(b_lat, nd, 1), (b_lat, nd, LANE)),
            s5_re=s5_l[..., 0].transpose(1, 0, 2), s5_im=s5_l[..., 1].transpose(1, 0, 2),
            cos=cos_lat, sin=sin_lat,
            past_ckv=cache_mla_ckv[:, l], past_kr=_rope_slot(cache_mla_krope[:, l]))
        x_lat, _ = _layer(x_lat, b_lat, t_lat, mod_all[l], row_lat, p, lat_state, tm)

    y_prompt = _final_norm(x_ctx, final_g[None, :], tm).reshape(b_ctx, t_ctx, d)
    y_sample = _final_norm(x_lat, final_g[None, :], tm).reshape(b_lat, t_lat, d)
    stack = lambda i: jnp.stack([o[i] for o in outs], axis=1)
    new_ckv = stack(0)
    new_krope = stack(1)
    new_c = jnp.swapaxes(stack(2), -1, -2).reshape(b_ctx, n_layer, 2, HEADS, HD, HD)
    new_n = stack(3).reshape(b_ctx, n_layer, 2, HEADS, HD)
    new_m = stack(4)[..., 0].reshape(b_ctx, n_layer, 2, HEADS)
    s_re = jnp.stack([o[5] for o in outs], axis=0)
    s_im = jnp.stack([o[6] for o in outs], axis=0)
    new_s5 = jnp.stack([s_re, s_im], axis=-1).transpose(2, 0, 1, 3, 4)
    new_s5 = new_s5.reshape(b_ctx, n_layer, 2, S5_G, S5_N, 2)
    return (y_prompt, y_sample, new_ckv, new_krope, new_c, new_n, new_m, new_s5)
```

```python
import functools
import math

import jax
import jax.numpy as jnp
from jax import lax
from jax.experimental import pallas as pl
from jax.experimental.pallas import tpu as pltpu

F32 = jnp.float32
BF16 = jnp.bfloat16
EPS = 1e-6

D_MODEL = 1024
MIX = 256
CONV_WIDTH = 31
HEADS = 4
HD = 64
CHUNK = 128
_STATIC_CHUNKS = 16
ROPE = 32
KV_RANK = 128
Q_RANK = 256
GRID_W = 64
ROPE_BASE = 10000.0
S5_G = 16
S5_GC = 16
S5_N = 64
S5_STATES = S5_G * S5_N
N_EXPERT = 32
PER_GROUP = 8
N_GROUP = 4
MOE_FF = 256
LANE = 128
Z_COLS = 2688
VMEM_LIMIT = 56 * 1024 * 1024


def _cp(*sem):
    return pltpu.CompilerParams(dimension_semantics=sem, vmem_limit_bytes=VMEM_LIMIT)


def _rms(x):
    return x * lax.rsqrt(jnp.mean(x * x, axis=-1, keepdims=True) + EPS)


def _sigmoid(x):
    return 1.0 / (1.0 + jnp.exp(-x))


def _bdot(a, b):
    return jnp.dot(a.astype(BF16), b.astype(BF16), preferred_element_type=F32)


def _mod_kernel(c_ref, w_ref, b_ref, o_ref):
    c = c_ref[...]
    o_ref[0] = _bdot(c * _sigmoid(c), w_ref[0]) + b_ref[0]


def _modulation(c_all, w_mod, b_mod):
    n_layer, d, n = w_mod.shape
    tn = 1536
    return pl.pallas_call(
        _mod_kernel,
        out_shape=jax.ShapeDtypeStruct((n_layer, 16, n), F32),
        grid=(n_layer, n // tn),
        in_specs=[pl.BlockSpec((16, d), lambda l, j: (0, 0)),
                  pl.BlockSpec((1, d, tn), lambda l, j: (l, 0, j)),
                  pl.BlockSpec((1, 1, tn), lambda l, j: (l, 0, j))],
        out_specs=pl.BlockSpec((1, 16, tn), lambda l, j: (l, 0, j)),
        compiler_params=_cp("parallel", "parallel"),
        name="modulation",
    )(c_all, w_mod, b_mod.reshape(n_layer, 1, n))


def _s5_disc_kernel(are_ref, aim_ref, ldt_ref, bre_ref, bim_ref, abre_ref, abim_ref, bbre_ref, bbim_ref):
    a_re = are_ref[...]
    a_im = aim_ref[...]
    dt = jnp.exp(ldt_ref[...])
    mag = jnp.exp(a_re * dt)
    ab_re = mag * jnp.cos(a_im * dt)
    ab_im = mag * jnp.sin(a_im * dt)
    den = a_re * a_re + a_im * a_im
    f_re = ((ab_re - 1.0) * a_re + ab_im * a_im) / den
    f_im = (ab_im * a_re - (ab_re - 1.0) * a_im) / den
    b_re = bre_ref[...]
    b_im = bim_ref[...]
    abre_ref[...] = ab_re
    abim_ref[...] = ab_im
    bbre_ref[...] = f_re * b_re - f_im * b_im
    bbim_ref[...] = f_re * b_im + f_im * b_re


def _s5_discretise(a_re, a_im, log_dt, b_re, b_im):
    n_layer = a_re.shape[0]
    rows = n_layer * 2 * S5_G
    cols = S5_N * S5_GC
    rep = lambda a: jnp.repeat(a.reshape(rows, S5_N), S5_GC, axis=1)
    ldt = jnp.broadcast_to(log_dt.reshape(rows, 1), (rows, cols))
    spec = pl.BlockSpec((rows, cols), lambda: (0, 0))
    ab_re, ab_im, bb_re, bb_im = pl.pallas_call(
        _s5_disc_kernel,
        out_shape=[jax.ShapeDtypeStruct((rows, cols), F32)] * 4,
        in_specs=[spec] * 5,
        out_specs=[spec] * 4,
        name="s5_discretise",
    )(rep(a_re), rep(a_im), ldt, b_re.reshape(rows, cols), b_im.reshape(rows, cols))
    pick = lambda a: a[:, ::S5_GC].reshape(n_layer, 2, S5_STATES)
    shp = (n_layer, 2, S5_G, S5_N, S5_GC)
    return pick(ab_re), pick(ab_im), bb_re.reshape(shp), bb_im.reshape(shp)


def _pre_kernel(x_ref, mod_ref, g_ref, w_ref, za_ref, zb_ref, zg_ref, zc_ref, zd_ref):
    m = mod_ref[0]
    h = _rms(x_ref[...]) * g_ref[...] * (1.0 + m[1:2]) + m[0:1]
    hb = h.astype(BF16)
    col = 0
    for o_ref in (za_ref, zb_ref, zg_ref, zc_ref, zd_ref):
        n = o_ref.shape[-1]
        o_ref[...] = jnp.dot(hb, w_ref[:, col:col + n], preferred_element_type=F32).astype(o_ref.dtype)
        col += n


def _pre_mixer(x, mod, row_fn, norm_g, w_in_ext, tm):
    n = x.shape[0]
    widths = (512, 1024, 2 * LANE, 640, MIX)
    return pl.pallas_call(
        _pre_kernel,
        out_shape=[jax.ShapeDtypeStruct((n, w), dt) for w, dt in zip(widths, (BF16, F32, F32, BF16, F32))],
        grid=(n // tm,),
        in_specs=[pl.BlockSpec((tm, D_MODEL), lambda i: (i, 0)),
                  pl.BlockSpec((1, 6, D_MODEL), lambda i: (row_fn(i, tm), 0, 0)),
                  pl.BlockSpec((1, D_MODEL), lambda i: (0, 0)),
                  pl.BlockSpec((D_MODEL, Z_COLS), lambda i: (0, 0))],
        out_specs=[pl.BlockSpec((tm, w), lambda i: (i, 0)) for w in widths],
        compiler_params=_cp("parallel"),
        name="pre_mixer",
    )(x, mod, norm_g, w_in_ext)


_CONV_PAD = 16
_CONV_TT = 128


def _conv_kernel(za_ref, w_ref, b_ref, lg_ref, lb_ref, o_ref, hp_ref, sh_ref):
    t_len = o_ref.shape[1]
    u = za_ref[0].astype(F32)
    hp_ref[0:_CONV_PAD, :] = jnp.zeros((_CONV_PAD, MIX), F32)
    hp_ref[_CONV_PAD + t_len:2 * _CONV_PAD + t_len, :] = jnp.zeros((_CONV_PAD, MIX), F32)
    hp_ref[_CONV_PAD:_CONV_PAD + t_len, :] = u[:, :MIX] * _sigmoid(u[:, MIX:])
    span = t_len + 2 * _CONV_PAD - 8
    for off in range(8):
        sh_ref[off, 0:span, :] = hp_ref[off:off + span, :]
    w = w_ref[...]
    half = CONV_WIDTH // 2
    for t0 in range(0, t_len, _CONV_TT):
        acc = jnp.zeros((_CONV_TT, MIX), F32) + b_ref[...]
        for k in range(CONV_WIDTH):
            start = t0 + _CONV_PAD - half + k
            aligned = start // 8 * 8
            acc = acc + sh_ref[start - aligned, aligned:aligned + _CONV_TT, :] * w[k:k + 1, :]
        mu = jnp.mean(acc, axis=-1, keepdims=True)
        cen = acc - mu
        var = jnp.mean(cen * cen, axis=-1, keepdims=True)
        yn = cen * lax.rsqrt(var + EPS) * lg_ref[...] + lb_ref[...]
        o_ref[0, t0:t0 + _CONV_TT, :] = (yn * _sigmoid(yn)).astype(o_ref.dtype)


def _conv_module(za, w, b, ln_g, ln_b):
    bsz, t_len, _ = za.shape
    vec = pl.BlockSpec((1, MIX), lambda i: (0, 0))
    return pl.pallas_call(
        _conv_kernel,
        out_shape=jax.ShapeDtypeStruct((bsz, t_len, MIX), BF16),
        grid=(bsz,),
        in_specs=[pl.BlockSpec((1, t_len, 2 * MIX), lambda i: (i, 0, 0)),
                  pl.BlockSpec((32, MIX), lambda i: (0, 0)), vec, vec, vec],
        out_specs=pl.BlockSpec((1, t_len, MIX), lambda i: (i, 0, 0)),
        scratch_shapes=[pltpu.VMEM((t_len + 2 * _CONV_PAD, MIX), F32),
                        pltpu.VMEM((8, t_len + 2 * _CONV_PAD, MIX), F32)],
        compiler_params=_cp("parallel"),
        name="conv_module",
    )(za, w, b, ln_g, ln_b)


def _log_sigmoid(x):
    return jnp.minimum(x, 0.0) - jnp.log(1.0 + jnp.exp(-jnp.abs(x)))


def _split3_dot(tri, x):
    hi = x.astype(BF16)
    r1 = x - hi.astype(F32)
    mid = r1.astype(BF16)
    lo = (r1 - mid.astype(F32)).astype(BF16)
    dot = lambda v: jnp.dot(tri, v, preferred_element_type=F32)
    return dot(hi) + dot(mid) + dot(lo)


def _chunk_rows(c):
    if isinstance(c, int):
        return slice(c * CHUNK, (c + 1) * CHUNK)
    return pl.ds(pl.multiple_of(c * CHUNK, CHUNK), CHUNK)


def _loop(n, body, init):
    if n <= _STATIC_CHUNKS:
        for i in range(n):
            init = body(i, init)
        return init
    return lax.fori_loop(0, n, body, init)


def _split2(x):
    hi = x.astype(BF16)
    return hi, (x - hi.astype(F32)).astype(BF16)


def _mlstm_chunk(zb_ref, zg_ref, gb_ref, qt_sc, qbd_sc, vt_sc, vbd_sc, ct_sc, nb_sc, c, direction, ms):
    L = CHUNK
    rows = _chunk_rows(c)
    si = lax.broadcasted_iota(jnp.int32, (L, L), 0)
    ti = lax.broadcasted_iota(jnp.int32, (L, L), 1)
    if direction == 0:
        valid = si <= ti
        tri = jnp.where(ti <= si, 1.0, 0.0).astype(BF16)
        last = L - 1
    else:
        valid = si >= ti
        tri = jnp.where(ti >= si, 1.0, 0.0).astype(BF16)
        last = 0
    g_in = zg_ref[0, rows, 0:LANE] + gb_ref[:, 0:LANE]
    g_f = zg_ref[0, rows, LANE:2 * LANE] + gb_ref[:, LANE:2 * LANE]
    bc = _split3_dot(tri, _log_sigmoid(g_f))
    r = g_in - bc
    g_in_t = g_in.T
    bct = bc.T
    k = zb_ref[0, rows, MIX:2 * MIX]
    k_hi, k_lo = _split2(k)
    qt = qt_sc[c]
    st_all = jnp.dot(k_hi, qbd_sc[c], preferred_element_type=F32)
    sts, inters, floors, colsums, ws, decays, new_m = [], [], [], [], [], [], []
    for h in range(HEADS):
        j = direction * HEADS + h
        i_row = g_in_t[j:j + 1, :]
        b_row = bct[j:j + 1, :]
        m_prev = ms[h]
        rb = jnp.where(valid, jnp.broadcast_to(r[:, j:j + 1], (L, L)), -jnp.inf)
        c_row = jnp.maximum(m_prev, jnp.max(rb, axis=0, keepdims=True))
        st = st_all[:, h * L:(h + 1) * L] * jnp.exp(rb - c_row)
        sts.append(st.astype(BF16))
        colsums.append(jnp.sum(st, axis=0, keepdims=True))
        inters.append(jnp.exp(m_prev - c_row))
        floors.append(jnp.exp(-(b_row + c_row)))
        b_last = b_row[:, last:last + 1]
        lw = b_last - b_row + i_row
        m_new = jnp.maximum(b_last + m_prev, jnp.max(lw, axis=-1, keepdims=True))
        decays.append(jnp.exp(b_last + m_prev - m_new))
        ws.append(jnp.exp(lw - m_new))
        new_m.append(m_new)
    per_head_rows = lambda vs: jnp.concatenate([jnp.broadcast_to(v, (HD, L)) for v in vs], axis=0)
    ct = ct_sc[direction]
    nb = nb_sc[direction]
    num = (per_head_rows(inters) * jnp.dot(ct.astype(BF16), qt, preferred_element_type=F32)
           + jnp.dot(vbd_sc[c], jnp.concatenate(sts, axis=0), preferred_element_type=F32))
    n_hi, n_lo = _split2(nb)
    qn = jnp.dot(n_hi, qt, preferred_element_type=F32) + jnp.dot(n_lo, qt, preferred_element_type=F32)
    dens = [jnp.maximum(jnp.abs(inters[h] * qn[h:h + 1, :] + colsums[h]), floors[h]) for h in range(HEADS)]
    ht = num / per_head_rows(dens)
    lane = lax.broadcasted_iota(jnp.int32, (1, MIX), 1) // HD
    dec_row = sum(jnp.where(lane == h, decays[h], 0.0) for h in range(HEADS))
    upd = jnp.dot((vt_sc[c] * per_head_rows(ws)).astype(BF16), k_hi, preferred_element_type=F32)
    same_head = (lax.broadcasted_iota(jnp.int32, (MIX, MIX), 0) // HD
                 == lax.broadcasted_iota(jnp.int32, (MIX, MIX), 1) // HD)
    ct_sc[direction] = jnp.where(same_head, ct * dec_row + upd, 0.0)
    w_hi, w_lo = _split2(jnp.concatenate(ws + [jnp.zeros((8 - HEADS, L), F32)], axis=0))
    wk = (jnp.dot(w_hi, k_hi, preferred_element_type=F32) + jnp.dot(w_lo, k_hi, preferred_element_type=F32)
          + jnp.dot(w_hi, k_lo, preferred_element_type=F32))
    own = lax.broadcasted_iota(jnp.int32, (8, MIX), 0) == lax.broadcasted_iota(jnp.int32, (8, MIX), 1) // HD
    nb_sc[direction] = jnp.where(own, nb * dec_row + wk, 0.0)
    return ht, tuple(new_m)


def _mlstm_kernel(zb_ref, zg_ref, gb_ref, ng_ref, c0_ref, n0_ref, m0_ref,
                  y_ref, c_ref, n_ref, m_ref, qt_sc, qbd_sc, vt_sc, vbd_sc, ht_sc, ct_sc, nb_sc):
    t_len = y_ref.shape[1]
    nc = t_len // CHUNK
    L = CHUNK
    qbd_sc[...] = jnp.zeros_like(qbd_sc)
    vbd_sc[...] = jnp.zeros_like(vbd_sc)
    ct_sc[...] = jnp.zeros_like(ct_sc)
    nb_sc[...] = jnp.zeros_like(nb_sc)

    def transpose_chunk(c, carry):
        rows = _chunk_rows(c)
        qt = (zb_ref[0, rows, 0:MIX] * (HD ** -0.5)).T.astype(BF16)
        vt = zb_ref[0, rows, 2 * MIX:3 * MIX].T
        qt_sc[c] = qt
        vt_sc[c] = vt
        for h in range(HEADS):
            sl = slice(h * HD, (h + 1) * HD)
            qbd_sc[c, sl, h * L:(h + 1) * L] = qt[sl, :]
            vbd_sc[c, sl, h * L:(h + 1) * L] = vt[sl, :].astype(BF16)
        return carry

    _loop(nc, transpose_chunk, 0)

    for j in range(2 * HEADS):
        d, h = divmod(j, HEADS)
        sl = slice(h * HD, (h + 1) * HD)
        ct_sc[d, sl, sl] = c0_ref[0, j]
        nb_sc[d, h:h + 1, sl] = n0_ref[0, j:j + 1, :]
    m_init = tuple(tuple(m0_ref[0, j:j + 1, 0:1] for j in range(d * HEADS, (d + 1) * HEADS)) for d in range(2))

    def scan_body(i, ms):
        cb = nc - 1 - i
        scr = (zb_ref, zg_ref, gb_ref, qt_sc, qbd_sc, vt_sc, vbd_sc, ct_sc, nb_sc)
        ht_sc[0, i], ms_f = _mlstm_chunk(*scr, i, 0, ms[0])
        ht_sc[1, cb], ms_b = _mlstm_chunk(*scr, cb, 1, ms[1])
        return ms_f, ms_b

    m_fin = _loop(nc, scan_body, m_init)
    for j in range(2 * HEADS):
        d, h = divmod(j, HEADS)
        sl = slice(h * HD, (h + 1) * HD)
        c_ref[0, j] = ct_sc[d, sl, sl]
        n_ref[0, j:j + 1, :] = nb_sc[d, h:h + 1, sl]
        m_ref[0, j:j + 1, :] = jnp.broadcast_to(m_fin[d][h], (1, LANE))

    def out_body(c, carry):
        rows = _chunk_rows(c)
        normed = []
        for h in range(HEADS):
            tot = ht_sc[0, c, h * HD:(h + 1) * HD, :] + ht_sc[1, c, h * HD:(h + 1) * HD, :]
            normed.append(tot * lax.rsqrt(jnp.mean(tot * tot, axis=0, keepdims=True) + EPS))
        hn = jnp.concatenate(normed, axis=0).T
        y_ref[0, rows, :] = (hn * ng_ref[...] * _sigmoid(zb_ref[0, rows, 3 * MIX:4 * MIX])).astype(y_ref.dtype)
        return carry

    _loop(nc, out_body, 0)


def _mlstm_mixer(zb, zg, gate_b, norm_g, c0, n0, m0):
    bsz, t_len, _ = zb.shape
    nd = 2 * HEADS
    return pl.pallas_call(
        _mlstm_kernel,
        out_shape=[jax.ShapeDtypeStruct((bsz, t_len, MIX), BF16),
                   jax.ShapeDtypeStruct((bsz, nd, HD, HD), F32),
                   jax.ShapeDtypeStruct((bsz, nd, HD), F32),
                   jax.ShapeDtypeStruct((bsz, nd, LANE), F32)],
        grid=(bsz,),
        in_specs=[pl.BlockSpec((1, t_len, 4 * MIX), lambda i: (i, 0, 0)),
                  pl.BlockSpec((1, t_len, 2 * LANE), lambda i: (i, 0, 0)),
                  pl.BlockSpec((1, 2 * LANE), lambda i: (0, 0)),
                  pl.BlockSpec((1, MIX), lambda i: (0, 0)),
                  pl.BlockSpec((1, nd, HD, HD), lambda i: (i, 0, 0, 0)),
                  pl.BlockSpec((1, nd, HD), lambda i: (i, 0, 0)),
                  pl.BlockSpec((1, nd, LANE), lambda i: (i, 0, 0))],
        out_specs=[pl.BlockSpec((1, t_len, MIX), lambda i: (i, 0, 0)),
                   pl.BlockSpec((1, nd, HD, HD), lambda i: (i, 0, 0, 0)),
                   pl.BlockSpec((1, nd, HD), lambda i: (i, 0, 0)),
                   pl.BlockSpec((1, nd, LANE), lambda i: (i, 0, 0))],
        scratch_shapes=[pltpu.VMEM((t_len // CHUNK, MIX, CHUNK), BF16),
                        pltpu.VMEM((t_len // CHUNK, MIX, HEADS * CHUNK), BF16),
                        pltpu.VMEM((t_len // CHUNK, MIX, CHUNK), F32),
                        pltpu.VMEM((t_len // CHUNK, MIX, HEADS * CHUNK), BF16),
                        pltpu.VMEM((2, t_len // CHUNK, MIX, CHUNK), F32),
                        pltpu.VMEM((2, MIX, MIX), F32),
                        pltpu.VMEM((2, 8, MIX), F32)],
        compiler_params=_cp("parallel"),
        name="mlstm_mixer",
    )(zb, zg, gate_b, norm_g, c0, n0, m0)


_ATT_TQ = 256


def _mla_kernel(*refs, past):
    if past:
        (zc_ref, cos_ref, sin_ref, gq_ref, gkv_ref, wqa_ref, wqb_ref, wk_ref, wv_ref,
         pckv_ref, pkr_ref, y_ref, ckv_ref, q_sc, k_sc, v_sc) = refs
    else:
        (zc_ref, cos_ref, sin_ref, gq_ref, gkv_ref, wqa_ref, wqb_ref, wk_ref, wv_ref,
         y_ref, ckv_ref, q_sc, k_sc, v_sc) = refs
    t_len = y_ref.shape[1]
    cosf = cos_ref[...]
    sinf = sin_ref[...]
    zc = lambda a, b: zc_ref[0, :, a:b].astype(F32)
    cq = _rms(zc(0, Q_RANK)) * gq_ref[...]
    ckv = _rms(zc(Q_RANK, Q_RANK + KV_RANK)) * gkv_ref[...]
    ckv_ref[0] = ckv
    kr = (zc(Q_RANK + KV_RANK, Q_RANK + KV_RANK + LANE) * cosf
          + zc(Q_RANK + KV_RANK + LANE, Q_RANK + KV_RANK + 2 * LANE) * sinf)
    cqb = cq.astype(BF16)
    ckvb = ckv.astype(BF16)
    pair = lambda a: jnp.concatenate([a, a], axis=-1)
    cos2, sin2, kr2 = pair(cosf), pair(sinf), pair(kr)
    for hp in range(HEADS // 2):
        sl = slice(hp * 2 * LANE, (hp + 1) * 2 * LANE)
        qa = jnp.dot(cqb, wqa_ref[:, sl], preferred_element_type=F32)
        qb = jnp.dot(cqb, wqb_ref[:, sl], preferred_element_type=F32)
        q_sc[:, sl] = (qa * cos2 + qb * sin2).astype(BF16)
        kn = jnp.dot(ckvb, wk_ref[:, sl], preferred_element_type=F32)
        k_sc[past:past + t_len, sl] = (kn + kr2).astype(BF16)
        v_sc[past:past + t_len, sl] = jnp.dot(ckvb, wv_ref[:, sl], preferred_element_type=F32).astype(BF16)
    if past:
        pckv = pckv_ref[0].astype(BF16)
        pkr2 = pair(pkr_ref[0])
        for hp in range(HEADS // 2):
            sl = slice(hp * 2 * LANE, (hp + 1) * 2 * LANE)
            k_sc[0:past, sl] = (jnp.dot(pckv, wk_ref[:, sl], preferred_element_type=F32) + pkr2).astype(BF16)
            v_sc[0:past, sl] = jnp.dot(pckv, wv_ref[:, sl], preferred_element_type=F32).astype(BF16)

    def q_block(i, carry):
        rows = pl.ds(pl.multiple_of(i * _ATT_TQ, _ATT_TQ), _ATT_TQ)
        outs = []
        for h in range(HEADS):
            sl = slice(h * LANE, (h + 1) * LANE)
            s = lax.dot_general(q_sc[rows, sl], k_sc[:, sl], (((1,), (1,)), ((), ())),
                                preferred_element_type=F32)
            p = jnp.exp(s - jnp.max(s, axis=-1, keepdims=True))
            o = jnp.dot(p.astype(BF16), v_sc[:, sl], preferred_element_type=F32)
            outs.append(o / jnp.sum(p, axis=-1, keepdims=True))
        y_ref[0, rows, 0:LANE] = (outs[0] + outs[1]).astype(y_ref.dtype)
        y_ref[0, rows, LANE:2 * LANE] = (outs[2] + outs[3]).astype(y_ref.dtype)
        return carry

    n_blocks = t_len // _ATT_TQ
    if n_blocks % 2 == 0:
        lax.fori_loop(0, n_blocks // 2, lambda i, c: q_block(2 * i + 1, q_block(2 * i, c)), 0)
    else:
        lax.fori_loop(0, n_blocks, q_block, 0)


def _mla_mixer(zc, cosf, sinf, gq, gkv, wqa, wqb, wk, wv, past_ckv=None, past_kr=None):
    bsz, t_len, zw = zc.shape
    past = 0 if past_ckv is None else past_ckv.shape[1]
    full = lambda a: pl.BlockSpec(a.shape, lambda i: (0,) * a.ndim)
    args = [zc, cosf, sinf, gq, gkv, wqa, wqb, wk, wv]
    in_specs = [pl.BlockSpec((1, t_len, zw), lambda i: (i, 0, 0))] + [full(a) for a in args[1:]]
    if past:
        args += [past_ckv, past_kr]
        in_specs += [pl.BlockSpec((1, past, KV_RANK), lambda i: (i, 0, 0)),
                     pl.BlockSpec((1, past, LANE), lambda i: (i, 0, 0))]
    wide = HEADS * LANE
    return pl.pallas_call(
        functools.partial(_mla_kernel, past=past),
        out_shape=[jax.ShapeDtypeStruct((bsz, t_len, MIX), BF16),
                   jax.ShapeDtypeStruct((bsz, t_len, KV_RANK), F32)],
        grid=(bsz,),
        in_specs=in_specs,
        out_specs=[pl.BlockSpec((1, t_len, MIX), lambda i: (i, 0, 0)),
                   pl.BlockSpec((1, t_len, KV_RANK), lambda i: (i, 0, 0))],
        scratch_shapes=[pltpu.VMEM((t_len, wide), BF16),
                        pltpu.VMEM((past + t_len, wide), BF16),
                        pltpu.VMEM((past + t_len, wide), BF16)],
        compiler_params=_cp("parallel"),
        name="mla_mixer",
    )(*args)


_S5_SEQ = 8
_S5_TT = 128
_S5_PARTS = 4


def _s5_kernel(zd_ref, bd_ref, cd_ref, are_ref, aim_ref, s0re_ref, s0im_ref,
               y_ref, fre_ref, fim_ref, bre_sc, bim_sc, sre_sc, sim_sc):
    direction = pl.program_id(0)
    k = pl.program_id(2)
    tt = _S5_TT
    rows = _S5_SEQ * tt

    @pl.when(k == 0)
    def _():
        sre_sc[...] = s0re_ref[0]
        sim_sc[...] = s0im_ref[0]

    part_steps = tt // _S5_PARTS
    part_rows = part_steps * _S5_SEQ

    def run(backward):
        parts = list(range(_S5_PARTS))[::-1] if backward else list(range(_S5_PARTS))
        a_re = jnp.broadcast_to(are_ref[0], (_S5_SEQ, S5_STATES))
        a_im = jnp.broadcast_to(aim_ref[0], (_S5_SEQ, S5_STATES))
        for p in parts:
            r = slice(p * part_rows, (p + 1) * part_rows)
            u = zd_ref[p * part_steps:(p + 1) * part_steps].reshape(part_rows, MIX).astype(BF16)
            bre_sc[r, :] = jnp.dot(u, bd_ref[0, :, 0:S5_STATES], preferred_element_type=F32)
            bim_sc[r, :] = jnp.dot(u, bd_ref[0, :, S5_STATES:2 * S5_STATES], preferred_element_type=F32)
        s_re, s_im = sre_sc[...], sim_sc[...]
        for p in parts:
            steps = range(p * part_steps, (p + 1) * part_steps)
            for t in (reversed(steps) if backward else steps):
                sel = slice(t * _S5_SEQ, (t + 1) * _S5_SEQ)
                s_re, s_im = (a_re * s_re - a_im * s_im + bre_sc[sel, :],
                              a_re * s_im + a_im * s_re + bim_sc[sel, :])
                bre_sc[sel, :] = s_re
                bim_sc[sel, :] = s_im
            r = slice(p * part_rows, (p + 1) * part_rows)
            y = (jnp.dot(bre_sc[r, :].astype(BF16), cd_ref[0, 0:S5_STATES, :], preferred_element_type=F32)
                 + jnp.dot(bim_sc[r, :].astype(BF16), cd_ref[0, S5_STATES:2 * S5_STATES, :],
                           preferred_element_type=F32))
            y_ref[0, p * part_steps:(p + 1) * part_steps] = y.reshape(part_steps, _S5_SEQ, MIX)
        sre_sc[...] = s_re
        sim_sc[...] = s_im

    pl.when(direction == 0)(lambda: run(False))
    pl.when(direction == 1)(lambda: run(True))

    @pl.when(k == pl.num_programs(2) - 1)
    def _():
        fre_ref[0] = sre_sc[...]
        fim_ref[0] = sim_sc[...]


def _s5_scan(zd, bd, cd, ab_re, ab_im, s0_re, s0_im):
    t_len, bsz, _ = zd.shape
    nt = t_len // _S5_TT
    tile = lambda d, k: k + d * (nt - 1 - 2 * k)
    st = pl.BlockSpec((1, _S5_SEQ, S5_STATES), lambda d, g, k: (d, g, 0))
    return pl.pallas_call(
        _s5_kernel,
        out_shape=[jax.ShapeDtypeStruct((2, t_len, bsz, MIX), F32),
                   jax.ShapeDtypeStruct((2, bsz, S5_STATES), F32),
                   jax.ShapeDtypeStruct((2, bsz, S5_STATES), F32)],
        grid=(2, bsz // _S5_SEQ, nt),
        in_specs=[pl.BlockSpec((_S5_TT, _S5_SEQ, MIX), lambda d, g, k: (tile(d, k), g, 0)),
                  pl.BlockSpec((1, MIX, 2 * S5_STATES), lambda d, g, k: (d, 0, 0)),
                  pl.BlockSpec((1, 2 * S5_STATES, MIX), lambda d, g, k: (d, 0, 0)),
                  pl.BlockSpec((1, 1, S5_STATES), lambda d, g, k: (d, 0, 0)),
                  pl.BlockSpec((1, 1, S5_STATES), lambda d, g, k: (d, 0, 0)),
                  st, st],
        out_specs=[pl.BlockSpec((1, _S5_TT, _S5_SEQ, MIX), lambda d, g, k: (d, tile(d, k), g, 0)),
                   st, st],
        scratch_shapes=[pltpu.VMEM((_S5_SEQ * _S5_TT, S5_STATES), F32),
                        pltpu.VMEM((_S5_SEQ * _S5_TT, S5_STATES), F32),
                        pltpu.VMEM((_S5_SEQ, S5_STATES), F32),
                        pltpu.VMEM((_S5_SEQ, S5_STATES), F32)],
        compiler_params=_cp("parallel", "parallel", "arbitrary"),
        name="s5_scan",
    )(zd, bd, cd, ab_re, ab_im, s0_re, s0_im)


def _gelu_tanh(x):
    return 0.5 * x * (1.0 + jnp.tanh(math.sqrt(2.0 / math.pi) * (x + 0.044715 * x * x * x)))


def _route(logits):
    lane = lax.broadcasted_iota(jnp.int32, logits.shape, 1).astype(F32)
    big = float(LANE)
    neg = -jnp.inf
    g_mask = (lane >= N_EXPERT) & (lane < N_EXPERT + N_GROUP)
    gl = jnp.where(g_mask, logits, neg)
    g_max = jnp.max(gl, axis=-1, keepdims=True)
    g_idx = jnp.min(jnp.where(gl == g_max, lane, big), axis=-1, keepdims=True) - N_EXPERT
    g_sel = 1.0 / jnp.sum(jnp.where(g_mask, jnp.exp(logits - g_max), 0.0), axis=-1, keepdims=True)
    lo = g_idx * PER_GROUP
    el = jnp.where((lane >= lo) & (lane < lo + PER_GROUP), logits, neg)
    v1 = jnp.max(el, axis=-1, keepdims=True)
    i1 = jnp.min(jnp.where(el == v1, lane, big), axis=-1, keepdims=True)
    el2 = jnp.where(lane == i1, neg, el)
    v2 = jnp.max(el2, axis=-1, keepdims=True)
    i2 = jnp.min(jnp.where(el2 == v2, lane, big), axis=-1, keepdims=True)
    e2 = jnp.exp(v2 - v1)
    w1 = g_sel / (1.0 + e2)
    return jnp.where(lane == i1, w1, jnp.where(lane == i2, w1 * e2, 0.0))


def _post_kernel(x_ref, ya_ref, yb_ref, yc_ref, ys_ref, zd_ref, mod_ref, wo_ref, d_ref, wglu_ref, bglu_ref,
                 g2_ref, wr_ref, br_ref, x1_ref, h2_ref, gate_ref):
    m = mod_ref[0]
    zd = zd_ref[...]
    ys = ys_ref[0] + ys_ref[1] + d_ref[...] * zd
    gl = _bdot(_gelu_tanh(ys), wglu_ref[...]) + bglu_ref[...]
    yd = gl[:, 0:MIX] * _sigmoid(gl[:, MIX:2 * MIX])
    mix = (_bdot(ya_ref[...], wo_ref[0:MIX, :]) + _bdot(yb_ref[...], wo_ref[MIX:2 * MIX, :])
           + _bdot(yc_ref[...], wo_ref[2 * MIX:3 * MIX, :]) + _bdot(yd, wo_ref[3 * MIX:4 * MIX, :]))
    x1 = x_ref[...] + m[2:3] * mix
    x1_ref[...] = x1
    h2 = _rms(x1) * g2_ref[...] * (1.0 + m[4:5]) + m[3:4]
    h2b = h2.astype(BF16)
    h2_ref[...] = h2b
    gate_ref[...] = _route(jnp.dot(h2b, wr_ref[...], preferred_element_type=F32) + br_ref[...])


def _post_mixer(x, ya, yb, yc, ys, zd, mod, row_fn, w_out, s5_d, w_glu, b_glu, norm2_g, w_router, b_router, tm):
    n = x.shape[0]
    tok = lambda w: pl.BlockSpec((tm, w), lambda i: (i, 0))
    full = lambda a: pl.BlockSpec(a.shape, lambda i: (0,) * a.ndim)
    return pl.pallas_call(
        _post_kernel,
        out_shape=[jax.ShapeDtypeStruct((n, D_MODEL), F32),
                   jax.ShapeDtypeStruct((n, D_MODEL), BF16),
                   jax.ShapeDtypeStruct((n, LANE), F32)],
        grid=(n // tm,),
        in_specs=[tok(D_MODEL), tok(MIX), tok(MIX), tok(MIX),
                  pl.BlockSpec((2, tm, MIX), lambda i: (0, i, 0)), tok(MIX),
                  pl.BlockSpec((1, 6, D_MODEL), lambda i: (row_fn(i, tm), 0, 0)),
                  full(w_out), full(s5_d), full(w_glu), full(b_glu), full(norm2_g),
                  full(w_router), full(b_router)],
        out_specs=[tok(D_MODEL), tok(D_MODEL), tok(LANE)],
        compiler_params=_cp("parallel"),
        name="post_mixer",
    )(x, ya, yb, yc, ys, zd, mod, w_out, s5_d, w_glu, b_glu, norm2_g, w_router, b_router)


_MOE_EPS = 4
_MOE_TM = 1024


_MOE_ALIGN = 16
_MOE_ROWS = 2 * _MOE_TM + N_EXPERT * _MOE_ALIGN
_MOE_CH = 128
_MOE_GB = 512
_MOE_CB = 256


def _moe_kernel(h_ref, gate_ref, x_ref, mod_ref, w1_ref, w3_ref, w2_ref, o_ref,
                xs_sc, ys_sc, col_sc, row_sc, start_sm, rows_sm, done_sm):
    j = pl.program_id(1)
    tm = h_ref.shape[0]

    @pl.when(j == 0)
    def _dispatch():
        gate = gate_ref[...]
        cnt_row = jnp.sum(jnp.where(gate != 0.0, 1.0, 0.0), axis=0, keepdims=True)
        units_row = jnp.floor((cnt_row + (_MOE_ALIGN - 1)) * (1.0 / _MOE_ALIGN))
        li = lax.broadcasted_iota(jnp.int32, (LANE, LANE), 0)
        lj = lax.broadcasted_iota(jnp.int32, (LANE, LANE), 1)
        before = jnp.where(li < lj, 1.0, 0.0).astype(BF16)
        start_row = jnp.dot(jnp.broadcast_to(units_row, (8, LANE)).astype(BF16), before,
                            preferred_element_type=F32) * _MOE_ALIGN
        start_i = start_row.astype(jnp.int32)
        cnt_i = cnt_row.astype(jnp.int32)
        for e in range(N_EXPERT):
            start_sm[e] = start_i[0, e]
            rows_sm[e] = cnt_i[0, e]
        gt = gate.T[0:N_EXPERT, :]
        chosen = gt != 0.0
        ones = jnp.where(chosen, 1.0, 0.0)
        cnt = jnp.sum(ones, axis=-1, keepdims=True)
        units = jnp.floor((cnt + (_MOE_ALIGN - 1)) * (1.0 / _MOE_ALIGN))
        ei = lax.broadcasted_iota(jnp.int32, (N_EXPERT, N_EXPERT), 0)
        ej = lax.broadcasted_iota(jnp.int32, (N_EXPERT, N_EXPERT), 1)
        start = jnp.dot(jnp.where(ej < ei, 1.0, 0.0).astype(BF16),
                        jnp.broadcast_to(units, (N_EXPERT, LANE)).astype(BF16),
                        preferred_element_type=F32)[:, 0:1] * _MOE_ALIGN
        ti = lax.broadcasted_iota(jnp.int32, (tm, tm), 0)
        tj = lax.broadcasted_iota(jnp.int32, (tm, tm), 1)
        earlier = jnp.where(ti < tj, 1.0, 0.0).astype(BF16)
        pos = start + jnp.dot(ones.astype(BF16), earlier, preferred_element_type=F32)
        p_a = jnp.min(jnp.where(chosen, pos, float(_MOE_ROWS)), axis=0, keepdims=True)
        p_b = jnp.max(jnp.where(chosen, pos, -1.0), axis=0, keepdims=True)
        g_a = jnp.sum(jnp.where(chosen & (pos == p_a), gt, 0.0), axis=0, keepdims=True)
        g_b = jnp.sum(jnp.where(chosen & (pos == p_b) & (p_b != p_a), gt, 0.0), axis=0, keepdims=True)
        row_sc[...] = jnp.concatenate([p_a, p_b, jnp.zeros((6, tm), F32)], axis=0)
        packed = jnp.concatenate([p_a, p_b, g_a, g_b, jnp.zeros((LANE - 4, tm), F32)], axis=0)
        col_sc[...] = packed.T
        xs_sc[_MOE_ROWS:_MOE_ROWS + _MOE_CH, :] = jnp.zeros((_MOE_CH, D_MODEL), BF16)
        ys_sc[...] = jnp.zeros_like(ys_sc)
        o_ref[...] = x_ref[...]
        done_sm[0] = 0
        done_sm[1] = 0

    n_blocks = _MOE_ROWS // _MOE_GB
    e_last = j * _MOE_EPS + (_MOE_EPS - 1)
    seg_end = start_sm[e_last] + rows_sm[e_last]
    last_step = j == pl.num_programs(1) - 1

    def gather_block(b, carry):
        r0 = pl.multiple_of(b * _MOE_GB, _MOE_GB)
        ri = (lax.broadcasted_iota(jnp.int32, (_MOE_GB, tm), 0) + r0).astype(F32)
        sel = jnp.where((ri == row_sc[0:1, :]) | (ri == row_sc[1:2, :]), 1.0, 0.0).astype(BF16)
        xs_sc[pl.ds(r0, _MOE_GB), :] = jnp.dot(sel, h_ref[...], preferred_element_type=F32).astype(BF16)
        return carry

    gathered = jnp.minimum((seg_end + _MOE_CH + _MOE_GB - 1) // _MOE_GB, n_blocks)
    lax.fori_loop(done_sm[0], gathered, gather_block, 0)
    done_sm[0] = gathered

    def expert_rows(el, i):
        rows = pl.ds(pl.multiple_of(start_sm[j * _MOE_EPS + el] + i * _MOE_CH, _MOE_ALIGN), _MOE_CH)
        xs = xs_sc[rows, :]
        h1 = jnp.dot(xs, w1_ref[el], preferred_element_type=F32)
        h3 = jnp.dot(xs, w3_ref[el], preferred_element_type=F32)
        return rows, jnp.dot((h1 * _sigmoid(h1) * h3).astype(BF16), w2_ref[el], preferred_element_type=F32)

    first = [expert_rows(el, 0) for el in range(_MOE_EPS)]
    for rows, y in first:
        ys_sc[rows, :] = y.astype(BF16)
    for el in range(_MOE_EPS):
        n_rows = rows_sm[j * _MOE_EPS + el]

        def chunk(i, carry, el=el, n_rows=n_rows):
            rows, y = expert_rows(el, i)
            live = lax.broadcasted_iota(jnp.int32, (_MOE_CH, 1), 0) + i * _MOE_CH < n_rows
            ys_sc[rows, :] = jnp.where(live, y, ys_sc[rows, :].astype(F32)).astype(BF16)
            return carry

        lax.fori_loop(1, (n_rows + _MOE_CH - 1) // _MOE_CH, chunk, 0)

    g2 = mod_ref[0][5:6]

    def combine_block(b, carry):
        r0 = pl.multiple_of(b * _MOE_GB, _MOE_GB)
        y = ys_sc[pl.ds(r0, _MOE_GB), :]
        for t0 in range(0, tm, _MOE_CB):
            c = col_sc[t0:t0 + _MOE_CB, :]
            ri = (lax.broadcasted_iota(jnp.int32, (_MOE_CB, _MOE_GB), 1) + r0).astype(F32)
            w = (jnp.where(ri == c[:, 0:1], c[:, 2:3], 0.0)
                 + jnp.where(ri == c[:, 1:2], c[:, 3:4], 0.0)).astype(BF16)
            o_ref[t0:t0 + _MOE_CB, :] += g2 * jnp.dot(w, y, preferred_element_type=F32)
        return carry

    final_rows = (seg_end + _MOE_ALIGN - 1) // _MOE_ALIGN * _MOE_ALIGN
    combined = jnp.where(last_step, n_blocks, final_rows // _MOE_GB)
    lax.fori_loop(done_sm[1], combined, combine_block, 0)
    done_sm[1] = combined


def _moe(h2, gate, x1, mod, row_fn, w1, w3, w2, layer, tm):
    n = h2.shape[0]
    first_blk = layer * (N_EXPERT // _MOE_EPS)
    return pl.pallas_call(
        _moe_kernel,
        out_shape=jax.ShapeDtypeStruct((n, D_MODEL), F32),
        grid=(n // tm, N_EXPERT // _MOE_EPS),
        in_specs=[pl.BlockSpec((tm, D_MODEL), lambda i, j: (i, 0)),
                  pl.BlockSpec((tm, LANE), lambda i, j: (i, 0)),
                  pl.BlockSpec((tm, D_MODEL), lambda i, j: (i, 0)),
                  pl.BlockSpec((1, 6, D_MODEL), lambda i, j: (row_fn(i, tm), 0, 0)),
                  pl.BlockSpec((_MOE_EPS, D_MODEL, MOE_FF), lambda i, j: (first_blk + j, 0, 0)),
                  pl.BlockSpec((_MOE_EPS, D_MODEL, MOE_FF), lambda i, j: (first_blk + j, 0, 0)),
                  pl.BlockSpec((_MOE_EPS, MOE_FF, D_MODEL), lambda i, j: (first_blk + j, 0, 0))],
        out_specs=pl.BlockSpec((tm, D_MODEL), lambda i, j: (i, 0)),
        scratch_shapes=[pltpu.VMEM((_MOE_ROWS + _MOE_CH, D_MODEL), BF16),
                        pltpu.VMEM((_MOE_ROWS + _MOE_CH, D_MODEL), BF16),
                        pltpu.VMEM((tm, LANE), F32),
                        pltpu.VMEM((8, tm), F32),
                        pltpu.SMEM((N_EXPERT,), jnp.int32),
                        pltpu.SMEM((N_EXPERT,), jnp.int32),
                        pltpu.SMEM((2,), jnp.int32)],
        compiler_params=_cp("parallel", "arbitrary"),
        name="moe",
    )(h2, gate, x1, mod, w1, w3, w2)


def _final_kernel(x_ref, g_ref, o_ref):
    o_ref[...] = _rms(x_ref[...]) * g_ref[...]


def _final_norm(x, g, tm):
    n = x.shape[0]
    return pl.pallas_call(
        _final_kernel,
        out_shape=jax.ShapeDtypeStruct((n, D_MODEL), F32),
        grid=(n // tm,),
        in_specs=[pl.BlockSpec((tm, D_MODEL), lambda i: (i, 0)),
                  pl.BlockSpec((1, D_MODEL), lambda i: (0, 0))],
        out_specs=pl.BlockSpec((tm, D_MODEL), lambda i: (i, 0)),
        compiler_params=_cp("parallel"),
        name="final_norm",
    )(x, g)


def _rot_cols(w):
    q = ROPE // 4
    return jnp.concatenate([-w[..., q:2 * q], w[..., 0:q], -w[..., 3 * q:4 * q], w[..., 2 * q:3 * q]], axis=-1)


def _rope_slot(w):
    pad = [(0, 0)] * (w.ndim - 1) + [(HD, LANE - HD - ROPE)]
    return jnp.pad(w, pad)


def _gate_lanes(g):
    n = g.shape[-1] // 2
    pad = [(0, 0)] * (g.ndim - 1) + [(0, LANE - n)]
    return jnp.concatenate([jnp.pad(g[..., :n], pad), jnp.pad(g[..., n:], pad)], axis=-1)


def _prep_w_in(w_in, gate_cols):
    a = w_in[..., 0:512]
    qkvo = w_in[..., 512:1536]
    g = _gate_lanes(w_in[..., 1536:1536 + gate_cols])
    c0 = 1536 + gate_cols
    cq = w_in[..., c0:c0 + Q_RANK]
    ckv = w_in[..., c0 + Q_RANK:c0 + Q_RANK + KV_RANK]
    kr = w_in[..., c0 + Q_RANK + KV_RANK:c0 + Q_RANK + KV_RANK + ROPE]
    d = w_in[..., c0 + Q_RANK + KV_RANK + ROPE:]
    return jnp.concatenate([a, qkvo, g, cq, ckv, _rope_slot(kr), _rope_slot(_rot_cols(kr)), d],
                           axis=-1).astype(BF16)


def _prep_mla(w_uq, w_ukv):
    n_layer = w_uq.shape[0]
    scale = (HD + ROPE) ** -0.5
    wq = w_uq.reshape(n_layer, Q_RANK, HEADS, HD + ROPE) * scale
    nope, rope = wq[..., :HD], wq[..., HD:]
    zeros_r = jnp.zeros_like(rope)
    wqa = jnp.concatenate([nope, rope, zeros_r], axis=-1).reshape(n_layer, Q_RANK, HEADS * LANE)
    wqb = jnp.concatenate([jnp.zeros_like(nope), _rot_cols(rope), zeros_r], axis=-1)
    wqb = wqb.reshape(n_layer, Q_RANK, HEADS * LANE)
    wkv = w_ukv.reshape(n_layer, KV_RANK, HEADS, 2 * HD)
    k_nope, val = wkv[..., :HD], wkv[..., HD:]
    zeros_h = jnp.zeros_like(k_nope)
    wk = jnp.concatenate([k_nope, zeros_h], axis=-1).reshape(n_layer, KV_RANK, HEADS * LANE)
    even = jnp.concatenate([val, zeros_h], axis=-1)
    odd = jnp.concatenate([zeros_h, val], axis=-1)
    is_odd = (jnp.arange(HEADS) % 2 == 1)[None, None, :, None]
    wv = jnp.where(is_odd, odd, even).reshape(n_layer, KV_RANK, HEADS * LANE)
    return wqa.astype(BF16), wqb.astype(BF16), wk.astype(BF16), wv.astype(BF16)


def _rope_tables(t_len, rotate):
    ones = jnp.ones((t_len, HD), F32)
    zeros = jnp.zeros((t_len, HD), F32)
    tail = jnp.zeros((t_len, LANE - HD - ROPE), F32)
    if not rotate:
        return (jnp.concatenate([ones, jnp.ones((t_len, ROPE), F32), tail], axis=-1),
                jnp.zeros((t_len, LANE), F32))
    rows = t_len // GRID_W
    row = jnp.repeat(jnp.arange(rows, dtype=F32), GRID_W)
    col = jnp.tile(jnp.arange(GRID_W, dtype=F32), rows)
    nf = ROPE // 4
    inv = ROPE_BASE ** (-jnp.arange(nf, dtype=F32) / nf)
    ar = row[:, None] * inv
    ac = col[:, None] * inv
    cos = jnp.concatenate([jnp.cos(ar), jnp.cos(ar), jnp.cos(ac), jnp.cos(ac)], axis=-1)
    sin = jnp.concatenate([jnp.sin(ar), jnp.sin(ar), jnp.sin(ac), jnp.sin(ac)], axis=-1)
    return (jnp.concatenate([ones, cos, tail], axis=-1), jnp.concatenate([zeros, sin, tail], axis=-1))


def _prep_s5(bb_re, bb_im, c_re, c_im):
    eye = jnp.eye(S5_G, dtype=F32)
    to_b = lambda bb: jnp.einsum("ldgnc,gh->ldgchn", bb, eye).reshape(bb.shape[0], 2, MIX, S5_STATES)
    to_c = lambda cc: jnp.einsum("ldgcn,gh->ldgnhc", cc, eye).reshape(cc.shape[0], 2, S5_STATES, MIX)
    bd = jnp.concatenate([to_b(bb_re), to_b(bb_im)], axis=-1)
    cd = jnp.concatenate([to_c(c_re.astype(F32)), -to_c(c_im.astype(F32))], axis=-2)
    return bd.astype(BF16), cd.astype(BF16)


def _layer(x, bsz, t_len, mod, row_fn, p, ctx, tm):
    za, zb, zg, zc, zd = _pre_mixer(x, mod, row_fn, p["norm1_g"], p["w_in"], tm)
    seq = lambda a: a.reshape(bsz, t_len, a.shape[-1])
    ya = _conv_module(seq(za), p["conv_w"], p["conv_b"], p["conv_ln_g"], p["conv_ln_b"])
    yb, c_fin, n_fin, m_fin = _mlstm_mixer(seq(zb), seq(zg), p["gate_b"], p["mlstm_norm_g"],
                                           ctx["mlstm_c"], ctx["mlstm_n"], ctx["mlstm_m"])
    yc, ckv = _mla_mixer(seq(zc), ctx["cos"], ctx["sin"], p["mla_q_norm_g"], p["mla_kv_norm_g"],
                         p["wqa"], p["wqb"], p["wk"], p["wv"], ctx.get("past_ckv"), ctx.get("past_kr"))
    ys, s_re, s_im = _s5_scan(seq(zd).transpose(1, 0, 2), p["s5_bd"], p["s5_cd"], p["s5_ab_re"], p["s5_ab_im"],
                              ctx["s5_re"], ctx["s5_im"])
    ys = ys.transpose(0, 2, 1, 3)
    flat = lambda a: a.reshape(bsz * t_len, a.shape[-1])
    x1, h2, gate = _post_mixer(x, flat(ya), flat(yb), flat(yc), ys.reshape(2, bsz * t_len, MIX), zd, mod, row_fn,
                               p["w_out"], p["s5_d"], p["s5_w_glu"], p["s5_b_glu"], p["norm2_g"],
                               p["w_router"], p["b_router"], tm)
    x2 = _moe(h2, gate, x1, mod, row_fn, p["moe_w1"], p["moe_w3"], p["moe_w2"], p["layer"], _MOE_TM)
    krope = seq(zc)[:, :, Q_RANK + KV_RANK + HD:Q_RANK + KV_RANK + HD + ROPE].astype(F32)
    return x2, (ckv, krope, c_fin, n_fin, m_fin, s_re, s_im)


def kernel(x_prompt, x_sample, cache_mla_ckv, cache_mla_krope, state_mlstm_C, state_mlstm_n, state_mlstm_m, state_s5, c, c_ctx, norm1_g, norm2_g, final_g, w_mod, b_mod, w_in, w_out, conv_w, conv_b, conv_ln_g, conv_ln_b, mlstm_gate_b, mlstm_norm_g, mla_q_norm_g, mla_w_uq, mla_kv_norm_g, mla_w_ukv, s5_a_re, s5_a_im, s5_log_dt, s5_b_re, s5_b_im, s5_c_re, s5_c_im, s5_d, s5_w_glu, s5_b_glu, moe_w_group, moe_b_group, moe_w_expert, moe_b_expert, moe_w1, moe_w3, moe_w2):
    n_layer = w_in.shape[0]
    b_ctx, t_ctx, d = x_prompt.shape
    b_lat, t_lat, _ = x_sample.shape
    nd = 2 * HEADS
    tm = 512
    assert b_lat + 1 <= 16 and t_lat % _MOE_TM == 0 and (b_ctx * t_ctx) % _MOE_TM == 0

    c_all = jnp.zeros((16, d), F32).at[0].set(c_ctx).at[1:1 + b_lat].set(c)
    mod_all = _modulation(c_all, w_mod, b_mod).reshape(n_layer, 16, 6, d)
    gate_cols = 4 * HEADS
    w_in_ext = _prep_w_in(w_in, gate_cols)
    wqa, wqb, wk, wv = _prep_mla(mla_w_uq, mla_w_ukv)
    ab_re, ab_im, bb_re, bb_im = _s5_discretise(s5_a_re, s5_a_im, s5_log_dt, s5_b_re, s5_b_im)
    s5_bd, s5_cd = _prep_s5(bb_re, bb_im, s5_c_re, s5_c_im)
    w_router = jnp.pad(jnp.concatenate([moe_w_expert, moe_w_group], axis=-1),
                       [(0, 0), (0, 0), (0, LANE - N_EXPERT - N_GROUP)]).astype(BF16)
    b_router = jnp.pad(jnp.concatenate([moe_b_expert, moe_b_group], axis=-1),
                       [(0, 0), (0, LANE - N_EXPERT - N_GROUP)])
    gate_b = _gate_lanes(mlstm_gate_b.reshape(n_layer, gate_cols))
    conv_w_p = jnp.pad(conv_w, [(0, 0), (0, 32 - CONV_WIDTH), (0, 0)])
    w_out_b = w_out.astype(BF16)
    w_glu_b = s5_w_glu.astype(BF16)
    flat_experts = lambda w: w.astype(BF16).reshape((n_layer * N_EXPERT,) + w.shape[2:])
    w1_b, w3_b, w2_b = flat_experts(moe_w1), flat_experts(moe_w3), flat_experts(moe_w2)
    row = lambda a, l: a[l][None, :]

    cos_ctx, sin_ctx = _rope_tables(t_ctx, rotate=False)
    cos_lat, sin_lat = _rope_tables(t_lat, rotate=True)
    zero_state = dict(
        mlstm_c=jnp.zeros((b_ctx, nd, HD, HD), F32), mlstm_n=jnp.zeros((b_ctx, nd, HD), F32),
        mlstm_m=jnp.zeros((b_ctx, nd, LANE), F32),
        s5_re=jnp.zeros((2, b_ctx, S5_STATES), F32), s5_im=jnp.zeros((2, b_ctx, S5_STATES), F32),
        cos=cos_ctx, sin=sin_ctx)

    row_ctx = lambda i, tile: 0
    row_lat = lambda i, tile: 1 + (i * tile) // t_lat

    x_ctx = x_prompt.reshape(b_ctx * t_ctx, d)
    x_lat = x_sample.reshape(b_lat * t_lat, d)
    outs = []
    for l in range(n_layer):
        p = dict(norm1_g=row(norm1_g, l), norm2_g=row(norm2_g, l), w_in=w_in_ext[l], w_out=w_out_b[l],
                 conv_w=conv_w_p[l], conv_b=row(conv_b, l), conv_ln_g=row(conv_ln_g, l),
                 conv_ln_b=row(conv_ln_b, l), gate_b=row(gate_b, l), mlstm_norm_g=row(mlstm_norm_g, l),
                 mla_q_norm_g=row(mla_q_norm_g, l), mla_kv_norm_g=row(mla_kv_norm_g, l),
                 wqa=wqa[l], wqb=wqb[l], wk=wk[l], wv=wv[l],
                 s5_bd=s5_bd[l], s5_cd=s5_cd[l], s5_ab_re=ab_re[l][:, None, :], s5_ab_im=ab_im[l][:, None, :],
                 s5_d=row(s5_d, l), s5_w_glu=w_glu_b[l], s5_b_glu=row(s5_b_glu, l),
                 w_router=w_router[l], b_router=row(b_router, l),
                 moe_w1=w1_b, moe_w3=w3_b, moe_w2=w2_b, layer=l)
        x_ctx, st = _layer(x_ctx, b_ctx, t_ctx, mod_all[l], row_ctx, p, zero_state, tm)
        outs.append(st)
        s5_l = state_s5[:, l].reshape(b_lat, 2, S5_STATES, 2)
        lat_state = dict(
            mlstm_c=jnp.swapaxes(state_mlstm_C[:, l].reshape(b_lat, nd, HD, HD), -1, -2),
            mlstm_n=state_mlstm_n[:, l].reshape(b_lat, nd, HD),
            mlstm_m=jnp.broadcast_to(state_mlstm_m[:, l].reshape(b_lat, nd, 1), (b_lat, nd, LANE)),
            s5_re=s5_l[..., 0].transpose(1, 0, 2), s5_im=s5_l[..., 1].transpose(1, 0, 2),
            cos=cos_lat, sin=sin_lat,
            past_ckv=cache_mla_ckv[:, l], past_kr=_rope_slot(cache_mla_krope[:, l]))
        x_lat, _ = _layer(x_lat, b_lat, t_lat, mod_all[l], row_lat, p, lat_state, tm)

    y_prompt = _final_norm(x_ctx, final_g[None, :], tm).reshape(b_ctx, t_ctx, d)
    y_sample = _final_norm(x_lat, final_g[None, :], tm).reshape(b_lat, t_lat, d)
    stack = lambda i: jnp.stack([o[i] for o in outs], axis=1)
    new_ckv = stack(0)
    new_krope = stack(1)
    new_c = jnp.swapaxes(stack(2), -1, -2).reshape(b_ctx, n_layer, 2, HEADS, HD, HD)
    new_n = stack(3).reshape(b_ctx, n_layer, 2, HEADS, HD)
    new_m = stack(4)[..., 0].reshape(b_ctx, n_layer, 2, HEADS)
    s_re = jnp.stack([o[5] for o in outs], axis=0)
    s_im = jnp.stack([o[6] for o in outs], axis=0)
    new_s5 = jnp.stack([s_re, s_im], axis=-1).transpose(2, 0, 1, 3, 4)
    new_s5 = new_s5.reshape(b_ctx, n_layer, 2, S5_G, S5_N, 2)
    return (y_prompt, y_sample, new_ckv, new_krope, new_c, new_n, new_m, new_s5)
```

```python
import functools
import math

import jax
import jax.numpy as jnp
from jax import lax
from jax.experimental import pallas as pl
from jax.experimental.pallas import tpu as pltpu

F32 = jnp.float32
BF16 = jnp.bfloat16
EPS = 1e-6

D_MODEL = 1024
MIX = 256
CONV_WIDTH = 31
HEADS = 4
HD = 64
CHUNK = 128
_STATIC_CHUNKS = 16
ROPE = 32
KV_RANK = 128
Q_RANK = 256
GRID_W = 64
ROPE_BASE = 10000.0
S5_G = 16
S5_GC = 16
S5_N = 64
S5_STATES = S5_G * S5_N
N_EXPERT = 32
PER_GROUP = 8
N_GROUP = 4
MOE_FF = 256
LANE = 128
Z_COLS = 2688
VMEM_LIMIT = 56 * 1024 * 1024


def _cp(*sem):
    return pltpu.CompilerParams(dimension_semantics=sem, vmem_limit_bytes=VMEM_LIMIT)


def _rms(x):
    return x * lax.rsqrt(jnp.mean(x * x, axis=-1, keepdims=True) + EPS)


def _sigmoid(x):
    return 1.0 / (1.0 + jnp.exp(-x))


def _bdot(a, b):
    return jnp.dot(a.astype(BF16), b.astype(BF16), preferred_element_type=F32)


def _mod_kernel(c_ref, w_ref, b_ref, o_ref):
    c = c_ref[...]
    o_ref[0] = _bdot(c * _sigmoid(c), w_ref[0]) + b_ref[0]


def _modulation(c_all, w_mod, b_mod):
    n_layer, d, n = w_mod.shape
    tn = 1536
    return pl.pallas_call(
        _mod_kernel,
        out_shape=jax.ShapeDtypeStruct((n_layer, 16, n), F32),
        grid=(n_layer, n // tn),
        in_specs=[pl.BlockSpec((16, d), lambda l, j: (0, 0)),
                  pl.BlockSpec((1, d, tn), lambda l, j: (l, 0, j)),
                  pl.BlockSpec((1, 1, tn), lambda l, j: (l, 0, j))],
        out_specs=pl.BlockSpec((1, 16, tn), lambda l, j: (l, 0, j)),
        compiler_params=_cp("parallel", "parallel"),
        name="modulation",
    )(c_all, w_mod, b_mod.reshape(n_layer, 1, n))


def _s5_disc_kernel(are_ref, aim_ref, ldt_ref, bre_ref, bim_ref, abre_ref, abim_ref, bbre_ref, bbim_ref):
    a_re = are_ref[...]
    a_im = aim_ref[...]
    dt = jnp.exp(ldt_ref[...])
    mag = jnp.exp(a_re * dt)
    ab_re = mag * jnp.cos(a_im * dt)
    ab_im = mag * jnp.sin(a_im * dt)
    den = a_re * a_re + a_im * a_im
    f_re = ((ab_re - 1.0) * a_re + ab_im * a_im) / den
    f_im = (ab_im * a_re - (ab_re - 1.0) * a_im) / den
    b_re = bre_ref[...]
    b_im = bim_ref[...]
    abre_ref[...] = ab_re
    abim_ref[...] = ab_im
    bbre_ref[...] = f_re * b_re - f_im * b_im
    bbim_ref[...] = f_re * b_im + f_im * b_re


def _s5_discretise(a_re, a_im, log_dt, b_re, b_im):
    n_layer = a_re.shape[0]
    rows = n_layer * 2 * S5_G
    cols = S5_N * S5_GC
    rep = lambda a: jnp.repeat(a.reshape(rows, S5_N), S5_GC, axis=1)
    ldt = jnp.broadcast_to(log_dt.reshape(rows, 1), (rows, cols))
    spec = pl.BlockSpec((rows, cols), lambda: (0, 0))
    ab_re, ab_im, bb_re, bb_im = pl.pallas_call(
        _s5_disc_kernel,
        out_shape=[jax.ShapeDtypeStruct((rows, cols), F32)] * 4,
        in_specs=[spec] * 5,
        out_specs=[spec] * 4,
        name="s5_discretise",
    )(rep(a_re), rep(a_im), ldt, b_re.reshape(rows, cols), b_im.reshape(rows, cols))
    pick = lambda a: a[:, ::S5_GC].reshape(n_layer, 2, S5_STATES)
    shp = (n_layer, 2, S5_G, S5_N, S5_GC)
    return pick(ab_re), pick(ab_im), bb_re.reshape(shp), bb_im.reshape(shp)


def _pre_kernel(x_ref, mod_ref, g_ref, w_ref, za_ref, zb_ref, zg_ref, zc_ref, zd_ref):
    m = mod_ref[0]
    h = _rms(x_ref[...]) * g_ref[...] * (1.0 + m[1:2]) + m[0:1]
    hb = h.astype(BF16)
    col = 0
    for o_ref in (za_ref, zb_ref, zg_ref, zc_ref, zd_ref):
        n = o_ref.shape[-1]
        o_ref[...] = jnp.dot(hb, w_ref[:, col:col + n], preferred_element_type=F32).astype(o_ref.dtype)
        col += n


def _pre_mixer(x, mod, row_fn, norm_g, w_in_ext, tm):
    n = x.shape[0]
    widths = (512, 1024, 2 * LANE, 640, MIX)
    return pl.pallas_call(
        _pre_kernel,
        out_shape=[jax.ShapeDtypeStruct((n, w), dt) for w, dt in zip(widths, (BF16, F32, F32, BF16, F32))],
        grid=(n // tm,),
        in_specs=[pl.BlockSpec((tm, D_MODEL), lambda i: (i, 0)),
                  pl.BlockSpec((1, 6, D_MODEL), lambda i: (row_fn(i, tm), 0, 0)),
                  pl.BlockSpec((1, D_MODEL), lambda i: (0, 0)),
                  pl.BlockSpec((D_MODEL, Z_COLS), lambda i: (0, 0))],
        out_specs=[pl.BlockSpec((tm, w), lambda i: (i, 0)) for w in widths],
        compiler_params=_cp("parallel"),
        name="pre_mixer",
    )(x, mod, norm_g, w_in_ext)


_CONV_PAD = 16
_CONV_TT = 128


def _conv_kernel(za_ref, w_ref, b_ref, lg_ref, lb_ref, o_ref, hp_ref, sh_ref):
    t_len = o_ref.shape[1]
    u = za_ref[0].astype(F32)
    hp_ref[0:_CONV_PAD, :] = jnp.zeros((_CONV_PAD, MIX), F32)
    hp_ref[_CONV_PAD + t_len:2 * _CONV_PAD + t_len, :] = jnp.zeros((_CONV_PAD, MIX), F32)
    hp_ref[_CONV_PAD:_CONV_PAD + t_len, :] = u[:, :MIX] * _sigmoid(u[:, MIX:])
    span = t_len + 2 * _CONV_PAD - 8
    for off in range(8):
        sh_ref[off, 0:span, :] = hp_ref[off:off + span, :]
    w = w_ref[...]
    half = CONV_WIDTH // 2
    for t0 in range(0, t_len, _CONV_TT):
        acc = jnp.zeros((_CONV_TT, MIX), F32) + b_ref[...]
        for k in range(CONV_WIDTH):
            start = t0 + _CONV_PAD - half + k
            aligned = start // 8 * 8
            acc = acc + sh_ref[start - aligned, aligned:aligned + _CONV_TT, :] * w[k:k + 1, :]
        mu = jnp.mean(acc, axis=-1, keepdims=True)
        cen = acc - mu
        var = jnp.mean(cen * cen, axis=-1, keepdims=True)
        yn = cen * lax.rsqrt(var + EPS) * lg_ref[...] + lb_ref[...]
        o_ref[0, t0:t0 + _CONV_TT, :] = (yn * _sigmoid(yn)).astype(o_ref.dtype)


def _conv_module(za, w, b, ln_g, ln_b):
    bsz, t_len, _ = za.shape
    vec = pl.BlockSpec((1, MIX), lambda i: (0, 0))
    return pl.pallas_call(
        _conv_kernel,
        out_shape=jax.ShapeDtypeStruct((bsz, t_len, MIX), BF16),
        grid=(bsz,),
        in_specs=[pl.BlockSpec((1, t_len, 2 * MIX), lambda i: (i, 0, 0)),
                  pl.BlockSpec((32, MIX), lambda i: (0, 0)), vec, vec, vec],
        out_specs=pl.BlockSpec((1, t_len, MIX), lambda i: (i, 0, 0)),
        scratch_shapes=[pltpu.VMEM((t_len + 2 * _CONV_PAD, MIX), F32),
                        pltpu.VMEM((8, t_len + 2 * _CONV_PAD, MIX), F32)],
        compiler_params=_cp("parallel"),
        name="conv_module",
    )(za, w, b, ln_g, ln_b)


def _log_sigmoid(x):
    return jnp.minimum(x, 0.0) - jnp.log(1.0 + jnp.exp(-jnp.abs(x)))


def _split3_dot(tri, x):
    hi = x.astype(BF16)
    r1 = x - hi.astype(F32)
    mid = r1.astype(BF16)
    lo = (r1 - mid.astype(F32)).astype(BF16)
    dot = lambda v: jnp.dot(tri, v, preferred_element_type=F32)
    return dot(hi) + dot(mid) + dot(lo)


def _chunk_rows(c):
    if isinstance(c, int):
        return slice(c * CHUNK, (c + 1) * CHUNK)
    return pl.ds(pl.multiple_of(c * CHUNK, CHUNK), CHUNK)


def _loop(n, body, init):
    if n <= _STATIC_CHUNKS:
        for i in range(n):
            init = body(i, init)
        return init
    return lax.fori_loop(0, n, body, init)


def _split2(x):
    hi = x.astype(BF16)
    return hi, (x - hi.astype(F32)).astype(BF16)


def _mlstm_chunk(zb_ref, zg_ref, gb_ref, qt_sc, qbd_sc, vt_sc, vbd_sc, ct_sc, nb_sc, c, direction, ms):
    L = CHUNK
    rows = _chunk_rows(c)
    si = lax.broadcasted_iota(jnp.int32, (L, L), 0)
    ti = lax.broadcasted_iota(jnp.int32, (L, L), 1)
    if direction == 0:
        valid = si <= ti
        tri = jnp.where(ti <= si, 1.0, 0.0).astype(BF16)
        last = L - 1
    else:
        valid = si >= ti
        tri = jnp.where(ti >= si, 1.0, 0.0).astype(BF16)
        last = 0
    g_in = zg_ref[0, rows, 0:LANE] + gb_ref[:, 0:LANE]
    g_f = zg_ref[0, rows, LANE:2 * LANE] + gb_ref[:, LANE:2 * LANE]
    bc = _split3_dot(tri, _log_sigmoid(g_f))
    r = g_in - bc
    g_in_t = g_in.T
    bct = bc.T
    k = zb_ref[0, rows, MIX:2 * MIX]
    k_hi, k_lo = _split2(k)
    qt = qt_sc[c]
    st_all = jnp.dot(k_hi, qbd_sc[c], preferred_element_type=F32)
    sts, inters, floors, colsums, ws, decays, new_m = [], [], [], [], [], [], []
    for h in range(HEADS):
        j = direction * HEADS + h
        i_row = g_in_t[j:j + 1, :]
        b_row = bct[j:j + 1, :]
        m_prev = ms[h]
        rb = jnp.where(valid, jnp.broadcast_to(r[:, j:j + 1], (L, L)), -jnp.inf)
        c_row = jnp.maximum(m_prev, jnp.max(rb, axis=0, keepdims=True))
        st = st_all[:, h * L:(h + 1) * L] * jnp.exp(rb - c_row)
        sts.append(st.astype(BF16))
        colsums.append(jnp.sum(st, axis=0, keepdims=True))
        inters.append(jnp.exp(m_prev - c_row))
        floors.append(jnp.exp(-(b_row + c_row)))
        b_last = b_row[:, last:last + 1]
        lw = b_last - b_row + i_row
        m_new = jnp.maximum(b_last + m_prev, jnp.max(lw, axis=-1, keepdims=True))
        decays.append(jnp.exp(b_last + m_prev - m_new))
        ws.append(jnp.exp(lw - m_new))
        new_m.append(m_new)
    per_head_rows = lambda vs: jnp.concatenate([jnp.broadcast_to(v, (HD, L)) for v in vs], axis=0)
    ct = ct_sc[direction]
    nb = nb_sc[direction]
    num = (per_head_rows(inters) * jnp.dot(ct.astype(BF16), qt, preferred_element_type=F32)
           + jnp.dot(vbd_sc[c], jnp.concatenate(sts, axis=0), preferred_element_type=F32))
    n_hi, n_lo = _split2(nb)
    qn = jnp.dot(n_hi, qt, preferred_element_type=F32) + jnp.dot(n_lo, qt, preferred_element_type=F32)
    dens = [jnp.maximum(jnp.abs(inters[h] * qn[h:h + 1, :] + colsums[h]), floors[h]) for h in range(HEADS)]
    ht = num / per_head_rows(dens)
    lane = lax.broadcasted_iota(jnp.int32, (1, MIX), 1) // HD
    dec_row = sum(jnp.where(lane == h, decays[h], 0.0) for h in range(HEADS))
    upd = jnp.dot((vt_sc[c] * per_head_rows(ws)).astype(BF16), k_hi, preferred_element_type=F32)
    same_head = (lax.broadcasted_iota(jnp.int32, (MIX, MIX), 0) // HD
                 == lax.broadcasted_iota(jnp.int32, (MIX, MIX), 1) // HD)
    ct_sc[direction] = jnp.where(same_head, ct * dec_row + upd, 0.0)
    w_hi, w_lo = _split2(jnp.concatenate(ws + [jnp.zeros((8 - HEADS, L), F32)], axis=0))
    wk = (jnp.dot(w_hi, k_hi, preferred_element_type=F32) + jnp.dot(w_lo, k_hi, preferred_element_type=F32)
          + jnp.dot(w_hi, k_lo, preferred_element_type=F32))
    own = lax.broadcasted_iota(jnp.int32, (8, MIX), 0) == lax.broadcasted_iota(jnp.int32, (8, MIX), 1) // HD
    nb_sc[direction] = jnp.where(own, nb * dec_row + wk, 0.0)
    return ht, tuple(new_m)


def _mlstm_kernel(zb_ref, zg_ref, gb_ref, ng_ref, c0_ref, n0_ref, m0_ref,
                  y_ref, c_ref, n_ref, m_ref, qt_sc, qbd_sc, vt_sc, vbd_sc, ht_sc, ct_sc, nb_sc):
    t_len = y_ref.shape[1]
    nc = t_len // CHUNK
    L = CHUNK
    qbd_sc[...] = jnp.zeros_like(qbd_sc)
    vbd_sc[...] = jnp.zeros_like(vbd_sc)
    ct_sc[...] = jnp.zeros_like(ct_sc)
    nb_sc[...] = jnp.zeros_like(nb_sc)

    def transpose_chunk(c, carry):
        rows = _chunk_rows(c)
        qt = (zb_ref[0, rows, 0:MIX] * (HD ** -0.5)).T.astype(BF16)
        vt = zb_ref[0, rows, 2 * MIX:3 * MIX].T
        qt_sc[c] = qt
        vt_sc[c] = vt
        for h in range(HEADS):
            sl = slice(h * HD, (h + 1) * HD)
            qbd_sc[c, sl, h * L:(h + 1) * L] = qt[sl, :]
            vbd_sc[c, sl, h * L:(h + 1) * L] = vt[sl, :].astype(BF16)
        return carry

    _loop(nc, transpose_chunk, 0)

    for j in range(2 * HEADS):
        d, h = divmod(j, HEADS)
        sl = slice(h * HD, (h + 1) * HD)
        ct_sc[d, sl, sl] = c0_ref[0, j]
        nb_sc[d, h:h + 1, sl] = n0_ref[0, j:j + 1, :]
    m_init = tuple(tuple(m0_ref[0, j:j + 1, 0:1] for j in range(d * HEADS, (d + 1) * HEADS)) for d in range(2))

    def scan_body(i, ms):
        cb = nc - 1 - i
        scr = (zb_ref, zg_ref, gb_ref, qt_sc, qbd_sc, vt_sc, vbd_sc, ct_sc, nb_sc)
        ht_sc[0, i], ms_f = _mlstm_chunk(*scr, i, 0, ms[0])
        ht_sc[1, cb], ms_b = _mlstm_chunk(*scr, cb, 1, ms[1])
        return ms_f, ms_b

    m_fin = _loop(nc, scan_body, m_init)
    for j in range(2 * HEADS):
        d, h = divmod(j, HEADS)
        sl = slice(h * HD, (h + 1) * HD)
        c_ref[0, j] = ct_sc[d, sl, sl]
        n_ref[0, j:j + 1, :] = nb_sc[d, h:h + 1, sl]
        m_ref[0, j:j + 1, :] = jnp.broadcast_to(m_fin[d][h], (1, LANE))

    def out_body(c, carry):
        rows = _chunk_rows(c)
        normed = []
        for h in range(HEADS):
            tot = ht_sc[0, c, h * HD:(h + 1) * HD, :] + ht_sc[1, c, h * HD:(h + 1) * HD, :]
            normed.append(tot * lax.rsqrt(jnp.mean(tot * tot, axis=0, keepdims=True) + EPS))
        hn = jnp.concatenate(normed, axis=0).T
        y_ref[0, rows, :] = (hn * ng_ref[...] * _sigmoid(zb_ref[0, rows, 3 * MIX:4 * MIX])).astype(y_ref.dtype)
        return carry

    _loop(nc, out_body, 0)


def _mlstm_mixer(zb, zg, gate_b, norm_g, c0, n0, m0):
    bsz, t_len, _ = zb.shape
    nd = 2 * HEADS
    return pl.pallas_call(
        _mlstm_kernel,
        out_shape=[jax.ShapeDtypeStruct((bsz, t_len, MIX), BF16),
                   jax.ShapeDtypeStruct((bsz, nd, HD, HD), F32),
                   jax.ShapeDtypeStruct((bsz, nd, HD), F32),
                   jax.ShapeDtypeStruct((bsz, nd, LANE), F32)],
        grid=(bsz,),
        in_specs=[pl.BlockSpec((1, t_len, 4 * MIX), lambda i: (i, 0, 0)),
                  pl.BlockSpec((1, t_len, 2 * LANE), lambda i: (i, 0, 0)),
                  pl.BlockSpec((1, 2 * LANE), lambda i: (0, 0)),
                  pl.BlockSpec((1, MIX), lambda i: (0, 0)),
                  pl.BlockSpec((1, nd, HD, HD), lambda i: (i, 0, 0, 0)),
                  pl.BlockSpec((1, nd, HD), lambda i: (i, 0, 0)),
                  pl.BlockSpec((1, nd, LANE), lambda i: (i, 0, 0))],
        out_specs=[pl.BlockSpec((1, t_len, MIX), lambda i: (i, 0, 0)),
                   pl.BlockSpec((1, nd, HD, HD), lambda i: (i, 0, 0, 0)),
                   pl.BlockSpec((1, nd, HD), lambda i: (i, 0, 0)),
                   pl.BlockSpec((1, nd, LANE), lambda i: (i, 0, 0))],
        scratch_shapes=[pltpu.VMEM((t_len // CHUNK, MIX, CHUNK), BF16),
                        pltpu.VMEM((t_len // CHUNK, MIX, HEADS * CHUNK), BF16),
                        pltpu.VMEM((t_len // CHUNK, MIX, CHUNK), F32),
                        pltpu.VMEM((t_len // CHUNK, MIX, HEADS * CHUNK), BF16),
                        pltpu.VMEM((2, t_len // CHUNK, MIX, CHUNK), F32),
                        pltpu.VMEM((2, MIX, MIX), F32),
                        pltpu.VMEM((2, 8, MIX), F32)],
        compiler_params=_cp("parallel"),
        name="mlstm_mixer",
    )(zb, zg, gate_b, norm_g, c0, n0, m0)


_ATT_TQ = 512


def _mla_kernel(*refs, past):
    if past:
        (zc_ref, cos_ref, sin_ref, gq_ref, gkv_ref, wqa_ref, wqb_ref, wk_ref, wv_ref,
         pckv_ref, pkr_ref, y_ref, ckv_ref, q_sc, k_sc, v_sc) = refs
    else:
        (zc_ref, cos_ref, sin_ref, gq_ref, gkv_ref, wqa_ref, wqb_ref, wk_ref, wv_ref,
         y_ref, ckv_ref, q_sc, k_sc, v_sc) = refs
    t_len = y_ref.shape[1]
    cosf = cos_ref[...]
    sinf = sin_ref[...]
    zc = lambda a, b: zc_ref[0, :, a:b].astype(F32)
    cq = _rms(zc(0, Q_RANK)) * gq_ref[...]
    ckv = _rms(zc(Q_RANK, Q_RANK + KV_RANK)) * gkv_ref[...]
    ckv_ref[0] = ckv
    kr = (zc(Q_RANK + KV_RANK, Q_RANK + KV_RANK + LANE) * cosf
          + zc(Q_RANK + KV_RANK + LANE, Q_RANK + KV_RANK + 2 * LANE) * sinf)
    cqb = cq.astype(BF16)
    ckvb = ckv.astype(BF16)
    pair = lambda a: jnp.concatenate([a, a], axis=-1)
    cos2, sin2, kr2 = pair(cosf), pair(sinf), pair(kr)
    for hp in range(HEADS // 2):
        sl = slice(hp * 2 * LANE, (hp + 1) * 2 * LANE)
        qa = jnp.dot(cqb, wqa_ref[:, sl], preferred_element_type=F32)
        qb = jnp.dot(cqb, wqb_ref[:, sl], preferred_element_type=F32)
        q_sc[:, sl] = (qa * cos2 + qb * sin2).astype(BF16)
        kn = jnp.dot(ckvb, wk_ref[:, sl], preferred_element_type=F32)
        k_sc[past:past + t_len, sl] = (kn + kr2).astype(BF16)
        v_sc[past:past + t_len, sl] = jnp.dot(ckvb, wv_ref[:, sl], preferred_element_type=F32).astype(BF16)
    if past:
        pckv = pckv_ref[0].astype(BF16)
        pkr2 = pair(pkr_ref[0])
        for hp in range(HEADS // 2):
            sl = slice(hp * 2 * LANE, (hp + 1) * 2 * LANE)
            k_sc[0:past, sl] = (jnp.dot(pckv, wk_ref[:, sl], preferred_element_type=F32) + pkr2).astype(BF16)
            v_sc[0:past, sl] = jnp.dot(pckv, wv_ref[:, sl], preferred_element_type=F32).astype(BF16)

    tq = min(_ATT_TQ, t_len)

    def q_block(i, carry):
        rows = pl.ds(pl.multiple_of(i * tq, tq), tq)
        outs = []
        for h in range(HEADS):
            sl = slice(h * LANE, (h + 1) * LANE)
            s = lax.dot_general(q_sc[rows, sl], k_sc[:, sl], (((1,), (1,)), ((), ())),
                                preferred_element_type=F32)
            p = jnp.exp(s - jnp.max(s, axis=-1, keepdims=True))
            o = jnp.dot(p.astype(BF16), v_sc[:, sl], preferred_element_type=F32)
            outs.append(o / jnp.sum(p, axis=-1, keepdims=True))
        y_ref[0, rows, 0:LANE] = (outs[0] + outs[1]).astype(y_ref.dtype)
        y_ref[0, rows, LANE:2 * LANE] = (outs[2] + outs[3]).astype(y_ref.dtype)
        return carry

    assert t_len % tq == 0
    n_blocks = t_len // tq
    if n_blocks % 2 == 0:
        lax.fori_loop(0, n_blocks // 2, lambda i, c: q_block(2 * i + 1, q_block(2 * i, c)), 0)
    else:
        lax.fori_loop(0, n_blocks, q_block, 0)


def _mla_mixer(zc, cosf, sinf, gq, gkv, wqa, wqb, wk, wv, past_ckv=None, past_kr=None):
    bsz, t_len, zw = zc.shape
    past = 0 if past_ckv is None else past_ckv.shape[1]
    full = lambda a: pl.BlockSpec(a.shape, lambda i: (0,) * a.ndim)
    args = [zc, cosf, sinf, gq, gkv, wqa, wqb, wk, wv]
    in_specs = [pl.BlockSpec((1, t_len, zw), lambda i: (i, 0, 0))] + [full(a) for a in args[1:]]
    if past:
        args += [past_ckv, past_kr]
        in_specs += [pl.BlockSpec((1, past, KV_RANK), lambda i: (i, 0, 0)),
                     pl.BlockSpec((1, past, LANE), lambda i: (i, 0, 0))]
    wide = HEADS * LANE
    return pl.pallas_call(
        functools.partial(_mla_kernel, past=past),
        out_shape=[jax.ShapeDtypeStruct((bsz, t_len, MIX), BF16),
                   jax.ShapeDtypeStruct((bsz, t_len, KV_RANK), F32)],
        grid=(bsz,),
        in_specs=in_specs,
        out_specs=[pl.BlockSpec((1, t_len, MIX), lambda i: (i, 0, 0)),
                   pl.BlockSpec((1, t_len, KV_RANK), lambda i: (i, 0, 0))],
        scratch_shapes=[pltpu.VMEM((t_len, wide), BF16),
                        pltpu.VMEM((past + t_len, wide), BF16),
                        pltpu.VMEM((past + t_len, wide), BF16)],
        compiler_params=_cp("parallel"),
        name="mla_mixer",
    )(*args)


_S5_SEQ = 8
_S5_TT = 256
_S5_PARTS = 8


def _s5_kernel(zd_ref, bd_ref, cd_ref, are_ref, aim_ref, s0re_ref, s0im_ref,
               y_ref, fre_ref, fim_ref, bre_sc, bim_sc, sre_sc, sim_sc):
    direction = pl.program_id(0)
    k = pl.program_id(2)
    tt = _S5_TT
    rows = _S5_SEQ * tt

    @pl.when(k == 0)
    def _():
        sre_sc[...] = s0re_ref[0]
        sim_sc[...] = s0im_ref[0]

    part_steps = tt // _S5_PARTS
    part_rows = part_steps * _S5_SEQ

    def run(backward):
        parts = list(range(_S5_PARTS))[::-1] if backward else list(range(_S5_PARTS))
        a_re = jnp.broadcast_to(are_ref[0], (_S5_SEQ, S5_STATES))
        a_im = jnp.broadcast_to(aim_ref[0], (_S5_SEQ, S5_STATES))
        for p in parts:
            r = slice(p * part_rows, (p + 1) * part_rows)
            u = zd_ref[p * part_steps:(p + 1) * part_steps].reshape(part_rows, MIX).astype(BF16)
            bre_sc[r, :] = jnp.dot(u, bd_ref[0, :, 0:S5_STATES], preferred_element_type=F32)
            bim_sc[r, :] = jnp.dot(u, bd_ref[0, :, S5_STATES:2 * S5_STATES], preferred_element_type=F32)
        s_re, s_im = sre_sc[...], sim_sc[...]
        for p in parts:
            steps = range(p * part_steps, (p + 1) * part_steps)
            for t in (reversed(steps) if backward else steps):
                sel = slice(t * _S5_SEQ, (t + 1) * _S5_SEQ)
                s_re, s_im = (a_re * s_re - a_im * s_im + bre_sc[sel, :],
                              a_re * s_im + a_im * s_re + bim_sc[sel, :])
                bre_sc[sel, :] = s_re
                bim_sc[sel, :] = s_im
            r = slice(p * part_rows, (p + 1) * part_rows)
            y = (jnp.dot(bre_sc[r, :].astype(BF16), cd_ref[0, 0:S5_STATES, :], preferred_element_type=F32)
                 + jnp.dot(bim_sc[r, :].astype(BF16), cd_ref[0, S5_STATES:2 * S5_STATES, :],
                           preferred_element_type=F32))
            y_ref[0, p * part_steps:(p + 1) * part_steps] = y.reshape(part_steps, _S5_SEQ, MIX)
        sre_sc[...] = s_re
        sim_sc[...] = s_im

    pl.when(direction == 0)(lambda: run(False))
    pl.when(direction == 1)(lambda: run(True))

    @pl.when(k == pl.num_programs(2) - 1)
    def _():
        fre_ref[0] = sre_sc[...]
        fim_ref[0] = sim_sc[...]


def _s5_scan(zd, bd, cd, ab_re, ab_im, s0_re, s0_im):
    t_len, bsz, _ = zd.shape
    nt = t_len // _S5_TT
    tile = lambda d, k: k + d * (nt - 1 - 2 * k)
    st = pl.BlockSpec((1, _S5_SEQ, S5_STATES), lambda d, g, k: (d, g, 0))
    return pl.pallas_call(
        _s5_kernel,
        out_shape=[jax.ShapeDtypeStruct((2, t_len, bsz, MIX), F32),
                   jax.ShapeDtypeStruct((2, bsz, S5_STATES), F32),
                   jax.ShapeDtypeStruct((2, bsz, S5_STATES), F32)],
        grid=(2, bsz // _S5_SEQ, nt),
        in_specs=[pl.BlockSpec((_S5_TT, _S5_SEQ, MIX), lambda d, g, k: (tile(d, k), g, 0)),
                  pl.BlockSpec((1, MIX, 2 * S5_STATES), lambda d, g, k: (d, 0, 0)),
                  pl.BlockSpec((1, 2 * S5_STATES, MIX), lambda d, g, k: (d, 0, 0)),
                  pl.BlockSpec((1, 1, S5_STATES), lambda d, g, k: (d, 0, 0)),
                  pl.BlockSpec((1, 1, S5_STATES), lambda d, g, k: (d, 0, 0)),
                  st, st],
        out_specs=[pl.BlockSpec((1, _S5_TT, _S5_SEQ, MIX), lambda d, g, k: (d, tile(d, k), g, 0)),
                   st, st],
        scratch_shapes=[pltpu.VMEM((_S5_SEQ * _S5_TT, S5_STATES), F32),
                        pltpu.VMEM((_S5_SEQ * _S5_TT, S5_STATES), F32),
                        pltpu.VMEM((_S5_SEQ, S5_STATES), F32),
                        pltpu.VMEM((_S5_SEQ, S5_STATES), F32)],
        compiler_params=_cp("parallel", "parallel", "arbitrary"),
        name="s5_scan",
    )(zd, bd, cd, ab_re, ab_im, s0_re, s0_im)


def _gelu_tanh(x):
    return 0.5 * x * (1.0 + jnp.tanh(math.sqrt(2.0 / math.pi) * (x + 0.044715 * x * x * x)))


def _route(logits):
    lane = lax.broadcasted_iota(jnp.int32, logits.shape, 1).astype(F32)
    big = float(LANE)
    neg = -jnp.inf
    g_mask = (lane >= N_EXPERT) & (lane < N_EXPERT + N_GROUP)
    gl = jnp.where(g_mask, logits, neg)
    g_max = jnp.max(gl, axis=-1, keepdims=True)
    g_idx = jnp.min(jnp.where(gl == g_max, lane, big), axis=-1, keepdims=True) - N_EXPERT
    g_sel = 1.0 / jnp.sum(jnp.where(g_mask, jnp.exp(logits - g_max), 0.0), axis=-1, keepdims=True)
    lo = g_idx * PER_GROUP
    el = jnp.where((lane >= lo) & (lane < lo + PER_GROUP), logits, neg)
    v1 = jnp.max(el, axis=-1, keepdims=True)
    i1 = jnp.min(jnp.where(el == v1, lane, big), axis=-1, keepdims=True)
    el2 = jnp.where(lane == i1, neg, el)
    v2 = jnp.max(el2, axis=-1, keepdims=True)
    i2 = jnp.min(jnp.where(el2 == v2, lane, big), axis=-1, keepdims=True)
    e2 = jnp.exp(v2 - v1)
    w1 = g_sel / (1.0 + e2)
    return jnp.where(lane == i1, w1, jnp.where(lane == i2, w1 * e2, 0.0))


def _post_kernel(x_ref, ya_ref, yb_ref, yc_ref, ys_ref, zd_ref, mod_ref, wo_ref, d_ref, wglu_ref, bglu_ref,
                 g2_ref, wr_ref, br_ref, x1_ref, h2_ref, gate_ref):
    m = mod_ref[0]
    zd = zd_ref[...]
    ys = ys_ref[0] + ys_ref[1] + d_ref[...] * zd
    gl = _bdot(_gelu_tanh(ys), wglu_ref[...]) + bglu_ref[...]
    yd = gl[:, 0:MIX] * _sigmoid(gl[:, MIX:2 * MIX])
    mix = (_bdot(ya_ref[...], wo_ref[0:MIX, :]) + _bdot(yb_ref[...], wo_ref[MIX:2 * MIX, :])
           + _bdot(yc_ref[...], wo_ref[2 * MIX:3 * MIX, :]) + _bdot(yd, wo_ref[3 * MIX:4 * MIX, :]))
    x1 = x_ref[...] + m[2:3] * mix
    x1_ref[...] = x1
    h2 = _rms(x1) * g2_ref[...] * (1.0 + m[4:5]) + m[3:4]
    h2b = h2.astype(BF16)
    h2_ref[...] = h2b
    gate_ref[...] = _route(jnp.dot(h2b, wr_ref[...], preferred_element_type=F32) + br_ref[...])


def _post_mixer(x, ya, yb, yc, ys, zd, mod, row_fn, w_out, s5_d, w_glu, b_glu, norm2_g, w_router, b_router, tm):
    n = x.shape[0]
    tok = lambda w: pl.BlockSpec((tm, w), lambda i: (i, 0))
    full = lambda a: pl.BlockSpec(a.shape, lambda i: (0,) * a.ndim)
    return pl.pallas_call(
        _post_kernel,
        out_shape=[jax.ShapeDtypeStruct((n, D_MODEL), F32),
                   jax.ShapeDtypeStruct((n, D_MODEL), BF16),
                   jax.ShapeDtypeStruct((n, LANE), F32)],
        grid=(n // tm,),
        in_specs=[tok(D_MODEL), tok(MIX), tok(MIX), tok(MIX),
                  pl.BlockSpec((2, tm, MIX), lambda i: (0, i, 0)), tok(MIX),
                  pl.BlockSpec((1, 6, D_MODEL), lambda i: (row_fn(i, tm), 0, 0)),
                  full(w_out), full(s5_d), full(w_glu), full(b_glu), full(norm2_g),
                  full(w_router), full(b_router)],
        out_specs=[tok(D_MODEL), tok(D_MODEL), tok(LANE)],
        compiler_params=_cp("parallel"),
        name="post_mixer",
    )(x, ya, yb, yc, ys, zd, mod, w_out, s5_d, w_glu, b_glu, norm2_g, w_router, b_router)


_MOE_EPS = 4
_MOE_TM = 1024


_MOE_ALIGN = 16
_MOE_ROWS = 2 * _MOE_TM + N_EXPERT * _MOE_ALIGN
_MOE_CH = 128
_MOE_GB = 512
_MOE_CB = 256


def _moe_kernel(h_ref, gate_ref, x_ref, mod_ref, w1_ref, w3_ref, w2_ref, o_ref,
                xs_sc, ys_sc, col_sc, row_sc, start_sm, rows_sm, done_sm):
    j = pl.program_id(1)
    tm = h_ref.shape[0]

    @pl.when(j == 0)
    def _dispatch():
        gate = gate_ref[...]
        cnt_row = jnp.sum(jnp.where(gate != 0.0, 1.0, 0.0), axis=0, keepdims=True)
        units_row = jnp.floor((cnt_row + (_MOE_ALIGN - 1)) * (1.0 / _MOE_ALIGN))
        li = lax.broadcasted_iota(jnp.int32, (LANE, LANE), 0)
        lj = lax.broadcasted_iota(jnp.int32, (LANE, LANE), 1)
        before = jnp.where(li < lj, 1.0, 0.0).astype(BF16)
        start_row = jnp.dot(jnp.broadcast_to(units_row, (8, LANE)).astype(BF16), before,
                            preferred_element_type=F32) * _MOE_ALIGN
        start_i = start_row.astype(jnp.int32)
        cnt_i = cnt_row.astype(jnp.int32)
        for e in range(N_EXPERT):
            start_sm[e] = start_i[0, e]
            rows_sm[e] = cnt_i[0, e]
        gt = gate.T[0:N_EXPERT, :]
        chosen = gt != 0.0
        ones = jnp.where(chosen, 1.0, 0.0)
        cnt = jnp.sum(ones, axis=-1, keepdims=True)
        units = jnp.floor((cnt + (_MOE_ALIGN - 1)) * (1.0 / _MOE_ALIGN))
        ei = lax.broadcasted_iota(jnp.int32, (N_EXPERT, N_EXPERT), 0)
        ej = lax.broadcasted_iota(jnp.int32, (N_EXPERT, N_EXPERT), 1)
        start = jnp.dot(jnp.where(ej < ei, 1.0, 0.0).astype(BF16),
                        jnp.broadcast_to(units, (N_EXPERT, LANE)).astype(BF16),
                        preferred_element_type=F32)[:, 0:1] * _MOE_ALIGN
        ti = lax.broadcasted_iota(jnp.int32, (tm, tm), 0)
        tj = lax.broadcasted_iota(jnp.int32, (tm, tm), 1)
        earlier = jnp.where(ti < tj, 1.0, 0.0).astype(BF16)
        pos = start + jnp.dot(ones.astype(BF16), earlier, preferred_element_type=F32)
        p_a = jnp.min(jnp.where(chosen, pos, float(_MOE_ROWS)), axis=0, keepdims=True)
        p_b = jnp.max(jnp.where(chosen, pos, -1.0), axis=0, keepdims=True)
        g_a = jnp.sum(jnp.where(chosen & (pos == p_a), gt, 0.0), axis=0, keepdims=True)
        g_b = jnp.sum(jnp.where(chosen & (pos == p_b) & (p_b != p_a), gt, 0.0), axis=0, keepdims=True)
        row_sc[...] = jnp.concatenate([p_a, p_b, jnp.zeros((6, tm), F32)], axis=0)
        packed = jnp.concatenate([p_a, p_b, g_a, g_b, jnp.zeros((LANE - 4, tm), F32)], axis=0)
        col_sc[...] = packed.T
        xs_sc[_MOE_ROWS:_MOE_ROWS + _MOE_CH, :] = jnp.zeros((_MOE_CH, D_MODEL), BF16)
        ys_sc[...] = jnp.zeros_like(ys_sc)
        o_ref[...] = x_ref[...]
        done_sm[0] = 0
        done_sm[1] = 0

    n_blocks = _MOE_ROWS // _MOE_GB
    e_last = j * _MOE_EPS + (_MOE_EPS - 1)
    seg_end = start_sm[e_last] + rows_sm[e_last]
    last_step = j == pl.num_programs(1) - 1

    def gather_block(b, carry):
        r0 = pl.multiple_of(b * _MOE_GB, _MOE_GB)
        ri = (lax.broadcasted_iota(jnp.int32, (_MOE_GB, tm), 0) + r0).astype(F32)
        sel = jnp.where((ri == row_sc[0:1, :]) | (ri == row_sc[1:2, :]), 1.0, 0.0).astype(BF16)
        xs_sc[pl.ds(r0, _MOE_GB), :] = jnp.dot(sel, h_ref[...], preferred_element_type=F32).astype(BF16)
        return carry

    gathered = jnp.minimum((seg_end + _MOE_CH + _MOE_GB - 1) // _MOE_GB, n_blocks)
    lax.fori_loop(done_sm[0], gathered, gather_block, 0)
    done_sm[0] = gathered

    def expert_rows(el, i):
        rows = pl.ds(pl.multiple_of(start_sm[j * _MOE_EPS + el] + i * _MOE_CH, _MOE_ALIGN), _MOE_CH)
        xs = xs_sc[rows, :]
        h1 = jnp.dot(xs, w1_ref[el], preferred_element_type=F32)
        h3 = jnp.dot(xs, w3_ref[el], preferred_element_type=F32)
        return rows, jnp.dot((h1 * _sigmoid(h1) * h3).astype(BF16), w2_ref[el], preferred_element_type=F32)

    first = [expert_rows(el, 0) for el in range(_MOE_EPS)]
    for rows, y in first:
        ys_sc[rows, :] = y.astype(BF16)
    for el in range(_MOE_EPS):
        n_rows = rows_sm[j * _MOE_EPS + el]

        def chunk(i, carry, el=el, n_rows=n_rows):
            rows, y = expert_rows(el, i)
            live = lax.broadcasted_iota(jnp.int32, (_MOE_CH, 1), 0) + i * _MOE_CH < n_rows
            ys_sc[rows, :] = jnp.where(live, y, ys_sc[rows, :].astype(F32)).astype(BF16)
            return carry

        lax.fori_loop(1, (n_rows + _MOE_CH - 1) // _MOE_CH, chunk, 0)

    g2 = mod_ref[0][5:6]

    def combine_block(b, carry):
        r0 = pl.multiple_of(b * _MOE_GB, _MOE_GB)
        y = ys_sc[pl.ds(r0, _MOE_GB), :]
        for t0 in range(0, tm, _MOE_CB):
            c = col_sc[t0:t0 + _MOE_CB, :]
            ri = (lax.broadcasted_iota(jnp.int32, (_MOE_CB, _MOE_GB), 1) + r0).astype(F32)
            w = (jnp.where(ri == c[:, 0:1], c[:, 2:3], 0.0)
                 + jnp.where(ri == c[:, 1:2], c[:, 3:4], 0.0)).astype(BF16)
            o_ref[t0:t0 + _MOE_CB, :] += g2 * jnp.dot(w, y, preferred_element_type=F32)
        return carry

    final_rows = (seg_end + _MOE_ALIGN - 1) // _MOE_ALIGN * _MOE_ALIGN
    combined = jnp.where(last_step, n_blocks, final_rows // _MOE_GB)
    lax.fori_loop(done_sm[1], combined, combine_block, 0)
    done_sm[1] = combined


def _moe(h2, gate, x1, mod, row_fn, w1, w3, w2, layer, tm):
    n = h2.shape[0]
    first_blk = layer * (N_EXPERT // _MOE_EPS)
    return pl.pallas_call(
        _moe_kernel,
        out_shape=jax.ShapeDtypeStruct((n, D_MODEL), F32),
        grid=(n // tm, N_EXPERT // _MOE_EPS),
        in_specs=[pl.BlockSpec((tm, D_MODEL), lambda i, j: (i, 0)),
                  pl.BlockSpec((tm, LANE), lambda i, j: (i, 0)),
                  pl.BlockSpec((tm, D_MODEL), lambda i, j: (i, 0)),
                  pl.BlockSpec((1, 6, D_MODEL), lambda i, j: (row_fn(i, tm), 0, 0)),
                  pl.BlockSpec((_MOE_EPS, D_MODEL, MOE_FF), lambda i, j: (first_blk + j, 0, 0)),
                  pl.BlockSpec((_MOE_EPS, D_MODEL, MOE_FF), lambda i, j: (first_blk + j, 0, 0)),
                  pl.BlockSpec((_MOE_EPS, MOE_FF, D_MODEL), lambda i, j: (first_blk + j, 0, 0))],
        out_specs=pl.BlockSpec((tm, D_MODEL), lambda i, j: (i, 0)),
        scratch_shapes=[pltpu.VMEM((_MOE_ROWS + _MOE_CH, D_MODEL), BF16),
                        pltpu.VMEM((_MOE_ROWS + _MOE_CH, D_MODEL), BF16),
                        pltpu.VMEM((tm, LANE), F32),
                        pltpu.VMEM((8, tm), F32),
                        pltpu.SMEM((N_EXPERT,), jnp.int32),
                        pltpu.SMEM((N_EXPERT,), jnp.int32),
                        pltpu.SMEM((2,), jnp.int32)],
        compiler_params=_cp("parallel", "arbitrary"),
        name="moe",
    )(h2, gate, x1, mod, w1, w3, w2)


def _final_kernel(x_ref, g_ref, o_ref):
    o_ref[...] = _rms(x_ref[...]) * g_ref[...]


def _final_norm(x, g, tm):
    n = x.shape[0]
    return pl.pallas_call(
        _final_kernel,
        out_shape=jax.ShapeDtypeStruct((n, D_MODEL), F32),
        grid=(n // tm,),
        in_specs=[pl.BlockSpec((tm, D_MODEL), lambda i: (i, 0)),
                  pl.BlockSpec((1, D_MODEL), lambda i: (0, 0))],
        out_specs=pl.BlockSpec((tm, D_MODEL), lambda i: (i, 0)),
        compiler_params=_cp("parallel"),
        name="final_norm",
    )(x, g)


def _rot_cols(w):
    q = ROPE // 4
    return jnp.concatenate([-w[..., q:2 * q], w[..., 0:q], -w[..., 3 * q:4 * q], w[..., 2 * q:3 * q]], axis=-1)


def _rope_slot(w):
    pad = [(0, 0)] * (w.ndim - 1) + [(HD, LANE - HD - ROPE)]
    return jnp.pad(w, pad)


def _gate_lanes(g):
    n = g.shape[-1] // 2
    pad = [(0, 0)] * (g.ndim - 1) + [(0, LANE - n)]
    return jnp.concatenate([jnp.pad(g[..., :n], pad), jnp.pad(g[..., n:], pad)], axis=-1)


def _prep_w_in(w_in, gate_cols):
    a = w_in[..., 0:512]
    qkvo = w_in[..., 512:1536]
    g = _gate_lanes(w_in[..., 1536:1536 + gate_cols])
    c0 = 1536 + gate_cols
    cq = w_in[..., c0:c0 + Q_RANK]
    ckv = w_in[..., c0 + Q_RANK:c0 + Q_RANK + KV_RANK]
    kr = w_in[..., c0 + Q_RANK + KV_RANK:c0 + Q_RANK + KV_RANK + ROPE]
    d = w_in[..., c0 + Q_RANK + KV_RANK + ROPE:]
    return jnp.concatenate([a, qkvo, g, cq, ckv, _rope_slot(kr), _rope_slot(_rot_cols(kr)), d],
                           axis=-1).astype(BF16)


def _prep_mla(w_uq, w_ukv):
    n_layer = w_uq.shape[0]
    scale = (HD + ROPE) ** -0.5
    wq = w_uq.reshape(n_layer, Q_RANK, HEADS, HD + ROPE) * scale
    nope, rope = wq[..., :HD], wq[..., HD:]
    zeros_r = jnp.zeros_like(rope)
    wqa = jnp.concatenate([nope, rope, zeros_r], axis=-1).reshape(n_layer, Q_RANK, HEADS * LANE)
    wqb = jnp.concatenate([jnp.zeros_like(nope), _rot_cols(rope), zeros_r], axis=-1)
    wqb = wqb.reshape(n_layer, Q_RANK, HEADS * LANE)
    wkv = w_ukv.reshape(n_layer, KV_RANK, HEADS, 2 * HD)
    k_nope, val = wkv[..., :HD], wkv[..., HD:]
    zeros_h = jnp.zeros_like(k_nope)
    wk = jnp.concatenate([k_nope, zeros_h], axis=-1).reshape(n_layer, KV_RANK, HEADS * LANE)
    even = jnp.concatenate([val, zeros_h], axis=-1)
    odd = jnp.concatenate([zeros_h, val], axis=-1)
    is_odd = (jnp.arange(HEADS) % 2 == 1)[None, None, :, None]
    wv = jnp.where(is_odd, odd, even).reshape(n_layer, KV_RANK, HEADS * LANE)
    return wqa.astype(BF16), wqb.astype(BF16), wk.astype(BF16), wv.astype(BF16)


def _rope_tables(t_len, rotate):
    ones = jnp.ones((t_len, HD), F32)
    zeros = jnp.zeros((t_len, HD), F32)
    tail = jnp.zeros((t_len, LANE - HD - ROPE), F32)
    if not rotate:
        return (jnp.concatenate([ones, jnp.ones((t_len, ROPE), F32), tail], axis=-1),
                jnp.zeros((t_len, LANE), F32))
    rows = t_len // GRID_W
    row = jnp.repeat(jnp.arange(rows, dtype=F32), GRID_W)
    col = jnp.tile(jnp.arange(GRID_W, dtype=F32), rows)
    nf = ROPE // 4
    inv = ROPE_BASE ** (-jnp.arange(nf, dtype=F32) / nf)
    ar = row[:, None] * inv
    ac = col[:, None] * inv
    cos = jnp.concatenate([jnp.cos(ar), jnp.cos(ar), jnp.cos(ac), jnp.cos(ac)], axis=-1)
    sin = jnp.concatenate([jnp.sin(ar), jnp.sin(ar), jnp.sin(ac), jnp.sin(ac)], axis=-1)
    return (jnp.concatenate([ones, cos, tail], axis=-1), jnp.concatenate([zeros, sin, tail], axis=-1))


def _prep_s5(bb_re, bb_im, c_re, c_im):
    eye = jnp.eye(S5_G, dtype=F32)
    to_b = lambda bb: jnp.einsum("ldgnc,gh->ldgchn", bb, eye).reshape(bb.shape[0], 2, MIX, S5_STATES)
    to_c = lambda cc: jnp.einsum("ldgcn,gh->ldgnhc", cc, eye).reshape(cc.shape[0], 2, S5_STATES, MIX)
    bd = jnp.concatenate([to_b(bb_re), to_b(bb_im)], axis=-1)
    cd = jnp.concatenate([to_c(c_re.astype(F32)), -to_c(c_im.astype(F32))], axis=-2)
    return bd.astype(BF16), cd.astype(BF16)


def _layer(x, bsz, t_len, mod, row_fn, p, ctx, tm):
    za, zb, zg, zc, zd = _pre_mixer(x, mod, row_fn, p["norm1_g"], p["w_in"], tm)
    seq = lambda a: a.reshape(bsz, t_len, a.shape[-1])
    ya = _conv_module(seq(za), p["conv_w"], p["conv_b"], p["conv_ln_g"], p["conv_ln_b"])
    yb, c_fin, n_fin, m_fin = _mlstm_mixer(seq(zb), seq(zg), p["gate_b"], p["mlstm_norm_g"],
                                           ctx["mlstm_c"], ctx["mlstm_n"], ctx["mlstm_m"])
    yc, ckv = _mla_mixer(seq(zc), ctx["cos"], ctx["sin"], p["mla_q_norm_g"], p["mla_kv_norm_g"],
                         p["wqa"], p["wqb"], p["wk"], p["wv"], ctx.get("past_ckv"), ctx.get("past_kr"))
    ys, s_re, s_im = _s5_scan(seq(zd).transpose(1, 0, 2), p["s5_bd"], p["s5_cd"], p["s5_ab_re"], p["s5_ab_im"],
                              ctx["s5_re"], ctx["s5_im"])
    ys = ys.transpose(0, 2, 1, 3)
    flat = lambda a: a.reshape(bsz * t_len, a.shape[-1])
    x1, h2, gate = _post_mixer(x, flat(ya), flat(yb), flat(yc), ys.reshape(2, bsz * t_len, MIX), zd, mod, row_fn,
                               p["w_out"], p["s5_d"], p["s5_w_glu"], p["s5_b_glu"], p["norm2_g"],
                               p["w_router"], p["b_router"], tm)
    x2 = _moe(h2, gate, x1, mod, row_fn, p["moe_w1"], p["moe_w3"], p["moe_w2"], p["layer"], _MOE_TM)
    krope = seq(zc)[:, :, Q_RANK + KV_RANK + HD:Q_RANK + KV_RANK + HD + ROPE].astype(F32)
    return x2, (ckv, krope, c_fin, n_fin, m_fin, s_re, s_im)


def kernel(x_prompt, x_sample, cache_mla_ckv, cache_mla_krope, state_mlstm_C, state_mlstm_n, state_mlstm_m, state_s5, c, c_ctx, norm1_g, norm2_g, final_g, w_mod, b_mod, w_in, w_out, conv_w, conv_b, conv_ln_g, conv_ln_b, mlstm_gate_b, mlstm_norm_g, mla_q_norm_g, mla_w_uq, mla_kv_norm_g, mla_w_ukv, s5_a_re, s5_a_im, s5_log_dt, s5_b_re, s5_b_im, s5_c_re, s5_c_im, s5_d, s5_w_glu, s5_b_glu, moe_w_group, moe_b_group, moe_w_expert, moe_b_expert, moe_w1, moe_w3, moe_w2):
    n_layer = w_in.shape[0]
    b_ctx, t_ctx, d = x_prompt.shape
    b_lat, t_lat, _ = x_sample.shape
    nd = 2 * HEADS
    tm = 512
    assert b_lat + 1 <= 16 and t_lat % _MOE_TM == 0 and (b_ctx * t_ctx) % _MOE_TM == 0

    c_all = jnp.zeros((16, d), F32).at[0].set(c_ctx).at[1:1 + b_lat].set(c)
    mod_all = _modulation(c_all, w_mod, b_mod).reshape(n_layer, 16, 6, d)
    gate_cols = 4 * HEADS
    w_in_ext = _prep_w_in(w_in, gate_cols)
    wqa, wqb, wk, wv = _prep_mla(mla_w_uq, mla_w_ukv)
    ab_re, ab_im, bb_re, bb_im = _s5_discretise(s5_a_re, s5_a_im, s5_log_dt, s5_b_re, s5_b_im)
    s5_bd, s5_cd = _prep_s5(bb_re, bb_im, s5_c_re, s5_c_im)
    w_router = jnp.pad(jnp.concatenate([moe_w_expert, moe_w_group], axis=-1),
                       [(0, 0), (0, 0), (0, LANE - N_EXPERT - N_GROUP)]).astype(BF16)
    b_router = jnp.pad(jnp.concatenate([moe_b_expert, moe_b_group], axis=-1),
                       [(0, 0), (0, LANE - N_EXPERT - N_GROUP)])
    gate_b = _gate_lanes(mlstm_gate_b.reshape(n_layer, gate_cols))
    conv_w_p = jnp.pad(conv_w, [(0, 0), (0, 32 - CONV_WIDTH), (0, 0)])
    w_out_b = w_out.astype(BF16)
    w_glu_b = s5_w_glu.astype(BF16)
    flat_experts = lambda w: w.astype(BF16).reshape((n_layer * N_EXPERT,) + w.shape[2:])
    w1_b, w3_b, w2_b = flat_experts(moe_w1), flat_experts(moe_w3), flat_experts(moe_w2)
    row = lambda a, l: a[l][None, :]

    cos_ctx, sin_ctx = _rope_tables(t_ctx, rotate=False)
    cos_lat, sin_lat = _rope_tables(t_lat, rotate=True)
    zero_state = dict(
        mlstm_c=jnp.zeros((b_ctx, nd, HD, HD), F32), mlstm_n=jnp.zeros((b_ctx, nd, HD), F32),
        mlstm_m=jnp.zeros((b_ctx, nd, LANE), F32),
        s5_re=jnp.zeros((2, b_ctx, S5_STATES), F32), s5_im=jnp.zeros((2, b_ctx, S5_STATES), F32),
        cos=cos_ctx, sin=sin_ctx)

    row_ctx = lambda i, tile: 0
    row_lat = lambda i, tile: 1 + (i * tile) // t_lat

    x_ctx = x_prompt.reshape(b_ctx * t_ctx, d)
    x_lat = x_sample.reshape(b_lat * t_lat, d)
    outs = []
    for l in range(n_layer):
        p = dict(norm1_g=row(norm1_g, l), norm2_g=row(norm2_g, l), w_in=w_in_ext[l], w_out=w_out_b[l],
                 conv_w=conv_w_p[l], conv_b=row(conv_b, l), conv_ln_g=row(conv_ln_g, l),
                 conv_ln_b=row(conv_ln_b, l), gate_b=row(gate_b, l), mlstm_norm_g=row(mlstm_norm_g, l),
                 mla_q_norm_g=row(mla_q_norm_g, l), mla_kv_norm_g=row(mla_kv_norm_g, l),
                 wqa=wqa[l], wqb=wqb[l], wk=wk[l], wv=wv[l],
                 s5_bd=s5_bd[l], s5_cd=s5_cd[l], s5_ab_re=ab_re[l][:, None, :], s5_ab_im=ab_im[l][:, None, :],
                 s5_d=row(s5_d, l), s5_w_glu=w_glu_b[l], s5_b_glu=row(s5_b_glu, l),
                 w_router=w_router[l], b_router=row(b_router, l),
                 moe_w1=w1_b, moe_w3=w3_b, moe_w2=w2_b, layer=l)
        x_ctx, st = _layer(x_ctx, b_ctx, t_ctx, mod_all[l], row_ctx, p, zero_state, tm)
        outs.append(st)
        s5_l = state_s5[:, l].reshape(b_lat, 2, S5_STATES, 2)
        lat_state = dict(
            mlstm_c=jnp.swapaxes(state_mlstm_C[:, l].reshape(b_lat, nd, HD, HD), -1, -2),
            mlstm_n=state_mlstm_n[:, l].reshape(b_lat, nd, HD),
            mlstm_m=jnp.broadcast_to(state_mlstm_m[:, l].reshape(b_lat, nd, 1), (b_lat, nd, LANE)),
            s5_re=s5_l[..., 0].transpose(1, 0, 2), s5_im=s5_l[..., 1].transpose(1, 0, 2),
            cos=cos_lat, sin=sin_lat,
            past_ckv=cache_mla_ckv[:, l], past_kr=_rope_slot(cache_mla_krope[:, l]))
        x_lat, _ = _layer(x_lat, b_lat, t_lat, mod_all[l], row_lat, p, lat_state, tm)

    y_prompt = _final_norm(x_ctx, final_g[None, :], tm).reshape(b_ctx, t_ctx, d)
    y_sample = _final_norm(x_lat, final_g[None, :], tm).reshape(b_lat, t_lat, d)
    stack = lambda i: jnp.stack([o[i] for o in outs], axis=1)
    new_ckv = stack(0)
    new_krope = stack(1)
    new_c = jnp.swapaxes(stack(2), -1, -2).reshape(b_ctx, n_layer, 2, HEADS, HD, HD)
    new_n = stack(3).reshape(b_ctx, n_layer, 2, HEADS, HD)
    new_m = stack(4)[..., 0].reshape(b_ctx, n_layer, 2, HEADS)
    s_re = jnp.stack([o[5] for o in outs], axis=0)
    s_im = jnp.stack([o[6] for o in outs], axis=0)
    new_s5 = jnp.stack([s_re, s_im], axis=-1).transpose(2, 0, 1, 3, 4)
    new_s5 = new_s5.reshape(b_ctx, n_layer, 2, S5_G, S5_N, 2)
    return (y_prompt, y_sample, new_ckv, new_krope, new_c, new_n, new_m, new_s5)
```

```python
import functools
import math

import jax
import jax.numpy as jnp
from jax import lax
from jax.experimental import pallas as pl
from jax.experimental.pallas import tpu as pltpu

F32 = jnp.float32
BF16 = jnp.bfloat16
EPS = 1e-6

D_MODEL = 1024
MIX = 256
CONV_WIDTH = 31
HEADS = 4
HD = 64
CHUNK = 128
_STATIC_CHUNKS = 16
ROPE = 32
KV_RANK = 128
Q_RANK = 256
GRID_W = 64
ROPE_BASE = 10000.0
S5_G = 16
S5_GC = 16
S5_N = 64
S5_STATES = S5_G * S5_N
N_EXPERT = 32
PER_GROUP = 8
N_GROUP = 4
MOE_FF = 256
LANE = 128
Z_COLS = 2688
VMEM_LIMIT = 56 * 1024 * 1024


def _cp(*sem):
    return pltpu.CompilerParams(dimension_semantics=sem, vmem_limit_bytes=VMEM_LIMIT)


def _rms(x):
    return x * lax.rsqrt(jnp.mean(x * x, axis=-1, keepdims=True) + EPS)


def _sigmoid(x):
    return 1.0 / (1.0 + jnp.exp(-x))


def _bdot(a, b):
    return jnp.dot(a.astype(BF16), b.astype(BF16), preferred_element_type=F32)


def _mod_kernel(c_ref, w_ref, b_ref, o_ref):
    c = c_ref[...]
    o_ref[0] = _bdot(c * _sigmoid(c), w_ref[0]) + b_ref[0]


def _modulation(c_all, w_mod, b_mod):
    n_layer, d, n = w_mod.shape
    tn = 1536
    return pl.pallas_call(
        _mod_kernel,
        out_shape=jax.ShapeDtypeStruct((n_layer, 16, n), F32),
        grid=(n_layer, n // tn),
        in_specs=[pl.BlockSpec((16, d), lambda l, j: (0, 0)),
                  pl.BlockSpec((1, d, tn), lambda l, j: (l, 0, j)),
                  pl.BlockSpec((1, 1, tn), lambda l, j: (l, 0, j))],
        out_specs=pl.BlockSpec((1, 16, tn), lambda l, j: (l, 0, j)),
        compiler_params=_cp("parallel", "parallel"),
        name="modulation",
    )(c_all, w_mod, b_mod.reshape(n_layer, 1, n))


def _s5_disc_kernel(are_ref, aim_ref, ldt_ref, bre_ref, bim_ref, abre_ref, abim_ref, bbre_ref, bbim_ref):
    a_re = are_ref[...]
    a_im = aim_ref[...]
    dt = jnp.exp(ldt_ref[...])
    mag = jnp.exp(a_re * dt)
    ab_re = mag * jnp.cos(a_im * dt)
    ab_im = mag * jnp.sin(a_im * dt)
    den = a_re * a_re + a_im * a_im
    f_re = ((ab_re - 1.0) * a_re + ab_im * a_im) / den
    f_im = (ab_im * a_re - (ab_re - 1.0) * a_im) / den
    b_re = bre_ref[...]
    b_im = bim_ref[...]
    abre_ref[...] = ab_re
    abim_ref[...] = ab_im
    bbre_ref[...] = f_re * b_re - f_im * b_im
    bbim_ref[...] = f_re * b_im + f_im * b_re


def _s5_discretise(a_re, a_im, log_dt, b_re, b_im):
    n_layer = a_re.shape[0]
    rows = n_layer * 2 * S5_G
    cols = S5_N * S5_GC
    rep = lambda a: jnp.repeat(a.reshape(rows, S5_N), S5_GC, axis=1)
    ldt = jnp.broadcast_to(log_dt.reshape(rows, 1), (rows, cols))
    spec = pl.BlockSpec((rows, cols), lambda: (0, 0))
    ab_re, ab_im, bb_re, bb_im = pl.pallas_call(
        _s5_disc_kernel,
        out_shape=[jax.ShapeDtypeStruct((rows, cols), F32)] * 4,
        in_specs=[spec] * 5,
        out_specs=[spec] * 4,
        name="s5_discretise",
    )(rep(a_re), rep(a_im), ldt, b_re.reshape(rows, cols), b_im.reshape(rows, cols))
    pick = lambda a: a[:, ::S5_GC].reshape(n_layer, 2, S5_STATES)
    shp = (n_layer, 2, S5_G, S5_N, S5_GC)
    return pick(ab_re), pick(ab_im), bb_re.reshape(shp), bb_im.reshape(shp)


def _pre_kernel(x_ref, mod_ref, g_ref, w_ref, za_ref, zb_ref, zg_ref, zc_ref, zd_ref):
    m = mod_ref[0]
    h = _rms(x_ref[...]) * g_ref[...] * (1.0 + m[1:2]) + m[0:1]
    hb = h.astype(BF16)
    col = 0
    for o_ref in (za_ref, zb_ref, zg_ref, zc_ref, zd_ref):
        n = o_ref.shape[-1]
        o_ref[...] = jnp.dot(hb, w_ref[:, col:col + n], preferred_element_type=F32).astype(o_ref.dtype)
        col += n


def _pre_mixer(x, mod, row_fn, norm_g, w_in_ext, tm):
    n = x.shape[0]
    widths = (512, 1024, 2 * LANE, 640, MIX)
    return pl.pallas_call(
        _pre_kernel,
        out_shape=[jax.ShapeDtypeStruct((n, w), dt) for w, dt in zip(widths, (BF16, F32, F32, BF16, F32))],
        grid=(n // tm,),
        in_specs=[pl.BlockSpec((tm, D_MODEL), lambda i: (i, 0)),
                  pl.BlockSpec((1, 6, D_MODEL), lambda i: (row_fn(i, tm), 0, 0)),
                  pl.BlockSpec((1, D_MODEL), lambda i: (0, 0)),
                  pl.BlockSpec((D_MODEL, Z_COLS), lambda i: (0, 0))],
        out_specs=[pl.BlockSpec((tm, w), lambda i: (i, 0)) for w in widths],
        compiler_params=_cp("parallel"),
        name="pre_mixer",
    )(x, mod, norm_g, w_in_ext)


_CONV_PAD = 16
_CONV_TT = 128


def _conv_kernel(za_ref, w_ref, b_ref, lg_ref, lb_ref, o_ref, hp_ref, sh_ref):
    t_len = o_ref.shape[1]
    u = za_ref[0].astype(F32)
    hp_ref[0:_CONV_PAD, :] = jnp.zeros((_CONV_PAD, MIX), F32)
    hp_ref[_CONV_PAD + t_len:2 * _CONV_PAD + t_len, :] = jnp.zeros((_CONV_PAD, MIX), F32)
    hp_ref[_CONV_PAD:_CONV_PAD + t_len, :] = u[:, :MIX] * _sigmoid(u[:, MIX:])
    span = t_len + 2 * _CONV_PAD - 8
    for off in range(8):
        sh_ref[off, 0:span, :] = hp_ref[off:off + span, :]
    w = w_ref[...]
    half = CONV_WIDTH // 2
    for t0 in range(0, t_len, _CONV_TT):
        acc = jnp.zeros((_CONV_TT, MIX), F32) + b_ref[...]
        for k in range(CONV_WIDTH):
            start = t0 + _CONV_PAD - half + k
            aligned = start // 8 * 8
            acc = acc + sh_ref[start - aligned, aligned:aligned + _CONV_TT, :] * w[k:k + 1, :]
        mu = jnp.mean(acc, axis=-1, keepdims=True)
        cen = acc - mu
        var = jnp.mean(cen * cen, axis=-1, keepdims=True)
        yn = cen * lax.rsqrt(var + EPS) * lg_ref[...] + lb_ref[...]
        o_ref[0, t0:t0 + _CONV_TT, :] = (yn * _sigmoid(yn)).astype(o_ref.dtype)


def _conv_module(za, w, b, ln_g, ln_b):
    bsz, t_len, _ = za.shape
    vec = pl.BlockSpec((1, MIX), lambda i: (0, 0))
    return pl.pallas_call(
        _conv_kernel,
        out_shape=jax.ShapeDtypeStruct((bsz, t_len, MIX), BF16),
        grid=(bsz,),
        in_specs=[pl.BlockSpec((1, t_len, 2 * MIX), lambda i: (i, 0, 0)),
                  pl.BlockSpec((32, MIX), lambda i: (0, 0)), vec, vec, vec],
        out_specs=pl.BlockSpec((1, t_len, MIX), lambda i: (i, 0, 0)),
        scratch_shapes=[pltpu.VMEM((t_len + 2 * _CONV_PAD, MIX), F32),
                        pltpu.VMEM((8, t_len + 2 * _CONV_PAD, MIX), F32)],
        compiler_params=_cp("parallel"),
        name="conv_module",
    )(za, w, b, ln_g, ln_b)


def _log_sigmoid(x):
    return jnp.minimum(x, 0.0) - jnp.log(1.0 + jnp.exp(-jnp.abs(x)))


def _split3_dot(tri, x):
    hi = x.astype(BF16)
    r1 = x - hi.astype(F32)
    mid = r1.astype(BF16)
    lo = (r1 - mid.astype(F32)).astype(BF16)
    dot = lambda v: jnp.dot(tri, v, preferred_element_type=F32)
    return dot(hi) + dot(mid) + dot(lo)


def _chunk_rows(c):
    if isinstance(c, int):
        return slice(c * CHUNK, (c + 1) * CHUNK)
    return pl.ds(pl.multiple_of(c * CHUNK, CHUNK), CHUNK)


def _loop(n, body, init):
    if n <= _STATIC_CHUNKS:
        for i in range(n):
            init = body(i, init)
        return init
    return lax.fori_loop(0, n, body, init)


def _split2(x):
    hi = x.astype(BF16)
    return hi, (x - hi.astype(F32)).astype(BF16)


def _mlstm_chunk(zb_ref, zg_ref, gb_ref, qt_sc, qbd_sc, vt_sc, vbd_sc, ct_sc, nb_sc, c, direction, ms):
    L = CHUNK
    rows = _chunk_rows(c)
    si = lax.broadcasted_iota(jnp.int32, (L, L), 0)
    ti = lax.broadcasted_iota(jnp.int32, (L, L), 1)
    if direction == 0:
        valid = si <= ti
        tri = jnp.where(ti <= si, 1.0, 0.0).astype(BF16)
        last = L - 1
    else:
        valid = si >= ti
        tri = jnp.where(ti >= si, 1.0, 0.0).astype(BF16)
        last = 0
    g_in = zg_ref[0, rows, 0:LANE] + gb_ref[:, 0:LANE]
    g_f = zg_ref[0, rows, LANE:2 * LANE] + gb_ref[:, LANE:2 * LANE]
    bc = _split3_dot(tri, _log_sigmoid(g_f))
    r = g_in - bc
    g_in_t = g_in.T
    bct = bc.T
    k = zb_ref[0, rows, MIX:2 * MIX]
    k_hi, k_lo = _split2(k)
    qt = qt_sc[c]
    st_all = jnp.dot(k_hi, qbd_sc[c], preferred_element_type=F32)
    sts, inters, floors, colsums, ws, decays, new_m = [], [], [], [], [], [], []
    for h in range(HEADS):
        j = direction * HEADS + h
        i_row = g_in_t[j:j + 1, :]
        b_row = bct[j:j + 1, :]
        m_prev = ms[h]
        rb = jnp.where(valid, jnp.broadcast_to(r[:, j:j + 1], (L, L)), -jnp.inf)
        c_row = jnp.maximum(m_prev, jnp.max(rb, axis=0, keepdims=True))
        st = st_all[:, h * L:(h + 1) * L] * jnp.exp(rb - c_row)
        sts.append(st.astype(BF16))
        colsums.append(jnp.sum(st, axis=0, keepdims=True))
        inters.append(jnp.exp(m_prev - c_row))
        floors.append(jnp.exp(-(b_row + c_row)))
        b_last = b_row[:, last:last + 1]
        lw = b_last - b_row + i_row
        m_new = jnp.maximum(b_last + m_prev, jnp.max(lw, axis=-1, keepdims=True))
        decays.append(jnp.exp(b_last + m_prev - m_new))
        ws.append(jnp.exp(lw - m_new))
        new_m.append(m_new)
    per_head_rows = lambda vs: jnp.concatenate([jnp.broadcast_to(v, (HD, L)) for v in vs], axis=0)
    ct = ct_sc[direction]
    nb = nb_sc[direction]
    num = (per_head_rows(inters) * jnp.dot(ct.astype(BF16), qt, preferred_element_type=F32)
           + jnp.dot(vbd_sc[c], jnp.concatenate(sts, axis=0), preferred_element_type=F32))
    n_hi, n_lo = _split2(nb)
    qn = jnp.dot(n_hi, qt, preferred_element_type=F32) + jnp.dot(n_lo, qt, preferred_element_type=F32)
    dens = [jnp.maximum(jnp.abs(inters[h] * qn[h:h + 1, :] + colsums[h]), floors[h]) for h in range(HEADS)]
    ht = num / per_head_rows(dens)
    lane = lax.broadcasted_iota(jnp.int32, (1, MIX), 1) // HD
    dec_row = sum(jnp.where(lane == h, decays[h], 0.0) for h in range(HEADS))
    upd = jnp.dot((vt_sc[c] * per_head_rows(ws)).astype(BF16), k_hi, preferred_element_type=F32)
    same_head = (lax.broadcasted_iota(jnp.int32, (MIX, MIX), 0) // HD
                 == lax.broadcasted_iota(jnp.int32, (MIX, MIX), 1) // HD)
    ct_sc[direction] = jnp.where(same_head, ct * dec_row + upd, 0.0)
    w_hi, w_lo = _split2(jnp.concatenate(ws + [jnp.zeros((8 - HEADS, L), F32)], axis=0))
    wk = (jnp.dot(w_hi, k_hi, preferred_element_type=F32) + jnp.dot(w_lo, k_hi, preferred_element_type=F32)
          + jnp.dot(w_hi, k_lo, preferred_element_type=F32))
    own = lax.broadcasted_iota(jnp.int32, (8, MIX), 0) == lax.broadcasted_iota(jnp.int32, (8, MIX), 1) // HD
    nb_sc[direction] = jnp.where(own, nb * dec_row + wk, 0.0)
    return ht, tuple(new_m)


def _mlstm_kernel(zb_ref, zg_ref, gb_ref, ng_ref, c0_ref, n0_ref, m0_ref,
                  y_ref, c_ref, n_ref, m_ref, qt_sc, qbd_sc, vt_sc, vbd_sc, ht_sc, ct_sc, nb_sc):
    t_len = y_ref.shape[1]
    nc = t_len // CHUNK
    L = CHUNK
    qbd_sc[...] = jnp.zeros_like(qbd_sc)
    vbd_sc[...] = jnp.zeros_like(vbd_sc)
    ct_sc[...] = jnp.zeros_like(ct_sc)
    nb_sc[...] = jnp.zeros_like(nb_sc)

    def transpose_chunk(c, carry):
        rows = _chunk_rows(c)
        qt = (zb_ref[0, rows, 0:MIX] * (HD ** -0.5)).T.astype(BF16)
        vt = zb_ref[0, rows, 2 * MIX:3 * MIX].T
        qt_sc[c] = qt
        vt_sc[c] = vt
        for h in range(HEADS):
            sl = slice(h * HD, (h + 1) * HD)
            qbd_sc[c, sl, h * L:(h + 1) * L] = qt[sl, :]
            vbd_sc[c, sl, h * L:(h + 1) * L] = vt[sl, :].astype(BF16)
        return carry

    _loop(nc, transpose_chunk, 0)

    for j in range(2 * HEADS):
        d, h = divmod(j, HEADS)
        sl = slice(h * HD, (h + 1) * HD)
        ct_sc[d, sl, sl] = c0_ref[0, j]
        nb_sc[d, h:h + 1, sl] = n0_ref[0, j:j + 1, :]
    m_init = tuple(tuple(m0_ref[0, j:j + 1, 0:1] for j in range(d * HEADS, (d + 1) * HEADS)) for d in range(2))

    def scan_body(i, ms):
        cb = nc - 1 - i
        scr = (zb_ref, zg_ref, gb_ref, qt_sc, qbd_sc, vt_sc, vbd_sc, ct_sc, nb_sc)
        ht_sc[0, i], ms_f = _mlstm_chunk(*scr, i, 0, ms[0])
        ht_sc[1, cb], ms_b = _mlstm_chunk(*scr, cb, 1, ms[1])
        return ms_f, ms_b

    m_fin = _loop(nc, scan_body, m_init)
    for j in range(2 * HEADS):
        d, h = divmod(j, HEADS)
        sl = slice(h * HD, (h + 1) * HD)
        c_ref[0, j] = ct_sc[d, sl, sl]
        n_ref[0, j:j + 1, :] = nb_sc[d, h:h + 1, sl]
        m_ref[0, j:j + 1, :] = jnp.broadcast_to(m_fin[d][h], (1, LANE))

    def out_body(c, carry):
        rows = _chunk_rows(c)
        normed = []
        for h in range(HEADS):
            tot = ht_sc[0, c, h * HD:(h + 1) * HD, :] + ht_sc[1, c, h * HD:(h + 1) * HD, :]
            normed.append(tot * lax.rsqrt(jnp.mean(tot * tot, axis=0, keepdims=True) + EPS))
        hn = jnp.concatenate(normed, axis=0).T
        y_ref[0, rows, :] = (hn * ng_ref[...] * _sigmoid(zb_ref[0, rows, 3 * MIX:4 * MIX])).astype(y_ref.dtype)
        return carry

    _loop(nc, out_body, 0)


def _mlstm_mixer(zb, zg, gate_b, norm_g, c0, n0, m0):
    bsz, t_len, _ = zb.shape
    nd = 2 * HEADS
    return pl.pallas_call(
        _mlstm_kernel,
        out_shape=[jax.ShapeDtypeStruct((bsz, t_len, MIX), BF16),
                   jax.ShapeDtypeStruct((bsz, nd, HD, HD), F32),
                   jax.ShapeDtypeStruct((bsz, nd, HD), F32),
                   jax.ShapeDtypeStruct((bsz, nd, LANE), F32)],
        grid=(bsz,),
        in_specs=[pl.BlockSpec((1, t_len, 4 * MIX), lambda i: (i, 0, 0)),
                  pl.BlockSpec((1, t_len, 2 * LANE), lambda i: (i, 0, 0)),
                  pl.BlockSpec((1, 2 * LANE), lambda i: (0, 0)),
                  pl.BlockSpec((1, MIX), lambda i: (0, 0)),
                  pl.BlockSpec((1, nd, HD, HD), lambda i: (i, 0, 0, 0)),
                  pl.BlockSpec((1, nd, HD), lambda i: (i, 0, 0)),
                  pl.BlockSpec((1, nd, LANE), lambda i: (i, 0, 0))],
        out_specs=[pl.BlockSpec((1, t_len, MIX), lambda i: (i, 0, 0)),
                   pl.BlockSpec((1, nd, HD, HD), lambda i: (i, 0, 0, 0)),
                   pl.BlockSpec((1, nd, HD), lambda i: (i, 0, 0)),
                   pl.BlockSpec((1, nd, LANE), lambda i: (i, 0, 0))],
        scratch_shapes=[pltpu.VMEM((t_len // CHUNK, MIX, CHUNK), BF16),
                        pltpu.VMEM((t_len // CHUNK, MIX, HEADS * CHUNK), BF16),
                        pltpu.VMEM((t_len // CHUNK, MIX, CHUNK), F32),
                        pltpu.VMEM((t_len // CHUNK, MIX, HEADS * CHUNK), BF16),
                        pltpu.VMEM((2, t_len // CHUNK, MIX, CHUNK), F32),
                        pltpu.VMEM((2, MIX, MIX), F32),
                        pltpu.VMEM((2, 8, MIX), F32)],
        compiler_params=_cp("parallel"),
        name="mlstm_mixer",
    )(zb, zg, gate_b, norm_g, c0, n0, m0)


_ATT_TQ = 512


def _mla_kernel(*refs, past):
    if past:
        (zc_ref, cos_ref, sin_ref, gq_ref, gkv_ref, wqa_ref, wqb_ref, wk_ref, wv_ref,
         pckv_ref, pkr_ref, y_ref, ckv_ref, q_sc, k_sc, v_sc) = refs
    else:
        (zc_ref, cos_ref, sin_ref, gq_ref, gkv_ref, wqa_ref, wqb_ref, wk_ref, wv_ref,
         y_ref, ckv_ref, q_sc, k_sc, v_sc) = refs
    t_len = y_ref.shape[1]
    cosf = cos_ref[...]
    sinf = sin_ref[...]
    zc = lambda a, b: zc_ref[0, :, a:b].astype(F32)
    cq = _rms(zc(0, Q_RANK)) * gq_ref[...]
    ckv = _rms(zc(Q_RANK, Q_RANK + KV_RANK)) * gkv_ref[...]
    ckv_ref[0] = ckv
    kr = (zc(Q_RANK + KV_RANK, Q_RANK + KV_RANK + LANE) * cosf
          + zc(Q_RANK + KV_RANK + LANE, Q_RANK + KV_RANK + 2 * LANE) * sinf)
    cqb = cq.astype(BF16)
    ckvb = ckv.astype(BF16)
    pair = lambda a: jnp.concatenate([a, a], axis=-1)
    cos2, sin2, kr2 = pair(cosf), pair(sinf), pair(kr)
    for hp in range(HEADS // 2):
        sl = slice(hp * 2 * LANE, (hp + 1) * 2 * LANE)
        qa = jnp.dot(cqb, wqa_ref[:, sl], preferred_element_type=F32)
        qb = jnp.dot(cqb, wqb_ref[:, sl], preferred_element_type=F32)
        q_sc[:, sl] = (qa * cos2 + qb * sin2).astype(BF16)
        kn = jnp.dot(ckvb, wk_ref[:, sl], preferred_element_type=F32)
        k_sc[past:past + t_len, sl] = (kn + kr2).astype(BF16)
        v_sc[past:past + t_len, sl] = jnp.dot(ckvb, wv_ref[:, sl], preferred_element_type=F32).astype(BF16)
    if past:
        pckv = pckv_ref[0].astype(BF16)
        pkr2 = pair(pkr_ref[0])
        for hp in range(HEADS // 2):
            sl = slice(hp * 2 * LANE, (hp + 1) * 2 * LANE)
            k_sc[0:past, sl] = (jnp.dot(pckv, wk_ref[:, sl], preferred_element_type=F32) + pkr2).astype(BF16)
            v_sc[0:past, sl] = jnp.dot(pckv, wv_ref[:, sl], preferred_element_type=F32).astype(BF16)

    tq = min(_ATT_TQ, t_len)

    def q_block(i, carry):
        rows = pl.ds(pl.multiple_of(i * tq, tq), tq)
        outs = []
        for h in range(HEADS):
            sl = slice(h * LANE, (h + 1) * LANE)
            s = lax.dot_general(q_sc[rows, sl], k_sc[:, sl], (((1,), (1,)), ((), ())),
                                preferred_element_type=F32)
            p = jnp.exp(s - jnp.max(s, axis=-1, keepdims=True))
            o = jnp.dot(p.astype(BF16), v_sc[:, sl], preferred_element_type=F32)
            outs.append(o / jnp.sum(p, axis=-1, keepdims=True))
        y_ref[0, rows, 0:LANE] = (outs[0] + outs[1]).astype(y_ref.dtype)
        y_ref[0, rows, LANE:2 * LANE] = (outs[2] + outs[3]).astype(y_ref.dtype)
        return carry

    assert t_len % tq == 0
    n_blocks = t_len // tq
    if n_blocks % 2 == 0:
        lax.fori_loop(0, n_blocks // 2, lambda i, c: q_block(2 * i + 1, q_block(2 * i, c)), 0)
    else:
        lax.fori_loop(0, n_blocks, q_block, 0)


def _mla_mixer(zc, cosf, sinf, gq, gkv, wqa, wqb, wk, wv, past_ckv=None, past_kr=None):
    bsz, t_len, zw = zc.shape
    past = 0 if past_ckv is None else past_ckv.shape[1]
    full = lambda a: pl.BlockSpec(a.shape, lambda i: (0,) * a.ndim)
    args = [zc, cosf, sinf, gq, gkv, wqa, wqb, wk, wv]
    in_specs = [pl.BlockSpec((1, t_len, zw), lambda i: (i, 0, 0))] + [full(a) for a in args[1:]]
    if past:
        args += [past_ckv, past_kr]
        in_specs += [pl.BlockSpec((1, past, KV_RANK), lambda i: (i, 0, 0)),
                     pl.BlockSpec((1, past, LANE), lambda i: (i, 0, 0))]
    wide = HEADS * LANE
    return pl.pallas_call(
        functools.partial(_mla_kernel, past=past),
        out_shape=[jax.ShapeDtypeStruct((bsz, t_len, MIX), BF16),
                   jax.ShapeDtypeStruct((bsz, t_len, KV_RANK), F32)],
        grid=(bsz,),
        in_specs=in_specs,
        out_specs=[pl.BlockSpec((1, t_len, MIX), lambda i: (i, 0, 0)),
                   pl.BlockSpec((1, t_len, KV_RANK), lambda i: (i, 0, 0))],
        scratch_shapes=[pltpu.VMEM((t_len, wide), BF16),
                        pltpu.VMEM((past + t_len, wide), BF16),
                        pltpu.VMEM((past + t_len, wide), BF16)],
        compiler_params=_cp("parallel"),
        name="mla_mixer",
    )(*args)


_S5_SEQ = 8
_S5_TT = 256
_S5_PARTS = 8


def _s5_kernel(zd_ref, bd_ref, cd_ref, are_ref, aim_ref, s0re_ref, s0im_ref,
               y_ref, fre_ref, fim_ref, bre_sc, bim_sc, sre_sc, sim_sc):
    direction = pl.program_id(0)
    k = pl.program_id(2)
    tt = _S5_TT
    rows = _S5_SEQ * tt

    @pl.when(k == 0)
    def _():
        sre_sc[...] = s0re_ref[0]
        sim_sc[...] = s0im_ref[0]

    part_steps = tt // _S5_PARTS
    part_rows = part_steps * _S5_SEQ

    def run(backward):
        parts = list(range(_S5_PARTS))[::-1] if backward else list(range(_S5_PARTS))
        a_re = jnp.broadcast_to(are_ref[0], (_S5_SEQ, S5_STATES))
        a_im = jnp.broadcast_to(aim_ref[0], (_S5_SEQ, S5_STATES))
        for p in parts:
            r = slice(p * part_rows, (p + 1) * part_rows)
            u = zd_ref[p * part_steps:(p + 1) * part_steps].reshape(part_rows, MIX).astype(BF16)
            bre_sc[r, :] = jnp.dot(u, bd_ref[0, :, 0:S5_STATES], preferred_element_type=F32)
            bim_sc[r, :] = jnp.dot(u, bd_ref[0, :, S5_STATES:2 * S5_STATES], preferred_element_type=F32)
        s_re, s_im = sre_sc[...], sim_sc[...]
        for p in parts:
            steps = range(p * part_steps, (p + 1) * part_steps)
            for t in (reversed(steps) if backward else steps):
                sel = slice(t * _S5_SEQ, (t + 1) * _S5_SEQ)
                s_re, s_im = (a_re * s_re - a_im * s_im + bre_sc[sel, :],
                              a_re * s_im + a_im * s_re + bim_sc[sel, :])
                bre_sc[sel, :] = s_re
                bim_sc[sel, :] = s_im
            r = slice(p * part_rows, (p + 1) * part_rows)
            y = (jnp.dot(bre_sc[r, :].astype(BF16), cd_ref[0, 0:S5_STATES, :], preferred_element_type=F32)
                 + jnp.dot(bim_sc[r, :].astype(BF16), cd_ref[0, S5_STATES:2 * S5_STATES, :],
                           preferred_element_type=F32))
            y_ref[0, p * part_steps:(p + 1) * part_steps] = y.reshape(part_steps, _S5_SEQ, MIX)
        sre_sc[...] = s_re
        sim_sc[...] = s_im

    pl.when(direction == 0)(lambda: run(False))
    pl.when(direction == 1)(lambda: run(True))

    @pl.when(k == pl.num_programs(2) - 1)
    def _():
        fre_ref[0] = sre_sc[...]
        fim_ref[0] = sim_sc[...]


def _s5_scan(zd, bd, cd, ab_re, ab_im, s0_re, s0_im):
    t_len, bsz, _ = zd.shape
    nt = t_len // _S5_TT
    tile = lambda d, k: k + d * (nt - 1 - 2 * k)
    st = pl.BlockSpec((1, _S5_SEQ, S5_STATES), lambda d, g, k: (d, g, 0))
    return pl.pallas_call(
        _s5_kernel,
        out_shape=[jax.ShapeDtypeStruct((2, t_len, bsz, MIX), F32),
                   jax.ShapeDtypeStruct((2, bsz, S5_STATES), F32),
                   jax.ShapeDtypeStruct((2, bsz, S5_STATES), F32)],
        grid=(2, bsz // _S5_SEQ, nt),
        in_specs=[pl.BlockSpec((_S5_TT, _S5_SEQ, MIX), lambda d, g, k: (tile(d, k), g, 0)),
                  pl.BlockSpec((1, MIX, 2 * S5_STATES), lambda d, g, k: (d, 0, 0)),
                  pl.BlockSpec((1, 2 * S5_STATES, MIX), lambda d, g, k: (d, 0, 0)),
                  pl.BlockSpec((1, 1, S5_STATES), lambda d, g, k: (d, 0, 0)),
                  pl.BlockSpec((1, 1, S5_STATES), lambda d, g, k: (d, 0, 0)),
                  st, st],
        out_specs=[pl.BlockSpec((1, _S5_TT, _S5_SEQ, MIX), lambda d, g, k: (d, tile(d, k), g, 0)),
                   st, st],
        scratch_shapes=[pltpu.VMEM((_S5_SEQ * _S5_TT, S5_STATES), F32),
                        pltpu.VMEM((_S5_SEQ * _S5_TT, S5_STATES), F32),
                        pltpu.VMEM((_S5_SEQ, S5_STATES), F32),
                        pltpu.VMEM((_S5_SEQ, S5_STATES), F32)],
        compiler_params=_cp("parallel", "parallel", "arbitrary"),
        name="s5_scan",
    )(zd, bd, cd, ab_re, ab_im, s0_re, s0_im)


def _gelu_tanh(x):
    return 0.5 * x * (1.0 + jnp.tanh(math.sqrt(2.0 / math.pi) * (x + 0.044715 * x * x * x)))


def _route(logits):
    lane = lax.broadcasted_iota(jnp.int32, logits.shape, 1).astype(F32)
    big = float(LANE)
    neg = -jnp.inf
    g_mask = (lane >= N_EXPERT) & (lane < N_EXPERT + N_GROUP)
    gl = jnp.where(g_mask, logits, neg)
    g_max = jnp.max(gl, axis=-1, keepdims=True)
    g_idx = jnp.min(jnp.where(gl == g_max, lane, big), axis=-1, keepdims=True) - N_EXPERT
    g_sel = 1.0 / jnp.sum(jnp.where(g_mask, jnp.exp(logits - g_max), 0.0), axis=-1, keepdims=True)
    lo = g_idx * PER_GROUP
    el = jnp.where((lane >= lo) & (lane < lo + PER_GROUP), logits, neg)
    v1 = jnp.max(el, axis=-1, keepdims=True)
    i1 = jnp.min(jnp.where(el == v1, lane, big), axis=-1, keepdims=True)
    el2 = jnp.where(lane == i1, neg, el)
    v2 = jnp.max(el2, axis=-1, keepdims=True)
    i2 = jnp.min(jnp.where(el2 == v2, lane, big), axis=-1, keepdims=True)
    e2 = jnp.exp(v2 - v1)
    w1 = g_sel / (1.0 + e2)
    return jnp.where(lane == i1, w1, jnp.where(lane == i2, w1 * e2, 0.0))


_POST_SUBTILES = 2


def _post_kernel(x_ref, ya_ref, yb_ref, yc_ref, ys_ref, zd_ref, mod_ref, wo_ref, d_ref, wglu_ref, bglu_ref,
                 g2_ref, wr_ref, br_ref, x1_ref, h2_ref, gate_ref):
    m = mod_ref[0]
    sub = x_ref.shape[0] // _POST_SUBTILES
    for s in range(_POST_SUBTILES):
        r = slice(s * sub, (s + 1) * sub)
        ys = ys_ref[0, r, :] + ys_ref[1, r, :] + d_ref[...] * zd_ref[r, :]
        gl = _bdot(_gelu_tanh(ys), wglu_ref[...]) + bglu_ref[...]
        yd = gl[:, 0:MIX] * _sigmoid(gl[:, MIX:2 * MIX])
        mix = (_bdot(ya_ref[r, :], wo_ref[0:MIX, :]) + _bdot(yb_ref[r, :], wo_ref[MIX:2 * MIX, :])
               + _bdot(yc_ref[r, :], wo_ref[2 * MIX:3 * MIX, :]) + _bdot(yd, wo_ref[3 * MIX:4 * MIX, :]))
        x1 = x_ref[r, :] + m[2:3] * mix
        x1_ref[r, :] = x1
        h2b = (_rms(x1) * g2_ref[...] * (1.0 + m[4:5]) + m[3:4]).astype(BF16)
        h2_ref[r, :] = h2b
        gate_ref[r, :] = _route(jnp.dot(h2b, wr_ref[...], preferred_element_type=F32) + br_ref[...])


def _post_mixer(x, ya, yb, yc, ys, zd, mod, row_fn, w_out, s5_d, w_glu, b_glu, norm2_g, w_router, b_router, tm):
    n = x.shape[0]
    tok = lambda w: pl.BlockSpec((tm, w), lambda i: (i, 0))
    full = lambda a: pl.BlockSpec(a.shape, lambda i: (0,) * a.ndim)
    return pl.pallas_call(
        _post_kernel,
        out_shape=[jax.ShapeDtypeStruct((n, D_MODEL), F32),
                   jax.ShapeDtypeStruct((n, D_MODEL), BF16),
                   jax.ShapeDtypeStruct((n, LANE), F32)],
        grid=(n // tm,),
        in_specs=[tok(D_MODEL), tok(MIX), tok(MIX), tok(MIX),
                  pl.BlockSpec((2, tm, MIX), lambda i: (0, i, 0)), tok(MIX),
                  pl.BlockSpec((1, 6, D_MODEL), lambda i: (row_fn(i, tm), 0, 0)),
                  full(w_out), full(s5_d), full(w_glu), full(b_glu), full(norm2_g),
                  full(w_router), full(b_router)],
        out_specs=[tok(D_MODEL), tok(D_MODEL), tok(LANE)],
        compiler_params=_cp("parallel"),
        name="post_mixer",
    )(x, ya, yb, yc, ys, zd, mod, w_out, s5_d, w_glu, b_glu, norm2_g, w_router, b_router)


_MOE_EPS = 4
_MOE_TM = 1024


_MOE_ALIGN = 16
_MOE_ROWS = 2 * _MOE_TM + N_EXPERT * _MOE_ALIGN
_MOE_CH = 128
_MOE_GB = 512
_MOE_CB = 256


def _moe_kernel(h_ref, gate_ref, x_ref, mod_ref, w1_ref, w3_ref, w2_ref, o_ref,
                xs_sc, ys_sc, col_sc, row_sc, start_sm, rows_sm, done_sm):
    j = pl.program_id(1)
    tm = h_ref.shape[0]

    @pl.when(j == 0)
    def _dispatch():
        gate = gate_ref[...]
        cnt_row = jnp.sum(jnp.where(gate != 0.0, 1.0, 0.0), axis=0, keepdims=True)
        units_row = jnp.floor((cnt_row + (_MOE_ALIGN - 1)) * (1.0 / _MOE_ALIGN))
        li = lax.broadcasted_iota(jnp.int32, (LANE, LANE), 0)
        lj = lax.broadcasted_iota(jnp.int32, (LANE, LANE), 1)
        before = jnp.where(li < lj, 1.0, 0.0).astype(BF16)
        start_row = jnp.dot(jnp.broadcast_to(units_row, (8, LANE)).astype(BF16), before,
                            preferred_element_type=F32) * _MOE_ALIGN
        start_i = start_row.astype(jnp.int32)
        cnt_i = cnt_row.astype(jnp.int32)
        for e in range(N_EXPERT):
            start_sm[e] = start_i[0, e]
            rows_sm[e] = cnt_i[0, e]
        gt = gate.T[0:N_EXPERT, :]
        chosen = gt != 0.0
        ones = jnp.where(chosen, 1.0, 0.0)
        cnt = jnp.sum(ones, axis=-1, keepdims=True)
        units = jnp.floor((cnt + (_MOE_ALIGN - 1)) * (1.0 / _MOE_ALIGN))
        ei = lax.broadcasted_iota(jnp.int32, (N_EXPERT, N_EXPERT), 0)
        ej = lax.broadcasted_iota(jnp.int32, (N_EXPERT, N_EXPERT), 1)
        start = jnp.dot(jnp.where(ej < ei, 1.0, 0.0).astype(BF16),
                        jnp.broadcast_to(units, (N_EXPERT, LANE)).astype(BF16),
                        preferred_element_type=F32)[:, 0:1] * _MOE_ALIGN
        ti = lax.broadcasted_iota(jnp.int32, (tm, tm), 0)
        tj = lax.broadcasted_iota(jnp.int32, (tm, tm), 1)
        earlier = jnp.where(ti < tj, 1.0, 0.0).astype(BF16)
        pos = start + jnp.dot(ones.astype(BF16), earlier, preferred_element_type=F32)
        p_a = jnp.min(jnp.where(chosen, pos, float(_MOE_ROWS)), axis=0, keepdims=True)
        p_b = jnp.max(jnp.where(chosen, pos, -1.0), axis=0, keepdims=True)
        g_a = jnp.sum(jnp.where(chosen & (pos == p_a), gt, 0.0), axis=0, keepdims=True)
        g_b = jnp.sum(jnp.where(chosen & (pos == p_b) & (p_b != p_a), gt, 0.0), axis=0, keepdims=True)
        row_sc[...] = jnp.concatenate([p_a, p_b, jnp.zeros((6, tm), F32)], axis=0)
        packed = jnp.concatenate([p_a, p_b, g_a, g_b, jnp.zeros((LANE - 4, tm), F32)], axis=0)
        col_sc[...] = packed.T
        xs_sc[_MOE_ROWS:_MOE_ROWS + _MOE_CH, :] = jnp.zeros((_MOE_CH, D_MODEL), BF16)
        ys_sc[...] = jnp.zeros_like(ys_sc)
        o_ref[...] = x_ref[...]
        done_sm[0] = 0
        done_sm[1] = 0

    n_blocks = _MOE_ROWS // _MOE_GB
    e_last = j * _MOE_EPS + (_MOE_EPS - 1)
    seg_end = start_sm[e_last] + rows_sm[e_last]
    last_step = j == pl.num_programs(1) - 1

    def gather_block(b, carry):
        r0 = pl.multiple_of(b * _MOE_GB, _MOE_GB)
        ri = (lax.broadcasted_iota(jnp.int32, (_MOE_GB, tm), 0) + r0).astype(F32)
        sel = jnp.where((ri == row_sc[0:1, :]) | (ri == row_sc[1:2, :]), 1.0, 0.0).astype(BF16)
        xs_sc[pl.ds(r0, _MOE_GB), :] = jnp.dot(sel, h_ref[...], preferred_element_type=F32).astype(BF16)
        return carry

    gathered = jnp.minimum((seg_end + _MOE_CH + _MOE_GB - 1) // _MOE_GB, n_blocks)
    lax.fori_loop(done_sm[0], gathered, gather_block, 0)
    done_sm[0] = gathered

    def expert_rows(el, i):
        rows = pl.ds(pl.multiple_of(start_sm[j * _MOE_EPS + el] + i * _MOE_CH, _MOE_ALIGN), _MOE_CH)
        xs = xs_sc[rows, :]
        h1 = jnp.dot(xs, w1_ref[el], preferred_element_type=F32)
        h3 = jnp.dot(xs, w3_ref[el], preferred_element_type=F32)
        return rows, jnp.dot((h1 * _sigmoid(h1) * h3).astype(BF16), w2_ref[el], preferred_element_type=F32)

    first = [expert_rows(el, 0) for el in range(_MOE_EPS)]
    for rows, y in first:
        ys_sc[rows, :] = y.astype(BF16)
    for el in range(_MOE_EPS):
        n_rows = rows_sm[j * _MOE_EPS + el]

        def chunk(i, carry, el=el, n_rows=n_rows):
            rows, y = expert_rows(el, i)
            live = lax.broadcasted_iota(jnp.int32, (_MOE_CH, 1), 0) + i * _MOE_CH < n_rows
            ys_sc[rows, :] = jnp.where(live, y, ys_sc[rows, :].astype(F32)).astype(BF16)
            return carry

        lax.fori_loop(1, (n_rows + _MOE_CH - 1) // _MOE_CH, chunk, 0)

    g2 = mod_ref[0][5:6]

    def combine_block(b, carry):
        r0 = pl.multiple_of(b * _MOE_GB, _MOE_GB)
        y = ys_sc[pl.ds(r0, _MOE_GB), :]
        for t0 in range(0, tm, _MOE_CB):
            c = col_sc[t0:t0 + _MOE_CB, :]
            ri = (lax.broadcasted_iota(jnp.int32, (_MOE_CB, _MOE_GB), 1) + r0).astype(F32)
            w = (jnp.where(ri == c[:, 0:1], c[:, 2:3], 0.0)
                 + jnp.where(ri == c[:, 1:2], c[:, 3:4], 0.0)).astype(BF16)
            o_ref[t0:t0 + _MOE_CB, :] += g2 * jnp.dot(w, y, preferred_element_type=F32)
        return carry

    final_rows = (seg_end + _MOE_ALIGN - 1) // _MOE_ALIGN * _MOE_ALIGN
    combined = jnp.where(last_step, n_blocks, final_rows // _MOE_GB)
    lax.fori_loop(done_sm[1], combined, combine_block, 0)
    done_sm[1] = combined


def _moe(h2, gate, x1, mod, row_fn, w1, w3, w2, layer, tm):
    n = h2.shape[0]
    first_blk = layer * (N_EXPERT // _MOE_EPS)
    return pl.pallas_call(
        _moe_kernel,
        out_shape=jax.ShapeDtypeStruct((n, D_MODEL), F32),
        grid=(n // tm, N_EXPERT // _MOE_EPS),
        in_specs=[pl.BlockSpec((tm, D_MODEL), lambda i, j: (i, 0)),
                  pl.BlockSpec((tm, LANE), lambda i, j: (i, 0)),
                  pl.BlockSpec((tm, D_MODEL), lambda i, j: (i, 0)),
                  pl.BlockSpec((1, 6, D_MODEL), lambda i, j: (row_fn(i, tm), 0, 0)),
                  pl.BlockSpec((_MOE_EPS, D_MODEL, MOE_FF), lambda i, j: (first_blk + j, 0, 0)),
                  pl.BlockSpec((_MOE_EPS, D_MODEL, MOE_FF), lambda i, j: (first_blk + j, 0, 0)),
                  pl.BlockSpec((_MOE_EPS, MOE_FF, D_MODEL), lambda i, j: (first_blk + j, 0, 0))],
        out_specs=pl.BlockSpec((tm, D_MODEL), lambda i, j: (i, 0)),
        scratch_shapes=[pltpu.VMEM((_MOE_ROWS + _MOE_CH, D_MODEL), BF16),
                        pltpu.VMEM((_MOE_ROWS + _MOE_CH, D_MODEL), BF16),
                        pltpu.VMEM((tm, LANE), F32),
                        pltpu.VMEM((8, tm), F32),
                        pltpu.SMEM((N_EXPERT,), jnp.int32),
                        pltpu.SMEM((N_EXPERT,), jnp.int32),
                        pltpu.SMEM((2,), jnp.int32)],
        compiler_params=_cp("parallel", "arbitrary"),
        name="moe",
    )(h2, gate, x1, mod, w1, w3, w2)


def _final_kernel(x_ref, g_ref, o_ref):
    o_ref[...] = _rms(x_ref[...]) * g_ref[...]


def _final_norm(x, g, tm):
    n = x.shape[0]
    return pl.pallas_call(
        _final_kernel,
        out_shape=jax.ShapeDtypeStruct((n, D_MODEL), F32),
        grid=(n // tm,),
        in_specs=[pl.BlockSpec((tm, D_MODEL), lambda i: (i, 0)),
                  pl.BlockSpec((1, D_MODEL), lambda i: (0, 0))],
        out_specs=pl.BlockSpec((tm, D_MODEL), lambda i: (i, 0)),
        compiler_params=_cp("parallel"),
        name="final_norm",
    )(x, g)


def _rot_cols(w):
    q = ROPE // 4
    return jnp.concatenate([-w[..., q:2 * q], w[..., 0:q], -w[..., 3 * q:4 * q], w[..., 2 * q:3 * q]], axis=-1)


def _rope_slot(w):
    pad = [(0, 0)] * (w.ndim - 1) + [(HD, LANE - HD - ROPE)]
    return jnp.pad(w, pad)


def _gate_lanes(g):
    n = g.shape[-1] // 2
    pad = [(0, 0)] * (g.ndim - 1) + [(0, LANE - n)]
    return jnp.concatenate([jnp.pad(g[..., :n], pad), jnp.pad(g[..., n:], pad)], axis=-1)


def _prep_w_in(w_in, gate_cols):
    a = w_in[..., 0:512]
    qkvo = w_in[..., 512:1536]
    g = _gate_lanes(w_in[..., 1536:1536 + gate_cols])
    c0 = 1536 + gate_cols
    cq = w_in[..., c0:c0 + Q_RANK]
    ckv = w_in[..., c0 + Q_RANK:c0 + Q_RANK + KV_RANK]
    kr = w_in[..., c0 + Q_RANK + KV_RANK:c0 + Q_RANK + KV_RANK + ROPE]
    d = w_in[..., c0 + Q_RANK + KV_RANK + ROPE:]
    return jnp.concatenate([a, qkvo, g, cq, ckv, _rope_slot(kr), _rope_slot(_rot_cols(kr)), d],
                           axis=-1).astype(BF16)


def _prep_mla(w_uq, w_ukv):
    n_layer = w_uq.shape[0]
    scale = (HD + ROPE) ** -0.5
    wq = w_uq.reshape(n_layer, Q_RANK, HEADS, HD + ROPE) * scale
    nope, rope = wq[..., :HD], wq[..., HD:]
    zeros_r = jnp.zeros_like(rope)
    wqa = jnp.concatenate([nope, rope, zeros_r], axis=-1).reshape(n_layer, Q_RANK, HEADS * LANE)
    wqb = jnp.concatenate([jnp.zeros_like(nope), _rot_cols(rope), zeros_r], axis=-1)
    wqb = wqb.reshape(n_layer, Q_RANK, HEADS * LANE)
    wkv = w_ukv.reshape(n_layer, KV_RANK, HEADS, 2 * HD)
    k_nope, val = wkv[..., :HD], wkv[..., HD:]
    zeros_h = jnp.zeros_like(k_nope)
    wk = jnp.concatenate([k_nope, zeros_h], axis=-1).reshape(n_layer, KV_RANK, HEADS * LANE)
    even = jnp.concatenate([val, zeros_h], axis=-1)
    odd = jnp.concatenate([zeros_h, val], axis=-1)
    is_odd = (jnp.arange(HEADS) % 2 == 1)[None, None, :, None]
    wv = jnp.where(is_odd, odd, even).reshape(n_layer, KV_RANK, HEADS * LANE)
    return wqa.astype(BF16), wqb.astype(BF16), wk.astype(BF16), wv.astype(BF16)


def _rope_tables(t_len, rotate):
    ones = jnp.ones((t_len, HD), F32)
    zeros = jnp.zeros((t_len, HD), F32)
    tail = jnp.zeros((t_len, LANE - HD - ROPE), F32)
    if not rotate:
        return (jnp.concatenate([ones, jnp.ones((t_len, ROPE), F32), tail], axis=-1),
                jnp.zeros((t_len, LANE), F32))
    rows = t_len // GRID_W
    row = jnp.repeat(jnp.arange(rows, dtype=F32), GRID_W)
    col = jnp.tile(jnp.arange(GRID_W, dtype=F32), rows)
    nf = ROPE // 4
    inv = ROPE_BASE ** (-jnp.arange(nf, dtype=F32) / nf)
    ar = row[:, None] * inv
    ac = col[:, None] * inv
    cos = jnp.concatenate([jnp.cos(ar), jnp.cos(ar), jnp.cos(ac), jnp.cos(ac)], axis=-1)
    sin = jnp.concatenate([jnp.sin(ar), jnp.sin(ar), jnp.sin(ac), jnp.sin(ac)], axis=-1)
    return (jnp.concatenate([ones, cos, tail], axis=-1), jnp.concatenate([zeros, sin, tail], axis=-1))


def _prep_s5(bb_re, bb_im, c_re, c_im):
    eye = jnp.eye(S5_G, dtype=F32)
    to_b = lambda bb: jnp.einsum("ldgnc,gh->ldgchn", bb, eye).reshape(bb.shape[0], 2, MIX, S5_STATES)
    to_c = lambda cc: jnp.einsum("ldgcn,gh->ldgnhc", cc, eye).reshape(cc.shape[0], 2, S5_STATES, MIX)
    bd = jnp.concatenate([to_b(bb_re), to_b(bb_im)], axis=-1)
    cd = jnp.concatenate([to_c(c_re.astype(F32)), -to_c(c_im.astype(F32))], axis=-2)
    return bd.astype(BF16), cd.astype(BF16)


def _layer(x, bsz, t_len, mod, row_fn, p, ctx, tm):
    za, zb, zg, zc, zd = _pre_mixer(x, mod, row_fn, p["norm1_g"], p["w_in"], tm)
    seq = lambda a: a.reshape(bsz, t_len, a.shape[-1])
    ya = _conv_module(seq(za), p["conv_w"], p["conv_b"], p["conv_ln_g"], p["conv_ln_b"])
    yb, c_fin, n_fin, m_fin = _mlstm_mixer(seq(zb), seq(zg), p["gate_b"], p["mlstm_norm_g"],
                                           ctx["mlstm_c"], ctx["mlstm_n"], ctx["mlstm_m"])
    yc, ckv = _mla_mixer(seq(zc), ctx["cos"], ctx["sin"], p["mla_q_norm_g"], p["mla_kv_norm_g"],
                         p["wqa"], p["wqb"], p["wk"], p["wv"], ctx.get("past_ckv"), ctx.get("past_kr"))
    ys, s_re, s_im = _s5_scan(seq(zd).transpose(1, 0, 2), p["s5_bd"], p["s5_cd"], p["s5_ab_re"], p["s5_ab_im"],
                              ctx["s5_re"], ctx["s5_im"])
    ys = ys.transpose(0, 2, 1, 3)
    flat = lambda a: a.reshape(bsz * t_len, a.shape[-1])
    x1, h2, gate = _post_mixer(x, flat(ya), flat(yb), flat(yc), ys.reshape(2, bsz * t_len, MIX), zd, mod, row_fn,
                               p["w_out"], p["s5_d"], p["s5_w_glu"], p["s5_b_glu"], p["norm2_g"],
                               p["w_router"], p["b_router"], _POST_SUBTILES * tm)
    x2 = _moe(h2, gate, x1, mod, row_fn, p["moe_w1"], p["moe_w3"], p["moe_w2"], p["layer"], _MOE_TM)
    krope = seq(zc)[:, :, Q_RANK + KV_RANK + HD:Q_RANK + KV_RANK + HD + ROPE].astype(F32)
    return x2, (ckv, krope, c_fin, n_fin, m_fin, s_re, s_im)


def kernel(x_prompt, x_sample, cache_mla_ckv, cache_mla_krope, state_mlstm_C, state_mlstm_n, state_mlstm_m, state_s5, c, c_ctx, norm1_g, norm2_g, final_g, w_mod, b_mod, w_in, w_out, conv_w, conv_b, conv_ln_g, conv_ln_b, mlstm_gate_b, mlstm_norm_g, mla_q_norm_g, mla_w_uq, mla_kv_norm_g, mla_w_ukv, s5_a_re, s5_a_im, s5_log_dt, s5_b_re, s5_b_im, s5_c_re, s5_c_im, s5_d, s5_w_glu, s5_b_glu, moe_w_group, moe_b_group, moe_w_expert, moe_b_expert, moe_w1, moe_w3, moe_w2):
    n_layer = w_in.shape[0]
    b_ctx, t_ctx, d = x_prompt.shape
    b_lat, t_lat, _ = x_sample.shape
    nd = 2 * HEADS
    tm = 512
    assert b_lat + 1 <= 16 and t_lat % _MOE_TM == 0 and (b_ctx * t_ctx) % _MOE_TM == 0

    c_all = jnp.zeros((16, d), F32).at[0].set(c_ctx).at[1:1 + b_lat].set(c)
    mod_all = _modulation(c_all, w_mod, b_mod).reshape(n_layer, 16, 6, d)
    gate_cols = 4 * HEADS
    w_in_ext = _prep_w_in(w_in, gate_cols)
    wqa, wqb, wk, wv = _prep_mla(mla_w_uq, mla_w_ukv)
    ab_re, ab_im, bb_re, bb_im = _s5_discretise(s5_a_re, s5_a_im, s5_log_dt, s5_b_re, s5_b_im)
    s5_bd, s5_cd = _prep_s5(bb_re, bb_im, s5_c_re, s5_c_im)
    w_router = jnp.pad(jnp.concatenate([moe_w_expert, moe_w_group], axis=-1),
                       [(0, 0), (0, 0), (0, LANE - N_EXPERT - N_GROUP)]).astype(BF16)
    b_router = jnp.pad(jnp.concatenate([moe_b_expert, moe_b_group], axis=-1),
                       [(0, 0), (0, LANE - N_EXPERT - N_GROUP)])
    gate_b = _gate_lanes(mlstm_gate_b.reshape(n_layer, gate_cols))
    conv_w_p = jnp.pad(conv_w, [(0, 0), (0, 32 - CONV_WIDTH), (0, 0)])
    w_out_b = w_out.astype(BF16)
    w_glu_b = s5_w_glu.astype(BF16)
    flat_experts = lambda w: w.astype(BF16).reshape((n_layer * N_EXPERT,) + w.shape[2:])
    w1_b, w3_b, w2_b = flat_experts(moe_w1), flat_experts(moe_w3), flat_experts(moe_w2)
    row = lambda a, l: a[l][None, :]

    cos_ctx, sin_ctx = _rope_tables(t_ctx, rotate=False)
    cos_lat, sin_lat = _rope_tables(t_lat, rotate=True)
    zero_state = dict(
        mlstm_c=jnp.zeros((b_ctx, nd, HD, HD), F32), mlstm_n=jnp.zeros((b_ctx, nd, HD), F32),
        mlstm_m=jnp.zeros((b_ctx, nd, LANE), F32),
        s5_re=jnp.zeros((2, b_ctx, S5_STATES), F32), s5_im=jnp.zeros((2, b_ctx, S5_STATES), F32),
        cos=cos_ctx, sin=sin_ctx)

    row_ctx = lambda i, tile: 0
    row_lat = lambda i, tile: 1 + (i * tile) // t_lat

    x_ctx = x_prompt.reshape(b_ctx * t_ctx, d)
    x_lat = x_sample.reshape(b_lat * t_lat, d)
    outs = []
    for l in range(n_layer):
        p = dict(norm1_g=row(norm1_g, l), norm2_g=row(norm2_g, l), w_in=w_in_ext[l], w_out=w_out_b[l],
                 conv_w=conv_w_p[l], conv_b=row(conv_b, l), conv_ln_g=row(conv_ln_g, l),
                 conv_ln_b=row(conv_ln_b, l), gate_b=row(gate_b, l), mlstm_norm_g=row(mlstm_norm_g, l),
                 mla_q_norm_g=row(mla_q_norm_g, l), mla_kv_norm_g=row(mla_kv_norm_g, l),
                 wqa=wqa[l], wqb=wqb[l], wk=wk[l], wv=wv[l],
                 s5_bd=s5_bd[l], s5_cd=s5_cd[l], s5_ab_re=ab_re[l][:, None, :], s5_ab_im=ab_im[l][:, None, :],
                 s5_d=row(s5_d, l), s5_w_glu=w_glu_b[l], s5_b_glu=row(s5_b_glu, l),
                 w_router=w_router[l], b_router=row(b_router, l),
                 moe_w1=w1_b, moe_w3=w3_b, moe_w2=w2_b, layer=l)
        x_ctx, st = _layer(x_ctx, b_ctx, t_ctx, mod_all[l], row_ctx, p, zero_state, tm)
        outs.append(st)
        s5_l = state_s5[:, l].reshape(b_lat, 2, S5_STATES, 2)
        lat_state = dict(
            mlstm_c=jnp.swapaxes(state_mlstm_C[:, l].reshape(b_lat, nd, HD, HD), -1, -2),
            mlstm_n=state_mlstm_n[:, l].reshape(b_lat, nd, HD),
            mlstm_m=jnp.broadcast_to(state_mlstm_m[:, l].reshape(b_lat, nd, 1), (b_lat, nd, LANE)),
            s5_re=s5_l[..., 0].transpose(1, 0, 2), s5_im=s5_l[..., 1].transpose(1, 0, 2),
            cos=cos_lat, sin=sin_lat,
            past_ckv=cache_mla_ckv[:, l], past_kr=_rope_slot(cache_mla_krope[:, l]))
        x_lat, _ = _layer(x_lat, b_lat, t_lat, mod_all[l], row_lat, p, lat_state, tm)

    y_prompt = _final_norm(x_ctx, final_g[None, :], tm).reshape(b_ctx, t_ctx, d)
    y_sample = _final_norm(x_lat, final_g[None, :], tm).reshape(b_lat, t_lat, d)
    stack = lambda i: jnp.stack([o[i] for o in outs], axis=1)
    new_ckv = stack(0)
    new_krope = stack(1)
    new_c = jnp.swapaxes(stack(2), -1, -2).reshape(b_ctx, n_layer, 2, HEADS, HD, HD)
    new_n = stack(3).reshape(b_ctx, n_layer, 2, HEADS, HD)
    new_m = stack(4)[..., 0].reshape(b_ctx, n_layer, 2, HEADS)
    s_re = jnp.stack([o[5] for o in outs], axis=0)
    s_im = jnp.stack([o[6] for o in outs], axis=0)
    new_s5 = jnp.stack([s_re, s_im], axis=-1).transpose(2, 0, 1, 3, 4)
    new_s5 = new_s5.reshape(b_ctx, n_layer, 2, S5_G, S5_N, 2)
    return (y_prompt, y_sample, new_ckv, new_krope, new_c, new_n, new_m, new_s5)
```

```python
import functools
import math

import jax
import jax.numpy as jnp
from jax import lax
from jax.experimental import pallas as pl
from jax.experimental.pallas import tpu as pltpu

F32 = jnp.float32
BF16 = jnp.bfloat16
EPS = 1e-6

D_MODEL = 1024
MIX = 256
CONV_WIDTH = 31
HEADS = 4
HD = 64
CHUNK = 128
_STATIC_CHUNKS = 16
ROPE = 32
KV_RANK = 128
Q_RANK = 256
GRID_W = 64
ROPE_BASE = 10000.0
S5_G = 16
S5_GC = 16
S5_N = 64
S5_STATES = S5_G * S5_N
N_EXPERT = 32
PER_GROUP = 8
N_GROUP = 4
MOE_FF = 256
LANE = 128
Z_COLS = 2688
VMEM_LIMIT = 56 * 1024 * 1024


def _cp(*sem):
    return pltpu.CompilerParams(dimension_semantics=sem, vmem_limit_bytes=VMEM_LIMIT)


def _rms(x):
    return x * lax.rsqrt(jnp.mean(x * x, axis=-1, keepdims=True) + EPS)


def _sigmoid(x):
    return 1.0 / (1.0 + jnp.exp(-x))


def _bdot(a, b):
    return jnp.dot(a.astype(BF16), b.astype(BF16), preferred_element_type=F32)


def _mod_kernel(c_ref, w_ref, b_ref, o_ref):
    c = c_ref[...]
    o_ref[0] = _bdot(c * _sigmoid(c), w_ref[0]) + b_ref[0]


def _modulation(c_all, w_mod, b_mod):
    n_layer, d, n = w_mod.shape
    tn = 1536
    return pl.pallas_call(
        _mod_kernel,
        out_shape=jax.ShapeDtypeStruct((n_layer, 16, n), F32),
        grid=(n_layer, n // tn),
        in_specs=[pl.BlockSpec((16, d), lambda l, j: (0, 0)),
                  pl.BlockSpec((1, d, tn), lambda l, j: (l, 0, j)),
                  pl.BlockSpec((1, 1, tn), lambda l, j: (l, 0, j))],
        out_specs=pl.BlockSpec((1, 16, tn), lambda l, j: (l, 0, j)),
        compiler_params=_cp("parallel", "parallel"),
        name="modulation",
    )(c_all, w_mod, b_mod.reshape(n_layer, 1, n))


def _s5_disc_kernel(are_ref, aim_ref, ldt_ref, bre_ref, bim_ref, abre_ref, abim_ref, bbre_ref, bbim_ref):
    a_re = are_ref[...]
    a_im = aim_ref[...]
    dt = jnp.exp(ldt_ref[...])
    mag = jnp.exp(a_re * dt)
    ab_re = mag * jnp.cos(a_im * dt)
    ab_im = mag * jnp.sin(a_im * dt)
    den = a_re * a_re + a_im * a_im
    f_re = ((ab_re - 1.0) * a_re + ab_im * a_im) / den
    f_im = (ab_im * a_re - (ab_re - 1.0) * a_im) / den
    b_re = bre_ref[...]
    b_im = bim_ref[...]
    abre_ref[...] = ab_re
    abim_ref[...] = ab_im
    bbre_ref[...] = f_re * b_re - f_im * b_im
    bbim_ref[...] = f_re * b_im + f_im * b_re


def _s5_discretise(a_re, a_im, log_dt, b_re, b_im):
    n_layer = a_re.shape[0]
    rows = n_layer * 2 * S5_G
    cols = S5_N * S5_GC
    rep = lambda a: jnp.repeat(a.reshape(rows, S5_N), S5_GC, axis=1)
    ldt = jnp.broadcast_to(log_dt.reshape(rows, 1), (rows, cols))
    spec = pl.BlockSpec((rows, cols), lambda: (0, 0))
    ab_re, ab_im, bb_re, bb_im = pl.pallas_call(
        _s5_disc_kernel,
        out_shape=[jax.ShapeDtypeStruct((rows, cols), F32)] * 4,
        in_specs=[spec] * 5,
        out_specs=[spec] * 4,
        name="s5_discretise",
    )(rep(a_re), rep(a_im), ldt, b_re.reshape(rows, cols), b_im.reshape(rows, cols))
    pick = lambda a: a[:, ::S5_GC].reshape(n_layer, 2, S5_STATES)
    shp = (n_layer, 2, S5_G, S5_N, S5_GC)
    return pick(ab_re), pick(ab_im), bb_re.reshape(shp), bb_im.reshape(shp)


def _pre_kernel(x_ref, mod_ref, g_ref, w_ref, za_ref, zb_ref, zg_ref, zc_ref, zd_ref):
    m = mod_ref[0]
    h = _rms(x_ref[...]) * g_ref[...] * (1.0 + m[1:2]) + m[0:1]
    hb = h.astype(BF16)
    col = 0
    for o_ref in (za_ref, zb_ref, zg_ref, zc_ref, zd_ref):
        n = o_ref.shape[-1]
        o_ref[...] = jnp.dot(hb, w_ref[:, col:col + n], preferred_element_type=F32).astype(o_ref.dtype)
        col += n


def _pre_mixer(x, mod, row_fn, norm_g, w_in_ext, tm):
    n = x.shape[0]
    widths = (512, 1024, 2 * LANE, 640, MIX)
    return pl.pallas_call(
        _pre_kernel,
        out_shape=[jax.ShapeDtypeStruct((n, w), dt) for w, dt in zip(widths, (BF16, F32, F32, BF16, F32))],
        grid=(n // tm,),
        in_specs=[pl.BlockSpec((tm, D_MODEL), lambda i: (i, 0)),
                  pl.BlockSpec((1, 6, D_MODEL), lambda i: (row_fn(i, tm), 0, 0)),
                  pl.BlockSpec((1, D_MODEL), lambda i: (0, 0)),
                  pl.BlockSpec((D_MODEL, Z_COLS), lambda i: (0, 0))],
        out_specs=[pl.BlockSpec((tm, w), lambda i: (i, 0)) for w in widths],
        compiler_params=_cp("parallel"),
        name="pre_mixer",
    )(x, mod, norm_g, w_in_ext)


_CONV_PAD = 16
_CONV_TT = 128


def _conv_kernel(za_ref, w_ref, b_ref, lg_ref, lb_ref, o_ref, hp_ref, sh_ref):
    t_len = o_ref.shape[1]
    u = za_ref[0].astype(F32)
    hp_ref[0:_CONV_PAD, :] = jnp.zeros((_CONV_PAD, MIX), F32)
    hp_ref[_CONV_PAD + t_len:2 * _CONV_PAD + t_len, :] = jnp.zeros((_CONV_PAD, MIX), F32)
    hp_ref[_CONV_PAD:_CONV_PAD + t_len, :] = u[:, :MIX] * _sigmoid(u[:, MIX:])
    span = t_len + 2 * _CONV_PAD - 8
    for off in range(8):
        sh_ref[off, 0:span, :] = hp_ref[off:off + span, :]
    w = w_ref[...]
    half = CONV_WIDTH // 2
    for t0 in range(0, t_len, _CONV_TT):
        acc = jnp.zeros((_CONV_TT, MIX), F32) + b_ref[...]
        for k in range(CONV_WIDTH):
            start = t0 + _CONV_PAD - half + k
            aligned = start // 8 * 8
            acc = acc + sh_ref[start - aligned, aligned:aligned + _CONV_TT, :] * w[k:k + 1, :]
        mu = jnp.mean(acc, axis=-1, keepdims=True)
        cen = acc - mu
        var = jnp.mean(cen * cen, axis=-1, keepdims=True)
        yn = cen * lax.rsqrt(var + EPS) * lg_ref[...] + lb_ref[...]
        o_ref[0, t0:t0 + _CONV_TT, :] = (yn * _sigmoid(yn)).astype(o_ref.dtype)


def _conv_module(za, w, b, ln_g, ln_b):
    bsz, t_len, _ = za.shape
    vec = pl.BlockSpec((1, MIX), lambda i: (0, 0))
    return pl.pallas_call(
        _conv_kernel,
        out_shape=jax.ShapeDtypeStruct((bsz, t_len, MIX), BF16),
        grid=(bsz,),
        in_specs=[pl.BlockSpec((1, t_len, 2 * MIX), lambda i: (i, 0, 0)),
                  pl.BlockSpec((32, MIX), lambda i: (0, 0)), vec, vec, vec],
        out_specs=pl.BlockSpec((1, t_len, MIX), lambda i: (i, 0, 0)),
        scratch_shapes=[pltpu.VMEM((t_len + 2 * _CONV_PAD, MIX), F32),
                        pltpu.VMEM((8, t_len + 2 * _CONV_PAD, MIX), F32)],
        compiler_params=_cp("parallel"),
        name="conv_module",
    )(za, w, b, ln_g, ln_b)


def _log_sigmoid(x):
    return jnp.minimum(x, 0.0) - jnp.log(1.0 + jnp.exp(-jnp.abs(x)))


def _split3_dot(tri, x):
    hi = x.astype(BF16)
    r1 = x - hi.astype(F32)
    mid = r1.astype(BF16)
    lo = (r1 - mid.astype(F32)).astype(BF16)
    dot = lambda v: jnp.dot(tri, v, preferred_element_type=F32)
    return dot(hi) + dot(mid) + dot(lo)


def _chunk_rows(c):
    if isinstance(c, int):
        return slice(c * CHUNK, (c + 1) * CHUNK)
    return pl.ds(pl.multiple_of(c * CHUNK, CHUNK), CHUNK)


def _loop(n, body, init):
    if n <= _STATIC_CHUNKS:
        for i in range(n):
            init = body(i, init)
        return init
    return lax.fori_loop(0, n, body, init)


def _split2(x):
    hi = x.astype(BF16)
    return hi, (x - hi.astype(F32)).astype(BF16)


def _mlstm_chunk(zb_ref, zg_ref, gb_ref, qt_sc, qbd_sc, vt_sc, vbd_sc, ct_sc, nb_sc, c, direction, ms):
    L = CHUNK
    rows = _chunk_rows(c)
    si = lax.broadcasted_iota(jnp.int32, (L, L), 0)
    ti = lax.broadcasted_iota(jnp.int32, (L, L), 1)
    if direction == 0:
        valid = si <= ti
        tri = jnp.where(ti <= si, 1.0, 0.0).astype(BF16)
        last = L - 1
    else:
        valid = si >= ti
        tri = jnp.where(ti >= si, 1.0, 0.0).astype(BF16)
        last = 0
    g_in = zg_ref[0, rows, 0:LANE] + gb_ref[:, 0:LANE]
    g_f = zg_ref[0, rows, LANE:2 * LANE] + gb_ref[:, LANE:2 * LANE]
    bc = _split3_dot(tri, _log_sigmoid(g_f))
    r = g_in - bc
    g_in_t = g_in.T
    bct = bc.T
    k = zb_ref[0, rows, MIX:2 * MIX]
    k_hi, k_lo = _split2(k)
    qt = qt_sc[c]
    st_all = jnp.dot(k_hi, qbd_sc[c], preferred_element_type=F32)
    sts, inters, floors, colsums, ws, decays, new_m = [], [], [], [], [], [], []
    for h in range(HEADS):
        j = direction * HEADS + h
        i_row = g_in_t[j:j + 1, :]
        b_row = bct[j:j + 1, :]
        m_prev = ms[h]
        rb = jnp.where(valid, jnp.broadcast_to(r[:, j:j + 1], (L, L)), -jnp.inf)
        c_row = jnp.maximum(m_prev, jnp.max(rb, axis=0, keepdims=True))
        st = st_all[:, h * L:(h + 1) * L] * jnp.exp(rb - c_row)
        sts.append(st.astype(BF16))
        colsums.append(jnp.sum(st, axis=0, keepdims=True))
        inters.append(jnp.exp(m_prev - c_row))
        floors.append(jnp.exp(-(b_row + c_row)))
        b_last = b_row[:, last:last + 1]
        lw = b_last - b_row + i_row
        m_new = jnp.maximum(b_last + m_prev, jnp.max(lw, axis=-1, keepdims=True))
        decays.append(jnp.exp(b_last + m_prev - m_new))
        ws.append(jnp.exp(lw - m_new))
        new_m.append(m_new)
    per_head_rows = lambda vs: jnp.concatenate([jnp.broadcast_to(v, (HD, L)) for v in vs], axis=0)
    ct = ct_sc[direction]
    nb = nb_sc[direction]
    num = (per_head_rows(inters) * jnp.dot(ct.astype(BF16), qt, preferred_element_type=F32)
           + jnp.dot(vbd_sc[c], jnp.concatenate(sts, axis=0), preferred_element_type=F32))
    n_hi, n_lo = _split2(nb)
    qn = jnp.dot(n_hi, qt, preferred_element_type=F32) + jnp.dot(n_lo, qt, preferred_element_type=F32)
    dens = [jnp.maximum(jnp.abs(inters[h] * qn[h:h + 1, :] + colsums[h]), floors[h]) for h in range(HEADS)]
    ht = num / per_head_rows(dens)
    lane = lax.broadcasted_iota(jnp.int32, (1, MIX), 1) // HD
    dec_row = sum(jnp.where(lane == h, decays[h], 0.0) for h in range(HEADS))
    upd = jnp.dot((vt_sc[c] * per_head_rows(ws)).astype(BF16), k_hi, preferred_element_type=F32)
    same_head = (lax.broadcasted_iota(jnp.int32, (MIX, MIX), 0) // HD
                 == lax.broadcasted_iota(jnp.int32, (MIX, MIX), 1) // HD)
    ct_sc[direction] = jnp.where(same_head, ct * dec_row + upd, 0.0)
    w_hi, w_lo = _split2(jnp.concatenate(ws + [jnp.zeros((8 - HEADS, L), F32)], axis=0))
    wk = (jnp.dot(w_hi, k_hi, preferred_element_type=F32) + jnp.dot(w_lo, k_hi, preferred_element_type=F32)
          + jnp.dot(w_hi, k_lo, preferred_element_type=F32))
    own = lax.broadcasted_iota(jnp.int32, (8, MIX), 0) == lax.broadcasted_iota(jnp.int32, (8, MIX), 1) // HD
    nb_sc[direction] = jnp.where(own, nb * dec_row + wk, 0.0)
    return ht, tuple(new_m)


def _mlstm_kernel(zb_ref, zg_ref, gb_ref, ng_ref, c0_ref, n0_ref, m0_ref,
                  y_ref, c_ref, n_ref, m_ref, qt_sc, qbd_sc, vt_sc, vbd_sc, ht_sc, ct_sc, nb_sc):
    t_len = y_ref.shape[1]
    nc = t_len // CHUNK
    L = CHUNK
    qbd_sc[...] = jnp.zeros_like(qbd_sc)
    vbd_sc[...] = jnp.zeros_like(vbd_sc)
    ct_sc[...] = jnp.zeros_like(ct_sc)
    nb_sc[...] = jnp.zeros_like(nb_sc)

    def transpose_chunk(c, carry):
        rows = _chunk_rows(c)
        qt = (zb_ref[0, rows, 0:MIX] * (HD ** -0.5)).T.astype(BF16)
        vt = zb_ref[0, rows, 2 * MIX:3 * MIX].T
        qt_sc[c] = qt
        vt_sc[c] = vt
        for h in range(HEADS):
            sl = slice(h * HD, (h + 1) * HD)
            qbd_sc[c, sl, h * L:(h + 1) * L] = qt[sl, :]
            vbd_sc[c, sl, h * L:(h + 1) * L] = vt[sl, :].astype(BF16)
        return carry

    _loop(nc, transpose_chunk, 0)

    for j in range(2 * HEADS):
        d, h = divmod(j, HEADS)
        sl = slice(h * HD, (h + 1) * HD)
        ct_sc[d, sl, sl] = c0_ref[0, j]
        nb_sc[d, h:h + 1, sl] = n0_ref[0, j:j + 1, :]
    m_init = tuple(tuple(m0_ref[0, j:j + 1, 0:1] for j in range(d * HEADS, (d + 1) * HEADS)) for d in range(2))

    def scan_body(i, ms):
        cb = nc - 1 - i
        scr = (zb_ref, zg_ref, gb_ref, qt_sc, qbd_sc, vt_sc, vbd_sc, ct_sc, nb_sc)
        ht_sc[0, i], ms_f = _mlstm_chunk(*scr, i, 0, ms[0])
        ht_sc[1, cb], ms_b = _mlstm_chunk(*scr, cb, 1, ms[1])
        return ms_f, ms_b

    m_fin = _loop(nc, scan_body, m_init)
    for j in range(2 * HEADS):
        d, h = divmod(j, HEADS)
        sl = slice(h * HD, (h + 1) * HD)
        c_ref[0, j] = ct_sc[d, sl, sl]
        n_ref[0, j:j + 1, :] = nb_sc[d, h:h + 1, sl]
        m_ref[0, j:j + 1, :] = jnp.broadcast_to(m_fin[d][h], (1, LANE))

    def out_body(c, carry):
        rows = _chunk_rows(c)
        normed = []
        for h in range(HEADS):
            tot = ht_sc[0, c, h * HD:(h + 1) * HD, :] + ht_sc[1, c, h * HD:(h + 1) * HD, :]
            normed.append(tot * lax.rsqrt(jnp.mean(tot * tot, axis=0, keepdims=True) + EPS))
        hn = jnp.concatenate(normed, axis=0).T
        y_ref[0, rows, :] = (hn * ng_ref[...] * _sigmoid(zb_ref[0, rows, 3 * MIX:4 * MIX])).astype(y_ref.dtype)
        return carry

    _loop(nc, out_body, 0)


def _mlstm_mixer(zb, zg, gate_b, norm_g, c0, n0, m0):
    bsz, t_len, _ = zb.shape
    nd = 2 * HEADS
    return pl.pallas_call(
        _mlstm_kernel,
        out_shape=[jax.ShapeDtypeStruct((bsz, t_len, MIX), BF16),
                   jax.ShapeDtypeStruct((bsz, nd, HD, HD), F32),
                   jax.ShapeDtypeStruct((bsz, nd, HD), F32),
                   jax.ShapeDtypeStruct((bsz, nd, LANE), F32)],
        grid=(bsz,),
        in_specs=[pl.BlockSpec((1, t_len, 4 * MIX), lambda i: (i, 0, 0)),
                  pl.BlockSpec((1, t_len, 2 * LANE), lambda i: (i, 0, 0)),
                  pl.BlockSpec((1, 2 * LANE), lambda i: (0, 0)),
                  pl.BlockSpec((1, MIX), lambda i: (0, 0)),
                  pl.BlockSpec((1, nd, HD, HD), lambda i: (i, 0, 0, 0)),
                  pl.BlockSpec((1, nd, HD), lambda i: (i, 0, 0)),
                  pl.BlockSpec((1, nd, LANE), lambda i: (i, 0, 0))],
        out_specs=[pl.BlockSpec((1, t_len, MIX), lambda i: (i, 0, 0)),
                   pl.BlockSpec((1, nd, HD, HD), lambda i: (i, 0, 0, 0)),
                   pl.BlockSpec((1, nd, HD), lambda i: (i, 0, 0)),
                   pl.BlockSpec((1, nd, LANE), lambda i: (i, 0, 0))],
        scratch_shapes=[pltpu.VMEM((t_len // CHUNK, MIX, CHUNK), BF16),
                        pltpu.VMEM((t_len // CHUNK, MIX, HEADS * CHUNK), BF16),
                        pltpu.VMEM((t_len // CHUNK, MIX, CHUNK), F32),
                        pltpu.VMEM((t_len // CHUNK, MIX, HEADS * CHUNK), BF16),
                        pltpu.VMEM((2, t_len // CHUNK, MIX, CHUNK), F32),
                        pltpu.VMEM((2, MIX, MIX), F32),
                        pltpu.VMEM((2, 8, MIX), F32)],
        compiler_params=_cp("parallel"),
        name="mlstm_mixer",
    )(zb, zg, gate_b, norm_g, c0, n0, m0)


_ATT_TQ = 512


def _mla_kernel(*refs, past):
    if past:
        (zc_ref, cos_ref, sin_ref, gq_ref, gkv_ref, wqa_ref, wqb_ref, wk_ref, wv_ref,
         pckv_ref, pkr_ref, y_ref, ckv_ref, q_sc, k_sc, v_sc) = refs
    else:
        (zc_ref, cos_ref, sin_ref, gq_ref, gkv_ref, wqa_ref, wqb_ref, wk_ref, wv_ref,
         y_ref, ckv_ref, q_sc, k_sc, v_sc) = refs
    t_len = y_ref.shape[1]
    cosf = cos_ref[...]
    sinf = sin_ref[...]
    zc = lambda a, b: zc_ref[0, :, a:b].astype(F32)
    cq = _rms(zc(0, Q_RANK)) * gq_ref[...]
    ckv = _rms(zc(Q_RANK, Q_RANK + KV_RANK)) * gkv_ref[...]
    ckv_ref[0] = ckv
    kr = (zc(Q_RANK + KV_RANK, Q_RANK + KV_RANK + LANE) * cosf
          + zc(Q_RANK + KV_RANK + LANE, Q_RANK + KV_RANK + 2 * LANE) * sinf)
    cqb = cq.astype(BF16)
    ckvb = ckv.astype(BF16)
    pair = lambda a: jnp.concatenate([a, a], axis=-1)
    cos2, sin2, kr2 = pair(cosf), pair(sinf), pair(kr)
    for hp in range(HEADS // 2):
        sl = slice(hp * 2 * LANE, (hp + 1) * 2 * LANE)
        qa = jnp.dot(cqb, wqa_ref[:, sl], preferred_element_type=F32)
        qb = jnp.dot(cqb, wqb_ref[:, sl], preferred_element_type=F32)
        q_sc[:, sl] = (qa * cos2 + qb * sin2).astype(BF16)
        kn = jnp.dot(ckvb, wk_ref[:, sl], preferred_element_type=F32)
        k_sc[past:past + t_len, sl] = (kn + kr2).astype(BF16)
        v_sc[past:past + t_len, sl] = jnp.dot(ckvb, wv_ref[:, sl], preferred_element_type=F32).astype(BF16)
    if past:
        pckv = pckv_ref[0].astype(BF16)
        pkr2 = pair(pkr_ref[0])
        for hp in range(HEADS // 2):
            sl = slice(hp * 2 * LANE, (hp + 1) * 2 * LANE)
            k_sc[0:past, sl] = (jnp.dot(pckv, wk_ref[:, sl], preferred_element_type=F32) + pkr2).astype(BF16)
            v_sc[0:past, sl] = jnp.dot(pckv, wv_ref[:, sl], preferred_element_type=F32).astype(BF16)

    tq = min(_ATT_TQ, t_len)

    def q_block(i, carry):
        rows = pl.ds(pl.multiple_of(i * tq, tq), tq)
        outs = []
        for h in range(HEADS):
            sl = slice(h * LANE, (h + 1) * LANE)
            s = lax.dot_general(q_sc[rows, sl], k_sc[:, sl], (((1,), (1,)), ((), ())),
                                preferred_element_type=F32)
            p = jnp.exp(s - jnp.max(s, axis=-1, keepdims=True))
            o = jnp.dot(p.astype(BF16), v_sc[:, sl], preferred_element_type=F32)
            outs.append(o / jnp.sum(p, axis=-1, keepdims=True))
        y_ref[0, rows, 0:LANE] = (outs[0] + outs[1]).astype(y_ref.dtype)
        y_ref[0, rows, LANE:2 * LANE] = (outs[2] + outs[3]).astype(y_ref.dtype)
        return carry

    assert t_len % tq == 0
    n_blocks = t_len // tq
    if n_blocks % 2 == 0:
        lax.fori_loop(0, n_blocks // 2, lambda i, c: q_block(2 * i + 1, q_block(2 * i, c)), 0)
    else:
        lax.fori_loop(0, n_blocks, q_block, 0)


def _mla_mixer(zc, cosf, sinf, gq, gkv, wqa, wqb, wk, wv, past_ckv=None, past_kr=None):
    bsz, t_len, zw = zc.shape
    past = 0 if past_ckv is None else past_ckv.shape[1]
    full = lambda a: pl.BlockSpec(a.shape, lambda i: (0,) * a.ndim)
    args = [zc, cosf, sinf, gq, gkv, wqa, wqb, wk, wv]
    in_specs = [pl.BlockSpec((1, t_len, zw), lambda i: (i, 0, 0))] + [full(a) for a in args[1:]]
    if past:
        args += [past_ckv, past_kr]
        in_specs += [pl.BlockSpec((1, past, KV_RANK), lambda i: (i, 0, 0)),
                     pl.BlockSpec((1, past, LANE), lambda i: (i, 0, 0))]
    wide = HEADS * LANE
    return pl.pallas_call(
        functools.partial(_mla_kernel, past=past),
        out_shape=[jax.ShapeDtypeStruct((bsz, t_len, MIX), BF16),
                   jax.ShapeDtypeStruct((bsz, t_len, KV_RANK), F32)],
        grid=(bsz,),
        in_specs=in_specs,
        out_specs=[pl.BlockSpec((1, t_len, MIX), lambda i: (i, 0, 0)),
                   pl.BlockSpec((1, t_len, KV_RANK), lambda i: (i, 0, 0))],
        scratch_shapes=[pltpu.VMEM((t_len, wide), BF16),
                        pltpu.VMEM((past + t_len, wide), BF16),
                        pltpu.VMEM((past + t_len, wide), BF16)],
        compiler_params=_cp("parallel"),
        name="mla_mixer",
    )(*args)


_S5_SEQ = 8
_S5_TT = 256
_S5_PARTS = 8


def _s5_kernel(zd_ref, bd_ref, cd_ref, are_ref, aim_ref, s0re_ref, s0im_ref,
               y_ref, fre_ref, fim_ref, bre_sc, bim_sc, sre_sc, sim_sc):
    direction = pl.program_id(0)
    k = pl.program_id(2)
    tt = _S5_TT
    rows = _S5_SEQ * tt

    @pl.when(k == 0)
    def _():
        sre_sc[...] = s0re_ref[0]
        sim_sc[...] = s0im_ref[0]

    part_steps = tt // _S5_PARTS
    part_rows = part_steps * _S5_SEQ

    def run(backward):
        parts = list(range(_S5_PARTS))[::-1] if backward else list(range(_S5_PARTS))
        a_re = jnp.broadcast_to(are_ref[0], (_S5_SEQ, S5_STATES))
        a_im = jnp.broadcast_to(aim_ref[0], (_S5_SEQ, S5_STATES))
        for p in parts:
            r = slice(p * part_rows, (p + 1) * part_rows)
            u = zd_ref[p * part_steps:(p + 1) * part_steps].reshape(part_rows, MIX).astype(BF16)
            bre_sc[r, :] = jnp.dot(u, bd_ref[0, :, 0:S5_STATES], preferred_element_type=F32)
            bim_sc[r, :] = jnp.dot(u, bd_ref[0, :, S5_STATES:2 * S5_STATES], preferred_element_type=F32)
        s_re, s_im = sre_sc[...], sim_sc[...]
        for p in parts:
            steps = range(p * part_steps, (p + 1) * part_steps)
            for t in (reversed(steps) if backward else steps):
                sel = slice(t * _S5_SEQ, (t + 1) * _S5_SEQ)
                s_re, s_im = (a_re * s_re - a_im * s_im + bre_sc[sel, :],
                              a_re * s_im + a_im * s_re + bim_sc[sel, :])
                bre_sc[sel, :] = s_re
                bim_sc[sel, :] = s_im
            r = slice(p * part_rows, (p + 1) * part_rows)
            y = (jnp.dot(bre_sc[r, :].astype(BF16), cd_ref[0, 0:S5_STATES, :], preferred_element_type=F32)
                 + jnp.dot(bim_sc[r, :].astype(BF16), cd_ref[0, S5_STATES:2 * S5_STATES, :],
                           preferred_element_type=F32))
            y_ref[0, p * part_steps:(p + 1) * part_steps] = y.reshape(part_steps, _S5_SEQ, MIX)
        sre_sc[...] = s_re
        sim_sc[...] = s_im

    pl.when(direction == 0)(lambda: run(False))
    pl.when(direction == 1)(lambda: run(True))

    @pl.when(k == pl.num_programs(2) - 1)
    def _():
        fre_ref[0] = sre_sc[...]
        fim_ref[0] = sim_sc[...]


def _s5_scan(zd, bd, cd, ab_re, ab_im, s0_re, s0_im):
    t_len, bsz, _ = zd.shape
    nt = t_len // _S5_TT
    tile = lambda d, k: k + d * (nt - 1 - 2 * k)
    st = pl.BlockSpec((1, _S5_SEQ, S5_STATES), lambda d, g, k: (d, g, 0))
    return pl.pallas_call(
        _s5_kernel,
        out_shape=[jax.ShapeDtypeStruct((2, t_len, bsz, MIX), F32),
                   jax.ShapeDtypeStruct((2, bsz, S5_STATES), F32),
                   jax.ShapeDtypeStruct((2, bsz, S5_STATES), F32)],
        grid=(2, bsz // _S5_SEQ, nt),
        in_specs=[pl.BlockSpec((_S5_TT, _S5_SEQ, MIX), lambda d, g, k: (tile(d, k), g, 0)),
                  pl.BlockSpec((1, MIX, 2 * S5_STATES), lambda d, g, k: (d, 0, 0)),
                  pl.BlockSpec((1, 2 * S5_STATES, MIX), lambda d, g, k: (d, 0, 0)),
                  pl.BlockSpec((1, 1, S5_STATES), lambda d, g, k: (d, 0, 0)),
                  pl.BlockSpec((1, 1, S5_STATES), lambda d, g, k: (d, 0, 0)),
                  st, st],
        out_specs=[pl.BlockSpec((1, _S5_TT, _S5_SEQ, MIX), lambda d, g, k: (d, tile(d, k), g, 0)),
                   st, st],
        scratch_shapes=[pltpu.VMEM((_S5_SEQ * _S5_TT, S5_STATES), F32),
                        pltpu.VMEM((_S5_SEQ * _S5_TT, S5_STATES), F32),
                        pltpu.VMEM((_S5_SEQ, S5_STATES), F32),
                        pltpu.VMEM((_S5_SEQ, S5_STATES), F32)],
        compiler_params=_cp("parallel", "parallel", "arbitrary"),
        name="s5_scan",
    )(zd, bd, cd, ab_re, ab_im, s0_re, s0_im)


def _gelu_tanh(x):
    return 0.5 * x * (1.0 + jnp.tanh(math.sqrt(2.0 / math.pi) * (x + 0.044715 * x * x * x)))


def _route(logits):
    lane = lax.broadcasted_iota(jnp.int32, logits.shape, 1).astype(F32)
    big = float(LANE)
    neg = -jnp.inf
    g_mask = (lane >= N_EXPERT) & (lane < N_EXPERT + N_GROUP)
    gl = jnp.where(g_mask, logits, neg)
    g_max = jnp.max(gl, axis=-1, keepdims=True)
    g_idx = jnp.min(jnp.where(gl == g_max, lane, big), axis=-1, keepdims=True) - N_EXPERT
    g_sel = 1.0 / jnp.sum(jnp.where(g_mask, jnp.exp(logits - g_max), 0.0), axis=-1, keepdims=True)
    lo = g_idx * PER_GROUP
    el = jnp.where((lane >= lo) & (lane < lo + PER_GROUP), logits, neg)
    v1 = jnp.max(el, axis=-1, keepdims=True)
    i1 = jnp.min(jnp.where(el == v1, lane, big), axis=-1, keepdims=True)
    el2 = jnp.where(lane == i1, neg, el)
    v2 = jnp.max(el2, axis=-1, keepdims=True)
    i2 = jnp.min(jnp.where(el2 == v2, lane, big), axis=-1, keepdims=True)
    e2 = jnp.exp(v2 - v1)
    w1 = g_sel / (1.0 + e2)
    return jnp.where(lane == i1, w1, jnp.where(lane == i2, w1 * e2, 0.0))


_POST_SUBTILES = 2


def _post_kernel(x_ref, ya_ref, yb_ref, yc_ref, ys_ref, zd_ref, mod_ref, wo_ref, d_ref, wglu_ref, bglu_ref,
                 g2_ref, wr_ref, br_ref, x1_ref, h2_ref, gate_ref):
    m = mod_ref[0]
    sub = x_ref.shape[0] // _POST_SUBTILES
    for s in range(_POST_SUBTILES):
        r = slice(s * sub, (s + 1) * sub)
        ys = ys_ref[0, r, :] + ys_ref[1, r, :] + d_ref[...] * zd_ref[r, :]
        gl = _bdot(_gelu_tanh(ys), wglu_ref[...]) + bglu_ref[...]
        yd = gl[:, 0:MIX] * _sigmoid(gl[:, MIX:2 * MIX])
        mix = (_bdot(ya_ref[r, :], wo_ref[0:MIX, :]) + _bdot(yb_ref[r, :], wo_ref[MIX:2 * MIX, :])
               + _bdot(yc_ref[r, :], wo_ref[2 * MIX:3 * MIX, :]) + _bdot(yd, wo_ref[3 * MIX:4 * MIX, :]))
        x1 = x_ref[r, :] + m[2:3] * mix
        x1_ref[r, :] = x1
        h2b = (_rms(x1) * g2_ref[...] * (1.0 + m[4:5]) + m[3:4]).astype(BF16)
        h2_ref[r, :] = h2b
        gate_ref[r, :] = _route(jnp.dot(h2b, wr_ref[...], preferred_element_type=F32) + br_ref[...])


def _post_mixer(x, ya, yb, yc, ys, zd, mod, row_fn, w_out, s5_d, w_glu, b_glu, norm2_g, w_router, b_router, tm):
    n = x.shape[0]
    tok = lambda w: pl.BlockSpec((tm, w), lambda i: (i, 0))
    full = lambda a: pl.BlockSpec(a.shape, lambda i: (0,) * a.ndim)
    return pl.pallas_call(
        _post_kernel,
        out_shape=[jax.ShapeDtypeStruct((n, D_MODEL), F32),
                   jax.ShapeDtypeStruct((n, D_MODEL), BF16),
                   jax.ShapeDtypeStruct((n, LANE), F32)],
        grid=(n // tm,),
        in_specs=[tok(D_MODEL), tok(MIX), tok(MIX), tok(MIX),
                  pl.BlockSpec((2, tm, MIX), lambda i: (0, i, 0)), tok(MIX),
                  pl.BlockSpec((1, 6, D_MODEL), lambda i: (row_fn(i, tm), 0, 0)),
                  full(w_out), full(s5_d), full(w_glu), full(b_glu), full(norm2_g),
                  full(w_router), full(b_router)],
        out_specs=[tok(D_MODEL), tok(D_MODEL), tok(LANE)],
        compiler_params=_cp("parallel"),
        name="post_mixer",
    )(x, ya, yb, yc, ys, zd, mod, w_out, s5_d, w_glu, b_glu, norm2_g, w_router, b_router)


_MOE_EPS = 4
_MOE_TM = 1024


_MOE_ALIGN = 16
_MOE_ROWS = 2 * _MOE_TM + N_EXPERT * _MOE_ALIGN
_MOE_CH = 128
_MOE_GB = 512
_MOE_CB = 256


def _moe_kernel(h_ref, gate_ref, x_ref, mod_ref, w1_ref, w3_ref, w2_ref, o_ref,
                xs_sc, ys_sc, col_sc, row_sc, start_sm, rows_sm, done_sm):
    j = pl.program_id(1)
    tm = h_ref.shape[0]

    @pl.when(j == 0)
    def _dispatch():
        gate = gate_ref[...]
        cnt_row = jnp.sum(jnp.where(gate != 0.0, 1.0, 0.0), axis=0, keepdims=True)
        units_row = jnp.floor((cnt_row + (_MOE_ALIGN - 1)) * (1.0 / _MOE_ALIGN))
        li = lax.broadcasted_iota(jnp.int32, (LANE, LANE), 0)
        lj = lax.broadcasted_iota(jnp.int32, (LANE, LANE), 1)
        before = jnp.where(li < lj, 1.0, 0.0).astype(BF16)
        start_row = jnp.dot(jnp.broadcast_to(units_row, (8, LANE)).astype(BF16), before,
                            preferred_element_type=F32) * _MOE_ALIGN
        start_i = start_row.astype(jnp.int32)
        cnt_i = cnt_row.astype(jnp.int32)
        for e in range(N_EXPERT):
            start_sm[e] = start_i[0, e]
            rows_sm[e] = cnt_i[0, e]
        gt = gate.T[0:N_EXPERT, :]
        chosen = gt != 0.0
        ones = jnp.where(chosen, 1.0, 0.0)
        cnt = jnp.sum(ones, axis=-1, keepdims=True)
        units = jnp.floor((cnt + (_MOE_ALIGN - 1)) * (1.0 / _MOE_ALIGN))
        ei = lax.broadcasted_iota(jnp.int32, (N_EXPERT, N_EXPERT), 0)
        ej = lax.broadcasted_iota(jnp.int32, (N_EXPERT, N_EXPERT), 1)
        start = jnp.dot(jnp.where(ej < ei, 1.0, 0.0).astype(BF16),
                        jnp.broadcast_to(units, (N_EXPERT, LANE)).astype(BF16),
                        preferred_element_type=F32)[:, 0:1] * _MOE_ALIGN
        ti = lax.broadcasted_iota(jnp.int32, (tm, tm), 0)
        tj = lax.broadcasted_iota(jnp.int32, (tm, tm), 1)
        earlier = jnp.where(ti < tj, 1.0, 0.0).astype(BF16)
        pos = start + jnp.dot(ones.astype(BF16), earlier, preferred_element_type=F32)
        p_a = jnp.min(jnp.where(chosen, pos, float(_MOE_ROWS)), axis=0, keepdims=True)
        p_b = jnp.max(jnp.where(chosen, pos, -1.0), axis=0, keepdims=True)
        g_a = jnp.sum(jnp.where(chosen & (pos == p_a), gt, 0.0), axis=0, keepdims=True)
        g_b = jnp.sum(jnp.where(chosen & (pos == p_b) & (p_b != p_a), gt, 0.0), axis=0, keepdims=True)
        row_sc[...] = jnp.concatenate([p_a, p_b, jnp.zeros((6, tm), F32)], axis=0)
        packed = jnp.concatenate([p_a, p_b, g_a, g_b, jnp.zeros((LANE - 4, tm), F32)], axis=0)
        col_sc[...] = packed.T
        xs_sc[_MOE_ROWS:_MOE_ROWS + _MOE_CH, :] = jnp.zeros((_MOE_CH, D_MODEL), BF16)
        @pl.when(pl.program_id(0) == 0)
        def _():
            ys_sc[...] = jnp.zeros_like(ys_sc)

        o_ref[...] = x_ref[...]
        done_sm[0] = 0
        done_sm[1] = 0

    n_blocks = _MOE_ROWS // _MOE_GB
    e_last = j * _MOE_EPS + (_MOE_EPS - 1)
    seg_end = start_sm[e_last] + rows_sm[e_last]
    last_step = j == pl.num_programs(1) - 1

    def gather_block(b, carry):
        r0 = pl.multiple_of(b * _MOE_GB, _MOE_GB)
        ri = (lax.broadcasted_iota(jnp.int32, (_MOE_GB, tm), 0) + r0).astype(F32)
        sel = jnp.where((ri == row_sc[0:1, :]) | (ri == row_sc[1:2, :]), 1.0, 0.0).astype(BF16)
        xs_sc[pl.ds(r0, _MOE_GB), :] = jnp.dot(sel, h_ref[...], preferred_element_type=F32).astype(BF16)
        return carry

    gathered = jnp.minimum((seg_end + _MOE_CH + _MOE_GB - 1) // _MOE_GB, n_blocks)
    lax.fori_loop(done_sm[0], gathered, gather_block, 0)
    done_sm[0] = gathered

    def expert_rows(el, i):
        rows = pl.ds(pl.multiple_of(start_sm[j * _MOE_EPS + el] + i * _MOE_CH, _MOE_ALIGN), _MOE_CH)
        xs = xs_sc[rows, :]
        h1 = jnp.dot(xs, w1_ref[el], preferred_element_type=F32)
        h3 = jnp.dot(xs, w3_ref[el], preferred_element_type=F32)
        return rows, jnp.dot((h1 * _sigmoid(h1) * h3).astype(BF16), w2_ref[el], preferred_element_type=F32)

    first = [expert_rows(el, 0) for el in range(_MOE_EPS)]
    for rows, y in first:
        ys_sc[rows, :] = y.astype(BF16)
    for el in range(_MOE_EPS):
        n_rows = rows_sm[j * _MOE_EPS + el]

        def chunk(i, carry, el=el, n_rows=n_rows):
            rows, y = expert_rows(el, i)
            live = lax.broadcasted_iota(jnp.int32, (_MOE_CH, 1), 0) + i * _MOE_CH < n_rows
            ys_sc[rows, :] = jnp.where(live, y, ys_sc[rows, :].astype(F32)).astype(BF16)
            return carry

        lax.fori_loop(1, (n_rows + _MOE_CH - 1) // _MOE_CH, chunk, 0)

    g2 = mod_ref[0][5:6]

    def combine_block(b, carry):
        r0 = pl.multiple_of(b * _MOE_GB, _MOE_GB)
        y = ys_sc[pl.ds(r0, _MOE_GB), :]
        for t0 in range(0, tm, _MOE_CB):
            c = col_sc[t0:t0 + _MOE_CB, :]
            ri = (lax.broadcasted_iota(jnp.int32, (_MOE_CB, _MOE_GB), 1) + r0).astype(F32)
            w = (jnp.where(ri == c[:, 0:1], c[:, 2:3], 0.0)
                 + jnp.where(ri == c[:, 1:2], c[:, 3:4], 0.0)).astype(BF16)
            o_ref[t0:t0 + _MOE_CB, :] += g2 * jnp.dot(w, y, preferred_element_type=F32)
        return carry

    final_rows = (seg_end + _MOE_ALIGN - 1) // _MOE_ALIGN * _MOE_ALIGN
    combined = jnp.where(last_step, n_blocks, final_rows // _MOE_GB)
    lax.fori_loop(done_sm[1], combined, combine_block, 0)
    done_sm[1] = combined


def _moe(h2, gate, x1, mod, row_fn, w1, w3, w2, layer, tm):
    n = h2.shape[0]
    first_blk = layer * (N_EXPERT // _MOE_EPS)
    return pl.pallas_call(
        _moe_kernel,
        out_shape=jax.ShapeDtypeStruct((n, D_MODEL), F32),
        grid=(n // tm, N_EXPERT // _MOE_EPS),
        in_specs=[pl.BlockSpec((tm, D_MODEL), lambda i, j: (i, 0)),
                  pl.BlockSpec((tm, LANE), lambda i, j: (i, 0)),
                  pl.BlockSpec((tm, D_MODEL), lambda i, j: (i, 0)),
                  pl.BlockSpec((1, 6, D_MODEL), lambda i, j: (row_fn(i, tm), 0, 0)),
                  pl.BlockSpec((_MOE_EPS, D_MODEL, MOE_FF), lambda i, j: (first_blk + j, 0, 0)),
                  pl.BlockSpec((_MOE_EPS, D_MODEL, MOE_FF), lambda i, j: (first_blk + j, 0, 0)),
                  pl.BlockSpec((_MOE_EPS, MOE_FF, D_MODEL), lambda i, j: (first_blk + j, 0, 0))],
        out_specs=pl.BlockSpec((tm, D_MODEL), lambda i, j: (i, 0)),
        scratch_shapes=[pltpu.VMEM((_MOE_ROWS + _MOE_CH, D_MODEL), BF16),
                        pltpu.VMEM((_MOE_ROWS + _MOE_CH, D_MODEL), BF16),
                        pltpu.VMEM((tm, LANE), F32),
                        pltpu.VMEM((8, tm), F32),
                        pltpu.SMEM((N_EXPERT,), jnp.int32),
                        pltpu.SMEM((N_EXPERT,), jnp.int32),
                        pltpu.SMEM((2,), jnp.int32)],
        compiler_params=_cp("arbitrary", "arbitrary"),
        name="moe",
    )(h2, gate, x1, mod, w1, w3, w2)


def _final_kernel(x_ref, g_ref, o_ref):
    o_ref[...] = _rms(x_ref[...]) * g_ref[...]


def _final_norm(x, g, tm):
    n = x.shape[0]
    return pl.pallas_call(
        _final_kernel,
        out_shape=jax.ShapeDtypeStruct((n, D_MODEL), F32),
        grid=(n // tm,),
        in_specs=[pl.BlockSpec((tm, D_MODEL), lambda i: (i, 0)),
                  pl.BlockSpec((1, D_MODEL), lambda i: (0, 0))],
        out_specs=pl.BlockSpec((tm, D_MODEL), lambda i: (i, 0)),
        compiler_params=_cp("parallel"),
        name="final_norm",
    )(x, g)


def _rot_cols(w):
    q = ROPE // 4
    return jnp.concatenate([-w[..., q:2 * q], w[..., 0:q], -w[..., 3 * q:4 * q], w[..., 2 * q:3 * q]], axis=-1)


def _rope_slot(w):
    pad = [(0, 0)] * (w.ndim - 1) + [(HD, LANE - HD - ROPE)]
    return jnp.pad(w, pad)


def _gate_lanes(g):
    n = g.shape[-1] // 2
    pad = [(0, 0)] * (g.ndim - 1) + [(0, LANE - n)]
    return jnp.concatenate([jnp.pad(g[..., :n], pad), jnp.pad(g[..., n:], pad)], axis=-1)


def _prep_w_in(w_in, gate_cols):
    a = w_in[..., 0:512]
    qkvo = w_in[..., 512:1536]
    g = _gate_lanes(w_in[..., 1536:1536 + gate_cols])
    c0 = 1536 + gate_cols
    cq = w_in[..., c0:c0 + Q_RANK]
    ckv = w_in[..., c0 + Q_RANK:c0 + Q_RANK + KV_RANK]
    kr = w_in[..., c0 + Q_RANK + KV_RANK:c0 + Q_RANK + KV_RANK + ROPE]
    d = w_in[..., c0 + Q_RANK + KV_RANK + ROPE:]
    return jnp.concatenate([a, qkvo, g, cq, ckv, _rope_slot(kr), _rope_slot(_rot_cols(kr)), d],
                           axis=-1).astype(BF16)


def _prep_mla(w_uq, w_ukv):
    n_layer = w_uq.shape[0]
    scale = (HD + ROPE) ** -0.5
    wq = w_uq.reshape(n_layer, Q_RANK, HEADS, HD + ROPE) * scale
    nope, rope = wq[..., :HD], wq[..., HD:]
    zeros_r = jnp.zeros_like(rope)
    wqa = jnp.concatenate([nope, rope, zeros_r], axis=-1).reshape(n_layer, Q_RANK, HEADS * LANE)
    wqb = jnp.concatenate([jnp.zeros_like(nope), _rot_cols(rope), zeros_r], axis=-1)
    wqb = wqb.reshape(n_layer, Q_RANK, HEADS * LANE)
    wkv = w_ukv.reshape(n_layer, KV_RANK, HEADS, 2 * HD)
    k_nope, val = wkv[..., :HD], wkv[..., HD:]
    zeros_h = jnp.zeros_like(k_nope)
    wk = jnp.concatenate([k_nope, zeros_h], axis=-1).reshape(n_layer, KV_RANK, HEADS * LANE)
    even = jnp.concatenate([val, zeros_h], axis=-1)
    odd = jnp.concatenate([zeros_h, val], axis=-1)
    is_odd = (jnp.arange(HEADS) % 2 == 1)[None, None, :, None]
    wv = jnp.where(is_odd, odd, even).reshape(n_layer, KV_RANK, HEADS * LANE)
    return wqa.astype(BF16), wqb.astype(BF16), wk.astype(BF16), wv.astype(BF16)


def _rope_tables(t_len, rotate):
    ones = jnp.ones((t_len, HD), F32)
    zeros = jnp.zeros((t_len, HD), F32)
    tail = jnp.zeros((t_len, LANE - HD - ROPE), F32)
    if not rotate:
        return (jnp.concatenate([ones, jnp.ones((t_len, ROPE), F32), tail], axis=-1),
                jnp.zeros((t_len, LANE), F32))
    rows = t_len // GRID_W
    row = jnp.repeat(jnp.arange(rows, dtype=F32), GRID_W)
    col = jnp.tile(jnp.arange(GRID_W, dtype=F32), rows)
    nf = ROPE // 4
    inv = ROPE_BASE ** (-jnp.arange(nf, dtype=F32) / nf)
    ar = row[:, None] * inv
    ac = col[:, None] * inv
    cos = jnp.concatenate([jnp.cos(ar), jnp.cos(ar), jnp.cos(ac), jnp.cos(ac)], axis=-1)
    sin = jnp.concatenate([jnp.sin(ar), jnp.sin(ar), jnp.sin(ac), jnp.sin(ac)], axis=-1)
    return (jnp.concatenate([ones, cos, tail], axis=-1), jnp.concatenate([zeros, sin, tail], axis=-1))


def _prep_s5(bb_re, bb_im, c_re, c_im):
    eye = jnp.eye(S5_G, dtype=F32)
    to_b = lambda bb: jnp.einsum("ldgnc,gh->ldgchn", bb, eye).reshape(bb.shape[0], 2, MIX, S5_STATES)
    to_c = lambda cc: jnp.einsum("ldgcn,gh->ldgnhc", cc, eye).reshape(cc.shape[0], 2, S5_STATES, MIX)
    bd = jnp.concatenate([to_b(bb_re), to_b(bb_im)], axis=-1)
    cd = jnp.concatenate([to_c(c_re.astype(F32)), -to_c(c_im.astype(F32))], axis=-2)
    return bd.astype(BF16), cd.astype(BF16)


def _layer(x, bsz, t_len, mod, row_fn, p, ctx, tm):
    za, zb, zg, zc, zd = _pre_mixer(x, mod, row_fn, p["norm1_g"], p["w_in"], tm)
    seq = lambda a: a.reshape(bsz, t_len, a.shape[-1])
    ya = _conv_module(seq(za), p["conv_w"], p["conv_b"], p["conv_ln_g"], p["conv_ln_b"])
    yb, c_fin, n_fin, m_fin = _mlstm_mixer(seq(zb), seq(zg), p["gate_b"], p["mlstm_norm_g"],
                                           ctx["mlstm_c"], ctx["mlstm_n"], ctx["mlstm_m"])
    yc, ckv = _mla_mixer(seq(zc), ctx["cos"], ctx["sin"], p["mla_q_norm_g"], p["mla_kv_norm_g"],
                         p["wqa"], p["wqb"], p["wk"], p["wv"], ctx.get("past_ckv"), ctx.get("past_kr"))
    ys, s_re, s_im = _s5_scan(seq(zd).transpose(1, 0, 2), p["s5_bd"], p["s5_cd"], p["s5_ab_re"], p["s5_ab_im"],
                              ctx["s5_re"], ctx["s5_im"])
    ys = ys.transpose(0, 2, 1, 3)
    flat = lambda a: a.reshape(bsz * t_len, a.shape[-1])
    x1, h2, gate = _post_mixer(x, flat(ya), flat(yb), flat(yc), ys.reshape(2, bsz * t_len, MIX), zd, mod, row_fn,
                               p["w_out"], p["s5_d"], p["s5_w_glu"], p["s5_b_glu"], p["norm2_g"],
                               p["w_router"], p["b_router"], _POST_SUBTILES * tm)
    x2 = _moe(h2, gate, x1, mod, row_fn, p["moe_w1"], p["moe_w3"], p["moe_w2"], p["layer"], _MOE_TM)
    krope = seq(zc)[:, :, Q_RANK + KV_RANK + HD:Q_RANK + KV_RANK + HD + ROPE].astype(F32)
    return x2, (ckv, krope, c_fin, n_fin, m_fin, s_re, s_im)


def kernel(x_prompt, x_sample, cache_mla_ckv, cache_mla_krope, state_mlstm_C, state_mlstm_n, state_mlstm_m, state_s5, c, c_ctx, norm1_g, norm2_g, final_g, w_mod, b_mod, w_in, w_out, conv_w, conv_b, conv_ln_g, conv_ln_b, mlstm_gate_b, mlstm_norm_g, mla_q_norm_g, mla_w_uq, mla_kv_norm_g, mla_w_ukv, s5_a_re, s5_a_im, s5_log_dt, s5_b_re, s5_b_im, s5_c_re, s5_c_im, s5_d, s5_w_glu, s5_b_glu, moe_w_group, moe_b_group, moe_w_expert, moe_b_expert, moe_w1, moe_w3, moe_w2):
    n_layer = w_in.shape[0]
    b_ctx, t_ctx, d = x_prompt.shape
    b_lat, t_lat, _ = x_sample.shape
    nd = 2 * HEADS
    tm = 512
    assert b_lat + 1 <= 16 and t_lat % _MOE_TM == 0 and (b_ctx * t_ctx) % _MOE_TM == 0

    c_all = jnp.zeros((16, d), F32).at[0].set(c_ctx).at[1:1 + b_lat].set(c)
    mod_all = _modulation(c_all, w_mod, b_mod).reshape(n_layer, 16, 6, d)
    gate_cols = 4 * HEADS
    w_in_ext = _prep_w_in(w_in, gate_cols)
    wqa, wqb, wk, wv = _prep_mla(mla_w_uq, mla_w_ukv)
    ab_re, ab_im, bb_re, bb_im = _s5_discretise(s5_a_re, s5_a_im, s5_log_dt, s5_b_re, s5_b_im)
    s5_bd, s5_cd = _prep_s5(bb_re, bb_im, s5_c_re, s5_c_im)
    w_router = jnp.pad(jnp.concatenate([moe_w_expert, moe_w_group], axis=-1),
                       [(0, 0), (0, 0), (0, LANE - N_EXPERT - N_GROUP)]).astype(BF16)
    b_router = jnp.pad(jnp.concatenate([moe_b_expert, moe_b_group], axis=-1),
                       [(0, 0), (0, LANE - N_EXPERT - N_GROUP)])
    gate_b = _gate_lanes(mlstm_gate_b.reshape(n_layer, gate_cols))
    conv_w_p = jnp.pad(conv_w, [(0, 0), (0, 32 - CONV_WIDTH), (0, 0)])
    w_out_b = w_out.astype(BF16)
    w_glu_b = s5_w_glu.astype(BF16)
    flat_experts = lambda w: w.astype(BF16).reshape((n_layer * N_EXPERT,) + w.shape[2:])
    w1_b, w3_b, w2_b = flat_experts(moe_w1), flat_experts(moe_w3), flat_experts(moe_w2)
    row = lambda a, l: a[l][None, :]

    cos_ctx, sin_ctx = _rope_tables(t_ctx, rotate=False)
    cos_lat, sin_lat = _rope_tables(t_lat, rotate=True)
    zero_state = dict(
        mlstm_c=jnp.zeros((b_ctx, nd, HD, HD), F32), mlstm_n=jnp.zeros((b_ctx, nd, HD), F32),
        mlstm_m=jnp.zeros((b_ctx, nd, LANE), F32),
        s5_re=jnp.zeros((2, b_ctx, S5_STATES), F32), s5_im=jnp.zeros((2, b_ctx, S5_STATES), F32),
        cos=cos_ctx, sin=sin_ctx)

    row_ctx = lambda i, tile: 0
    row_lat = lambda i, tile: 1 + (i * tile) // t_lat

    x_ctx = x_prompt.reshape(b_ctx * t_ctx, d)
    x_lat = x_sample.reshape(b_lat * t_lat, d)
    outs = []
    for l in range(n_layer):
        p = dict(norm1_g=row(norm1_g, l), norm2_g=row(norm2_g, l), w_in=w_in_ext[l], w_out=w_out_b[l],
                 conv_w=conv_w_p[l], conv_b=row(conv_b, l), conv_ln_g=row(conv_ln_g, l),
                 conv_ln_b=row(conv_ln_b, l), gate_b=row(gate_b, l), mlstm_norm_g=row(mlstm_norm_g, l),
                 mla_q_norm_g=row(mla_q_norm_g, l), mla_kv_norm_g=row(mla_kv_norm_g, l),
                 wqa=wqa[l], wqb=wqb[l], wk=wk[l], wv=wv[l],
                 s5_bd=s5_bd[l], s5_cd=s5_cd[l], s5_ab_re=ab_re[l][:, None, :], s5_ab_im=ab_im[l][:, None, :],
                 s5_d=row(s5_d, l), s5_w_glu=w_glu_b[l], s5_b_glu=row(s5_b_glu, l),
                 w_router=w_router[l], b_router=row(b_router, l),
                 moe_w1=w1_b, moe_w3=w3_b, moe_w2=w2_b, layer=l)
        x_ctx, st = _layer(x_ctx, b_ctx, t_ctx, mod_all[l], row_ctx, p, zero_state, tm)
        outs.append(st)
        s5_l = state_s5[:, l].reshape(b_lat, 2, S5_STATES, 2)
        lat_state = dict(
            mlstm_c=jnp.swapaxes(state_mlstm_C[:, l].reshape(b_lat, nd, HD, HD), -1, -2),
            mlstm_n=state_mlstm_n[:, l].reshape(b_lat, nd, HD),
            mlstm_m=jnp.broadcast_to(state_mlstm_m[:, l].reshape(b_lat, nd, 1), (b_lat, nd, LANE)),
            s5_re=s5_l[..., 0].transpose(1, 0, 2), s5_im=s5_l[..., 1].transpose(1, 0, 2),
            cos=cos_lat, sin=sin_lat,
            past_ckv=cache_mla_ckv[:, l], past_kr=_rope_slot(cache_mla_krope[:, l]))
        x_lat, _ = _layer(x_lat, b_lat, t_lat, mod_all[l], row_lat, p, lat_state, tm)

    y_prompt = _final_norm(x_ctx, final_g[None, :], tm).reshape(b_ctx, t_ctx, d)
    y_sample = _final_norm(x_lat, final_g[None, :], tm).reshape(b_lat, t_lat, d)
    stack = lambda i: jnp.stack([o[i] for o in outs], axis=1)
    new_ckv = stack(0)
    new_krope = stack(1)
    new_c = jnp.swapaxes(stack(2), -1, -2).reshape(b_ctx, n_layer, 2, HEADS, HD, HD)
    new_n = stack(3).reshape(b_ctx, n_layer, 2, HEADS, HD)
    new_m = stack(4)[..., 0].reshape(b_ctx, n_layer, 2, HEADS)
    s_re = jnp.stack([o[5] for o in outs], axis=0)
    s_im = jnp.stack([o[6] for o in outs], axis=0)
    new_s5 = jnp.stack([s_re, s_im], axis=-1).transpose(2, 0, 1, 3, 4)
    new_s5 = new_s5.reshape(b_ctx, n_layer, 2, S5_G, S5_N, 2)
    return (y_prompt, y_sample, new_ckv, new_krope, new_c, new_n, new_m, new_s5)
```
